```python
import math
import jax, jax.numpy as jnp
from jax import lax
import numpy as np

D_MODEL = 4096
BATCH = 8
SEQ = 4096
DEPTH = 1

GRID_W = 64
CTX_LEN = 256
DN = D_MODEL // 2
NH = 16
HD = DN // NH
CONF = D_MODEL - DN
SHORT_CONV = 7
CONF_K = 31
CHUNK = 64
D_FF = 4 * D_MODEL
MOD_INIT = 0.5
Z_OFF = 3 * DN
G_OFF = 4 * DN
CONF_OFF = 4 * DN + 4 * NH
IN_COLS = CONF_OFF + 2 * CONF

kernel_name = "hybrid_deltanet_conformer_dit_block"


def layer_norm(x, g, b, eps=1e-5):
    xf = x.astype(jnp.float32)
    mu = jnp.mean(xf, axis=-1, keepdims=True)
    var = jnp.mean(jnp.square(xf - mu), axis=-1, keepdims=True)
    return ((xf - mu) * lax.rsqrt(var + eps) * g.astype(jnp.float32) + b.astype(jnp.float32)).astype(x.dtype)


def l2norm(x, eps=1e-6):
    xf = x.astype(jnp.float32)
    return xf * lax.rsqrt(jnp.sum(jnp.square(xf), axis=-1, keepdims=True) + eps)


def modulate(x, shift, scale):
    return x * (1.0 + scale) + shift


def dwconv1d(x, w):
    k = w.shape[0]
    return lax.conv_general_dilated(
        x, w[:, None, :].astype(x.dtype), window_strides=(1,), padding=[(k // 2, k // 2)],
        dimension_numbers=('NWC', 'WIO', 'NWC'), feature_group_count=x.shape[-1])


def gated_delta_chunked(q, k, v, g, beta, s0):
    b_, l_, h_, dk = q.shape
    dv = v.shape[-1]
    n = l_ // CHUNK
    f32 = jnp.float32

    def chunks(t):
        t = t.astype(f32).reshape((b_, n, CHUNK, h_) + t.shape[3:])
        return jnp.moveaxis(t, (1, 3), (0, 2))

    q = chunks(q) * (dk ** -0.5)
    k, v, beta = chunks(k), chunks(v), chunks(beta)
    gc = jnp.cumsum(chunks(g), axis=-1)
    tri = jnp.tril(jnp.ones((CHUNK, CHUNK), dtype=bool))
    diff = gc[..., :, None] - gc[..., None, :]
    gamma = jnp.where(tri, jnp.exp(jnp.where(tri, diff, 0.0)), 0.0)
    kb = k * beta[..., None]
    a_mat = jnp.tril(jnp.einsum('nbhck,nbhdk->nbhcd', kb, k) * gamma, -1)
    eye = jnp.eye(CHUNK, dtype=f32)
    rhs = jnp.concatenate([v * beta[..., None], kb * jnp.exp(gc)[..., None]], axis=-1)
    sol = lax.linalg.triangular_solve(eye + a_mat, rhs, left_side=True, lower=True, unit_diagonal=True)
    u, w = sol[..., :dv], sol[..., dv:]
    attn = jnp.einsum('nbhck,nbhdk->nbhcd', q, k) * gamma
    q_dec = q * jnp.exp(gc)[..., None]
    g_last = gc[..., -1:]
    k_dec = k * jnp.exp(g_last - gc)[..., None]
    d_last = jnp.exp(g_last)[..., None]

    def step(s, xs):
        qd, kd, u_c, w_c, at, dl = xs
        v_new = u_c - jnp.einsum('bhck,bhkv->bhcv', w_c, s)
        o = jnp.einsum('bhck,bhkv->bhcv', qd, s) + jnp.einsum('bhcd,bhdv->bhcv', at, v_new)
        s = s * dl + jnp.einsum('bhck,bhcv->bhkv', kd, v_new)
        return s, o

    s_fin, o = lax.scan(step, s0.astype(f32), (q_dec, k_dec, u, w, attn, d_last))
    o = jnp.moveaxis(o, (0, 2), (1, 3)).reshape(b_, l_, h_, dv)
    return o, s_fin


def delta_inputs(h, w_conv, a_log_f, dt_f, a_log_b, dt_b):
    b_, l_, _ = h.shape
    qkv = jax.nn.silu(dwconv1d(h[..., :3 * DN], w_conv))
    q, k, v = [t.reshape(b_, l_, NH, HD) for t in jnp.split(qkv, 3, axis=-1)]
    q, k = l2norm(q), l2norm(k)
    gates = h[..., G_OFF:CONF_OFF].astype(jnp.float32)
    a_f, b_f, a_b, b_b = jnp.split(gates, 4, axis=-1)
    g_f = -jnp.exp(a_log_f.astype(jnp.float32)) * jax.nn.softplus(a_f + dt_f.astype(jnp.float32))
    g_b = -jnp.exp(a_log_b.astype(jnp.float32)) * jax.nn.softplus(a_b + dt_b.astype(jnp.float32))
    return q, k, v, g_f, jax.nn.sigmoid(b_f), g_b, jax.nn.sigmoid(b_b)


def bidir_delta(q, k, v, g_f, beta_f, g_b, beta_b, s0_f, s0_b):
    o_f, s_f = gated_delta_chunked(q, k, v, g_f, beta_f, s0_f)
    flip = lambda t: t[:, ::-1]
    o_b, s_b = gated_delta_chunked(flip(q), flip(k), flip(v), flip(g_b), flip(beta_b), s0_b)
    return o_f + flip(o_b), s_f, s_b


def conformer_conv(h, w_dw, b_dw, ln_g, ln_b, grid):
    val = h[..., CONF_OFF:CONF_OFF + CONF]
    gate = h[..., CONF_OFF + CONF:]
    y = val * jax.nn.sigmoid(gate)
    b_, l_, c_ = y.shape
    if grid:
        rows = l_ // GRID_W
        half = c_ // 2
        yh = dwconv1d(y[..., :half].reshape(b_ * rows, GRID_W, half), w_dw[:, :half]).reshape(b_, l_, half)
        yv = y[..., half:].reshape(b_, rows, GRID_W, half).transpose(0, 2, 1, 3).reshape(b_ * GRID_W, rows, half)
        yv = dwconv1d(yv, w_dw[:, half:]).reshape(b_, GRID_W, rows, half).transpose(0, 2, 1, 3).reshape(b_, l_, half)
        y = jnp.concatenate([yh, yv], axis=-1)
    else:
        y = dwconv1d(y, w_dw)
    y = layer_norm(y + b_dw, ln_g, ln_b)
    return jax.nn.silu(y)


def mixer_output(h, o, dn_norm_g, w_dw, b_dw, cln_g, cln_b, w_out, grid):
    b_, l_, _ = h.shape
    ms = jnp.mean(jnp.square(o), axis=-1, keepdims=True)
    o = (o * lax.rsqrt(ms + 1e-6) * dn_norm_g.astype(jnp.float32)).reshape(b_, l_, DN).astype(h.dtype)
    dn_out = o * jax.nn.silu(h[..., Z_OFF:G_OFF])
    conf_out = conformer_conv(h, w_dw, b_dw, cln_g, cln_b, grid)
    return jnp.concatenate([dn_out, conf_out], axis=-1) @ w_out


def sqrelu_mlp(u, w1, b1, w2, b2):
    return jnp.square(jax.nn.relu(u @ w1 + b1)) @ w2 + b2


def _fwd_setup_inputs(seed: int = 0) -> dict:
    key = jax.random.key(seed)
    ks = jax.random.split(key, 32)
    f32 = jnp.float32
    nrm = lambda k, shape, s: jax.random.normal(k, shape, f32) * s
    beta_dn = (8.0 * DEPTH) ** -0.25
    col_scale = jnp.concatenate([jnp.ones((2 * DN,), f32), jnp.full((DN,), beta_dn, f32),
                                 jnp.ones((IN_COLS - 3 * DN,), f32)])
    dt_f = jnp.exp(jax.random.uniform(ks[12], (DEPTH, NH), f32, math.log(1e-3), math.log(1e-1)))
    dt_b = jnp.exp(jax.random.uniform(ks[13], (DEPTH, NH), f32, math.log(1e-3), math.log(1e-1)))
    return {
        "x": nrm(ks[0], (BATCH, SEQ, D_MODEL), 1.0),
        "c": nrm(ks[1], (BATCH, D_MODEL), 1.0),
        "ctx": nrm(ks[2], (BATCH, CTX_LEN, D_MODEL), 1.0),
        "c_ctx": nrm(ks[3], (D_MODEL,), 1.0),
        "ln_in_g": 1.0 + nrm(ks[4], (D_MODEL,), 0.02),
        "ln_in_b": nrm(ks[5], (D_MODEL,), 0.02),
        "w_mod": nrm(ks[6], (DEPTH, D_MODEL, 6 * D_MODEL), MOD_INIT * D_MODEL ** -0.5),
        "b_mod": nrm(ks[7], (DEPTH, 6 * D_MODEL), 0.02),
        "w_in": nrm(ks[8], (DEPTH, D_MODEL, IN_COLS), D_MODEL ** -0.5) * col_scale,
        "w_qkv_conv": nrm(ks[9], (DEPTH, SHORT_CONV, 3 * DN), SHORT_CONV ** -0.5),
        "a_log_f": jnp.log(jax.random.uniform(ks[10], (DEPTH, NH), f32, 1.0, 16.0)),
        "dt_bias_f": dt_f + jnp.log(-jnp.expm1(-dt_f)),
        "a_log_b": jnp.log(jax.random.uniform(ks[11], (DEPTH, NH), f32, 1.0, 16.0)),
        "dt_bias_b": dt_b + jnp.log(-jnp.expm1(-dt_b)),
        "dn_norm_g": 1.0 + nrm(ks[14], (DEPTH, HD), 0.02),
        "conf_dw_w": nrm(ks[15], (DEPTH, CONF_K, CONF), CONF_K ** -0.5),
        "conf_dw_b": nrm(ks[16], (DEPTH, CONF), 0.02),
        "conf_ln_g": 1.0 + nrm(ks[17], (DEPTH, CONF), 0.02),
        "conf_ln_b": nrm(ks[18], (DEPTH, CONF), 0.02),
        "w_out": nrm(ks[19], (DEPTH, DN + CONF, D_MODEL), beta_dn * (DN + CONF) ** -0.5),
        "ln1_g": 1.0 + nrm(ks[20], (DEPTH, D_MODEL), 0.02),
        "ln1_b": nrm(ks[21], (DEPTH, D_MODEL), 0.02),
        "w_mlp1": nrm(ks[22], (DEPTH, D_MODEL, D_FF), D_MODEL ** -0.5),
        "b_mlp1": nrm(ks[23], (DEPTH, D_FF), 0.02),
        "w_mlp2": nrm(ks[24], (DEPTH, D_FF, D_MODEL), beta_dn * D_FF ** -0.5),
        "b_mlp2": nrm(ks[25], (DEPTH, D_MODEL), 0.02),
        "ln2_g": 1.0 + nrm(ks[26], (DEPTH, D_MODEL), 0.02),
        "ln2_b": nrm(ks[27], (DEPTH, D_MODEL), 0.02),
    }


def _fwd_reference(x, c, ctx, c_ctx, ln_in_g, ln_in_b, w_mod, b_mod, w_in, w_qkv_conv,
              a_log_f, dt_bias_f, a_log_b, dt_bias_b, dn_norm_g, conf_dw_w, conf_dw_b,
              conf_ln_g, conf_ln_b, w_out, ln1_g, ln1_b, w_mlp1, b_mlp1, w_mlp2, b_mlp2,
              ln2_g, ln2_b):
    alpha = (2.0 * DEPTH) ** 0.25
    b_ = x.shape[0]
    x = layer_norm(x, ln_in_g, ln_in_b)
    xc = layer_norm(ctx, ln_in_g, ln_in_b)
    for l in range(DEPTH):
        last = l == DEPTH - 1
        mod = (jax.nn.silu(c) @ w_mod[l] + b_mod[l])[:, None, :]
        mod_c = (jax.nn.silu(c_ctx) @ w_mod[l] + b_mod[l])[None, None, :]
        sh_a, sc_a, g_a, sh_m, sc_m, g_m = jnp.split(mod, 6, axis=-1)
        csh_a, csc_a, cg_a, csh_m, csc_m, cg_m = jnp.split(mod_c, 6, axis=-1)
        dn_par = (w_qkv_conv[l], a_log_f[l], dt_bias_f[l], a_log_b[l], dt_bias_b[l])
        conf_par = (conf_dw_w[l], conf_dw_b[l], conf_ln_g[l], conf_ln_b[l])

        hc = modulate(xc, csh_a, csc_a) @ w_in[l]
        s0 = jnp.zeros((b_, NH, HD, HD), jnp.float32)
        oc, s_f, s_b = bidir_delta(*delta_inputs(hc, *dn_par), s0, s0)

        h = modulate(x, sh_a, sc_a) @ w_in[l]
        o, _, _ = bidir_delta(*delta_inputs(h, *dn_par), s_f, s_b)
        y = mixer_output(h, o, dn_norm_g[l], *conf_par, w_out[l], True)
        x = layer_norm(alpha * x + g_a * y, ln1_g[l], ln1_b[l])
        y2 = sqrelu_mlp(modulate(x, sh_m, sc_m), w_mlp1[l], b_mlp1[l], w_mlp2[l], b_mlp2[l])
        x = layer_norm(alpha * x + g_m * y2, ln2_g[l], ln2_b[l])

        if not last:
            yc = mixer_output(hc, oc, dn_norm_g[l], *conf_par, w_out[l], False)
            xc = layer_norm(alpha * xc + cg_a * yc, ln1_g[l], ln1_b[l])
            yc2 = sqrelu_mlp(modulate(xc, csh_m, csc_m), w_mlp1[l], b_mlp1[l], w_mlp2[l], b_mlp2[l])
            xc = layer_norm(alpha * xc + cg_m * yc2, ln2_g[l], ln2_b[l])
    return x


import jax as _jax
import jax.numpy as _jnp

TWIN_FORMAT = 'train_step'
FWD_PARAMS = ['x', 'c', 'ctx', 'c_ctx', 'ln_in_g', 'ln_in_b', 'w_mod', 'b_mod', 'w_in', 'w_qkv_conv', 'a_log_f', 'dt_bias_f', 'a_log_b', 'dt_bias_b', 'dn_norm_g', 'conf_dw_w', 'conf_dw_b', 'conf_ln_g', 'conf_ln_b', 'w_out', 'ln1_g', 'ln1_b', 'w_mlp1', 'b_mlp1', 'w_mlp2', 'b_mlp2', 'ln2_g', 'ln2_b']
TWIN_WEIGHTS = ['c_ctx', 'ln_in_g', 'ln_in_b', 'w_mod', 'b_mod', 'w_in', 'w_qkv_conv', 'a_log_f', 'dt_bias_f', 'a_log_b', 'dt_bias_b', 'dn_norm_g', 'conf_dw_w', 'conf_dw_b', 'conf_ln_g', 'conf_ln_b', 'w_out', 'ln1_g', 'ln1_b', 'w_mlp1', 'b_mlp1', 'w_mlp2', 'b_mlp2', 'ln2_g', 'ln2_b']
TWIN_DIFF_INPUT = 'x'
TWIN_INPUTS = ['x', 'c', 'ctx', 'c_ctx', 'ln_in_g', 'ln_in_b', 'w_mod', 'b_mod', 'w_in', 'w_qkv_conv', 'a_log_f', 'dt_bias_f', 'a_log_b', 'dt_bias_b', 'dn_norm_g', 'conf_dw_w', 'conf_dw_b', 'conf_ln_g', 'conf_ln_b', 'w_out', 'ln1_g', 'ln1_b', 'w_mlp1', 'b_mlp1', 'w_mlp2', 'b_mlp2', 'ln2_g', 'ln2_b', 'loss_target', 'm_c_ctx', 'm_ln_in_g', 'm_ln_in_b', 'm_w_mod', 'm_b_mod', 'm_w_in', 'm_w_qkv_conv', 'm_a_log_f', 'm_dt_bias_f', 'm_a_log_b', 'm_dt_bias_b', 'm_dn_norm_g', 'm_conf_dw_w', 'm_conf_dw_b', 'm_conf_ln_g', 'm_conf_ln_b', 'm_w_out', 'm_ln1_g', 'm_ln1_b', 'm_w_mlp1', 'm_b_mlp1', 'm_w_mlp2', 'm_b_mlp2', 'm_ln2_g', 'm_ln2_b', 'v_c_ctx', 'v_ln_in_g', 'v_ln_in_b', 'v_w_mod', 'v_b_mod', 'v_w_in', 'v_w_qkv_conv', 'v_a_log_f', 'v_dt_bias_f', 'v_a_log_b', 'v_dt_bias_b', 'v_dn_norm_g', 'v_conf_dw_w', 'v_conf_dw_b', 'v_conf_ln_g', 'v_conf_ln_b', 'v_w_out', 'v_ln1_g', 'v_ln1_b', 'v_w_mlp1', 'v_b_mlp1', 'v_w_mlp2', 'v_b_mlp2', 'v_ln2_g', 'v_ln2_b']
TWIN_OUTPUTS = ['loss', 'grad_x', 'grad_c_ctx', 'grad_ln_in_g', 'grad_ln_in_b', 'grad_w_mod', 'grad_b_mod', 'grad_w_in', 'grad_w_qkv_conv', 'grad_a_log_f', 'grad_dt_bias_f', 'grad_a_log_b', 'grad_dt_bias_b', 'grad_dn_norm_g', 'grad_conf_dw_w', 'grad_conf_dw_b', 'grad_conf_ln_g', 'grad_conf_ln_b', 'grad_w_out', 'grad_ln1_g', 'grad_ln1_b', 'grad_w_mlp1', 'grad_b_mlp1', 'grad_w_mlp2', 'grad_b_mlp2', 'grad_ln2_g', 'grad_ln2_b', 'delta_c_ctx', 'delta_ln_in_g', 'delta_ln_in_b', 'delta_w_mod', 'delta_b_mod', 'delta_w_in', 'delta_w_qkv_conv', 'delta_a_log_f', 'delta_dt_bias_f', 'delta_a_log_b', 'delta_dt_bias_b', 'delta_dn_norm_g', 'delta_conf_dw_w', 'delta_conf_dw_b', 'delta_conf_ln_g', 'delta_conf_ln_b', 'delta_w_out', 'delta_ln1_g', 'delta_ln1_b', 'delta_w_mlp1', 'delta_b_mlp1', 'delta_w_mlp2', 'delta_b_mlp2', 'delta_ln2_g', 'delta_ln2_b', 'new_m_c_ctx', 'new_m_ln_in_g', 'new_m_ln_in_b', 'new_m_w_mod', 'new_m_b_mod', 'new_m_w_in', 'new_m_w_qkv_conv', 'new_m_a_log_f', 'new_m_dt_bias_f', 'new_m_a_log_b', 'new_m_dt_bias_b', 'new_m_dn_norm_g', 'new_m_conf_dw_w', 'new_m_conf_dw_b', 'new_m_conf_ln_g', 'new_m_conf_ln_b', 'new_m_w_out', 'new_m_ln1_g', 'new_m_ln1_b', 'new_m_w_mlp1', 'new_m_b_mlp1', 'new_m_w_mlp2', 'new_m_b_mlp2', 'new_m_ln2_g', 'new_m_ln2_b', 'new_v_c_ctx', 'new_v_ln_in_g', 'new_v_ln_in_b', 'new_v_w_mod', 'new_v_b_mod', 'new_v_w_in', 'new_v_w_qkv_conv', 'new_v_a_log_f', 'new_v_dt_bias_f', 'new_v_a_log_b', 'new_v_dt_bias_b', 'new_v_dn_norm_g', 'new_v_conf_dw_w', 'new_v_conf_dw_b', 'new_v_conf_ln_g', 'new_v_conf_ln_b', 'new_v_w_out', 'new_v_ln1_g', 'new_v_ln1_b', 'new_v_w_mlp1', 'new_v_b_mlp1', 'new_v_w_mlp2', 'new_v_b_mlp2', 'new_v_ln2_g', 'new_v_ln2_b']
TWIN_LEAF_KINDS = {'loss': 'loss', 'grad_x': 'grad_x', 'grad_c_ctx': 'grad_w', 'grad_ln_in_g': 'grad_w', 'grad_ln_in_b': 'grad_w', 'grad_w_mod': 'grad_w', 'grad_b_mod': 'grad_w', 'grad_w_in': 'grad_w', 'grad_w_qkv_conv': 'grad_w', 'grad_a_log_f': 'grad_w', 'grad_dt_bias_f': 'grad_w', 'grad_a_log_b': 'grad_w', 'grad_dt_bias_b': 'grad_w', 'grad_dn_norm_g': 'grad_w', 'grad_conf_dw_w': 'grad_w', 'grad_conf_dw_b': 'grad_w', 'grad_conf_ln_g': 'grad_w', 'grad_conf_ln_b': 'grad_w', 'grad_w_out': 'grad_w', 'grad_ln1_g': 'grad_w', 'grad_ln1_b': 'grad_w', 'grad_w_mlp1': 'grad_w', 'grad_b_mlp1': 'grad_w', 'grad_w_mlp2': 'grad_w', 'grad_b_mlp2': 'grad_w', 'grad_ln2_g': 'grad_w', 'grad_ln2_b': 'grad_w', 'delta_c_ctx': 'delta_w', 'delta_ln_in_g': 'delta_w', 'delta_ln_in_b': 'delta_w', 'delta_w_mod': 'delta_w', 'delta_b_mod': 'delta_w', 'delta_w_in': 'delta_w', 'delta_w_qkv_conv': 'delta_w', 'delta_a_log_f': 'delta_w', 'delta_dt_bias_f': 'delta_w', 'delta_a_log_b': 'delta_w', 'delta_dt_bias_b': 'delta_w', 'delta_dn_norm_g': 'delta_w', 'delta_conf_dw_w': 'delta_w', 'delta_conf_dw_b': 'delta_w', 'delta_conf_ln_g': 'delta_w', 'delta_conf_ln_b': 'delta_w', 'delta_w_out': 'delta_w', 'delta_ln1_g': 'delta_w', 'delta_ln1_b': 'delta_w', 'delta_w_mlp1': 'delta_w', 'delta_b_mlp1': 'delta_w', 'delta_w_mlp2': 'delta_w', 'delta_b_mlp2': 'delta_w', 'delta_ln2_g': 'delta_w', 'delta_ln2_b': 'delta_w', 'new_m_c_ctx': 'new_m', 'new_m_ln_in_g': 'new_m', 'new_m_ln_in_b': 'new_m', 'new_m_w_mod': 'new_m', 'new_m_b_mod': 'new_m', 'new_m_w_in': 'new_m', 'new_m_w_qkv_conv': 'new_m', 'new_m_a_log_f': 'new_m', 'new_m_dt_bias_f': 'new_m', 'new_m_a_log_b': 'new_m', 'new_m_dt_bias_b': 'new_m', 'new_m_dn_norm_g': 'new_m', 'new_m_conf_dw_w': 'new_m', 'new_m_conf_dw_b': 'new_m', 'new_m_conf_ln_g': 'new_m', 'new_m_conf_ln_b': 'new_m', 'new_m_w_out': 'new_m', 'new_m_ln1_g': 'new_m', 'new_m_ln1_b': 'new_m', 'new_m_w_mlp1': 'new_m', 'new_m_b_mlp1': 'new_m', 'new_m_w_mlp2': 'new_m', 'new_m_b_mlp2': 'new_m', 'new_m_ln2_g': 'new_m', 'new_m_ln2_b': 'new_m', 'new_v_c_ctx': 'new_v', 'new_v_ln_in_g': 'new_v', 'new_v_ln_in_b': 'new_v', 'new_v_w_mod': 'new_v', 'new_v_b_mod': 'new_v', 'new_v_w_in': 'new_v', 'new_v_w_qkv_conv': 'new_v', 'new_v_a_log_f': 'new_v', 'new_v_dt_bias_f': 'new_v', 'new_v_a_log_b': 'new_v', 'new_v_dt_bias_b': 'new_v', 'new_v_dn_norm_g': 'new_v', 'new_v_conf_dw_w': 'new_v', 'new_v_conf_dw_b': 'new_v', 'new_v_conf_ln_g': 'new_v', 'new_v_conf_ln_b': 'new_v', 'new_v_w_out': 'new_v', 'new_v_ln1_g': 'new_v', 'new_v_ln1_b': 'new_v', 'new_v_w_mlp1': 'new_v', 'new_v_b_mlp1': 'new_v', 'new_v_w_mlp2': 'new_v', 'new_v_b_mlp2': 'new_v', 'new_v_ln2_g': 'new_v', 'new_v_ln2_b': 'new_v'}


def _forward(args):
    return _fwd_reference(*[args[k] for k in FWD_PARAMS])


def _output_shape():
    out = _jax.eval_shape(lambda: _forward(_fwd_setup_inputs(0)))
    return out.shape, out.dtype

N_MICROBATCH = 1
ADAM_LR = 0.001
ADAM_B1 = 0.9
ADAM_B2 = 0.999
ADAM_EPS = 1e-08
ADAM_WD = 0.01
ADAM_STEP = 10
PER_EXAMPLE_BATCH_AXIS = {'x': 0, 'c': 0, 'ctx': 0, 'loss_target': 0}
SHARED_INPUTS = []
_WEIGHT_DTYPES = {'c_ctx': _jnp.float32, 'ln_in_g': _jnp.float32, 'ln_in_b': _jnp.float32, 'w_mod': _jnp.float32, 'b_mod': _jnp.float32, 'w_in': _jnp.float32, 'w_qkv_conv': _jnp.float32, 'a_log_f': _jnp.float32, 'dt_bias_f': _jnp.float32, 'a_log_b': _jnp.float32, 'dt_bias_b': _jnp.float32, 'dn_norm_g': _jnp.float32, 'conf_dw_w': _jnp.float32, 'conf_dw_b': _jnp.float32, 'conf_ln_g': _jnp.float32, 'conf_ln_b': _jnp.float32, 'w_out': _jnp.float32, 'ln1_g': _jnp.float32, 'ln1_b': _jnp.float32, 'w_mlp1': _jnp.float32, 'b_mlp1': _jnp.float32, 'w_mlp2': _jnp.float32, 'b_mlp2': _jnp.float32, 'ln2_g': _jnp.float32, 'ln2_b': _jnp.float32}
MOMENT_SCALE = {'c_ctx': 2.412750e-04, 'ln_in_g': 2.370679e-01, 'ln_in_b': 1.137078e-01, 'w_mod': 1.066333e-02, 'b_mod': 2.060346e-02, 'w_in': 4.055867e-03, 'w_qkv_conv': 3.270828e-03, 'a_log_f': 1.032129e-02, 'dt_bias_f': 1.012536e-02, 'a_log_b': 9.872353e-03, 'dt_bias_b': 9.502003e-03, 'dn_norm_g': 1.815744e-02, 'conf_dw_w': 4.275639e-03, 'conf_dw_b': 1.029581e-02, 'conf_ln_g': 5.549730e-03, 'conf_ln_b': 5.772613e-03, 'w_out': 7.338201e-03, 'ln1_g': 2.393018e-01, 'ln1_b': 1.139210e-01, 'w_mlp1': 5.480171e-03, 'b_mlp1': 6.177961e-03, 'w_mlp2': 1.775610e-02, 'b_mlp2': 1.771134e-02, 'ln2_g': 8.000468e+00, 'ln2_b': 4.363844e-01}


def _to_microbatches(a, axis):
    t = _jnp.moveaxis(a, axis, 0)
    t = t.reshape((N_MICROBATCH, t.shape[0] // N_MICROBATCH) + t.shape[1:])
    return _jnp.moveaxis(t, 1, axis + 1)


def setup_inputs(seed: int = 0) -> dict:
    inp = _fwd_setup_inputs(seed)
    key = _jax.random.fold_in(_jax.random.key(seed), 7919)
    shape, _ = _output_shape()
    out = dict(inp)
    out["loss_target"] = _jax.random.normal(_jax.random.fold_in(key, 0), shape, _jnp.float32)
    for i, name in enumerate(TWIN_WEIGHTS):
        w = inp[name].astype(_jnp.float32)
        if MOMENT_SCALE is None:
            s = _jnp.sqrt(_jnp.mean(_jnp.square(w)) + 1e-30)
        else:
            s = MOMENT_SCALE[name]
        km, kv = _jax.random.split(_jax.random.fold_in(key, i + 1))
        out[name] = w
        out["m_" + name] = s * _jax.random.normal(km, w.shape, _jnp.float32)
        out["v_" + name] = (s * s) * _jax.random.uniform(kv, w.shape, _jnp.float32, 0.5, 1.5)
    if N_MICROBATCH > 1:
        for name, axis in PER_EXAMPLE_BATCH_AXIS.items():
            out[name] = _to_microbatches(out[name], axis)
    return {'x': out['x'], 'c': out['c'], 'ctx': out['ctx'], 'c_ctx': out['c_ctx'], 'ln_in_g': out['ln_in_g'], 'ln_in_b': out['ln_in_b'], 'w_mod': out['w_mod'], 'b_mod': out['b_mod'], 'w_in': out['w_in'], 'w_qkv_conv': out['w_qkv_conv'], 'a_log_f': out['a_log_f'], 'dt_bias_f': out['dt_bias_f'], 'a_log_b': out['a_log_b'], 'dt_bias_b': out['dt_bias_b'], 'dn_norm_g': out['dn_norm_g'], 'conf_dw_w': out['conf_dw_w'], 'conf_dw_b': out['conf_dw_b'], 'conf_ln_g': out['conf_ln_g'], 'conf_ln_b': out['conf_ln_b'], 'w_out': out['w_out'], 'ln1_g': out['ln1_g'], 'ln1_b': out['ln1_b'], 'w_mlp1': out['w_mlp1'], 'b_mlp1': out['b_mlp1'], 'w_mlp2': out['w_mlp2'], 'b_mlp2': out['b_mlp2'], 'ln2_g': out['ln2_g'], 'ln2_b': out['ln2_b'], 'loss_target': out['loss_target'], 'm_c_ctx': out['m_c_ctx'], 'm_ln_in_g': out['m_ln_in_g'], 'm_ln_in_b': out['m_ln_in_b'], 'm_w_mod': out['m_w_mod'], 'm_b_mod': out['m_b_mod'], 'm_w_in': out['m_w_in'], 'm_w_qkv_conv': out['m_w_qkv_conv'], 'm_a_log_f': out['m_a_log_f'], 'm_dt_bias_f': out['m_dt_bias_f'], 'm_a_log_b': out['m_a_log_b'], 'm_dt_bias_b': out['m_dt_bias_b'], 'm_dn_norm_g': out['m_dn_norm_g'], 'm_conf_dw_w': out['m_conf_dw_w'], 'm_conf_dw_b': out['m_conf_dw_b'], 'm_conf_ln_g': out['m_conf_ln_g'], 'm_conf_ln_b': out['m_conf_ln_b'], 'm_w_out': out['m_w_out'], 'm_ln1_g': out['m_ln1_g'], 'm_ln1_b': out['m_ln1_b'], 'm_w_mlp1': out['m_w_mlp1'], 'm_b_mlp1': out['m_b_mlp1'], 'm_w_mlp2': out['m_w_mlp2'], 'm_b_mlp2': out['m_b_mlp2'], 'm_ln2_g': out['m_ln2_g'], 'm_ln2_b': out['m_ln2_b'], 'v_c_ctx': out['v_c_ctx'], 'v_ln_in_g': out['v_ln_in_g'], 'v_ln_in_b': out['v_ln_in_b'], 'v_w_mod': out['v_w_mod'], 'v_b_mod': out['v_b_mod'], 'v_w_in': out['v_w_in'], 'v_w_qkv_conv': out['v_w_qkv_conv'], 'v_a_log_f': out['v_a_log_f'], 'v_dt_bias_f': out['v_dt_bias_f'], 'v_a_log_b': out['v_a_log_b'], 'v_dt_bias_b': out['v_dt_bias_b'], 'v_dn_norm_g': out['v_dn_norm_g'], 'v_conf_dw_w': out['v_conf_dw_w'], 'v_conf_dw_b': out['v_conf_dw_b'], 'v_conf_ln_g': out['v_conf_ln_g'], 'v_conf_ln_b': out['v_conf_ln_b'], 'v_w_out': out['v_w_out'], 'v_ln1_g': out['v_ln1_g'], 'v_ln1_b': out['v_ln1_b'], 'v_w_mlp1': out['v_w_mlp1'], 'v_b_mlp1': out['v_b_mlp1'], 'v_w_mlp2': out['v_w_mlp2'], 'v_b_mlp2': out['v_b_mlp2'], 'v_ln2_g': out['v_ln2_g'], 'v_ln2_b': out['v_ln2_b']}


def _loss(weights, diff, rest, loss_target):
    with _jax.named_scope("forward"):
        args = {**rest, TWIN_DIFF_INPUT: diff, **{k: w.astype(_WEIGHT_DTYPES[k]) for k, w in weights.items()}}
        y = _forward(args)
    with _jax.named_scope("loss_head"):
        err = _jnp.square(y.astype(_jnp.float32) - loss_target)
        return 0.5 * _jnp.sum(_jnp.mean(err, axis=-1)) if err.ndim else 0.5 * err


def _adamw(w, g, m, v):
    m = ADAM_B1 * m + (1.0 - ADAM_B1) * g
    v = ADAM_B2 * v + (1.0 - ADAM_B2) * _jnp.square(g)
    m_hat = m / (1.0 - ADAM_B1 ** ADAM_STEP)
    v_hat = v / (1.0 - ADAM_B2 ** ADAM_STEP)
    delta = -ADAM_LR * (m_hat / (_jnp.sqrt(v_hat) + ADAM_EPS) + ADAM_WD * w)
    return delta, m, v


def reference(x, c, ctx, c_ctx, ln_in_g, ln_in_b, w_mod, b_mod, w_in, w_qkv_conv, a_log_f, dt_bias_f, a_log_b, dt_bias_b, dn_norm_g, conf_dw_w, conf_dw_b, conf_ln_g, conf_ln_b, w_out, ln1_g, ln1_b, w_mlp1, b_mlp1, w_mlp2, b_mlp2, ln2_g, ln2_b, loss_target, m_c_ctx, m_ln_in_g, m_ln_in_b, m_w_mod, m_b_mod, m_w_in, m_w_qkv_conv, m_a_log_f, m_dt_bias_f, m_a_log_b, m_dt_bias_b, m_dn_norm_g, m_conf_dw_w, m_conf_dw_b, m_conf_ln_g, m_conf_ln_b, m_w_out, m_ln1_g, m_ln1_b, m_w_mlp1, m_b_mlp1, m_w_mlp2, m_b_mlp2, m_ln2_g, m_ln2_b, v_c_ctx, v_ln_in_g, v_ln_in_b, v_w_mod, v_b_mod, v_w_in, v_w_qkv_conv, v_a_log_f, v_dt_bias_f, v_a_log_b, v_dt_bias_b, v_dn_norm_g, v_conf_dw_w, v_conf_dw_b, v_conf_ln_g, v_conf_ln_b, v_w_out, v_ln1_g, v_ln1_b, v_w_mlp1, v_b_mlp1, v_w_mlp2, v_b_mlp2, v_ln2_g, v_ln2_b):
    given = dict(x=x, c=c, ctx=ctx, c_ctx=c_ctx, ln_in_g=ln_in_g, ln_in_b=ln_in_b, w_mod=w_mod, b_mod=b_mod, w_in=w_in, w_qkv_conv=w_qkv_conv, a_log_f=a_log_f, dt_bias_f=dt_bias_f, a_log_b=a_log_b, dt_bias_b=dt_bias_b, dn_norm_g=dn_norm_g, conf_dw_w=conf_dw_w, conf_dw_b=conf_dw_b, conf_ln_g=conf_ln_g, conf_ln_b=conf_ln_b, w_out=w_out, ln1_g=ln1_g, ln1_b=ln1_b, w_mlp1=w_mlp1, b_mlp1=b_mlp1, w_mlp2=w_mlp2, b_mlp2=b_mlp2, ln2_g=ln2_g, ln2_b=ln2_b, loss_target=loss_target, m_c_ctx=m_c_ctx, m_ln_in_g=m_ln_in_g, m_ln_in_b=m_ln_in_b, m_w_mod=m_w_mod, m_b_mod=m_b_mod, m_w_in=m_w_in, m_w_qkv_conv=m_w_qkv_conv, m_a_log_f=m_a_log_f, m_dt_bias_f=m_dt_bias_f, m_a_log_b=m_a_log_b, m_dt_bias_b=m_dt_bias_b, m_dn_norm_g=m_dn_norm_g, m_conf_dw_w=m_conf_dw_w, m_conf_dw_b=m_conf_dw_b, m_conf_ln_g=m_conf_ln_g, m_conf_ln_b=m_conf_ln_b, m_w_out=m_w_out, m_ln1_g=m_ln1_g, m_ln1_b=m_ln1_b, m_w_mlp1=m_w_mlp1, m_b_mlp1=m_b_mlp1, m_w_mlp2=m_w_mlp2, m_b_mlp2=m_b_mlp2, m_ln2_g=m_ln2_g, m_ln2_b=m_ln2_b, v_c_ctx=v_c_ctx, v_ln_in_g=v_ln_in_g, v_ln_in_b=v_ln_in_b, v_w_mod=v_w_mod, v_b_mod=v_b_mod, v_w_in=v_w_in, v_w_qkv_conv=v_w_qkv_conv, v_a_log_f=v_a_log_f, v_dt_bias_f=v_dt_bias_f, v_a_log_b=v_a_log_b, v_dt_bias_b=v_dt_bias_b, v_dn_norm_g=v_dn_norm_g, v_conf_dw_w=v_conf_dw_w, v_conf_dw_b=v_conf_dw_b, v_conf_ln_g=v_conf_ln_g, v_conf_ln_b=v_conf_ln_b, v_w_out=v_w_out, v_ln1_g=v_ln1_g, v_ln1_b=v_ln1_b, v_w_mlp1=v_w_mlp1, v_b_mlp1=v_b_mlp1, v_w_mlp2=v_w_mlp2, v_b_mlp2=v_b_mlp2, v_ln2_g=v_ln2_g, v_ln2_b=v_ln2_b)
    weights = {n: given[n] for n in TWIN_WEIGHTS}
    shared = {n: given[n] for n in SHARED_INPUTS}
    per_example = {n: given[n] for n in ['x', 'c', 'ctx']}
    grad_fn = _jax.value_and_grad(_loss, argnums=(0, 1))

    def one_microbatch(ex, loss_target):
        ex = dict(ex)
        diff = ex.pop(TWIN_DIFF_INPUT)
        return grad_fn(weights, diff, {**shared, **ex}, loss_target)

    if N_MICROBATCH == 1:
        loss, (grad_w, grad_x) = one_microbatch(per_example, given["loss_target"])
    else:
        def body(carry, xs):
            loss_sum, grad_sum = carry
            l_k, (gw_k, gx_k) = one_microbatch(xs[0], xs[1])
            with _jax.named_scope("update"):
                return (loss_sum + l_k, _jax.tree.map(_jnp.add, grad_sum, gw_k)), gx_k

        init = (_jnp.zeros((), _jnp.float32), _jax.tree.map(_jnp.zeros_like, weights))
        (loss, grad_w), grad_x = _jax.lax.scan(body, init, (per_example, given["loss_target"]))
    with _jax.named_scope("update"):
        delta_w, new_m, new_v = {}, {}, {}
        for n in TWIN_WEIGHTS:
            delta_w[n], new_m[n], new_v[n] = _adamw(weights[n], grad_w[n], given["m_" + n], given["v_" + n])
    return (loss, grad_x, *[grad_w[n] for n in TWIN_WEIGHTS], *[delta_w[n] for n in TWIN_WEIGHTS],
            *[new_m[n] for n in TWIN_WEIGHTS], *[new_v[n] for n in TWIN_WEIGHTS])
```

```python
import functools

import jax
import jax.numpy as jnp
from jax import lax
from jax.experimental import pallas as pl
from jax.experimental.pallas import tpu as pltpu

f32 = jnp.float32
bf16 = jnp.bfloat16
HI = lax.Precision.HIGHEST
MESH = pl.DeviceIdType.MESH

NDEV = 8
HD = 128
CH = 64
GW = 64
LANES = 128
GPAD = 512
LN_EPS = 1e-5
ALPHA = 2.0 ** 0.25
ADAM_LR, ADAM_B1, ADAM_B2, ADAM_EPS, ADAM_WD, ADAM_STEP = 0.001, 0.9, 0.999, 1e-08, 0.01, 10
VMEM_LIMIT = 56 * 1024 * 1024
ROW_BLOCK_BYTES = 1 << 20


def _cparams(sem):
    return pltpu.CompilerParams(dimension_semantics=sem, vmem_limit_bytes=VMEM_LIMIT)


def _tile(dim, pref, align):
    t = min(pref, dim)
    t -= t % align
    while t >= align:
        if dim % t == 0:
            return t
        t -= align
    return dim


def _row_tile(nrows, width, cap=256):
    t = max(16, min(cap, ROW_BLOCK_BYTES // (4 * width)))
    t = 1 << (t.bit_length() - 1)
    while nrows % t:
        t //= 2
    return max(t, 1)


def _sigmoid(x):
    return 1.0 / (1.0 + jnp.exp(-x))


def _silu(x):
    return x * _sigmoid(x)


def _dsilu(x):
    s = _sigmoid(x)
    return s * (1.0 + x * (1.0 - s))


def _softplus(x):
    return jnp.maximum(x, 0.0) + jnp.log(1.0 + jnp.exp(-jnp.abs(x)))


def _ln_stats(r):
    mu = jnp.mean(r, axis=-1, keepdims=True)
    xc = r - mu
    rstd = lax.rsqrt(jnp.mean(xc * xc, axis=-1, keepdims=True) + LN_EPS)
    return xc * rstd, rstd


def _ln_bwd(dy, xhat, rstd, g):
    dxh = dy * g
    return rstd * (dxh - jnp.mean(dxh, axis=-1, keepdims=True) - xhat * jnp.mean(dxh * xhat, axis=-1, keepdims=True))


def _csum(a):
    return jnp.sum(a, axis=0, keepdims=True)


def _mm(name, a, b, mode, out_dtype, bias=None, out_split=1, tm=1024, tn=1024, tk=512):
    if mode == "tn":
        K, M = a.shape
    else:
        M, K = a.shape
    bp = b.shape[0] if b.ndim == 3 else 1
    brows, bcols = b.shape[-2], b.shape[-1] * bp
    N = brows if mode == "nt" else bcols
    assert K == (bcols if mode == "nt" else brows), (name, a.shape, b.shape)
    tm = _tile(M, tm, LANES if mode == "tn" else 16)
    nsplit = max(out_split, bp if mode != "nt" else 1)
    tn = _tile(N // nsplit, tn, LANES)
    tk = _tile(K // (bp if mode == "nt" else 1), tk, LANES)
    nk = K // tk
    dims = {"nn": (((1,), (0,)), ((), ())), "nt": (((1,), (1,)), ((), ())), "tn": (((0,), (0,)), ((), ()))}[mode]

    a_spec = pl.BlockSpec((tk, tm), lambda i, j, k: (k, i)) if mode == "tn" else pl.BlockSpec((tm, tk), lambda i, j, k: (i, k))
    if b.ndim == 3:
        if mode == "nt":
            per = (K // bp) // tk
            b_spec = pl.BlockSpec((None, tn, tk), lambda i, j, k: (k // per, j, k % per))
        else:
            per = (N // bp) // tn
            b_spec = pl.BlockSpec((None, tk, tn), lambda i, j, k: (j // per, k, j % per))
    elif mode == "nt":
        b_spec = pl.BlockSpec((tn, tk), lambda i, j, k: (j, k))
    else:
        b_spec = pl.BlockSpec((tk, tn), lambda i, j, k: (k, j))
    if out_split > 1:
        pero = (N // out_split) // tn
        o_spec = pl.BlockSpec((None, tm, tn), lambda i, j, k: (j // pero, i, j % pero))
        o_shape = jax.ShapeDtypeStruct((out_split, M, N // out_split), out_dtype)
    else:
        o_spec = pl.BlockSpec((tm, tn), lambda i, j, k: (i, j))
        o_shape = jax.ShapeDtypeStruct((M, N), out_dtype)
    in_specs = [a_spec, b_spec]
    args = [a, b]
    if bias is not None:
        in_specs.append(pl.BlockSpec((1, tn), lambda i, j, k: (0, j)))
        args.append(bias)

    def body(*refs):
        a_ref, b_ref = refs[0], refs[1]
        o_ref, acc = refs[-2], refs[-1]
        k = pl.program_id(2)

        @pl.when(k == 0)
        def _():
            acc[...] = jnp.zeros_like(acc)

        acc[...] += lax.dot_general(a_ref[...].astype(bf16), b_ref[...].astype(bf16), dims, preferred_element_type=f32)

        @pl.when(k == nk - 1)
        def _():
            r = acc[...]
            if bias is not None:
                r = r + refs[2][...]
            o_ref[...] = r.astype(o_ref.dtype)

    return pl.pallas_call(
        body, name=name, grid=(M // tm, N // tn, nk), in_specs=in_specs, out_specs=o_spec, out_shape=o_shape,
        scratch_shapes=[pltpu.VMEM((tm, tn), f32)], compiler_params=_cparams(("parallel", "parallel", "arbitrary")),
    )(*args)


def _rowcall(name, fn, nrows, tr, rows_in, vecs_in, rows_out, accs_out, ncol=1):
    nrt = nrows // tr
    in_specs, args = [], []
    for arr, w, ro, co, pc in rows_in:
        in_specs.append(pl.BlockSpec((tr, w), functools.partial(lambda j, i, ro, co, pc: (i + ro, co + (j if pc else 0)), ro=ro, co=co, pc=pc)))
        args.append(arr)
    for arr, w, co, pc in vecs_in:
        in_specs.append(pl.BlockSpec((arr.shape[0], w), functools.partial(lambda j, i, co, pc: (0, co + (j if pc else 0)), co=co, pc=pc)))
        args.append(arr)
    out_specs, out_shape = [], []
    for w, dt, pc in rows_out:
        out_specs.append(pl.BlockSpec((tr, w), functools.partial(lambda j, i, pc: (i, j if pc else 0), pc=pc)))
        out_shape.append(jax.ShapeDtypeStruct((nrows, w * (ncol if pc else 1)), dt))
    for k, w, pc in accs_out:
        out_specs.append(pl.BlockSpec((k, w), functools.partial(lambda j, i, pc: (0, j if pc else 0), pc=pc)))
        out_shape.append(jax.ShapeDtypeStruct((k, w * (ncol if pc else 1)), f32))
    n_in, n_ro = len(args), len(rows_out)

    def body(*refs):
        j, i = pl.program_id(0), pl.program_id(1)
        outs = fn(j, *[r[...] for r in refs[:n_in]])
        for r, val in zip(refs[n_in:n_in + n_ro], outs[:n_ro]):
            r[...] = val.astype(r.dtype)
        for (k, w, pc), r, val in zip(accs_out, refs[n_in + n_ro:], outs[n_ro:]):
            first = (i == 0) if pc else jnp.logical_and(i == 0, j == 0)

            @pl.when(first)
            def _(r=r):
                r[...] = jnp.zeros_like(r)

            r[...] += val

    res = pl.pallas_call(
        body, name=name, grid=(ncol, nrt), in_specs=in_specs, out_specs=out_specs, out_shape=out_shape,
        compiler_params=_cparams(("arbitrary", "arbitrary")),
    )(*args)
    return res


def _tap_valid(mode, t, d, nrows, nlat):
    if mode == "seg":
        tp = t + d
        return (tp >= 0) & (tp < nrows) & ((t < nlat) == (tp < nlat))
    if mode == "row":
        p = (t & (GW - 1)) + d
        return (p >= 0) & (p < GW)
    tp = t + d * GW
    return (tp >= 0) & (tp < nrows)


def _conv(x, w_ref, ktaps, mode, nlat, flip=False):
    nrows = x.shape[0]
    stride = GW if mode == "col" else 1
    t = lax.broadcasted_iota(jnp.int32, (nrows, 1), 0)
    acc = jnp.zeros_like(x)
    for j in range(ktaps):
        d = j - ktaps // 2
        jj = ktaps - 1 - j if flip else j
        wj = w_ref[jj:jj + 1, :]
        if d == 0:
            acc = acc + x * wj
        else:
            xs = pltpu.roll(x, (-d * stride) % nrows, 0)
            acc = acc + jnp.where(_tap_valid(mode, t, d, nrows, nlat), xs * wj, 0.0)
    return acc


def _conv_wgrad(dy, x, dw_ref, ktaps, mode, nlat):
    nrows = x.shape[0]
    stride = GW if mode == "col" else 1
    t = lax.broadcasted_iota(jnp.int32, (nrows, 1), 0)
    dw_ref[...] = jnp.zeros_like(dw_ref)
    for j in range(ktaps):
        d = j - ktaps // 2
        if d == 0:
            prod = x * dy
        else:
            xs = pltpu.roll(x, (-d * stride) % nrows, 0)
            prod = jnp.where(_tap_valid(mode, t, d, nrows, nlat), xs * dy, 0.0)
        dw_ref[j:j + 1, :] = _csum(prod)


def _qkv_post(j, pre, nh):
    a = _silu(pre)
    inv = lax.rsqrt(jnp.sum(a * a, axis=-1, keepdims=True) + 1e-6)
    scale = jnp.where(j < nh, HD ** -0.5, 1.0).astype(f32)
    nrm = jnp.where(j < 2 * nh, inv, 1.0) * scale
    return a, inv, nrm


def _qkv_conv_fwd(h_all, w7, nrows, nlat, nh, ktaps):
    ntile = 3 * nh

    def body(h_ref, w_ref, o_ref):
        j = pl.program_id(0)
        pre = _conv(h_ref[...], w_ref, ktaps, "seg", nlat)
        a, _, nrm = _qkv_post(j, pre, nh)
        o_ref[...] = a * nrm

    return pl.pallas_call(
        body, name="qkv_conv_fwd", grid=(ntile,),
        in_specs=[pl.BlockSpec((nrows, HD), lambda j: (0, j)), pl.BlockSpec((w7.shape[0], HD), lambda j: (0, j))],
        out_specs=pl.BlockSpec((nrows, HD), lambda j: (0, j)),
        out_shape=jax.ShapeDtypeStruct((nrows, ntile * HD), f32), compiler_params=_cparams(("parallel",)),
    )(h_all, w7)


def _qkv_conv_bwd(h_all, w7, dn0, dn1, nrows, nlat, nh, ktaps):
    ntile = 3 * nh

    def body(h_ref, w_ref, d0_ref, d1_ref, dh_ref, dw_ref):
        j = pl.program_id(0)
        hx = h_ref[...]
        pre = _conv(hx, w_ref, ktaps, "seg", nlat)
        a, inv, nrm = _qkv_post(j, pre, nh)
        dn = (d0_ref[...] + d1_ref[...]) * jnp.where(j < nh, HD ** -0.5, 1.0).astype(f32)
        n = a * inv
        da_norm = inv * (dn - n * jnp.sum(dn * n, axis=-1, keepdims=True))
        da = jnp.where(j < 2 * nh, da_norm, dn)
        dpre = da * _dsilu(pre)
        dh_ref[...] = _conv(dpre, w_ref, ktaps, "seg", nlat, flip=True).astype(dh_ref.dtype)
        _conv_wgrad(dpre, hx, dw_ref, ktaps, "seg", nlat)

    blk = pl.BlockSpec((nrows, HD), lambda j: (0, j))
    wblk = pl.BlockSpec((w7.shape[0], HD), lambda j: (0, j))
    return pl.pallas_call(
        body, name="qkv_conv_bwd", grid=(ntile,), in_specs=[blk, wblk, blk, blk], out_specs=[blk, wblk],
        out_shape=[jax.ShapeDtypeStruct((nrows, ntile * HD), bf16), jax.ShapeDtypeStruct((w7.shape[0], ntile * HD), f32)],
        compiler_params=_cparams(("parallel",)),
    )(h_all, w7, dn0, dn1)


def _conf_conv_fwd(h_all, w31, nlat, conf, val_blk, ktaps):
    nt = conf // LANES
    nhalf = nt // 2

    def body(v_ref, g_ref, w_ref, o_ref):
        j = pl.program_id(0)
        glu = v_ref[...] * _sigmoid(g_ref[...])

        @pl.when(j < nhalf)
        def _():
            o_ref[...] = _conv(glu, w_ref, ktaps, "row", nlat)

        @pl.when(j >= nhalf)
        def _():
            o_ref[...] = _conv(glu, w_ref, ktaps, "col", nlat)

    return pl.pallas_call(
        body, name="conf_conv_fwd", grid=(nt,),
        in_specs=[pl.BlockSpec((nlat, LANES), lambda j: (0, val_blk + j)), pl.BlockSpec((nlat, LANES), lambda j: (0, val_blk + nt + j)),
                  pl.BlockSpec((w31.shape[0], LANES), lambda j: (0, j))],
        out_specs=pl.BlockSpec((nlat, LANES), lambda j: (0, j)),
        out_shape=jax.ShapeDtypeStruct((nlat, conf), f32), compiler_params=_cparams(("parallel",)),
    )(h_all, h_all, w31)


def _conf_conv_bwd(h_all, w31, dyc, nlat, conf, val_blk, ktaps):
    nt = conf // LANES
    nhalf = nt // 2

    def body(v_ref, g_ref, w_ref, dy_ref, dv_ref, dg_ref, dw_ref):
        j = pl.program_id(0)
        val, sg = v_ref[...], _sigmoid(g_ref[...])
        glu = val * sg
        dy = dy_ref[...]

        def run(mode):
            dglu = _conv(dy, w_ref, ktaps, mode, nlat, flip=True)
            dv_ref[...] = (dglu * sg).astype(dv_ref.dtype)
            dg_ref[...] = (dglu * val * sg * (1.0 - sg)).astype(dg_ref.dtype)
            _conv_wgrad(dy, glu, dw_ref, ktaps, mode, nlat)

        @pl.when(j < nhalf)
        def _():
            run("row")

        @pl.when(j >= nhalf)
        def _():
            run("col")

    blk = pl.BlockSpec((nlat, LANES), lambda j: (0, j))
    wblk = pl.BlockSpec((w31.shape[0], LANES), lambda j: (0, j))
    return pl.pallas_call(
        body, name="conf_conv_bwd", grid=(nt,),
        in_specs=[pl.BlockSpec((nlat, LANES), lambda j: (0, val_blk + j)), pl.BlockSpec((nlat, LANES), lambda j: (0, val_blk + nt + j)), wblk, blk],
        out_specs=[blk, blk, wblk],
        out_shape=[jax.ShapeDtypeStruct((nlat, conf), bf16), jax.ShapeDtypeStruct((nlat, conf), bf16),
                   jax.ShapeDtypeStruct((w31.shape[0], conf), f32)],
        compiler_params=_cparams(("parallel",)),
    )(h_all, h_all, w31, dyc)


def _bd(eq, a, b):
    return jnp.einsum(eq, a.astype(bf16), b.astype(bf16), preferred_element_type=f32)


def _hd(eq, a, b):
    return jnp.einsum(eq, a, b, precision=HI, preferred_element_type=f32)


def _gdn_chunk(q, k, v, gt, nh):
    g = jnp.stack([jnp.broadcast_to(gt[:, h:h + 1], (CH, LANES)) for h in range(nh)])
    beta = jnp.stack([jnp.broadcast_to(gt[:, nh + h:nh + h + 1], (CH, LANES)) for h in range(nh)])
    ii = lax.broadcasted_iota(jnp.int32, (CH, CH), 0)
    jj = lax.broadcasted_iota(jnp.int32, (CH, CH), 1)
    tril, stril = (ii >= jj)[None], (ii > jj)[None]
    trilf = jnp.broadcast_to(tril.astype(f32), (nh, CH, CH))
    gc = _hd("hit,htl->hil", trilf, g)
    gc_row = _hd("hil,hjl->hij", jnp.full((nh, CH, LANES), 1.0 / LANES, f32), gc)
    diff = gc[:, :, :CH] - gc_row
    gam = jnp.where(tril, jnp.exp(jnp.where(tril, diff, 0.0)), 0.0)
    e = jnp.exp(gc)
    gl = gc[:, CH - 1:CH, :]
    el = jnp.exp(gl)
    r = jnp.exp(gl - gc)
    kb, vb = k * beta, v * beta
    kbe = kb * e
    amat = jnp.where(stril, _bd("hik,hjk->hij", kb, k) * gam, 0.0)
    eye = (ii == jj).astype(f32)[None]
    xp = -amat
    tinv = eye + xp
    for _ in range(5):
        xp = _hd("hij,hjk->hik", xp, xp)
        tinv = tinv + _hd("hij,hjk->hik", tinv, xp)
    u = _hd("hij,hjv->hiv", tinv, vb)
    w = _hd("hij,hjk->hik", tinv, kbe)
    pmat = jnp.where(tril, _bd("hik,hjk->hij", q, k) * gam, 0.0)
    return dict(beta=beta, gam=gam, e=e, el=el, r=r, kb=kb, vb=vb, kbe=kbe, amat=amat, tinv=tinv, u=u, w=w, pmat=pmat,
                qd=q * e, kd=k * r, tril=tril, stril=stril)


def _scan_row_chunk(n, ns, nt):
    return jnp.where(n < nt, ns + n, n - nt)


def _gdn_fwd(qkv5, gd, nh, nlat, nctx):
    nrows = nlat + nctx
    ns, nt = nlat // CH, nctx // CH
    nch = ns + nt

    def body(q_ref, k_ref, v_ref, g_ref, o_ref, st_ref, state):
        n = pl.program_id(1)

        @pl.when(n == 0)
        def _():
            state[...] = jnp.zeros_like(state)

        q, k = q_ref[...], k_ref[...]
        c = _gdn_chunk(q, k, v_ref[...], g_ref[...], nh)
        s = state[...]
        vn = c["u"] - _bd("hck,hkv->hcv", c["w"], s)
        o_ref[...] = _bd("hck,hkv->hcv", c["qd"], s) + _bd("hcd,hdv->hcv", c["pmat"], vn)
        st_ref[...] = s
        state[...] = s * c["el"] + _bd("hck,hcv->hkv", c["kd"], vn)

    def qspec(which):
        return pl.BlockSpec((None, None, nh, CH, HD), lambda d, n: (d, which, 0, _scan_row_chunk(n, ns, nt), 0))

    return pl.pallas_call(
        body, name="gdn_fwd", grid=(2, nch),
        in_specs=[qspec(0), qspec(1), qspec(2), pl.BlockSpec((None, CH, LANES), lambda d, n: (d, _scan_row_chunk(n, ns, nt), 0))],
        out_specs=[pl.BlockSpec((None, nh, CH, HD), lambda d, n: (d, 0, _scan_row_chunk(n, ns, nt), 0)),
                   pl.BlockSpec((None, None, nh, HD, HD), lambda d, n: (d, n, 0, 0, 0))],
        out_shape=[jax.ShapeDtypeStruct((2, nh, nrows, HD), f32), jax.ShapeDtypeStruct((2, nch, nh, HD, HD), f32)],
        scratch_shapes=[pltpu.VMEM((nh, HD, HD), f32)], compiler_params=_cparams(("arbitrary", "arbitrary")),
    )(qkv5, qkv5, qkv5, gd)


def _gdn_bwd(qkv5, gd, states, do, nh, nlat, nctx):
    nrows = nlat + nctx
    ns, nt = nlat // CH, nctx // CH
    nch = ns + nt

    def body(q_ref, k_ref, v_ref, g_ref, st_ref, do_ref, dq_ref, dk_ref, dv_ref, dg_ref, db_ref, dstate):
        @pl.when(pl.program_id(1) == 0)
        def _():
            dstate[...] = jnp.zeros_like(dstate)

        q, k, v = q_ref[...], k_ref[...], v_ref[...]
        c = _gdn_chunk(q, k, v, g_ref[...], nh)
        s, dsp, dout = st_ref[...], dstate[...], do_ref[...]
        beta, gam, e, el, r = c["beta"], c["gam"], c["e"], c["el"], c["r"]
        tinv, u, w, pmat, amat = c["tinv"], c["u"], c["w"], c["pmat"], c["amat"]
        vn = u - _bd("hck,hkv->hcv", w, s)
        dvn = _bd("hdc,hdv->hcv", pmat, dout) + _bd("hck,hkv->hcv", c["kd"], dsp)
        dp = jnp.where(c["tril"], _bd("hcv,hdv->hcd", dout, vn), 0.0)
        dqd = _bd("hcv,hkv->hck", dout, s)
        dkd = _bd("hcv,hkv->hck", vn, dsp)
        dstate[...] = _bd("hck,hcv->hkv", c["qd"], dout) + dsp * el - _bd("hck,hcv->hkv", w, dvn)
        del_ = jnp.sum(jnp.sum(s * dsp, axis=2, keepdims=True), axis=1, keepdims=True)
        dw = -_bd("hcv,hkv->hck", dvn, s)
        dvb = _hd("hji,hjv->hiv", tinv, dvn)
        dkbe = _hd("hji,hjk->hik", tinv, dw)
        z = _bd("hiv,hjv->hij", dvn, u) + _bd("hik,hjk->hij", dw, w)
        da = jnp.where(c["stril"], -_hd("hji,hjl->hil", tinv, z), 0.0)
        dm = da * gam
        dkb = _bd("hij,hjk->hik", dm, k) + dkbe * e
        dn = dp * gam
        dq_ref[...] = _bd("hij,hjk->hik", dn, k) + dqd * e
        dk_ref[...] = _bd("hji,hjk->hik", dm, c["kb"]) + _bd("hji,hjk->hik", dn, q) + dkd * r + dkb * beta
        dv_ref[...] = dvb * beta
        gmat = da * amat + dp * pmat
        ones = jnp.ones((nh, CH, LANES), f32)
        rows_minus_cols = _hd("hij,hjl->hil", gmat, ones) - _hd("hji,hjl->hil", gmat, ones)
        de = jnp.sum(dqd * q + dkbe * c["kb"], axis=-1, keepdims=True)
        drr = jnp.sum(dkd * k, axis=-1, keepdims=True) * r
        dgc = rows_minus_cols + de * e - drr
        dgl = jnp.sum(drr, axis=1, keepdims=True) + del_ * el
        last = lax.broadcasted_iota(jnp.int32, (1, CH, 1), 1) == CH - 1
        dgc = dgc + jnp.where(last, dgl, 0.0)
        ii = lax.broadcasted_iota(jnp.int32, (CH, CH), 0)
        jj = lax.broadcasted_iota(jnp.int32, (CH, CH), 1)
        triu = jnp.broadcast_to((jj >= ii).astype(f32)[None], (nh, CH, CH))
        dg_ref[...] = _hd("hij,hjl->hil", triu, dgc)
        db_ref[...] = jnp.broadcast_to(jnp.sum(dvb * v + dkb * k, axis=-1, keepdims=True), (nh, CH, LANES))

    def rc(n):
        return _scan_row_chunk(nch - 1 - n, ns, nt)

    def qspec(which):
        return pl.BlockSpec((None, None, nh, CH, HD), lambda d, n: (d, which, 0, rc(n), 0))

    hblk = pl.BlockSpec((None, nh, CH, HD), lambda d, n: (d, 0, rc(n), 0))
    gblk = pl.BlockSpec((None, nh, CH, LANES), lambda d, n: (d, 0, rc(n), 0))
    hshape = jax.ShapeDtypeStruct((2, nh, nrows, HD), f32)
    gshape = jax.ShapeDtypeStruct((2, nh, nrows, LANES), f32)
    return pl.pallas_call(
        body, name="gdn_bwd", grid=(2, nch),
        in_specs=[qspec(0), qspec(1), qspec(2), pl.BlockSpec((None, CH, LANES), lambda d, n: (d, rc(n), 0)),
                  pl.BlockSpec((None, None, nh, HD, HD), lambda d, n: (d, nch - 1 - n, 0, 0, 0)), hblk],
        out_specs=[hblk, hblk, hblk, gblk, gblk], out_shape=[hshape, hshape, hshape, gshape, gshape],
        scratch_shapes=[pltpu.VMEM((nh, HD, HD), f32)], compiler_params=_cparams(("arbitrary", "arbitrary")),
    )(qkv5, qkv5, qkv5, gd, states, do)


def _mesh_pos():
    return lax.axis_index("x"), lax.axis_index("y"), lax.axis_index("c")


def _lin(p):
    return 4 * p[0] + 2 * p[1] + p[2]


def _all_gather(name, xs):
    nx = len(xs)

    def body(*refs):
        xr, outr = refs[:nx], refs[nx:2 * nx]
        send, recv, loc = refs[2 * nx:]
        x, y, c = _mesh_pos()
        me, sib = (x, y, c), (x, y, 1 - c)
        chips = [(1 - x, y), (x, 1 - y), (1 - x, 1 - y)]

        def cp(l, k, block, to, src=None):
            rows = outr[l].at[_lin(block)]
            return pltpu.make_async_remote_copy(src_ref=rows if src is None else src, dst_ref=rows, send_sem=send.at[l, k],
                                                recv_sem=recv.at[l, k], device_id=to, device_id_type=MESH)

        mine = [pltpu.make_async_copy(xr[l], outr[l].at[_lin(me)], loc.at[l]) for l in range(nx)]
        for m in mine:
            m.start()
        first = []
        for l in range(nx):
            first.append(cp(l, 0, me, sib, src=xr[l]))
            first += [cp(l, 1 + j, me, (*chip, c), src=xr[l]) for j, chip in enumerate(chips)]
        for f in first:
            f.start()
        passed = []
        for l in range(nx):
            for j, chip in enumerate(chips):
                cp(l, 1 + j, (*chip, c), me).wait_recv()
                fwd = cp(l, 4 + j, (*chip, c), sib)
                fwd.start()
                passed.append(fwd)
        for l in range(nx):
            cp(l, 0, sib, me).wait_recv()
            for j, chip in enumerate(chips):
                cp(l, 4 + j, (*chip, 1 - c), me).wait_recv()
        for f in first + passed:
            f.wait_send()
        for m in mine:
            m.wait()

    anyspec = pl.BlockSpec(memory_space=pl.ANY)
    return pl.pallas_call(
        body, name=name, in_specs=[anyspec] * nx, out_specs=[anyspec] * nx,
        out_shape=[jax.ShapeDtypeStruct((NDEV,) + a.shape, a.dtype) for a in xs],
        scratch_shapes=[pltpu.SemaphoreType.DMA((nx, 7)), pltpu.SemaphoreType.DMA((nx, 7)), pltpu.SemaphoreType.DMA((nx,))],
    )(*xs)


def _all_to_all(name, xs):
    nx = len(xs)

    def body(*refs):
        xr, outr = refs[:nx], refs[nx:2 * nx]
        send, recv, loc = refs[2 * nx:]
        x, y, c = _mesh_pos()
        me = (x, y, c)

        def peer(rel):
            return (1 - x if rel & 4 else x, 1 - y if rel & 2 else y, 1 - c if rel & 1 else c)

        def cp(l, rel):
            p = peer(rel)
            return pltpu.make_async_remote_copy(src_ref=xr[l].at[_lin(p)], dst_ref=outr[l].at[_lin(me)], send_sem=send.at[l, rel - 1],
                                                recv_sem=recv.at[l, rel - 1], device_id=p, device_id_type=MESH)

        def landed(l, rel):
            p = peer(rel)
            return pltpu.make_async_remote_copy(src_ref=xr[l].at[_lin(p)], dst_ref=outr[l].at[_lin(p)], send_sem=send.at[l, rel - 1],
                                                recv_sem=recv.at[l, rel - 1], device_id=p, device_id_type=MESH)

        mine = [pltpu.make_async_copy(xr[l].at[_lin(me)], outr[l].at[_lin(me)], loc.at[l]) for l in range(nx)]
        for m in mine:
            m.start()
        sends = [cp(l, rel) for l in range(nx) for rel in range(1, NDEV)]
        for s in sends:
            s.start()
        for l in range(nx):
            for rel in range(1, NDEV):
                landed(l, rel).wait_recv()
        for s in sends:
            s.wait_send()
        for m in mine:
            m.wait()

    anyspec = pl.BlockSpec(memory_space=pl.ANY)
    return pl.pallas_call(
        body, name=name, in_specs=[anyspec] * nx, out_specs=[anyspec] * nx,
        out_shape=[jax.ShapeDtypeStruct(a.shape, a.dtype) for a in xs],
        scratch_shapes=[pltpu.SemaphoreType.DMA((nx, 7)), pltpu.SemaphoreType.DMA((nx, 7)), pltpu.SemaphoreType.DMA((nx,))],
    )(*xs)


def _sum8(name, g):
    def body(g_ref, o_ref):
        acc = g_ref[0:1, :]
        for p in range(1, NDEV):
            acc = acc + g_ref[p:p + 1, :]
        o_ref[...] = acc

    return pl.pallas_call(body, name=name, out_shape=jax.ShapeDtypeStruct((1, g.shape[1]), f32),
                          compiler_params=pltpu.CompilerParams(vmem_limit_bytes=VMEM_LIMIT))(g)


def _pack(arrs):
    flat = jnp.concatenate([a.reshape(-1).astype(f32) for a in arrs])
    pad = (-flat.shape[0]) % (8 * LANES)
    return jnp.pad(flat, (0, pad)).reshape(1, -1)


def _unpack(vec, shapes):
    out, off = [], 0
    flat = vec.reshape(-1)
    for s in shapes:
        n = 1
        for d in s:
            n *= d
        out.append(flat[off:off + n].reshape(s))
        off += n
    return out


def _adam(name, w, m, v, g, parts):
    nrows, width = w.shape
    tr = _row_tile(nrows, width * (3 if parts else 1), cap=512) if nrows >= 16 else nrows
    blk = pl.BlockSpec((tr, width), lambda i: (i, 0))
    gblk = pl.BlockSpec((NDEV, tr, width), lambda i: (0, i, 0)) if parts else blk
    c1 = 1.0 / (1.0 - ADAM_B1 ** ADAM_STEP)
    c2 = 1.0 / (1.0 - ADAM_B2 ** ADAM_STEP)

    def body(w_ref, m_ref, v_ref, g_ref, go_ref, d_ref, mo_ref, vo_ref):
        if parts:
            gg = g_ref[0].astype(f32)
            for p in range(1, NDEV):
                gg = gg + g_ref[p].astype(f32)
        else:
            gg = g_ref[...]
        mn = ADAM_B1 * m_ref[...] + (1.0 - ADAM_B1) * gg
        vn = ADAM_B2 * v_ref[...] + (1.0 - ADAM_B2) * (gg * gg)
        go_ref[...] = gg
        mo_ref[...] = mn
        vo_ref[...] = vn
        d_ref[...] = -ADAM_LR * ((mn * c1) / (jnp.sqrt(vn * c2) + ADAM_EPS) + ADAM_WD * w_ref[...])

    sh = jax.ShapeDtypeStruct((nrows, width), f32)
    return pl.pallas_call(body, name=name, grid=(nrows // tr,), in_specs=[blk, blk, blk, gblk], out_specs=[blk] * 4,
                          out_shape=[sh] * 4, compiler_params=_cparams(("parallel",)))(w, m, v, g)


def kernel(x, c, ctx, c_ctx, ln_in_g, ln_in_b, w_mod, b_mod, w_in, w_qkv_conv, a_log_f, dt_bias_f, a_log_b, dt_bias_b, dn_norm_g, conf_dw_w, conf_dw_b, conf_ln_g, conf_ln_b, w_out, ln1_g, ln1_b, w_mlp1, b_mlp1, w_mlp2, b_mlp2, ln2_g, ln2_b, loss_target, m_c_ctx, m_ln_in_g, m_ln_in_b, m_w_mod, m_b_mod, m_w_in, m_w_qkv_conv, m_a_log_f, m_dt_bias_f, m_a_log_b, m_dt_bias_b, m_dn_norm_g, m_conf_dw_w, m_conf_dw_b, m_conf_ln_g, m_conf_ln_b, m_w_out, m_ln1_g, m_ln1_b, m_w_mlp1, m_b_mlp1, m_w_mlp2, m_b_mlp2, m_ln2_g, m_ln2_b, v_c_ctx, v_ln_in_g, v_ln_in_b, v_w_mod, v_b_mod, v_w_in, v_w_qkv_conv, v_a_log_f, v_dt_bias_f, v_a_log_b, v_dt_bias_b, v_dn_norm_g, v_conf_dw_w, v_conf_dw_b, v_conf_ln_g, v_conf_ln_b, v_w_out, v_ln1_g, v_ln1_b, v_w_mlp1, v_b_mlp1, v_w_mlp2, v_b_mlp2, v_ln2_g, v_ln2_b):
    weights = dict(c_ctx=c_ctx, ln_in_g=ln_in_g, ln_in_b=ln_in_b, w_mod=w_mod, b_mod=b_mod, w_in=w_in, w_qkv_conv=w_qkv_conv, a_log_f=a_log_f, dt_bias_f=dt_bias_f, a_log_b=a_log_b, dt_bias_b=dt_bias_b, dn_norm_g=dn_norm_g, conf_dw_w=conf_dw_w, conf_dw_b=conf_dw_b, conf_ln_g=conf_ln_g, conf_ln_b=conf_ln_b, w_out=w_out, ln1_g=ln1_g, ln1_b=ln1_b, w_mlp1=w_mlp1, b_mlp1=b_mlp1, w_mlp2=w_mlp2, b_mlp2=b_mlp2, ln2_g=ln2_g, ln2_b=ln2_b)
    mom1 = dict(c_ctx=m_c_ctx, ln_in_g=m_ln_in_g, ln_in_b=m_ln_in_b, w_mod=m_w_mod, b_mod=m_b_mod, w_in=m_w_in, w_qkv_conv=m_w_qkv_conv, a_log_f=m_a_log_f, dt_bias_f=m_dt_bias_f, a_log_b=m_a_log_b, dt_bias_b=m_dt_bias_b, dn_norm_g=m_dn_norm_g, conf_dw_w=m_conf_dw_w, conf_dw_b=m_conf_dw_b, conf_ln_g=m_conf_ln_g, conf_ln_b=m_conf_ln_b, w_out=m_w_out, ln1_g=m_ln1_g, ln1_b=m_ln1_b, w_mlp1=m_w_mlp1, b_mlp1=m_b_mlp1, w_mlp2=m_w_mlp2, b_mlp2=m_b_mlp2, ln2_g=m_ln2_g, ln2_b=m_ln2_b)
    mom2 = dict(c_ctx=v_c_ctx, ln_in_g=v_ln_in_g, ln_in_b=v_ln_in_b, w_mod=v_w_mod, b_mod=v_b_mod, w_in=v_w_in, w_qkv_conv=v_w_qkv_conv, a_log_f=v_a_log_f, dt_bias_f=v_dt_bias_f, a_log_b=v_a_log_b, dt_bias_b=v_dt_bias_b, dn_norm_g=v_dn_norm_g, conf_dw_w=v_conf_dw_w, conf_dw_b=v_conf_dw_b, conf_ln_g=v_conf_ln_g, conf_ln_b=v_conf_ln_b, w_out=v_w_out, ln1_g=v_ln1_g, ln1_b=v_ln1_b, w_mlp1=v_w_mlp1, b_mlp1=v_b_mlp1, w_mlp2=v_w_mlp2, b_mlp2=v_b_mlp2, ln2_g=v_ln2_g, ln2_b=v_ln2_b)
    names = list(weights)

    me = _lin(_mesh_pos())
    S, D = x.shape[1], x.shape[2]
    T = ctx.shape[1]
    R = S + T
    DN = D // 2
    NH = DN // HD
    CONF = D - DN
    K7, K31 = w_qkv_conv.shape[1], conf_dw_w.shape[1]
    DFF = w_mlp1.shape[2] * NDEV
    INC = w_in.shape[2] * NDEV
    CONF_OFF = 4 * DN + 4 * NH
    NC = 4 * DN + 2 * CONF + GPAD
    QB, ZB, VB, GB = 0, 3 * DN // LANES, 4 * DN // LANES, (4 * DN + 2 * CONF) // LANES
    MODC = w_mod.shape[2]
    x2d, ctx2d, tgt = x[0], ctx[0], loss_target[0]
    row = lambda a: a.reshape(1, -1).astype(f32)

    def flipseg(a, axis):
        lat, cx = lax.slice_in_dim(a, 0, S, axis=axis), lax.slice_in_dim(a, S, R, axis=axis)
        return jnp.concatenate([jnp.flip(lat, axis), jnp.flip(cx, axis)], axis=axis)

    c_all, w7_all, w31_all = _all_gather("ag_small", [c.astype(f32), w_qkv_conv[0], conf_dw_w[0]])
    w7 = jnp.pad(jnp.transpose(w7_all, (1, 0, 2)).reshape(K7, 3 * DN), ((0, (-K7) % 8), (0, 0)))
    w31 = jnp.pad(jnp.transpose(w31_all, (1, 0, 2)).reshape(K31, CONF), ((0, (-K31) % 8), (0, 0)))
    g_win, g_wout, g_w1, g_w2 = _all_gather("ag_weights", [w_in[0].astype(bf16), w_out[0].astype(bf16), w_mlp1[0].astype(bf16), w_mlp2[0].astype(bf16)])
    win_full = jnp.transpose(g_win, (1, 0, 2)).reshape(D, INC)
    w_cat = jnp.concatenate([win_full[:, :4 * DN], win_full[:, CONF_OFF:], win_full[:, 4 * DN:CONF_OFF],
                             jnp.zeros((D, GPAD - 4 * NH), bf16)], axis=1)
    wout_full = g_wout.reshape(DN + CONF, D)
    w2_full = g_w2.reshape(DFF, D)

    c16 = jnp.concatenate([c_all.reshape(NDEV, D), c_ctx.reshape(1, D).astype(f32), jnp.zeros((7, D), f32)], axis=0)
    (sc16,) = _rowcall("silu_c", lambda j, a: (_silu(a),), 16, 16, [(c16, D, 0, 0, False)], [], [(D, f32, False)], [])
    bmod_mine = lax.dynamic_slice_in_dim(b_mod.astype(f32), me * MODC, MODC, axis=1)
    mod_part = _mm("mod_fwd", sc16, w_mod[0], "nn", f32, bias=bmod_mine)
    (mod_g,) = _all_gather("ag_mod", [mod_part])
    mod_all = jnp.transpose(mod_g, (1, 0, 2)).reshape(16, 6 * D)
    mod_me = lax.dynamic_slice_in_dim(mod_all, me, 1, axis=0)
    sh_a, sc_a, g_a, sh_m, sc_m, g_m = [mod_me[:, i * D:(i + 1) * D] for i in range(6)]
    csh_a, csc_a = mod_all[8:9, 0:D], mod_all[8:9, D:2 * D]
    g0, b0 = row(ln_in_g), row(ln_in_b)

    def ln_mod(j, xt, g, b, sh, sc):
        xh, _ = _ln_stats(xt)
        xn = xh * g + b
        return xn, xn * (1.0 + sc) + sh

    tr = _row_tile(T, D)
    vec = lambda a: (a, a.shape[1], 0, False)
    xn, xm = _rowcall("ln_in_lat", ln_mod, S, tr, [(x2d, D, 0, 0, False)], [vec(g0), vec(b0), vec(sh_a), vec(sc_a)], [(D, f32, False), (D, bf16, False)], [])
    xcn, xcm = _rowcall("ln_in_ctx", ln_mod, T, tr, [(ctx2d, D, 0, 0, False)], [vec(g0), vec(b0), vec(csh_a), vec(csc_a)], [(D, f32, False), (D, bf16, False)], [])
    xm_all = jnp.concatenate([xm, xcm], axis=0)

    h_all = _mm("in_proj", xm_all, w_cat, "nn", f32, tm=1088, tn=1280)

    qkv_n = _qkv_conv_fwd(h_all, w7, R, S, NH, K7)
    lane = jnp.arange(LANES)
    is_a = ((lane < 4 * NH) & ((lane // NH) % 2 == 0)).astype(f32).reshape(1, LANES)
    pad_g = lambda a, b: jnp.concatenate([a.reshape(-1), jnp.zeros((NH,), f32), b.reshape(-1), jnp.zeros((LANES - 3 * NH,), f32)]).reshape(1, LANES)
    neg_a = pad_g(-jnp.exp(a_log_f.astype(f32)), -jnp.exp(a_log_b.astype(f32)))
    dt_v = pad_g(dt_bias_f.astype(f32), dt_bias_b.astype(f32))

    def gates_f(j, hg, isa, na, dt):
        return (jnp.where(isa > 0.5, na * _softplus(hg + dt), _sigmoid(hg)),)

    trg = _row_tile(T, LANES)
    (gates,) = _rowcall("gates_fwd", gates_f, R, trg, [(h_all, LANES, 0, GB, False)], [vec(is_a), vec(neg_a), vec(dt_v)], [(LANES, f32, False)], [])
    gpad = jnp.zeros((R, LANES - 2 * NH), f32)
    gd = jnp.stack([jnp.concatenate([gates[:, :2 * NH], gpad], axis=1),
                    jnp.concatenate([flipseg(gates[:, 2 * NH:4 * NH], 0), gpad], axis=1)])
    q3 = jnp.transpose(qkv_n.reshape(R, 3, NH, HD), (1, 2, 0, 3))
    qkv5 = jnp.stack([q3, flipseg(q3, 2)])

    o5, states = _gdn_fwd(qkv5, gd, NH, S, T)
    to_rows = lambda a: jnp.transpose(a, (1, 0, 2)).reshape(a.shape[1], NH * a.shape[2])
    o_f = to_rows(o5[0][:, :S])
    o_b = to_rows(jnp.flip(o5[1][:, :S], axis=1))
    dng = row(dn_norm_g)

    def gate_norm(j, of, ob, z, g):
        o = of + ob
        rs = lax.rsqrt(jnp.mean(o * o, axis=-1, keepdims=True) + 1e-6)
        return ((o * rs * g) * _silu(z),)

    trh = _row_tile(S, HD)
    (dn_out,) = _rowcall("gate_norm_fwd", gate_norm, S, trh, [(o_f, HD, 0, 0, True), (o_b, HD, 0, 0, True), (h_all, HD, 0, ZB, True)],
                         [vec(dng)], [(HD, bf16, True)], [], ncol=NH)

    yconv = _conf_conv_fwd(h_all, w31, S, CONF, VB, K31)
    bdw, clg, clb = row(conf_dw_b), row(conf_ln_g), row(conf_ln_b)

    def conf_ln(j, yc, b, g, bb):
        xh, _ = _ln_stats(yc + b)
        return (_silu(xh * g + bb),)

    trc = _row_tile(S, CONF)
    (conf_out,) = _rowcall("conf_ln_fwd", conf_ln, S, trc, [(yconv, CONF, 0, 0, False)], [vec(bdw), vec(clg), vec(clb)], [(CONF, bf16, False)], [])
    mix = jnp.concatenate([dn_out, conf_out], axis=1)
    y = _mm("out_proj", mix, wout_full, "nn", f32)

    l1g, l1b, l2g, l2b = row(ln1_g), row(ln1_b), row(ln2_g), row(ln2_b)

    def ln1_mod(j, xnt, yt, ga, g, b, sh, sc):
        xh, _ = _ln_stats(ALPHA * xnt + ga * yt)
        x1 = xh * g + b
        return x1, x1 * (1.0 + sc) + sh

    trd = _row_tile(S, D)
    x1, u = _rowcall("ln1_fwd", ln1_mod, S, trd, [(xn, D, 0, 0, False), (y, D, 0, 0, False)],
                     [vec(g_a), vec(l1g), vec(l1b), vec(sh_m), vec(sc_m)], [(D, f32, False), (D, bf16, False)], [])
    hh = _mm("mlp1", u, g_w1, "nn", f32)
    b1, b2 = row(b_mlp1), row(b_mlp2)
    trf = _row_tile(S, DFF)
    (act,) = _rowcall("sqrelu_fwd", lambda j, h, b: (jnp.square(jnp.maximum(h + b, 0.0)),), S, trf, [(hh, DFF, 0, 0, False)], [vec(b1)], [(DFF, bf16, False)], [])
    y2 = _mm("mlp2", act, w2_full, "nn", f32)

    def ln2_loss(j, x1t, y2t, tg, bb2, gm, g, b):
        y2b = y2t + bb2
        xh, rstd = _ln_stats(ALPHA * x1t + gm * y2b)
        err = xh * g + b - tg
        dx2 = err * (1.0 / D)
        dr2 = _ln_bwd(dx2, xh, rstd, g)
        dy2 = dr2 * gm
        lsum = jnp.broadcast_to(jnp.sum(err * err).reshape(1, 1), (1, LANES))
        return dr2, dy2, _csum(dx2 * xh), _csum(dx2), _csum(dr2 * y2b), _csum(dy2), lsum

    dr2, dy2, d_l2g, d_l2b, d_gm, d_b2, lsum = _rowcall(
        "ln2_loss", ln2_loss, S, trd, [(x1, D, 0, 0, False), (y2, D, 0, 0, False), (tgt, D, 0, 0, False)],
        [vec(b2), vec(g_m), vec(l2g), vec(l2b)], [(D, f32, False), (D, bf16, False)], [(1, D, False)] * 4 + [(1, LANES, False)])
    loss = lax.psum(0.5 * lsum[0, 0] / D, ("x", "y", "c"))

    dact = _mm("mlp2_dx", dy2, w2_full, "nt", f32)
    dw2_p = _mm("mlp2_dw", act, dy2, "tn", bf16)

    def sqrelu_b(j, da, h, b):
        dh = da * 2.0 * jnp.maximum(h + b, 0.0)
        return dh, _csum(dh)

    dhh, d_b1 = _rowcall("sqrelu_bwd", sqrelu_b, S, trf, [(dact, DFF, 0, 0, False), (hh, DFF, 0, 0, False)], [vec(b1)], [(DFF, bf16, False)], [(1, DFF, False)])
    du = _mm("mlp1_dx", dhh, g_w1, "nt", f32)
    dw1_p = _mm("mlp1_dw", u, dhh, "tn", bf16, out_split=NDEV)

    def ln1_bwd(j, dr2t, dut, xnt, yt, ga, g, b, sc):
        xh, rstd = _ln_stats(ALPHA * xnt + ga * yt)
        x1t = xh * g + b
        dx1 = ALPHA * dr2t + dut * (1.0 + sc)
        dr1 = _ln_bwd(dx1, xh, rstd, g)
        return dr1, dr1 * ga, _csum(dut * x1t), _csum(dut), _csum(dx1 * xh), _csum(dx1), _csum(dr1 * yt)

    dr1, dy, d_scm, d_shm, d_l1g, d_l1b, d_ga = _rowcall(
        "ln1_bwd", ln1_bwd, S, trd, [(dr2, D, 0, 0, False), (du, D, 0, 0, False), (xn, D, 0, 0, False), (y, D, 0, 0, False)],
        [vec(g_a), vec(l1g), vec(l1b), vec(sc_m)], [(D, f32, False), (D, bf16, False)], [(1, D, False)] * 5)
    dmix = _mm("out_proj_dx", dy, wout_full, "nt", f32)
    dwout_p = _mm("out_proj_dw", mix, dy, "tn", bf16)

    def conf_ln_b(j, dm, yc, b, g, bb):
        xh, rstd = _ln_stats(yc + b)
        dln = dm * _dsilu(xh * g + bb)
        dyc = _ln_bwd(dln, xh, rstd, g)
        return dyc, _csum(dln * xh), _csum(dln), _csum(dyc)

    dyc, d_clg, d_clb, d_bdw = _rowcall("conf_ln_bwd", conf_ln_b, S, trc, [(dmix, CONF, 0, DN // CONF, False), (yconv, CONF, 0, 0, False)],
                                        [vec(bdw), vec(clg), vec(clb)], [(CONF, f32, False)], [(1, CONF, False)] * 3)
    dval, dgate, dw31 = _conf_conv_bwd(h_all, w31, dyc, S, CONF, VB, K31)

    def gate_norm_b(j, dm, of, ob, z, g):
        o = of + ob
        rs = lax.rsqrt(jnp.mean(o * o, axis=-1, keepdims=True) + 1e-6)
        on = o * rs * g
        don = dm * _silu(z)
        dz = dm * on * _dsilu(z)
        dog = don * g
        do = rs * (dog - o * rs * rs * jnp.mean(dog * o, axis=-1, keepdims=True))
        return do, dz, _csum(don * o * rs)

    do, dz, d_dng = _rowcall("gate_norm_bwd", gate_norm_b, S, trh,
                             [(dmix, HD, 0, 0, True), (o_f, HD, 0, 0, True), (o_b, HD, 0, 0, True), (h_all, HD, 0, ZB, True)],
                             [vec(dng)], [(HD, f32, True), (HD, bf16, True)], [(1, HD, False)], ncol=NH)

    do3 = jnp.transpose(do.reshape(S, NH, HD), (1, 0, 2))
    zc = jnp.zeros((NH, T, HD), f32)
    do5 = jnp.stack([jnp.concatenate([do3, zc], axis=1), jnp.concatenate([jnp.flip(do3, axis=1), zc], axis=1)])
    dq5, dk5, dv5, dg5, db5 = _gdn_bwd(qkv5, gd, states, do5, NH, S, T)
    dn_dir = lambda d: jnp.concatenate([to_rows(a[d] if d == 0 else flipseg(a[d], 1)) for a in (dq5, dk5, dv5)], axis=1)
    dh_qkv, dw7 = _qkv_conv_bwd(h_all, w7, dn_dir(0), dn_dir(1), R, S, NH, K7)
    dgate_cols = jnp.concatenate([dg5[0, :, :, 0].T, db5[0, :, :, 0].T, flipseg(dg5[1, :, :, 0].T, 0), flipseg(db5[1, :, :, 0].T, 0),
                                  jnp.zeros((R, LANES - 4 * NH), f32)], axis=1)

    def gates_b(j, hg, dgt, isa, na, dt):
        sg = _sigmoid(hg)
        sp = _softplus(hg + dt)
        dpre = jnp.where(isa > 0.5, dgt * na * _sigmoid(hg + dt), dgt * sg * (1.0 - sg))
        return dpre, _csum(jnp.where(isa > 0.5, dgt * na * sp, 0.0)), _csum(jnp.where(isa > 0.5, dpre, 0.0))

    dh_g, d_alog, d_dt = _rowcall("gates_bwd", gates_b, R, trg, [(h_all, LANES, 0, GB, False), (dgate_cols, LANES, 0, 0, False)],
                                  [vec(is_a), vec(neg_a), vec(dt_v)], [(LANES, bf16, False)], [(1, LANES, False)] * 2)

    zpad = lambda a: jnp.concatenate([a, jnp.zeros((T, a.shape[1]), bf16)], axis=0)
    dh_all = jnp.concatenate([dh_qkv, zpad(dz), zpad(dval), zpad(dgate), dh_g, jnp.zeros((R, GPAD - LANES), bf16)], axis=1)
    dxm_all = _mm("in_proj_dx", dh_all, w_cat, "nt", f32, tm=1088, tk=640)
    dwcat_p = _mm("in_proj_dw", xm_all, dh_all, "tn", bf16, tn=1280, tk=256)

    def ln_in_bwd_lat(j, xt, dr1t, dxm, g, b, sc):
        xh, rstd = _ln_stats(xt)
        xnt = xh * g + b
        dxn = ALPHA * dr1t + dxm * (1.0 + sc)
        return _ln_bwd(dxn, xh, rstd, g), _csum(dxm * xnt), _csum(dxm), _csum(dxn * xh), _csum(dxn)

    def ln_in_bwd_ctx(j, xt, dxm, g, b, sc):
        xh, rstd = _ln_stats(xt)
        xnt = xh * g + b
        dxn = dxm * (1.0 + sc)
        return _csum(dxm * xnt), _csum(dxm), _csum(dxn * xh), _csum(dxn)

    grad_x, d_sca, d_sha, d_g0a, d_b0a = _rowcall(
        "ln_in_bwd_lat", ln_in_bwd_lat, S, trd, [(x2d, D, 0, 0, False), (dr1, D, 0, 0, False), (dxm_all, D, 0, 0, False)],
        [vec(g0), vec(b0), vec(sc_a)], [(D, f32, False)], [(1, D, False)] * 4)
    d_csca, d_csha, d_g0b, d_b0b = _rowcall(
        "ln_in_bwd_ctx", ln_in_bwd_ctx, T, tr, [(ctx2d, D, 0, 0, False), (dxm_all, D, S // tr, 0, False)],
        [vec(g0), vec(b0), vec(csc_a)], [], [(1, D, False)] * 4)

    zD = jnp.zeros((1, D), f32)
    dmod_me = jnp.concatenate([d_sha, d_sca, d_ga, d_shm, d_scm, d_gm], axis=1)
    dmodc_me = jnp.concatenate([d_csha, d_csca, zD, zD, zD, zD], axis=1)
    small_names = ["ln_in_g", "ln_in_b", "a_log_f", "dt_bias_f", "a_log_b", "dt_bias_b", "dn_norm_g", "conf_dw_b", "conf_ln_g", "conf_ln_b",
                   "ln1_g", "ln1_b", "b_mlp1", "b_mlp2", "ln2_g", "ln2_b"]
    small_parts = [d_g0a + d_g0b, d_b0a + d_b0b, d_alog[:, 0:NH], d_dt[:, 0:NH], d_alog[:, 2 * NH:3 * NH], d_dt[:, 2 * NH:3 * NH], d_dng, d_bdw, d_clg, d_clb,
                   d_l1g, d_l1b, d_b1, d_b2, d_l2g, d_l2b]
    conv_parts = [dw7[:K7], dw31[:K31]]
    packed = _pack(small_parts + conv_parts + [dmodc_me])
    (pk_all, dmod_all) = _all_gather("ag_small_grads", [packed, dmod_me])
    summed = _sum8("sum_small_grads", pk_all.reshape(NDEV, -1))
    parts_sh = [a.shape for a in small_parts + conv_parts + [dmodc_me]]
    un = _unpack(summed, parts_sh)
    gsmall = dict(zip(small_names, un[:len(small_names)]))
    g_w7_full, g_w31_full, dmodc = un[len(small_names):]
    gsmall["w_qkv_conv"] = lax.dynamic_slice_in_dim(g_w7_full, me * w_qkv_conv.shape[2], w_qkv_conv.shape[2], axis=1)
    gsmall["conf_dw_w"] = lax.dynamic_slice_in_dim(g_w31_full, me * conf_dw_w.shape[2], conf_dw_w.shape[2], axis=1)

    dm16 = jnp.concatenate([dmod_all.reshape(NDEV, 6 * D), dmodc, jnp.zeros((7, 6 * D), f32)], axis=0)
    (gbmod,) = _rowcall("bmod_grad", lambda j, a: (_csum(a),), 16, 16, [(dm16, 6 * D, 0, 0, False)], [], [], [(1, 6 * D, False)])
    gsmall["b_mod"] = gbmod
    dm16_mine = lax.dynamic_slice_in_dim(dm16, me * MODC, MODC, axis=1)
    g_wmod = _mm("mod_dw", sc16, dm16_mine, "tn", f32)
    dsc16_part = _mm("mod_dx", dm16_mine, w_mod[0], "nt", f32)
    (dsc_all,) = _all_gather("ag_cctx", [dsc16_part[8:9]])
    dsilu_cctx = _sum8("sum_cctx", dsc_all.reshape(NDEV, D))
    (g_cctx,) = _rowcall("cctx_grad", lambda j, ds, cc: (ds * _dsilu(cc),), 1, 1, [(dsilu_cctx, D, 0, 0, False), (c_ctx.reshape(1, D).astype(f32), D, 0, 0, False)],
                         [], [(D, f32, False)], [])
    gsmall["c_ctx"] = g_cctx

    dwin = jnp.concatenate([dwcat_p[:, :4 * DN], dwcat_p[:, 4 * DN + 2 * CONF:4 * DN + 2 * CONF + 4 * NH], dwcat_p[:, 4 * DN:4 * DN + 2 * CONF]], axis=1)
    dwin_p = jnp.transpose(dwin.reshape(D, NDEV, INC // NDEV), (1, 0, 2))
    r_win, r_wout, r_w1, r_w2 = _all_to_all("a2a_grads", [dwin_p, dwout_p.reshape(NDEV, (DN + CONF) // NDEV, D), dw1_p, dw2_p.reshape(NDEV, DFF // NDEV, D)])

    grads, deltas, new_m, new_v = {}, {}, {}, {}
    for nm, g in (("w_in", r_win), ("w_out", r_wout), ("w_mlp1", r_w1), ("w_mlp2", r_w2)):
        w3 = weights[nm]
        res = _adam("adam_" + nm, w3[0], mom1[nm][0], mom2[nm][0], g, True)
        grads[nm], deltas[nm], new_m[nm], new_v[nm] = [a.reshape(w3.shape) for a in res]
    res = _adam("adam_w_mod", w_mod[0], m_w_mod[0], v_w_mod[0], g_wmod, False)
    grads["w_mod"], deltas["w_mod"], new_m["w_mod"], new_v["w_mod"] = [a.reshape(w_mod.shape) for a in res]
    snames = [n for n in names if n not in grads]
    res = _adam("adam_small", _pack([weights[n] for n in snames]), _pack([mom1[n] for n in snames]), _pack([mom2[n] for n in snames]),
                _pack([gsmall[n] for n in snames]), False)
    shapes = [weights[n].shape for n in snames]
    for dst, packed_out in zip((grads, deltas, new_m, new_v), res):
        for n, a in zip(snames, _unpack(packed_out, shapes)):
            dst[n] = a

    return (loss, grad_x.reshape(x.shape), *[grads[n] for n in names], *[deltas[n] for n in names],
            *[new_m[n] for n in names], *[new_v[n] for n in names])
```

```python
import functools

import jax
import jax.numpy as jnp
from jax import lax
from jax.experimental import pallas as pl
from jax.experimental.pallas import tpu as pltpu

f32 = jnp.float32
bf16 = jnp.bfloat16
HI = lax.Precision.HIGHEST
MESH = pl.DeviceIdType.MESH

NDEV = 8
HD = 128
CH = 64
GW = 64
LANES = 128
GPAD = 512
LN_EPS = 1e-5
ALPHA = 2.0 ** 0.25
ADAM_LR, ADAM_B1, ADAM_B2, ADAM_EPS, ADAM_WD, ADAM_STEP = 0.001, 0.9, 0.999, 1e-08, 0.01, 10
VMEM_LIMIT = 56 * 1024 * 1024
ROW_BLOCK_BYTES = 1 << 20


def _cparams(sem):
    return pltpu.CompilerParams(dimension_semantics=sem, vmem_limit_bytes=VMEM_LIMIT)


def _tile(dim, pref, align):
    t = min(pref, dim)
    t -= t % align
    while t >= align:
        if dim % t == 0:
            return t
        t -= align
    return dim


def _row_tile(nrows, width, cap=256):
    t = max(16, min(cap, ROW_BLOCK_BYTES // (4 * width)))
    t = 1 << (t.bit_length() - 1)
    while nrows % t:
        t //= 2
    return max(t, 1)


def _sigmoid(x):
    return 1.0 / (1.0 + jnp.exp(-x))


def _silu(x):
    return x * _sigmoid(x)


def _dsilu(x):
    s = _sigmoid(x)
    return s * (1.0 + x * (1.0 - s))


def _softplus(x):
    return jnp.maximum(x, 0.0) + jnp.log(1.0 + jnp.exp(-jnp.abs(x)))


def _ln_stats(r):
    mu = jnp.mean(r, axis=-1, keepdims=True)
    xc = r - mu
    rstd = lax.rsqrt(jnp.mean(xc * xc, axis=-1, keepdims=True) + LN_EPS)
    return xc * rstd, rstd


def _ln_bwd(dy, xhat, rstd, g):
    dxh = dy * g
    return rstd * (dxh - jnp.mean(dxh, axis=-1, keepdims=True) - xhat * jnp.mean(dxh * xhat, axis=-1, keepdims=True))


def _csum(a):
    return jnp.sum(a, axis=0, keepdims=True)


def _mm(name, a, b, mode, out_dtype, out_split=1, tm=1024, tn=1024, tk=2048, extras=(), epilogue=None):
    if mode == "tn":
        K, M = a.shape
    else:
        M, K = a.shape
    bp = b.shape[0] if b.ndim == 3 else 1
    brows, bcols = b.shape[-2], b.shape[-1] * bp
    N = brows if mode == "nt" else bcols
    assert K == (bcols if mode == "nt" else brows), (name, a.shape, b.shape)
    tm = _tile(M, tm, LANES if mode == "tn" else 16)
    nsplit = max(out_split, bp if mode != "nt" else 1)
    tn = _tile(N // nsplit, tn, LANES)
    tk = _tile(K // (bp if mode == "nt" else 1), tk, LANES)
    nk = K // tk
    dims = {"nn": (((1,), (0,)), ((), ())), "nt": (((1,), (1,)), ((), ())), "tn": (((0,), (0,)), ((), ()))}[mode]
    out_dtypes = out_dtype if isinstance(out_dtype, tuple) else (out_dtype,)

    a_spec = pl.BlockSpec((tk, tm), lambda i, j, k: (k, i)) if mode == "tn" else pl.BlockSpec((tm, tk), lambda i, j, k: (i, k))
    if b.ndim == 3:
        if mode == "nt":
            per = (K // bp) // tk
            b_spec = pl.BlockSpec((None, tn, tk), lambda i, j, k: (k // per, j, k % per))
        else:
            per = (N // bp) // tn
            b_spec = pl.BlockSpec((None, tk, tn), lambda i, j, k: (j // per, k, j % per))
    elif mode == "nt":
        b_spec = pl.BlockSpec((tn, tk), lambda i, j, k: (j, k))
    else:
        b_spec = pl.BlockSpec((tk, tn), lambda i, j, k: (k, j))
    if out_split > 1:
        pero = (N // out_split) // tn
        o_spec = pl.BlockSpec((None, tm, tn), lambda i, j, k: (j // pero, i, j % pero))
        o_shapes = [jax.ShapeDtypeStruct((out_split, M, N // out_split), dt) for dt in out_dtypes]
    else:
        o_spec = pl.BlockSpec((tm, tn), lambda i, j, k: (i, j))
        o_shapes = [jax.ShapeDtypeStruct((M, N), dt) for dt in out_dtypes]
    in_specs = [a_spec, b_spec]
    args = [a, b]
    for kind, arr in extras:
        in_specs.append(pl.BlockSpec((1, tn), lambda i, j, k: (0, j)) if kind == "row" else pl.BlockSpec((tm, tn), lambda i, j, k: (i, j)))
        args.append(arr)
    n_in, n_out = len(args), len(out_dtypes)

    def finish(refs, r):
        outs = (r,) if epilogue is None else epilogue(r, *[e[...] for e in refs[2:n_in]])
        for o_ref, val in zip(refs[n_in:n_in + n_out], outs):
            o_ref[...] = val.astype(o_ref.dtype)

    def body(*refs):
        part = lax.dot_general(refs[0][...].astype(bf16), refs[1][...].astype(bf16), dims, preferred_element_type=f32)
        if nk == 1:
            finish(refs, part)
            return
        acc = refs[-1]
        k = pl.program_id(2)

        @pl.when(k == 0)
        def _():
            acc[...] = part

        @pl.when(jnp.logical_and(k > 0, k < nk - 1))
        def _():
            acc[...] += part

        @pl.when(k == nk - 1)
        def _():
            finish(refs, acc[...] + part)

    res = pl.pallas_call(
        body, name=name, grid=(M // tm, N // tn, nk), in_specs=in_specs, out_specs=[o_spec] * n_out, out_shape=o_shapes,
        scratch_shapes=[pltpu.VMEM((tm, tn), f32)] if nk > 1 else [], compiler_params=_cparams(("parallel", "parallel", "arbitrary")),
    )(*args)
    return res[0] if n_out == 1 else res


def _rowcall(name, fn, nrows, tr, rows_in, vecs_in, rows_out, accs_out, ncol=1):
    nrt = nrows // tr
    in_specs, args = [], []
    for arr, w, ro, co, pc in rows_in:
        in_specs.append(pl.BlockSpec((tr, w), functools.partial(lambda j, i, ro, co, pc: (i + ro, co + (j if pc else 0)), ro=ro, co=co, pc=pc)))
        args.append(arr)
    for arr, w, co, pc in vecs_in:
        in_specs.append(pl.BlockSpec((arr.shape[0], w), functools.partial(lambda j, i, co, pc: (0, co + (j if pc else 0)), co=co, pc=pc)))
        args.append(arr)
    out_specs, out_shape = [], []
    for w, dt, pc in rows_out:
        out_specs.append(pl.BlockSpec((tr, w), functools.partial(lambda j, i, pc: (i, j if pc else 0), pc=pc)))
        out_shape.append(jax.ShapeDtypeStruct((nrows, w * (ncol if pc else 1)), dt))
    for k, w, pc in accs_out:
        out_specs.append(pl.BlockSpec((k, w), functools.partial(lambda j, i, pc: (0, j if pc else 0), pc=pc)))
        out_shape.append(jax.ShapeDtypeStruct((k, w * (ncol if pc else 1)), f32))
    n_in, n_ro = len(args), len(rows_out)

    def body(*refs):
        j, i = pl.program_id(0), pl.program_id(1)
        outs = fn(j, *[r[...] for r in refs[:n_in]])
        for r, val in zip(refs[n_in:n_in + n_ro], outs[:n_ro]):
            r[...] = val.astype(r.dtype)
        for (k, w, pc), r, val in zip(accs_out, refs[n_in + n_ro:], outs[n_ro:]):
            first = (i == 0) if pc else jnp.logical_and(i == 0, j == 0)

            @pl.when(first)
            def _(r=r):
                r[...] = jnp.zeros_like(r)

            r[...] += val

    res = pl.pallas_call(
        body, name=name, grid=(ncol, nrt), in_specs=in_specs, out_specs=out_specs, out_shape=out_shape,
        compiler_params=_cparams(("arbitrary", "arbitrary")),
    )(*args)
    return res


def _tap_valid(mode, t, d, nrows, nlat):
    if mode == "seg":
        tp = t + d
        return (tp >= 0) & (tp < nrows) & ((t < nlat) == (tp < nlat))
    if mode == "row":
        p = (t & (GW - 1)) + d
        return (p >= 0) & (p < GW)
    tp = t + d * GW
    return (tp >= 0) & (tp < nrows)


def _conv(x, w_ref, ktaps, mode, nlat, flip=False):
    nrows = x.shape[0]
    stride = GW if mode == "col" else 1
    t = lax.broadcasted_iota(jnp.int32, (nrows, 1), 0)
    acc = jnp.zeros_like(x)
    for j in range(ktaps):
        d = j - ktaps // 2
        jj = ktaps - 1 - j if flip else j
        wj = w_ref[jj:jj + 1, :]
        if d == 0:
            acc = acc + x * wj
        else:
            xs = pltpu.roll(x, (-d * stride) % nrows, 0)
            acc = acc + jnp.where(_tap_valid(mode, t, d, nrows, nlat), xs * wj, 0.0)
    return acc


def _conv_wgrad(dy, x, dw_ref, ktaps, mode, nlat):
    nrows = x.shape[0]
    stride = GW if mode == "col" else 1
    t = lax.broadcasted_iota(jnp.int32, (nrows, 1), 0)
    dw_ref[...] = jnp.zeros_like(dw_ref)
    for j in range(ktaps):
        d = j - ktaps // 2
        if d == 0:
            prod = x * dy
        else:
            xs = pltpu.roll(x, (-d * stride) % nrows, 0)
            prod = jnp.where(_tap_valid(mode, t, d, nrows, nlat), xs * dy, 0.0)
        dw_ref[j:j + 1, :] = _csum(prod)


def _qkv_post(j, pre, nh):
    a = _silu(pre)
    inv = lax.rsqrt(jnp.sum(a * a, axis=-1, keepdims=True) + 1e-6)
    scale = jnp.where(j < nh, HD ** -0.5, 1.0).astype(f32)
    nrm = jnp.where(j < 2 * nh, inv, 1.0) * scale
    return a, inv, nrm


def _qkv_conv_fwd(h_all, w7, nrows, nlat, nh, ktaps):
    ntile = 3 * nh

    def body(h_ref, w_ref, o_ref):
        j = pl.program_id(0)
        pre = _conv(h_ref[...], w_ref, ktaps, "seg", nlat)
        a, _, nrm = _qkv_post(j, pre, nh)
        o_ref[...] = a * nrm

    return pl.pallas_call(
        body, name="qkv_conv_fwd", grid=(ntile,),
        in_specs=[pl.BlockSpec((nrows, HD), lambda j: (0, j)), pl.BlockSpec((w7.shape[0], HD), lambda j: (0, j))],
        out_specs=pl.BlockSpec((None, nrows, HD), lambda j: (j, 0, 0)),
        out_shape=jax.ShapeDtypeStruct((ntile, nrows, HD), f32), compiler_params=_cparams(("parallel",)),
    )(h_all, w7)


def _qkv_conv_bwd(h_all, w7, dqkv, nrows, nlat, nh, ktaps):
    ntile = 3 * nh

    def body(h_ref, w_ref, d0_ref, d1_ref, dh_ref, dw_ref):
        j = pl.program_id(0)
        hx = h_ref[...]
        pre = _conv(hx, w_ref, ktaps, "seg", nlat)
        a, inv, nrm = _qkv_post(j, pre, nh)
        dn = (d0_ref[...] + d1_ref[...]) * jnp.where(j < nh, HD ** -0.5, 1.0).astype(f32)
        n = a * inv
        da_norm = inv * (dn - n * jnp.sum(dn * n, axis=-1, keepdims=True))
        da = jnp.where(j < 2 * nh, da_norm, dn)
        dpre = da * _dsilu(pre)
        dh_ref[...] = _conv(dpre, w_ref, ktaps, "seg", nlat, flip=True).astype(dh_ref.dtype)
        _conv_wgrad(dpre, hx, dw_ref, ktaps, "seg", nlat)

    blk = pl.BlockSpec((nrows, HD), lambda j: (0, j))
    wblk = pl.BlockSpec((w7.shape[0], HD), lambda j: (0, j))
    dblk = lambda d: pl.BlockSpec((None, None, nrows, HD), lambda j: (d, j, 0, 0))
    return pl.pallas_call(
        body, name="qkv_conv_bwd", grid=(ntile,), in_specs=[blk, wblk, dblk(0), dblk(1)], out_specs=[blk, wblk],
        out_shape=[jax.ShapeDtypeStruct((nrows, ntile * HD), bf16), jax.ShapeDtypeStruct((w7.shape[0], ntile * HD), f32)],
        compiler_params=_cparams(("parallel",)),
    )(h_all, w7, dqkv, dqkv)


def _conf_conv_fwd(h_all, w31, nlat, conf, val_blk, ktaps):
    nt = conf // LANES
    nhalf = nt // 2

    def body(v_ref, g_ref, w_ref, o_ref):
        j = pl.program_id(0)
        glu = v_ref[...] * _sigmoid(g_ref[...])

        @pl.when(j < nhalf)
        def _():
            o_ref[...] = _conv(glu, w_ref, ktaps, "row", nlat)

        @pl.when(j >= nhalf)
        def _():
            o_ref[...] = _conv(glu, w_ref, ktaps, "col", nlat)

    return pl.pallas_call(
        body, name="conf_conv_fwd", grid=(nt,),
        in_specs=[pl.BlockSpec((nlat, LANES), lambda j: (0, val_blk + j)), pl.BlockSpec((nlat, LANES), lambda j: (0, val_blk + nt + j)),
                  pl.BlockSpec((w31.shape[0], LANES), lambda j: (0, j))],
        out_specs=pl.BlockSpec((nlat, LANES), lambda j: (0, j)),
        out_shape=jax.ShapeDtypeStruct((nlat, conf), f32), compiler_params=_cparams(("parallel",)),
    )(h_all, h_all, w31)


def _conf_conv_bwd(h_all, w31, dyc, nlat, conf, val_blk, ktaps):
    nt = conf // LANES
    nhalf = nt // 2

    def body(v_ref, g_ref, w_ref, dy_ref, dv_ref, dg_ref, dw_ref):
        j = pl.program_id(0)
        val, sg = v_ref[...], _sigmoid(g_ref[...])
        glu = val * sg
        dy = dy_ref[...]

        def run(mode):
            dglu = _conv(dy, w_ref, ktaps, mode, nlat, flip=True)
            dv_ref[...] = (dglu * sg).astype(dv_ref.dtype)
            dg_ref[...] = (dglu * val * sg * (1.0 - sg)).astype(dg_ref.dtype)
            _conv_wgrad(dy, glu, dw_ref, ktaps, mode, nlat)

        @pl.when(j < nhalf)
        def _():
            run("row")

        @pl.when(j >= nhalf)
        def _():
            run("col")

    blk = pl.BlockSpec((nlat, LANES), lambda j: (0, j))
    wblk = pl.BlockSpec((w31.shape[0], LANES), lambda j: (0, j))
    return pl.pallas_call(
        body, name="conf_conv_bwd", grid=(nt,),
        in_specs=[pl.BlockSpec((nlat, LANES), lambda j: (0, val_blk + j)), pl.BlockSpec((nlat, LANES), lambda j: (0, val_blk + nt + j)), wblk, blk],
        out_specs=[blk, blk, wblk],
        out_shape=[jax.ShapeDtypeStruct((nlat, conf), bf16), jax.ShapeDtypeStruct((nlat, conf), bf16),
                   jax.ShapeDtypeStruct((w31.shape[0], conf), f32)],
        compiler_params=_cparams(("parallel",)),
    )(h_all, h_all, w31, dyc)


def _bd(eq, a, b):
    return jnp.einsum(eq, a.astype(bf16), b.astype(bf16), preferred_element_type=f32)


def _hd(eq, a, b):
    return jnp.einsum(eq, a, b, precision=HI, preferred_element_type=f32)


def _gdn_chunk(q, k, v, gt, nh, rev):
    g = jnp.stack([jnp.broadcast_to(gt[:, h:h + 1], (CH, LANES)) for h in range(nh)])
    beta = jnp.stack([jnp.broadcast_to(gt[:, nh + h:nh + h + 1], (CH, LANES)) for h in range(nh)])
    ii = lax.broadcasted_iota(jnp.int32, (CH, CH), 0)
    jj = lax.broadcasted_iota(jnp.int32, (CH, CH), 1)
    si, sj = jnp.where(rev, jj, ii), jnp.where(rev, ii, jj)
    tril, stril = (si >= sj)[None], (si > sj)[None]
    trilf = jnp.broadcast_to(tril.astype(f32), (nh, CH, CH))
    gc = _hd("hit,htl->hil", trilf, g)
    gc_row = _hd("hil,hjl->hij", jnp.full((nh, CH, LANES), 1.0 / LANES, f32), gc)
    diff = gc[:, :, :CH] - gc_row
    gam = jnp.where(tril, jnp.exp(jnp.where(tril, diff, 0.0)), 0.0)
    e = jnp.exp(gc)
    gl = jnp.where(rev, gc[:, 0:1, :], gc[:, CH - 1:CH, :])
    el = jnp.exp(gl)
    r = jnp.exp(gl - gc)
    kb, vb = k * beta, v * beta
    kbe = kb * e
    amat = jnp.where(stril, _bd("hik,hjk->hij", kb, k) * gam, 0.0)
    eye = (ii == jj).astype(f32)[None]
    xp = -amat
    tinv = eye + xp
    for _ in range(5):
        xp = _hd("hij,hjk->hik", xp, xp)
        tinv = tinv + _hd("hij,hjk->hik", tinv, xp)
    u = _hd("hij,hjv->hiv", tinv, vb)
    w = _hd("hij,hjk->hik", tinv, kbe)
    pmat = jnp.where(tril, _bd("hik,hjk->hij", q, k) * gam, 0.0)
    return dict(beta=beta, gam=gam, e=e, el=el, r=r, kb=kb, vb=vb, kbe=kbe, amat=amat, tinv=tinv, u=u, w=w, pmat=pmat,
                qd=q * e, kd=k * r, tril=tril, stril=stril, trilf=trilf)


def _scan_row_chunk(d, n, ns, nt):
    fwd = jnp.where(n < nt, ns + n, n - nt)
    return jnp.where(d == 0, fwd, ns + nt - 1 - n)


def _gdn_fwd(qkv, gd, nh, nlat, nctx):
    nrows = nlat + nctx
    ns, nt = nlat // CH, nctx // CH
    nch = ns + nt

    def body(qkv_ref, g_ref, o_ref, st_ref, state):
        d, n = pl.program_id(0), pl.program_id(1)

        @pl.when(n == 0)
        def _():
            state[...] = jnp.zeros_like(state)

        c = _gdn_chunk(qkv_ref[0], qkv_ref[1], qkv_ref[2], g_ref[...], nh, d == 1)
        s = state[...]
        vn = c["u"] - _bd("hck,hkv->hcv", c["w"], s)
        o_ref[...] = _bd("hck,hkv->hcv", c["qd"], s) + _bd("hcd,hdv->hcv", c["pmat"], vn)
        st_ref[...] = s
        state[...] = s * c["el"] + _bd("hck,hcv->hkv", c["kd"], vn)

    return pl.pallas_call(
        body, name="gdn_fwd", grid=(2, nch),
        in_specs=[pl.BlockSpec((3, nh, CH, HD), lambda d, n: (0, 0, _scan_row_chunk(d, n, ns, nt), 0)),
                  pl.BlockSpec((None, CH, LANES), lambda d, n: (d, _scan_row_chunk(d, n, ns, nt), 0))],
        out_specs=[pl.BlockSpec((None, nh, CH, HD), lambda d, n: (d, 0, _scan_row_chunk(d, n, ns, nt), 0)),
                   pl.BlockSpec((None, None, nh, HD, HD), lambda d, n: (d, n, 0, 0, 0))],
        out_shape=[jax.ShapeDtypeStruct((2, nh, nrows, HD), f32), jax.ShapeDtypeStruct((2, nch, nh, HD, HD), f32)],
        scratch_shapes=[pltpu.VMEM((nh, HD, HD), f32)], compiler_params=_cparams(("arbitrary", "arbitrary")),
    )(qkv, gd)


def _gdn_bwd(qkv, gd, states, do, nh, nlat, nctx):
    nrows = nlat + nctx
    ns, nt = nlat // CH, nctx // CH
    nch = ns + nt

    def body(qkv_ref, g_ref, st_ref, do_ref, dqkv_ref, dg_ref, db_ref, dstate):
        d, step = pl.program_id(0), pl.program_id(1)

        @pl.when(step == 0)
        def _():
            dstate[...] = jnp.zeros_like(dstate)

        q, k, v = qkv_ref[0], qkv_ref[1], qkv_ref[2]
        c = _gdn_chunk(q, k, v, g_ref[...], nh, d == 1)
        in_ctx = (nch - 1 - step) < nt
        s, dsp = st_ref[...], dstate[...]
        dout = jnp.where(in_ctx, 0.0, do_ref[...])
        beta, gam, e, el, r = c["beta"], c["gam"], c["e"], c["el"], c["r"]
        tinv, u, w, pmat, amat = c["tinv"], c["u"], c["w"], c["pmat"], c["amat"]
        vn = u - _bd("hck,hkv->hcv", w, s)
        dvn = _bd("hdc,hdv->hcv", pmat, dout) + _bd("hck,hkv->hcv", c["kd"], dsp)
        dp = jnp.where(c["tril"], _bd("hcv,hdv->hcd", dout, vn), 0.0)
        dqd = _bd("hcv,hkv->hck", dout, s)
        dkd = _bd("hcv,hkv->hck", vn, dsp)
        dstate[...] = _bd("hck,hcv->hkv", c["qd"], dout) + dsp * el - _bd("hck,hcv->hkv", w, dvn)
        del_ = jnp.sum(jnp.sum(s * dsp, axis=2, keepdims=True), axis=1, keepdims=True)
        dw = -_bd("hcv,hkv->hck", dvn, s)
        dvb = _hd("hji,hjv->hiv", tinv, dvn)
        dkbe = _hd("hji,hjk->hik", tinv, dw)
        z = _bd("hiv,hjv->hij", dvn, u) + _bd("hik,hjk->hij", dw, w)
        da = jnp.where(c["stril"], -_hd("hji,hjl->hil", tinv, z), 0.0)
        dm = da * gam
        dkb = _bd("hij,hjk->hik", dm, k) + dkbe * e
        dn = dp * gam
        dqkv_ref[0] = _bd("hij,hjk->hik", dn, k) + dqd * e
        dqkv_ref[1] = _bd("hji,hjk->hik", dm, c["kb"]) + _bd("hji,hjk->hik", dn, q) + dkd * r + dkb * beta
        dqkv_ref[2] = dvb * beta
        gmat = da * amat + dp * pmat
        ones = jnp.ones((nh, CH, LANES), f32)
        rows_minus_cols = _hd("hij,hjl->hil", gmat, ones) - _hd("hji,hjl->hil", gmat, ones)
        de = jnp.sum(dqd * q + dkbe * c["kb"], axis=-1, keepdims=True)
        drr = jnp.sum(dkd * k, axis=-1, keepdims=True) * r
        dgc = rows_minus_cols + de * e - drr
        dgl = jnp.sum(drr, axis=1, keepdims=True) + del_ * el
        row = lax.broadcasted_iota(jnp.int32, (1, CH, 1), 1)
        total_row = row == jnp.where(d == 1, 0, CH - 1)
        dgc = dgc + jnp.where(total_row, dgl, 0.0)
        dg_ref[...] = _hd("hji,hjl->hil", c["trilf"], dgc)
        db_ref[...] = jnp.broadcast_to(jnp.sum(dvb * v + dkb * k, axis=-1, keepdims=True), (nh, CH, LANES))

    def rc(d, n):
        return _scan_row_chunk(d, nch - 1 - n, ns, nt)

    gblk = pl.BlockSpec((None, nh, CH, LANES), lambda d, n: (d, 0, rc(d, n), 0))
    gshape = jax.ShapeDtypeStruct((2, nh, nrows, LANES), f32)
    return pl.pallas_call(
        body, name="gdn_bwd", grid=(2, nch),
        in_specs=[pl.BlockSpec((3, nh, CH, HD), lambda d, n: (0, 0, rc(d, n), 0)),
                  pl.BlockSpec((None, CH, LANES), lambda d, n: (d, rc(d, n), 0)),
                  pl.BlockSpec((None, None, nh, HD, HD), lambda d, n: (d, nch - 1 - n, 0, 0, 0)),
                  pl.BlockSpec((nh, CH, HD), lambda d, n: (0, jnp.minimum(rc(d, n), ns - 1), 0))],
        out_specs=[pl.BlockSpec((None, 3, nh, CH, HD), lambda d, n: (d, 0, 0, rc(d, n), 0)), gblk, gblk],
        out_shape=[jax.ShapeDtypeStruct((2, 3, nh, nrows, HD), f32), gshape, gshape],
        scratch_shapes=[pltpu.VMEM((nh, HD, HD), f32)], compiler_params=_cparams(("arbitrary", "arbitrary")),
    )(qkv, gd, states, do)


def _gate_norm_fwd(o5, h_all, dng, nh, nlat, zblk, tr):
    def body(of_ref, ob_ref, z_ref, g_ref, out_ref):
        o = of_ref[...] + ob_ref[...]
        rs = lax.rsqrt(jnp.mean(o * o, axis=-1, keepdims=True) + 1e-6)
        out_ref[...] = ((o * rs * g_ref[...]) * _silu(z_ref[...])).astype(out_ref.dtype)

    return pl.pallas_call(
        body, name="gate_norm_fwd", grid=(nh, nlat // tr),
        in_specs=[pl.BlockSpec((None, None, tr, HD), lambda j, i: (0, j, i, 0)), pl.BlockSpec((None, None, tr, HD), lambda j, i: (1, j, i, 0)),
                  pl.BlockSpec((tr, HD), lambda j, i: (i, zblk + j)), pl.BlockSpec((1, HD), lambda j, i: (0, 0))],
        out_specs=pl.BlockSpec((tr, HD), lambda j, i: (i, j)), out_shape=jax.ShapeDtypeStruct((nlat, nh * HD), bf16),
        compiler_params=_cparams(("parallel", "parallel")),
    )(o5, o5, h_all, dng)


def _gate_norm_bwd(dmix, o5, h_all, dng, nh, nlat, zblk, tr):
    def body(dm_ref, of_ref, ob_ref, z_ref, g_ref, do_ref, dz_ref, dg_ref):
        o = of_ref[...] + ob_ref[...]
        z, g, dm = z_ref[...], g_ref[...], dm_ref[...]
        rs = lax.rsqrt(jnp.mean(o * o, axis=-1, keepdims=True) + 1e-6)
        don = dm * _silu(z)
        dz_ref[...] = (dm * (o * rs * g) * _dsilu(z)).astype(dz_ref.dtype)
        dog = don * g
        do_ref[...] = rs * (dog - o * rs * rs * jnp.mean(dog * o, axis=-1, keepdims=True))

        @pl.when(jnp.logical_and(pl.program_id(0) == 0, pl.program_id(1) == 0))
        def _():
            dg_ref[...] = jnp.zeros_like(dg_ref)

        dg_ref[...] += _csum(don * o * rs)

    return pl.pallas_call(
        body, name="gate_norm_bwd", grid=(nh, nlat // tr),
        in_specs=[pl.BlockSpec((tr, HD), lambda j, i: (i, j)),
                  pl.BlockSpec((None, None, tr, HD), lambda j, i: (0, j, i, 0)), pl.BlockSpec((None, None, tr, HD), lambda j, i: (1, j, i, 0)),
                  pl.BlockSpec((tr, HD), lambda j, i: (i, zblk + j)), pl.BlockSpec((1, HD), lambda j, i: (0, 0))],
        out_specs=[pl.BlockSpec((None, tr, HD), lambda j, i: (j, i, 0)), pl.BlockSpec((tr, HD), lambda j, i: (i, j)),
                   pl.BlockSpec((1, HD), lambda j, i: (0, 0))],
        out_shape=[jax.ShapeDtypeStruct((nh, nlat, HD), f32), jax.ShapeDtypeStruct((nlat, nh * HD), bf16), jax.ShapeDtypeStruct((1, HD), f32)],
        compiler_params=_cparams(("arbitrary", "arbitrary")),
    )(dmix, o5, o5, h_all, dng)


def _mesh_pos():
    return lax.axis_index("x"), lax.axis_index("y"), lax.axis_index("c")


def _lin(p):
    return 4 * p[0] + 2 * p[1] + p[2]


def _all_gather(name, xs):
    nx = len(xs)

    def body(*refs):
        xr, outr = refs[:nx], refs[nx:2 * nx]
        send, recv, loc = refs[2 * nx:]
        x, y, c = _mesh_pos()
        me, sib = (x, y, c), (x, y, 1 - c)
        chips = [(1 - x, y), (x, 1 - y), (1 - x, 1 - y)]

        def cp(l, k, block, to, src=None):
            rows = outr[l].at[_lin(block)]
            return pltpu.make_async_remote_copy(src_ref=rows if src is None else src, dst_ref=rows, send_sem=send.at[l, k],
                                                recv_sem=recv.at[l, k], device_id=to, device_id_type=MESH)

        mine = [pltpu.make_async_copy(xr[l], outr[l].at[_lin(me)], loc.at[l]) for l in range(nx)]
        for m in mine:
            m.start()
        first = []
        for l in range(nx):
            first.append(cp(l, 0, me, sib, src=xr[l]))
            first += [cp(l, 1 + j, me, (*chip, c), src=xr[l]) for j, chip in enumerate(chips)]
        for f in first:
            f.start()
        passed = []
        for l in range(nx):
            for j, chip in enumerate(chips):
                cp(l, 1 + j, (*chip, c), me).wait_recv()
                fwd = cp(l, 4 + j, (*chip, c), sib)
                fwd.start()
                passed.append(fwd)
        for l in range(nx):
            cp(l, 0, sib, me).wait_recv()
            for j, chip in enumerate(chips):
                cp(l, 4 + j, (*chip, 1 - c), me).wait_recv()
        for f in first + passed:
            f.wait_send()
        for m in mine:
            m.wait()

    anyspec = pl.BlockSpec(memory_space=pl.ANY)
    return pl.pallas_call(
        body, name=name, in_specs=[anyspec] * nx, out_specs=[anyspec] * nx,
        out_shape=[jax.ShapeDtypeStruct((NDEV,) + a.shape, a.dtype) for a in xs],
        scratch_shapes=[pltpu.SemaphoreType.DMA((nx, 7)), pltpu.SemaphoreType.DMA((nx, 7)), pltpu.SemaphoreType.DMA((nx,))],
    )(*xs)


HBM_SPEC = pl.BlockSpec(memory_space=pltpu.HBM)
SEM_SPEC = pl.BlockSpec(memory_space=pltpu.SEMAPHORE)
ANY_SPEC = pl.BlockSpec(memory_space=pl.ANY)
EFFECT = pltpu.SideEffectType.DATAFLOW_SIDE_EFFECTING


def _peer(rel):
    x, y, c = _mesh_pos()
    return (1 - x if rel & 4 else x, 1 - y if rel & 2 else y, 1 - c if rel & 1 else c)


def _xchg_copy(x_ref, land_ref, send, recv, rel, gather, slot):
    p = _peer(rel)
    return pltpu.make_async_remote_copy(src_ref=x_ref if gather else x_ref.at[_lin(p)], dst_ref=land_ref.at[slot(p)],
                                        send_sem=send.at[rel - 1], recv_sem=recv.at[rel - 1], device_id=p, device_id_type=MESH)


def _xchg_start(name, x, gather, after):
    me = _lin(_mesh_pos())
    shape = (NDEV,) + x.shape if gather else x.shape
    own = x if gather else lax.dynamic_index_in_dim(x, me, 0, keepdims=False)
    land = lax.dynamic_update_index_in_dim(lax.empty(shape, x.dtype), own, me, 0)

    def body(x_ref, land_ref, after_ref, send, recv, x_thru, land_thru, token):
        mine = _lin(_mesh_pos())
        for rel in range(1, NDEV):
            _xchg_copy(x_ref, land_ref, send, recv, rel, gather, lambda p: mine).start()
        token[...] = jnp.zeros_like(token)

    return pl.pallas_call(
        body, name=name,
        out_shape=(pltpu.SemaphoreType.DMA((NDEV - 1,)), pltpu.SemaphoreType.DMA((NDEV - 1,)), pltpu.HBM(x.shape, x.dtype),
                   pltpu.HBM(shape, x.dtype), jax.ShapeDtypeStruct((8, LANES), f32)),
        in_specs=(HBM_SPEC, HBM_SPEC, ANY_SPEC), out_specs=(SEM_SPEC, SEM_SPEC, HBM_SPEC, HBM_SPEC, pl.BlockSpec(memory_space=pltpu.VMEM)),
        input_output_aliases={0: 2, 1: 3}, compiler_params=pltpu.CompilerParams(has_side_effects=EFFECT),
    )(pltpu.with_memory_space_constraint(x, pltpu.HBM), pltpu.with_memory_space_constraint(land, pltpu.HBM), after)


def _xchg_wait(name, started, gather, after):
    send, recv, x_thru, land_thru, _ = started

    def body(x_ref, land_ref, send, recv, after_ref, x_dead, got_ref):
        for rel in range(1, NDEV):
            cp = _xchg_copy(x_ref, land_ref, send, recv, rel, gather, _lin)
            cp.wait_send()
            cp.wait_recv()

    return pl.pallas_call(
        body, name=name, out_shape=(pltpu.HBM(x_thru.shape, x_thru.dtype), pltpu.HBM(land_thru.shape, land_thru.dtype)),
        in_specs=(HBM_SPEC, HBM_SPEC, SEM_SPEC, SEM_SPEC, ANY_SPEC), out_specs=(HBM_SPEC, HBM_SPEC), input_output_aliases={0: 0, 1: 1},
        compiler_params=pltpu.CompilerParams(has_side_effects=EFFECT),
    )(x_thru, land_thru, send, recv, after)[1]


def _behind(token, a):
    return a + token[0:1, 0:1].astype(a.dtype)


def _sum8(name, g):
    def body(g_ref, o_ref):
        acc = g_ref[0:1, :]
        for p in range(1, NDEV):
            acc = acc + g_ref[p:p + 1, :]
        o_ref[...] = acc

    return pl.pallas_call(body, name=name, out_shape=jax.ShapeDtypeStruct((1, g.shape[1]), f32),
                          compiler_params=pltpu.CompilerParams(vmem_limit_bytes=VMEM_LIMIT))(g)


def _pack(arrs):
    flat = jnp.concatenate([a.reshape(-1).astype(f32) for a in arrs])
    pad = (-flat.shape[0]) % (8 * LANES)
    return jnp.pad(flat, (0, pad)).reshape(1, -1)


def _unpack(vec, shapes):
    out, off = [], 0
    flat = vec.reshape(-1)
    for s in shapes:
        n = 1
        for d in s:
            n *= d
        out.append(flat[off:off + n].reshape(s))
        off += n
    return out


def _adam(name, w, m, v, g, parts):
    nrows, width = w.shape
    tr = _row_tile(nrows, width * (3 if parts else 1), cap=512) if nrows >= 16 else nrows
    blk = pl.BlockSpec((tr, width), lambda i: (i, 0))
    gblk = pl.BlockSpec((NDEV, tr, width), lambda i: (0, i, 0)) if parts else blk
    c1 = 1.0 / (1.0 - ADAM_B1 ** ADAM_STEP)
    c2 = 1.0 / (1.0 - ADAM_B2 ** ADAM_STEP)

    def body(w_ref, m_ref, v_ref, g_ref, go_ref, d_ref, mo_ref, vo_ref):
        if parts:
            gg = g_ref[0].astype(f32)
            for p in range(1, NDEV):
                gg = gg + g_ref[p].astype(f32)
        else:
            gg = g_ref[...]
        mn = ADAM_B1 * m_ref[...] + (1.0 - ADAM_B1) * gg
        vn = ADAM_B2 * v_ref[...] + (1.0 - ADAM_B2) * (gg * gg)
        go_ref[...] = gg
        mo_ref[...] = mn
        vo_ref[...] = vn
        d_ref[...] = -ADAM_LR * ((mn * c1) / (jnp.sqrt(vn * c2) + ADAM_EPS) + ADAM_WD * w_ref[...])

    sh = jax.ShapeDtypeStruct((nrows, width), f32)
    return pl.pallas_call(body, name=name, grid=(nrows // tr,), in_specs=[blk, blk, blk, gblk], out_specs=[blk] * 4,
                          out_shape=[sh] * 4, compiler_params=_cparams(("parallel",)))(w, m, v, g)


def kernel(x, c, ctx, c_ctx, ln_in_g, ln_in_b, w_mod, b_mod, w_in, w_qkv_conv, a_log_f, dt_bias_f, a_log_b, dt_bias_b, dn_norm_g, conf_dw_w, conf_dw_b, conf_ln_g, conf_ln_b, w_out, ln1_g, ln1_b, w_mlp1, b_mlp1, w_mlp2, b_mlp2, ln2_g, ln2_b, loss_target, m_c_ctx, m_ln_in_g, m_ln_in_b, m_w_mod, m_b_mod, m_w_in, m_w_qkv_conv, m_a_log_f, m_dt_bias_f, m_a_log_b, m_dt_bias_b, m_dn_norm_g, m_conf_dw_w, m_conf_dw_b, m_conf_ln_g, m_conf_ln_b, m_w_out, m_ln1_g, m_ln1_b, m_w_mlp1, m_b_mlp1, m_w_mlp2, m_b_mlp2, m_ln2_g, m_ln2_b, v_c_ctx, v_ln_in_g, v_ln_in_b, v_w_mod, v_b_mod, v_w_in, v_w_qkv_conv, v_a_log_f, v_dt_bias_f, v_a_log_b, v_dt_bias_b, v_dn_norm_g, v_conf_dw_w, v_conf_dw_b, v_conf_ln_g, v_conf_ln_b, v_w_out, v_ln1_g, v_ln1_b, v_w_mlp1, v_b_mlp1, v_w_mlp2, v_b_mlp2, v_ln2_g, v_ln2_b):
    weights = dict(c_ctx=c_ctx, ln_in_g=ln_in_g, ln_in_b=ln_in_b, w_mod=w_mod, b_mod=b_mod, w_in=w_in, w_qkv_conv=w_qkv_conv, a_log_f=a_log_f, dt_bias_f=dt_bias_f, a_log_b=a_log_b, dt_bias_b=dt_bias_b, dn_norm_g=dn_norm_g, conf_dw_w=conf_dw_w, conf_dw_b=conf_dw_b, conf_ln_g=conf_ln_g, conf_ln_b=conf_ln_b, w_out=w_out, ln1_g=ln1_g, ln1_b=ln1_b, w_mlp1=w_mlp1, b_mlp1=b_mlp1, w_mlp2=w_mlp2, b_mlp2=b_mlp2, ln2_g=ln2_g, ln2_b=ln2_b)
    mom1 = dict(c_ctx=m_c_ctx, ln_in_g=m_ln_in_g, ln_in_b=m_ln_in_b, w_mod=m_w_mod, b_mod=m_b_mod, w_in=m_w_in, w_qkv_conv=m_w_qkv_conv, a_log_f=m_a_log_f, dt_bias_f=m_dt_bias_f, a_log_b=m_a_log_b, dt_bias_b=m_dt_bias_b, dn_norm_g=m_dn_norm_g, conf_dw_w=m_conf_dw_w, conf_dw_b=m_conf_dw_b, conf_ln_g=m_conf_ln_g, conf_ln_b=m_conf_ln_b, w_out=m_w_out, ln1_g=m_ln1_g, ln1_b=m_ln1_b, w_mlp1=m_w_mlp1, b_mlp1=m_b_mlp1, w_mlp2=m_w_mlp2, b_mlp2=m_b_mlp2, ln2_g=m_ln2_g, ln2_b=m_ln2_b)
    mom2 = dict(c_ctx=v_c_ctx, ln_in_g=v_ln_in_g, ln_in_b=v_ln_in_b, w_mod=v_w_mod, b_mod=v_b_mod, w_in=v_w_in, w_qkv_conv=v_w_qkv_conv, a_log_f=v_a_log_f, dt_bias_f=v_dt_bias_f, a_log_b=v_a_log_b, dt_bias_b=v_dt_bias_b, dn_norm_g=v_dn_norm_g, conf_dw_w=v_conf_dw_w, conf_dw_b=v_conf_dw_b, conf_ln_g=v_conf_ln_g, conf_ln_b=v_conf_ln_b, w_out=v_w_out, ln1_g=v_ln1_g, ln1_b=v_ln1_b, w_mlp1=v_w_mlp1, b_mlp1=v_b_mlp1, w_mlp2=v_w_mlp2, b_mlp2=v_b_mlp2, ln2_g=v_ln2_g, ln2_b=v_ln2_b)
    names = list(weights)

    me = _lin(_mesh_pos())
    S, D = x.shape[1], x.shape[2]
    T = ctx.shape[1]
    R = S + T
    DN = D // 2
    NH = DN // HD
    CONF = D - DN
    K7, K31 = w_qkv_conv.shape[1], conf_dw_w.shape[1]
    DFF = w_mlp1.shape[2] * NDEV
    INC = w_in.shape[2] * NDEV
    CONF_OFF = 4 * DN + 4 * NH
    NC = 4 * DN + 2 * CONF + GPAD
    QB, ZB, VB, GB = 0, 3 * DN // LANES, 4 * DN // LANES, (4 * DN + 2 * CONF) // LANES
    MODC = w_mod.shape[2]
    x2d, ctx2d, tgt = x[0], ctx[0], loss_target[0]
    row = lambda a: a.reshape(1, -1).astype(f32)

    c_all, w7_all, w31_all = _all_gather("ag_small", [c.astype(f32), w_qkv_conv[0], conf_dw_w[0]])
    w7 = jnp.pad(jnp.transpose(w7_all, (1, 0, 2)).reshape(K7, 3 * DN), ((0, (-K7) % 8), (0, 0)))
    w31 = jnp.pad(jnp.transpose(w31_all, (1, 0, 2)).reshape(K31, CONF), ((0, (-K31) % 8), (0, 0)))
    g_win, g_wout = _all_gather("ag_w_in_out", [w_in[0].astype(bf16), w_out[0].astype(bf16)])
    ag_w1 = _xchg_start("ag_w1_start", w_mlp1[0].astype(bf16), True, g_win)
    ag_w2 = _xchg_start("ag_w2_start", w_mlp2[0].astype(bf16), True, g_win)
    win_full = jnp.transpose(g_win, (1, 0, 2)).reshape(D, INC)
    w_cat = jnp.concatenate([win_full[:, :4 * DN], win_full[:, CONF_OFF:], win_full[:, 4 * DN:CONF_OFF],
                             jnp.zeros((D, GPAD - 4 * NH), bf16)], axis=1)
    wout_full = g_wout.reshape(DN + CONF, D)

    c16 = jnp.concatenate([c_all.reshape(NDEV, D), c_ctx.reshape(1, D).astype(f32), jnp.zeros((7, D), f32)], axis=0)
    c16 = _behind(ag_w2[4], _behind(ag_w1[4], c16))
    (sc16,) = _rowcall("silu_c", lambda j, a: (_silu(a),), 16, 16, [(c16, D, 0, 0, False)], [], [(D, f32, False)], [])
    bmod_mine = lax.dynamic_slice_in_dim(b_mod.astype(f32), me * MODC, MODC, axis=1)
    mod_part = _mm("mod_fwd", sc16, w_mod[0], "nn", f32, extras=[("row", bmod_mine)], epilogue=lambda r, b: (r + b,))
    (mod_g,) = _all_gather("ag_mod", [mod_part])
    mod_all = jnp.transpose(mod_g, (1, 0, 2)).reshape(16, 6 * D)
    mod_me = lax.dynamic_slice_in_dim(mod_all, me, 1, axis=0)
    sh_a, sc_a, g_a, sh_m, sc_m, g_m = [mod_me[:, i * D:(i + 1) * D] for i in range(6)]
    csh_a, csc_a = mod_all[8:9, 0:D], mod_all[8:9, D:2 * D]
    g0, b0 = row(ln_in_g), row(ln_in_b)

    def ln_mod(j, xt, g, b, sh, sc):
        xh, _ = _ln_stats(xt)
        xn = xh * g + b
        return xn, xn * (1.0 + sc) + sh

    tr = _row_tile(T, D)
    vec = lambda a: (a, a.shape[1], 0, False)
    xn, xm = _rowcall("ln_in_lat", ln_mod, S, tr, [(x2d, D, 0, 0, False)], [vec(g0), vec(b0), vec(sh_a), vec(sc_a)], [(D, f32, False), (D, bf16, False)], [])
    xcn, xcm = _rowcall("ln_in_ctx", ln_mod, T, tr, [(ctx2d, D, 0, 0, False)], [vec(g0), vec(b0), vec(csh_a), vec(csc_a)], [(D, f32, False), (D, bf16, False)], [])
    xm_all = jnp.concatenate([xm, xcm], axis=0)

    h_all = _mm("in_proj", xm_all, w_cat, "nn", f32, tm=1088, tn=1280)

    qkv = _qkv_conv_fwd(h_all, w7, R, S, NH, K7).reshape(3, NH, R, HD)
    lane = jnp.arange(LANES)
    is_a = ((lane < 4 * NH) & ((lane // NH) % 2 == 0)).astype(f32).reshape(1, LANES)
    pad_g = lambda a, b: jnp.concatenate([a.reshape(-1), jnp.zeros((NH,), f32), b.reshape(-1), jnp.zeros((LANES - 3 * NH,), f32)]).reshape(1, LANES)
    neg_a = pad_g(-jnp.exp(a_log_f.astype(f32)), -jnp.exp(a_log_b.astype(f32)))
    dt_v = pad_g(dt_bias_f.astype(f32), dt_bias_b.astype(f32))

    def gates_f(j, hg, isa, na, dt):
        return (jnp.where(isa > 0.5, na * _softplus(hg + dt), _sigmoid(hg)),)

    trg = _row_tile(T, LANES)
    (gates,) = _rowcall("gates_fwd", gates_f, R, trg, [(h_all, LANES, 0, GB, False)], [vec(is_a), vec(neg_a), vec(dt_v)], [(LANES, f32, False)], [])
    gpad = jnp.zeros((R, LANES - 2 * NH), f32)
    gd = jnp.stack([jnp.concatenate([gates[:, :2 * NH], gpad], axis=1), jnp.concatenate([gates[:, 2 * NH:4 * NH], gpad], axis=1)])

    o5, states = _gdn_fwd(qkv, gd, NH, S, T)
    dng = row(dn_norm_g)
    trh = _row_tile(S, HD)
    dn_out = _gate_norm_fwd(o5, h_all, dng, NH, S, ZB, trh)

    yconv = _conf_conv_fwd(h_all, w31, S, CONF, VB, K31)
    bdw, clg, clb = row(conf_dw_b), row(conf_ln_g), row(conf_ln_b)

    def conf_ln(j, yc, b, g, bb):
        xh, _ = _ln_stats(yc + b)
        return (_silu(xh * g + bb),)

    trc = _row_tile(S, CONF)
    (conf_out,) = _rowcall("conf_ln_fwd", conf_ln, S, trc, [(yconv, CONF, 0, 0, False)], [vec(bdw), vec(clg), vec(clb)], [(CONF, bf16, False)], [])
    mix = jnp.concatenate([dn_out, conf_out], axis=1)
    y = _mm("out_proj", mix, wout_full, "nn", f32)

    l1g, l1b, l2g, l2b = row(ln1_g), row(ln1_b), row(ln2_g), row(ln2_b)

    def ln1_mod(j, xnt, yt, ga, g, b, sh, sc):
        xh, _ = _ln_stats(ALPHA * xnt + ga * yt)
        x1 = xh * g + b
        return x1, x1 * (1.0 + sc) + sh

    trd = _row_tile(S, D)
    x1, u = _rowcall("ln1_fwd", ln1_mod, S, trd, [(xn, D, 0, 0, False), (y, D, 0, 0, False)],
                     [vec(g_a), vec(l1g), vec(l1b), vec(sh_m), vec(sc_m)], [(D, f32, False), (D, bf16, False)], [])
    b1, b2 = row(b_mlp1), row(b_mlp2)
    g_w1 = _xchg_wait("ag_w1_wait", ag_w1, True, u)
    hh, act = _mm("mlp1", u, g_w1, "nn", (f32, bf16), extras=[("row", b1)], epilogue=lambda r, b: (r, jnp.square(jnp.maximum(r + b, 0.0))))
    w2_full = _xchg_wait("ag_w2_wait", ag_w2, True, act).reshape(DFF, D)
    y2 = _mm("mlp2", act, w2_full, "nn", f32)

    def ln2_loss(j, x1t, y2t, tg, bb2, gm, g, b):
        y2b = y2t + bb2
        xh, rstd = _ln_stats(ALPHA * x1t + gm * y2b)
        err = xh * g + b - tg
        dx2 = err * (1.0 / D)
        dr2 = _ln_bwd(dx2, xh, rstd, g)
        dy2 = dr2 * gm
        lsum = jnp.broadcast_to(jnp.sum(err * err).reshape(1, 1), (1, LANES))
        return dr2, dy2, _csum(dx2 * xh), _csum(dx2), _csum(dr2 * y2b), _csum(dy2), lsum

    dr2, dy2, d_l2g, d_l2b, d_gm, d_b2, lsum = _rowcall(
        "ln2_loss", ln2_loss, S, trd, [(x1, D, 0, 0, False), (y2, D, 0, 0, False), (tgt, D, 0, 0, False)],
        [vec(b2), vec(g_m), vec(l2g), vec(l2b)], [(D, f32, False), (D, bf16, False)], [(1, D, False)] * 4 + [(1, LANES, False)])
    loss = lax.psum(0.5 * lsum[0, 0] / D, ("x", "y", "c"))

    dw2_p = _mm("mlp2_dw", act, dy2, "tn", bf16)
    a2a_w2 = _xchg_start("a2a_w2_start", dw2_p.reshape(NDEV, DFF // NDEV, D), False, dw2_p)
    dhh = _mm("mlp2_dx", dy2, w2_full, "nt", bf16, extras=[("tile", hh), ("row", _behind(a2a_w2[4], b1))],
              epilogue=lambda r, h, b: (r * (2.0 * jnp.maximum(h + b, 0.0)),))
    trf = _row_tile(S, DFF)
    (d_b1,) = _rowcall("b1_grad", lambda j, a: (_csum(a.astype(f32)),), S, trf, [(dhh, DFF, 0, 0, False)], [], [], [(1, DFF, False)])
    dw1_p = _mm("mlp1_dw", u, dhh, "tn", bf16, out_split=NDEV)
    a2a_w1 = _xchg_start("a2a_w1_start", dw1_p, False, dw1_p)
    du = _mm("mlp1_dx", dhh, g_w1, "nt", f32)

    def ln1_bwd(j, dr2t, dut, xnt, yt, ga, g, b, sc):
        xh, rstd = _ln_stats(ALPHA * xnt + ga * yt)
        x1t = xh * g + b
        dx1 = ALPHA * dr2t + dut * (1.0 + sc)
        dr1 = _ln_bwd(dx1, xh, rstd, g)
        return dr1, dr1 * ga, _csum(dut * x1t), _csum(dut), _csum(dx1 * xh), _csum(dx1), _csum(dr1 * yt)

    dr1, dy, d_scm, d_shm, d_l1g, d_l1b, d_ga = _rowcall(
        "ln1_bwd", ln1_bwd, S, trd, [(dr2, D, 0, 0, False), (du, D, 0, 0, False), (xn, D, 0, 0, False), (y, D, 0, 0, False)],
        [vec(g_a), vec(l1g), vec(l1b), vec(_behind(a2a_w1[4], sc_m))], [(D, f32, False), (D, bf16, False)], [(1, D, False)] * 5)
    dwout_p = _mm("out_proj_dw", mix, dy, "tn", bf16)
    a2a_wout = _xchg_start("a2a_wout_start", dwout_p.reshape(NDEV, (DN + CONF) // NDEV, D), False, dwout_p)
    dmix = _mm("out_proj_dx", dy, wout_full, "nt", f32)

    def conf_ln_b(j, dm, yc, b, g, bb):
        xh, rstd = _ln_stats(yc + b)
        dln = dm * _dsilu(xh * g + bb)
        dyc = _ln_bwd(dln, xh, rstd, g)
        return dyc, _csum(dln * xh), _csum(dln), _csum(dyc)

    dyc, d_clg, d_clb, d_bdw = _rowcall("conf_ln_bwd", conf_ln_b, S, trc, [(dmix, CONF, 0, DN // CONF, False), (yconv, CONF, 0, 0, False)],
                                        [vec(_behind(a2a_wout[4], bdw)), vec(clg), vec(clb)], [(CONF, f32, False)], [(1, CONF, False)] * 3)
    dval, dgate, dw31 = _conf_conv_bwd(h_all, w31, dyc, S, CONF, VB, K31)

    do, dz, d_dng = _gate_norm_bwd(dmix, o5, h_all, dng, NH, S, ZB, trh)
    dqkv, dg5, db5 = _gdn_bwd(qkv, gd, states, do, NH, S, T)
    dh_qkv, dw7 = _qkv_conv_bwd(h_all, w7, dqkv.reshape(2, 3 * NH, R, HD), R, S, NH, K7)
    dgate_cols = jnp.concatenate([dg5[0, :, :, 0].T, db5[0, :, :, 0].T, dg5[1, :, :, 0].T, db5[1, :, :, 0].T,
                                  jnp.zeros((R, LANES - 4 * NH), f32)], axis=1)

    def gates_b(j, hg, dgt, isa, na, dt):
        sg = _sigmoid(hg)
        sp = _softplus(hg + dt)
        dpre = jnp.where(isa > 0.5, dgt * na * _sigmoid(hg + dt), dgt * sg * (1.0 - sg))
        return dpre, _csum(jnp.where(isa > 0.5, dgt * na * sp, 0.0)), _csum(jnp.where(isa > 0.5, dpre, 0.0))

    dh_g, d_alog, d_dt = _rowcall("gates_bwd", gates_b, R, trg, [(h_all, LANES, 0, GB, False), (dgate_cols, LANES, 0, 0, False)],
                                  [vec(is_a), vec(neg_a), vec(dt_v)], [(LANES, bf16, False)], [(1, LANES, False)] * 2)

    zpad = lambda a: jnp.concatenate([a, jnp.zeros((T, a.shape[1]), bf16)], axis=0)
    dh_all = jnp.concatenate([dh_qkv, zpad(dz), zpad(dval), zpad(dgate), dh_g, jnp.zeros((R, GPAD - LANES), bf16)], axis=1)
    dwcat_p = _mm("in_proj_dw", xm_all, dh_all, "tn", bf16, tn=1280, tk=2176)
    dwin = jnp.concatenate([dwcat_p[:, :4 * DN], dwcat_p[:, 4 * DN + 2 * CONF:4 * DN + 2 * CONF + 4 * NH], dwcat_p[:, 4 * DN:4 * DN + 2 * CONF]], axis=1)
    dwin_p = jnp.transpose(dwin.reshape(D, NDEV, INC // NDEV), (1, 0, 2))
    a2a_win = _xchg_start("a2a_win_start", dwin_p, False, dwin_p)
    dxm_all = _mm("in_proj_dx", dh_all, w_cat, "nt", f32, tm=1088, tk=2560)

    def ln_in_bwd_lat(j, xt, dr1t, dxm, g, b, sc):
        xh, rstd = _ln_stats(xt)
        xnt = xh * g + b
        dxn = ALPHA * dr1t + dxm * (1.0 + sc)
        return _ln_bwd(dxn, xh, rstd, g), _csum(dxm * xnt), _csum(dxm), _csum(dxn * xh), _csum(dxn)

    def ln_in_bwd_ctx(j, xt, dxm, g, b, sc):
        xh, rstd = _ln_stats(xt)
        xnt = xh * g + b
        dxn = dxm * (1.0 + sc)
        return _csum(dxm * xnt), _csum(dxm), _csum(dxn * xh), _csum(dxn)

    grad_x, d_sca, d_sha, d_g0a, d_b0a = _rowcall(
        "ln_in_bwd_lat", ln_in_bwd_lat, S, trd, [(x2d, D, 0, 0, False), (dr1, D, 0, 0, False), (dxm_all, D, 0, 0, False)],
        [vec(_behind(a2a_win[4], g0)), vec(b0), vec(sc_a)], [(D, f32, False)], [(1, D, False)] * 4)
    d_csca, d_csha, d_g0b, d_b0b = _rowcall(
        "ln_in_bwd_ctx", ln_in_bwd_ctx, T, tr, [(ctx2d, D, 0, 0, False), (dxm_all, D, S // tr, 0, False)],
        [vec(g0), vec(b0), vec(csc_a)], [], [(1, D, False)] * 4)

    zD = jnp.zeros((1, D), f32)
    dmod_me = jnp.concatenate([d_sha, d_sca, d_ga, d_shm, d_scm, d_gm], axis=1)
    dmodc_me = jnp.concatenate([d_csha, d_csca, zD, zD, zD, zD], axis=1)
    small_names = ["ln_in_g", "ln_in_b", "a_log_f", "dt_bias_f", "a_log_b", "dt_bias_b", "dn_norm_g", "conf_dw_b", "conf_ln_g", "conf_ln_b",
                   "ln1_g", "ln1_b", "b_mlp1", "b_mlp2", "ln2_g", "ln2_b"]
    small_parts = [d_g0a + d_g0b, d_b0a + d_b0b, d_alog[:, 0:NH], d_dt[:, 0:NH], d_alog[:, 2 * NH:3 * NH], d_dt[:, 2 * NH:3 * NH], d_dng, d_bdw, d_clg, d_clb,
                   d_l1g, d_l1b, d_b1, d_b2, d_l2g, d_l2b]
    conv_parts = [dw7[:K7], dw31[:K31]]
    packed = _pack(small_parts + conv_parts + [dmodc_me])
    (pk_all, dmod_all) = _all_gather("ag_small_grads", [packed, dmod_me])
    summed = _sum8("sum_small_grads", pk_all.reshape(NDEV, -1))
    parts_sh = [a.shape for a in small_parts + conv_parts + [dmodc_me]]
    un = _unpack(summed, parts_sh)
    gsmall = dict(zip(small_names, un[:len(small_names)]))
    g_w7_full, g_w31_full, dmodc = un[len(small_names):]
    gsmall["w_qkv_conv"] = lax.dynamic_slice_in_dim(g_w7_full, me * w_qkv_conv.shape[2], w_qkv_conv.shape[2], axis=1)
    gsmall["conf_dw_w"] = lax.dynamic_slice_in_dim(g_w31_full, me * conf_dw_w.shape[2], conf_dw_w.shape[2], axis=1)

    dm16 = jnp.concatenate([dmod_all.reshape(NDEV, 6 * D), dmodc, jnp.zeros((7, 6 * D), f32)], axis=0)
    (gbmod,) = _rowcall("bmod_grad", lambda j, a: (_csum(a),), 16, 16, [(dm16, 6 * D, 0, 0, False)], [], [], [(1, 6 * D, False)])
    gsmall["b_mod"] = gbmod
    dm16_mine = lax.dynamic_slice_in_dim(dm16, me * MODC, MODC, axis=1)
    g_wmod = _mm("mod_dw", sc16, dm16_mine, "tn", f32)
    dsc16_part = _mm("mod_dx", dm16_mine, w_mod[0], "nt", f32)
    (dsc_all,) = _all_gather("ag_cctx", [dsc16_part[8:9]])
    dsilu_cctx = _sum8("sum_cctx", dsc_all.reshape(NDEV, D))
    (g_cctx,) = _rowcall("cctx_grad", lambda j, ds, cc: (ds * _dsilu(cc),), 1, 1, [(dsilu_cctx, D, 0, 0, False), (c_ctx.reshape(1, D).astype(f32), D, 0, 0, False)],
                         [], [(D, f32, False)], [])
    gsmall["c_ctx"] = g_cctx

    grads, deltas, new_m, new_v = {}, {}, {}, {}
    res = _adam("adam_w_mod", w_mod[0], m_w_mod[0], v_w_mod[0], g_wmod, False)
    grads["w_mod"], deltas["w_mod"], new_m["w_mod"], new_v["w_mod"] = [a.reshape(w_mod.shape) for a in res]
    after = res[1]
    for nm, started in (("w_mlp2", a2a_w2), ("w_mlp1", a2a_w1), ("w_out", a2a_wout), ("w_in", a2a_win)):
        w3 = weights[nm]
        g = _xchg_wait("a2a_" + nm + "_wait", started, False, after)
        res = _adam("adam_" + nm, w3[0], mom1[nm][0], mom2[nm][0], g, True)
        grads[nm], deltas[nm], new_m[nm], new_v[nm] = [a.reshape(w3.shape) for a in res]
        after = res[1]
    snames = [n for n in names if n not in grads]
    res = _adam("adam_small", _pack([weights[n] for n in snames]), _pack([mom1[n] for n in snames]), _pack([mom2[n] for n in snames]),
                _pack([gsmall[n] for n in snames]), False)
    shapes = [weights[n].shape for n in snames]
    for dst, packed_out in zip((grads, deltas, new_m, new_v), res):
        for n, a in zip(snames, _unpack(packed_out, shapes)):
            dst[n] = a

    return (loss, grad_x.reshape(x.shape), *[grads[n] for n in names], *[deltas[n] for n in names],
            *[new_m[n] for n in names], *[new_v[n] for n in names])
```

```python
import functools

import jax
import jax.numpy as jnp
from jax import lax
from jax.experimental import pallas as pl
from jax.experimental.pallas import tpu as pltpu

f32 = jnp.float32
bf16 = jnp.bfloat16
MESH = pl.DeviceIdType.MESH

NDEV = 8
HD = 128
CH = 64
GW = 64
LANES = 128
GPAD = 512
LN_EPS = 1e-5
ALPHA = 2.0 ** 0.25
ADAM_LR, ADAM_B1, ADAM_B2, ADAM_EPS, ADAM_WD, ADAM_STEP = 0.001, 0.9, 0.999, 1e-08, 0.01, 10
VMEM_LIMIT = 56 * 1024 * 1024
ROW_BLOCK_BYTES = 1 << 20


def _cparams(sem):
    return pltpu.CompilerParams(dimension_semantics=sem, vmem_limit_bytes=VMEM_LIMIT)


def _tile(dim, pref, align):
    t = min(pref, dim)
    t -= t % align
    while t >= align:
        if dim % t == 0:
            return t
        t -= align
    return dim


def _row_tile(nrows, width, cap=256):
    t = max(16, min(cap, ROW_BLOCK_BYTES // (4 * width)))
    t = 1 << (t.bit_length() - 1)
    while nrows % t:
        t //= 2
    return max(t, 1)


def _sigmoid(x):
    return 1.0 / (1.0 + jnp.exp(-x))


def _silu(x):
    return x * _sigmoid(x)


def _dsilu(x):
    s = _sigmoid(x)
    return s * (1.0 + x * (1.0 - s))


def _softplus(x):
    return jnp.maximum(x, 0.0) + jnp.log(1.0 + jnp.exp(-jnp.abs(x)))


def _ln_stats(r):
    mu = jnp.mean(r, axis=-1, keepdims=True)
    xc = r - mu
    rstd = lax.rsqrt(jnp.mean(xc * xc, axis=-1, keepdims=True) + LN_EPS)
    return xc * rstd, rstd


def _ln_bwd(dy, xhat, rstd, g):
    dxh = dy * g
    return rstd * (dxh - jnp.mean(dxh, axis=-1, keepdims=True) - xhat * jnp.mean(dxh * xhat, axis=-1, keepdims=True))


def _csum(a):
    return jnp.sum(a, axis=0, keepdims=True)


def _mm(name, a, b, mode, out_dtype, out_split=1, tm=1024, tn=1024, tk=2048, extras=(), epilogue=None, after=None):
    if mode == "tn":
        K, M = a.shape
    else:
        M, K = a.shape
    bp = b.shape[0] if b.ndim == 3 else 1
    brows, bcols = b.shape[-2], b.shape[-1] * bp
    N = brows if mode == "nt" else bcols
    assert K == (bcols if mode == "nt" else brows), (name, a.shape, b.shape)
    tm = _tile(M, tm, LANES if mode == "tn" else 16)
    nsplit = max(out_split, bp if mode != "nt" else 1)
    tn = _tile(N // nsplit, tn, LANES)
    tk = _tile(K // (bp if mode == "nt" else 1), tk, LANES)
    nk = K // tk
    dims = {"nn": (((1,), (0,)), ((), ())), "nt": (((1,), (1,)), ((), ())), "tn": (((0,), (0,)), ((), ()))}[mode]
    out_dtypes = out_dtype if isinstance(out_dtype, tuple) else (out_dtype,)

    a_spec = pl.BlockSpec((tk, tm), lambda i, j, k: (k, i)) if mode == "tn" else pl.BlockSpec((tm, tk), lambda i, j, k: (i, k))
    if b.ndim == 3:
        if mode == "nt":
            per = (K // bp) // tk
            b_spec = pl.BlockSpec((None, tn, tk), lambda i, j, k: (k // per, j, k % per))
        else:
            per = (N // bp) // tn
            b_spec = pl.BlockSpec((None, tk, tn), lambda i, j, k: (j // per, k, j % per))
    elif mode == "nt":
        b_spec = pl.BlockSpec((tn, tk), lambda i, j, k: (j, k))
    else:
        b_spec = pl.BlockSpec((tk, tn), lambda i, j, k: (k, j))
    if out_split > 1:
        pero = (N // out_split) // tn
        o_spec = pl.BlockSpec((None, tm, tn), lambda i, j, k: (j // pero, i, j % pero))
        o_shapes = [jax.ShapeDtypeStruct((out_split, M, N // out_split), dt) for dt in out_dtypes]
    else:
        o_spec = pl.BlockSpec((tm, tn), lambda i, j, k: (i, j))
        o_shapes = [jax.ShapeDtypeStruct((M, N), dt) for dt in out_dtypes]
    in_specs = [a_spec, b_spec]
    args = [a, b]
    for kind, arr in extras:
        in_specs.append(pl.BlockSpec((1, tn), lambda i, j, k: (0, j)) if kind == "row" else pl.BlockSpec((tm, tn), lambda i, j, k: (i, j)))
        args.append(arr)
    n_in, n_out = len(args), len(out_dtypes)
    if after is not None:
        in_specs.append(pl.BlockSpec(memory_space=pl.ANY))
        args.append(after)
    n_all = len(args)

    def finish(refs, r):
        outs = (r,) if epilogue is None else epilogue(r, *[e[...] for e in refs[2:n_in]])
        for o_ref, val in zip(refs[n_all:n_all + n_out], outs):
            o_ref[...] = val.astype(o_ref.dtype)

    def body(*refs):
        part = lax.dot_general(refs[0][...].astype(bf16), refs[1][...].astype(bf16), dims, preferred_element_type=f32)
        if nk == 1:
            finish(refs, part)
            return
        acc = refs[-1]
        k = pl.program_id(2)

        @pl.when(k == 0)
        def _():
            acc[...] = part

        @pl.when(jnp.logical_and(k > 0, k < nk - 1))
        def _():
            acc[...] += part

        @pl.when(k == nk - 1)
        def _():
            finish(refs, acc[...] + part)

    res = pl.pallas_call(
        body, name=name, grid=(M // tm, N // tn, nk), in_specs=in_specs, out_specs=[o_spec] * n_out, out_shape=o_shapes,
        scratch_shapes=[pltpu.VMEM((tm, tn), f32)] if nk > 1 else [], compiler_params=_cparams(("parallel", "parallel", "arbitrary")),
    )(*args)
    return res[0] if n_out == 1 else res


def _rowcall(name, fn, nrows, tr, rows_in, vecs_in, rows_out, accs_out, ncol=1):
    nrt = nrows // tr
    in_specs, args = [], []
    for arr, w, ro, co, pc in rows_in:
        in_specs.append(pl.BlockSpec((tr, w), functools.partial(lambda j, i, ro, co, pc: (i + ro, co + (j if pc else 0)), ro=ro, co=co, pc=pc)))
        args.append(arr)
    for arr, w, co, pc in vecs_in:
        in_specs.append(pl.BlockSpec((arr.shape[0], w), functools.partial(lambda j, i, co, pc: (0, co + (j if pc else 0)), co=co, pc=pc)))
        args.append(arr)
    out_specs, out_shape = [], []
    for w, dt, pc in rows_out:
        out_specs.append(pl.BlockSpec((tr, w), functools.partial(lambda j, i, pc: (i, j if pc else 0), pc=pc)))
        out_shape.append(jax.ShapeDtypeStruct((nrows, w * (ncol if pc else 1)), dt))
    for k, w, pc in accs_out:
        out_specs.append(pl.BlockSpec((k, w), functools.partial(lambda j, i, pc: (0, j if pc else 0), pc=pc)))
        out_shape.append(jax.ShapeDtypeStruct((k, w * (ncol if pc else 1)), f32))
    n_in, n_ro = len(args), len(rows_out)

    def body(*refs):
        j, i = pl.program_id(0), pl.program_id(1)
        outs = fn(j, *[r[...] for r in refs[:n_in]])
        for r, val in zip(refs[n_in:n_in + n_ro], outs[:n_ro]):
            r[...] = val.astype(r.dtype)
        for (k, w, pc), r, val in zip(accs_out, refs[n_in + n_ro:], outs[n_ro:]):
            first = (i == 0) if pc else jnp.logical_and(i == 0, j == 0)

            @pl.when(first)
            def _(r=r):
                r[...] = jnp.zeros_like(r)

            r[...] += val

    res = pl.pallas_call(
        body, name=name, grid=(ncol, nrt), in_specs=in_specs, out_specs=out_specs, out_shape=out_shape,
        compiler_params=_cparams(("arbitrary", "arbitrary")),
    )(*args)
    return res


def _tap_valid(mode, t, d, nrows, nlat):
    if mode == "seg":
        tp = t + d
        return (tp >= 0) & (tp < nrows) & ((t < nlat) == (tp < nlat))
    if mode == "row":
        p = (t & (GW - 1)) + d
        return (p >= 0) & (p < GW)
    tp = t + d * GW
    return (tp >= 0) & (tp < nrows)


def _conv(x, w_ref, ktaps, mode, nlat, flip=False):
    nrows = x.shape[0]
    stride = GW if mode == "col" else 1
    t = lax.broadcasted_iota(jnp.int32, (nrows, 1), 0)
    acc = jnp.zeros_like(x)
    for j in range(ktaps):
        d = j - ktaps // 2
        jj = ktaps - 1 - j if flip else j
        wj = w_ref[jj:jj + 1, :]
        if d == 0:
            acc = acc + x * wj
        else:
            xs = pltpu.roll(x, (-d * stride) % nrows, 0)
            acc = acc + jnp.where(_tap_valid(mode, t, d, nrows, nlat), xs * wj, 0.0)
    return acc


def _conv_wgrad(dy, x, dw_ref, ktaps, mode, nlat):
    nrows = x.shape[0]
    stride = GW if mode == "col" else 1
    t = lax.broadcasted_iota(jnp.int32, (nrows, 1), 0)
    dw_ref[...] = jnp.zeros_like(dw_ref)
    for j in range(ktaps):
        d = j - ktaps // 2
        if d == 0:
            prod = x * dy
        else:
            xs = pltpu.roll(x, (-d * stride) % nrows, 0)
            prod = jnp.where(_tap_valid(mode, t, d, nrows, nlat), xs * dy, 0.0)
        dw_ref[j:j + 1, :] = _csum(prod)


def _qkv_post(j, pre, nh):
    a = _silu(pre)
    inv = lax.rsqrt(jnp.sum(a * a, axis=-1, keepdims=True) + 1e-6)
    scale = jnp.where(j < nh, HD ** -0.5, 1.0).astype(f32)
    nrm = jnp.where(j < 2 * nh, inv, 1.0) * scale
    return a, inv, nrm


def _qkv_conv_fwd(h_all, w7, nrows, nlat, nh, ktaps):
    ntile = 3 * nh

    def body(h_ref, w_ref, o_ref):
        j = pl.program_id(0)
        pre = _conv(h_ref[...], w_ref, ktaps, "seg", nlat)
        a, _, nrm = _qkv_post(j, pre, nh)
        o_ref[...] = a * nrm

    return pl.pallas_call(
        body, name="qkv_conv_fwd", grid=(ntile,),
        in_specs=[pl.BlockSpec((nrows, HD), lambda j: (0, j)), pl.BlockSpec((w7.shape[0], HD), lambda j: (0, j))],
        out_specs=pl.BlockSpec((None, nrows, HD), lambda j: (j, 0, 0)),
        out_shape=jax.ShapeDtypeStruct((ntile, nrows, HD), f32), compiler_params=_cparams(("parallel",)),
    )(h_all, w7)


def _qkv_conv_bwd(h_all, w7, dqkv, nrows, nlat, nh, ktaps):
    ntile = 3 * nh

    def body(h_ref, w_ref, d0_ref, d1_ref, dh_ref, dw_ref):
        j = pl.program_id(0)
        hx = h_ref[...]
        pre = _conv(hx, w_ref, ktaps, "seg", nlat)
        a, inv, nrm = _qkv_post(j, pre, nh)
        dn = (d0_ref[...] + d1_ref[...]) * jnp.where(j < nh, HD ** -0.5, 1.0).astype(f32)
        n = a * inv
        da_norm = inv * (dn - n * jnp.sum(dn * n, axis=-1, keepdims=True))
        da = jnp.where(j < 2 * nh, da_norm, dn)
        dpre = da * _dsilu(pre)
        dh_ref[...] = _conv(dpre, w_ref, ktaps, "seg", nlat, flip=True).astype(dh_ref.dtype)
        _conv_wgrad(dpre, hx, dw_ref, ktaps, "seg", nlat)

    blk = pl.BlockSpec((nrows, HD), lambda j: (0, j))
    wblk = pl.BlockSpec((w7.shape[0], HD), lambda j: (0, j))
    dblk = lambda d: pl.BlockSpec((None, None, nrows, HD), lambda j: (d, j, 0, 0))
    return pl.pallas_call(
        body, name="qkv_conv_bwd", grid=(ntile,), in_specs=[blk, wblk, dblk(0), dblk(1)], out_specs=[blk, wblk],
        out_shape=[jax.ShapeDtypeStruct((nrows, ntile * HD), bf16), jax.ShapeDtypeStruct((w7.shape[0], ntile * HD), f32)],
        compiler_params=_cparams(("parallel",)),
    )(h_all, w7, dqkv, dqkv)


def _conf_conv_fwd(h_all, w31, nlat, conf, val_blk, ktaps):
    nt = conf // LANES
    nhalf = nt // 2

    def body(v_ref, g_ref, w_ref, o_ref):
        j = pl.program_id(0)
        glu = v_ref[...] * _sigmoid(g_ref[...])

        @pl.when(j < nhalf)
        def _():
            o_ref[...] = _conv(glu, w_ref, ktaps, "row", nlat)

        @pl.when(j >= nhalf)
        def _():
            o_ref[...] = _conv(glu, w_ref, ktaps, "col", nlat)

    return pl.pallas_call(
        body, name="conf_conv_fwd", grid=(nt,),
        in_specs=[pl.BlockSpec((nlat, LANES), lambda j: (0, val_blk + j)), pl.BlockSpec((nlat, LANES), lambda j: (0, val_blk + nt + j)),
                  pl.BlockSpec((w31.shape[0], LANES), lambda j: (0, j))],
        out_specs=pl.BlockSpec((nlat, LANES), lambda j: (0, j)),
        out_shape=jax.ShapeDtypeStruct((nlat, conf), f32), compiler_params=_cparams(("parallel",)),
    )(h_all, h_all, w31)


def _conf_conv_bwd(h_all, w31, dyc, nlat, conf, val_blk, ktaps):
    nt = conf // LANES
    nhalf = nt // 2

    def body(v_ref, g_ref, w_ref, dy_ref, dv_ref, dg_ref, dw_ref):
        j = pl.program_id(0)
        val, sg = v_ref[...], _sigmoid(g_ref[...])
        glu = val * sg
        dy = dy_ref[...]

        def run(mode):
            dglu = _conv(dy, w_ref, ktaps, mode, nlat, flip=True)
            dv_ref[...] = (dglu * sg).astype(dv_ref.dtype)
            dg_ref[...] = (dglu * val * sg * (1.0 - sg)).astype(dg_ref.dtype)
            _conv_wgrad(dy, glu, dw_ref, ktaps, mode, nlat)

        @pl.when(j < nhalf)
        def _():
            run("row")

        @pl.when(j >= nhalf)
        def _():
            run("col")

    blk = pl.BlockSpec((nlat, LANES), lambda j: (0, j))
    wblk = pl.BlockSpec((w31.shape[0], LANES), lambda j: (0, j))
    return pl.pallas_call(
        body, name="conf_conv_bwd", grid=(nt,),
        in_specs=[pl.BlockSpec((nlat, LANES), lambda j: (0, val_blk + j)), pl.BlockSpec((nlat, LANES), lambda j: (0, val_blk + nt + j)), wblk, blk],
        out_specs=[blk, blk, wblk],
        out_shape=[jax.ShapeDtypeStruct((nlat, conf), bf16), jax.ShapeDtypeStruct((nlat, conf), bf16),
                   jax.ShapeDtypeStruct((w31.shape[0], conf), f32)],
        compiler_params=_cparams(("parallel",)),
    )(h_all, h_all, w31, dyc)


def _bd(eq, a, b):
    return jnp.einsum(eq, a.astype(bf16), b.astype(bf16), preferred_element_type=f32)


def _split(x, pieces):
    out = []
    for _ in range(pieces - 1):
        p = x.astype(bf16)
        out.append(p)
        x = x - p.astype(f32)
    return out + [x.astype(bf16)]


def _h3(eq, a, b):
    (ah, al), (bh, bl) = _split(a, 2), _split(b, 2)
    d = lambda p, q: jnp.einsum(eq, p, q, preferred_element_type=f32)
    return d(ah, bh) + (d(ah, bl) + d(al, bh))


def _x3(eq, a, b):
    d = lambda p, q: jnp.einsum(eq, p, q, preferred_element_type=f32)
    ae, (b1, b2, b3) = a.astype(bf16), _split(b, 3)
    return d(ae, b1) + (d(ae, b2) + d(ae, b3))


def _gdn_chunk(q, k, v, gt, nh, rev):
    g = jnp.stack([jnp.broadcast_to(gt[:, h:h + 1], (CH, LANES)) for h in range(nh)])
    beta = jnp.stack([jnp.broadcast_to(gt[:, nh + h:nh + h + 1], (CH, LANES)) for h in range(nh)])
    ii = lax.broadcasted_iota(jnp.int32, (CH, CH), 0)
    jj = lax.broadcasted_iota(jnp.int32, (CH, CH), 1)
    si, sj = jnp.where(rev, jj, ii), jnp.where(rev, ii, jj)
    tril, stril = (si >= sj)[None], (si > sj)[None]
    trilf = jnp.broadcast_to(tril.astype(f32), (nh, CH, CH))
    triuf = jnp.broadcast_to((sj >= si)[None].astype(f32), (nh, CH, CH))
    gc = _x3("hit,htl->hil", trilf, g)
    gc_row = _x3("hil,hjl->hij", jnp.full((nh, CH, LANES), 1.0 / LANES, f32), gc)
    diff = gc[:, :, :CH] - gc_row
    gam = jnp.where(tril, jnp.exp(jnp.where(tril, diff, 0.0)), 0.0)
    e = jnp.exp(gc)
    gl = jnp.where(rev, gc[:, 0:1, :], gc[:, CH - 1:CH, :])
    el = jnp.exp(gl)
    r = jnp.exp(gl - gc)
    kb, vb = k * beta, v * beta
    kbe = kb * e
    amat = jnp.where(stril, _bd("hik,hjk->hij", kb, k) * gam, 0.0)
    eye = (ii == jj).astype(f32)[None]
    xp = -amat
    tinv = eye + xp
    for _ in range(5):
        xp = _h3("hij,hjk->hik", xp, xp)
        tinv = tinv + _h3("hij,hjk->hik", tinv, xp)
    u = _h3("hij,hjv->hiv", tinv, vb)
    w = _h3("hij,hjk->hik", tinv, kbe)
    pmat = jnp.where(tril, _bd("hik,hjk->hij", q, k) * gam, 0.0)
    return dict(beta=beta, gam=gam, e=e, el=el, r=r, kb=kb, vb=vb, kbe=kbe, amat=amat, tinv=tinv, u=u, w=w, pmat=pmat,
                qd=q * e, kd=k * r, tril=tril, stril=stril, triuf=triuf)


def _scan_row_chunk(d, n, ns, nt):
    fwd = jnp.where(n < nt, ns + n, n - nt)
    return jnp.where(d == 0, fwd, ns + nt - 1 - n)


def _gdn_fwd(qkv, gd, nh, nlat, nctx):
    nrows = nlat + nctx
    ns, nt = nlat // CH, nctx // CH
    nch = ns + nt

    def body(qkv_ref, g_ref, o_ref, st_ref, state):
        d, n = pl.program_id(0), pl.program_id(1)

        @pl.when(n == 0)
        def _():
            state[...] = jnp.zeros_like(state)

        c = _gdn_chunk(qkv_ref[0], qkv_ref[1], qkv_ref[2], g_ref[...], nh, d == 1)
        s = state[...]
        vn = c["u"] - _bd("hck,hkv->hcv", c["w"], s)
        o_ref[...] = _bd("hck,hkv->hcv", c["qd"], s) + _bd("hcd,hdv->hcv", c["pmat"], vn)
        st_ref[...] = s
        state[...] = s * c["el"] + _bd("hck,hcv->hkv", c["kd"], vn)

    return pl.pallas_call(
        body, name="gdn_fwd", grid=(2, nch),
        in_specs=[pl.BlockSpec((3, nh, CH, HD), lambda d, n: (0, 0, _scan_row_chunk(d, n, ns, nt), 0)),
                  pl.BlockSpec((None, CH, LANES), lambda d, n: (d, _scan_row_chunk(d, n, ns, nt), 0))],
        out_specs=[pl.BlockSpec((None, nh, CH, HD), lambda d, n: (d, 0, _scan_row_chunk(d, n, ns, nt), 0)),
                   pl.BlockSpec((None, None, nh, HD, HD), lambda d, n: (d, n, 0, 0, 0))],
        out_shape=[jax.ShapeDtypeStruct((2, nh, nrows, HD), f32), jax.ShapeDtypeStruct((2, nch, nh, HD, HD), f32)],
        scratch_shapes=[pltpu.VMEM((nh, HD, HD), f32)], compiler_params=_cparams(("arbitrary", "arbitrary")),
    )(qkv, gd)


def _gdn_bwd(qkv, gd, states, do, nh, nlat, nctx):
    nrows = nlat + nctx
    ns, nt = nlat // CH, nctx // CH
    nch = ns + nt

    def body(qkv_ref, g_ref, st_ref, do_ref, dqkv_ref, dgd_ref, dstate):
        d, step = pl.program_id(0), pl.program_id(1)

        @pl.when(step == 0)
        def _():
            dstate[...] = jnp.zeros_like(dstate)

        q, k, v = qkv_ref[0], qkv_ref[1], qkv_ref[2]
        c = _gdn_chunk(q, k, v, g_ref[...], nh, d == 1)
        in_ctx = (nch - 1 - step) < nt
        s, dsp = st_ref[...], dstate[...]
        dout = jnp.where(in_ctx, 0.0, do_ref[...])
        beta, gam, e, el, r = c["beta"], c["gam"], c["e"], c["el"], c["r"]
        tinv, u, w, pmat, amat = c["tinv"], c["u"], c["w"], c["pmat"], c["amat"]
        vn = u - _bd("hck,hkv->hcv", w, s)
        dvn = _bd("hdc,hdv->hcv", pmat, dout) + _bd("hck,hkv->hcv", c["kd"], dsp)
        dp = jnp.where(c["tril"], _bd("hcv,hdv->hcd", dout, vn), 0.0)
        dqd = _bd("hcv,hkv->hck", dout, s)
        dkd = _bd("hcv,hkv->hck", vn, dsp)
        dstate[...] = _bd("hck,hcv->hkv", c["qd"], dout) + dsp * el - _bd("hck,hcv->hkv", w, dvn)
        del_ = jnp.sum(jnp.sum(s * dsp, axis=2, keepdims=True), axis=1, keepdims=True)
        dw = -_bd("hcv,hkv->hck", dvn, s)
        dvb = _h3("hji,hjv->hiv", tinv, dvn)
        dkbe = _h3("hji,hjk->hik", tinv, dw)
        z = _bd("hiv,hjv->hij", dvn, u) + _bd("hik,hjk->hij", dw, w)
        da = jnp.where(c["stril"], -_h3("hji,hjl->hil", tinv, z), 0.0)
        dm = da * gam
        dkb = _bd("hij,hjk->hik", dm, k) + dkbe * e
        dn = dp * gam
        dqkv_ref[0] = _bd("hij,hjk->hik", dn, k) + dqd * e
        dqkv_ref[1] = _bd("hji,hjk->hik", dm, c["kb"]) + _bd("hji,hjk->hik", dn, q) + dkd * r + dkb * beta
        dqkv_ref[2] = dvb * beta
        gmat = da * amat + dp * pmat
        rows_minus_cols = jnp.sum(gmat, axis=-1, keepdims=True) - jnp.sum(jnp.swapaxes(gmat, 1, 2), axis=-1, keepdims=True)
        de = jnp.sum(dqd * q + dkbe * c["kb"], axis=-1, keepdims=True)
        drr = jnp.sum(dkd * k, axis=-1, keepdims=True) * r
        dgc = rows_minus_cols + de * e - drr
        dgl = jnp.sum(drr, axis=1, keepdims=True) + del_ * el
        row = lax.broadcasted_iota(jnp.int32, (1, CH, 1), 1)
        total_row = row == jnp.where(d == 1, 0, CH - 1)
        dgc = dgc + jnp.where(total_row, dgl, 0.0)
        dg = _x3("hij,hjl->hil", c["triuf"], dgc)
        db = jnp.sum(dvb * v + dkb * k, axis=-1, keepdims=True)
        lane = lax.broadcasted_iota(jnp.int32, (CH, LANES), 1)
        packed = jnp.zeros((CH, LANES), f32)
        for h in range(nh):
            packed = packed + jnp.where(lane == h, dg[h], 0.0) + jnp.where(lane == nh + h, db[h], 0.0)
        dgd_ref[...] = packed

    def rc(d, n):
        return _scan_row_chunk(d, nch - 1 - n, ns, nt)

    gblk = pl.BlockSpec((None, CH, LANES), lambda d, n: (d, rc(d, n), 0))
    return pl.pallas_call(
        body, name="gdn_bwd", grid=(2, nch),
        in_specs=[pl.BlockSpec((3, nh, CH, HD), lambda d, n: (0, 0, rc(d, n), 0)),
                  gblk,
                  pl.BlockSpec((None, None, nh, HD, HD), lambda d, n: (d, nch - 1 - n, 0, 0, 0)),
                  pl.BlockSpec((nh, CH, HD), lambda d, n: (0, jnp.minimum(rc(d, n), ns - 1), 0))],
        out_specs=[pl.BlockSpec((None, 3, nh, CH, HD), lambda d, n: (d, 0, 0, rc(d, n), 0)), gblk],
        out_shape=[jax.ShapeDtypeStruct((2, 3, nh, nrows, HD), f32), jax.ShapeDtypeStruct((2, nrows, LANES), f32)],
        scratch_shapes=[pltpu.VMEM((nh, HD, HD), f32)], compiler_params=_cparams(("arbitrary", "arbitrary")),
    )(qkv, gd, states, do)


def _gate_norm_fwd(o5, h_all, dng, nh, nlat, zblk, tr):
    def body(of_ref, ob_ref, z_ref, g_ref, out_ref):
        o = of_ref[...] + ob_ref[...]
        rs = lax.rsqrt(jnp.mean(o * o, axis=-1, keepdims=True) + 1e-6)
        out_ref[...] = ((o * rs * g_ref[...]) * _silu(z_ref[...])).astype(out_ref.dtype)

    return pl.pallas_call(
        body, name="gate_norm_fwd", grid=(nh, nlat // tr),
        in_specs=[pl.BlockSpec((None, None, tr, HD), lambda j, i: (0, j, i, 0)), pl.BlockSpec((None, None, tr, HD), lambda j, i: (1, j, i, 0)),
                  pl.BlockSpec((tr, HD), lambda j, i: (i, zblk + j)), pl.BlockSpec((1, HD), lambda j, i: (0, 0))],
        out_specs=pl.BlockSpec((tr, HD), lambda j, i: (i, j)), out_shape=jax.ShapeDtypeStruct((nlat, nh * HD), bf16),
        compiler_params=_cparams(("parallel", "parallel")),
    )(o5, o5, h_all, dng)


def _gate_norm_bwd(dmix, o5, h_all, dng, nh, nlat, zblk, tr):
    def body(dm_ref, of_ref, ob_ref, z_ref, g_ref, do_ref, dz_ref, dg_ref):
        o = of_ref[...] + ob_ref[...]
        z, g, dm = z_ref[...], g_ref[...], dm_ref[...]
        rs = lax.rsqrt(jnp.mean(o * o, axis=-1, keepdims=True) + 1e-6)
        don = dm * _silu(z)
        dz_ref[...] = (dm * (o * rs * g) * _dsilu(z)).astype(dz_ref.dtype)
        dog = don * g
        do_ref[...] = rs * (dog - o * rs * rs * jnp.mean(dog * o, axis=-1, keepdims=True))

        @pl.when(jnp.logical_and(pl.program_id(0) == 0, pl.program_id(1) == 0))
        def _():
            dg_ref[...] = jnp.zeros_like(dg_ref)

        dg_ref[...] += _csum(don * o * rs)

    return pl.pallas_call(
        body, name="gate_norm_bwd", grid=(nh, nlat // tr),
        in_specs=[pl.BlockSpec((tr, HD), lambda j, i: (i, j)),
                  pl.BlockSpec((None, None, tr, HD), lambda j, i: (0, j, i, 0)), pl.BlockSpec((None, None, tr, HD), lambda j, i: (1, j, i, 0)),
                  pl.BlockSpec((tr, HD), lambda j, i: (i, zblk + j)), pl.BlockSpec((1, HD), lambda j, i: (0, 0))],
        out_specs=[pl.BlockSpec((None, tr, HD), lambda j, i: (j, i, 0)), pl.BlockSpec((tr, HD), lambda j, i: (i, j)),
                   pl.BlockSpec((1, HD), lambda j, i: (0, 0))],
        out_shape=[jax.ShapeDtypeStruct((nh, nlat, HD), f32), jax.ShapeDtypeStruct((nlat, nh * HD), bf16), jax.ShapeDtypeStruct((1, HD), f32)],
        compiler_params=_cparams(("arbitrary", "arbitrary")),
    )(dmix, o5, o5, h_all, dng)


def _mesh_pos():
    return lax.axis_index("x"), lax.axis_index("y"), lax.axis_index("c")


def _lin(p):
    return 4 * p[0] + 2 * p[1] + p[2]


def _all_gather(name, xs, after=None):
    nx = len(xs)
    extra = [] if after is None else [after]

    def body(*refs):
        xr, outr = refs[:nx], refs[nx + len(extra):2 * nx + len(extra)]
        send, recv, loc = refs[2 * nx + len(extra):]
        x, y, c = _mesh_pos()
        me, sib = (x, y, c), (x, y, 1 - c)
        chips = [(1 - x, y), (x, 1 - y), (1 - x, 1 - y)]

        def cp(l, k, block, to, src=None):
            rows = outr[l].at[_lin(block)]
            return pltpu.make_async_remote_copy(src_ref=rows if src is None else src, dst_ref=rows, send_sem=send.at[l, k],
                                                recv_sem=recv.at[l, k], device_id=to, device_id_type=MESH)

        mine = [pltpu.make_async_copy(xr[l], outr[l].at[_lin(me)], loc.at[l]) for l in range(nx)]
        for m in mine:
            m.start()
        first = []
        for l in range(nx):
            first.append(cp(l, 0, me, sib, src=xr[l]))
            first += [cp(l, 1 + j, me, (*chip, c), src=xr[l]) for j, chip in enumerate(chips)]
        for f in first:
            f.start()
        passed = []
        for l in range(nx):
            for j, chip in enumerate(chips):
                cp(l, 1 + j, (*chip, c), me).wait_recv()
                fwd = cp(l, 4 + j, (*chip, c), sib)
                fwd.start()
                passed.append(fwd)
        for l in range(nx):
            cp(l, 0, sib, me).wait_recv()
            for j, chip in enumerate(chips):
                cp(l, 4 + j, (*chip, 1 - c), me).wait_recv()
        for f in first + passed:
            f.wait_send()
        for m in mine:
            m.wait()

    anyspec = pl.BlockSpec(memory_space=pl.ANY)
    return pl.pallas_call(
        body, name=name, in_specs=[anyspec] * (nx + len(extra)), out_specs=[anyspec] * nx,
        out_shape=[jax.ShapeDtypeStruct((NDEV,) + a.shape, a.dtype) for a in xs],
        scratch_shapes=[pltpu.SemaphoreType.DMA((nx, 7)), pltpu.SemaphoreType.DMA((nx, 7)), pltpu.SemaphoreType.DMA((nx,))],
    )(*xs, *extra)


HBM_SPEC = pl.BlockSpec(memory_space=pltpu.HBM)
SEM_SPEC = pl.BlockSpec(memory_space=pltpu.SEMAPHORE)
ANY_SPEC = pl.BlockSpec(memory_space=pl.ANY)
EFFECT = pltpu.SideEffectType.DATAFLOW_SIDE_EFFECTING


def _peer(rel):
    x, y, c = _mesh_pos()
    return (1 - x if rel & 4 else x, 1 - y if rel & 2 else y, 1 - c if rel & 1 else c)


def _xchg_copy(x_ref, land_ref, send, recv, rel, gather, slot):
    p = _peer(rel)
    return pltpu.make_async_remote_copy(src_ref=x_ref if gather else x_ref.at[_lin(p)], dst_ref=land_ref.at[slot(p)],
                                        send_sem=send.at[rel - 1], recv_sem=recv.at[rel - 1], device_id=p, device_id_type=MESH)


def _xchg_start(name, x, gather, after):
    me = _lin(_mesh_pos())
    shape = (NDEV,) + x.shape if gather else x.shape
    own = x if gather else lax.dynamic_index_in_dim(x, me, 0, keepdims=False)
    land = lax.dynamic_update_index_in_dim(lax.empty(shape, x.dtype), own, me, 0)

    def body(x_ref, land_ref, after_ref, send, recv, x_thru, land_thru, token):
        mine = _lin(_mesh_pos())
        for rel in range(1, NDEV):
            _xchg_copy(x_ref, land_ref, send, recv, rel, gather, lambda p: mine).start()
        token[...] = jnp.zeros_like(token)

    return pl.pallas_call(
        body, name=name,
        out_shape=(pltpu.SemaphoreType.DMA((NDEV - 1,)), pltpu.SemaphoreType.DMA((NDEV - 1,)), pltpu.HBM(x.shape, x.dtype),
                   pltpu.HBM(shape, x.dtype), jax.ShapeDtypeStruct((8, LANES), f32)),
        in_specs=(HBM_SPEC, HBM_SPEC, ANY_SPEC), out_specs=(SEM_SPEC, SEM_SPEC, HBM_SPEC, HBM_SPEC, pl.BlockSpec(memory_space=pltpu.VMEM)),
        input_output_aliases={0: 2, 1: 3}, compiler_params=pltpu.CompilerParams(has_side_effects=EFFECT),
    )(pltpu.with_memory_space_constraint(x, pltpu.HBM), pltpu.with_memory_space_constraint(land, pltpu.HBM), after)


def _xchg_wait(name, started, gather, after):
    send, recv, x_thru, land_thru, _ = started

    def body(x_ref, land_ref, send, recv, after_ref, x_dead, got_ref):
        for rel in range(1, NDEV):
            cp = _xchg_copy(x_ref, land_ref, send, recv, rel, gather, _lin)
            cp.wait_send()
            cp.wait_recv()

    return pl.pallas_call(
        body, name=name, out_shape=(pltpu.HBM(x_thru.shape, x_thru.dtype), pltpu.HBM(land_thru.shape, land_thru.dtype)),
        in_specs=(HBM_SPEC, HBM_SPEC, SEM_SPEC, SEM_SPEC, ANY_SPEC), out_specs=(HBM_SPEC, HBM_SPEC), input_output_aliases={0: 0, 1: 1},
        compiler_params=pltpu.CompilerParams(has_side_effects=EFFECT),
    )(x_thru, land_thru, send, recv, after)[1]


def _behind(token, a):
    return a + token[0:1, 0:1].astype(a.dtype)


def _sum8(name, g):
    def body(g_ref, o_ref):
        acc = g_ref[0:1, :]
        for p in range(1, NDEV):
            acc = acc + g_ref[p:p + 1, :]
        o_ref[...] = acc

    return pl.pallas_call(body, name=name, out_shape=jax.ShapeDtypeStruct((1, g.shape[1]), f32),
                          compiler_params=pltpu.CompilerParams(vmem_limit_bytes=VMEM_LIMIT))(g)


def _pack(arrs):
    flat = jnp.concatenate([a.reshape(-1).astype(f32) for a in arrs])
    pad = (-flat.shape[0]) % (8 * LANES)
    return jnp.pad(flat, (0, pad)).reshape(1, -1)


def _unpack(vec, shapes):
    out, off = [], 0
    flat = vec.reshape(-1)
    for s in shapes:
        n = 1
        for d in s:
            n *= d
        out.append(flat[off:off + n].reshape(s))
        off += n
    return out


def _adam(name, w, m, v, g, parts, after=None):
    extra = [] if after is None else [after]
    nrows, width = w.shape
    tr = _row_tile(nrows, width * (3 if parts else 1), cap=512) if nrows >= 16 else nrows
    blk = pl.BlockSpec((tr, width), lambda i: (i, 0))
    gblk = pl.BlockSpec((NDEV, tr, width), lambda i: (0, i, 0)) if parts else blk
    c1 = 1.0 / (1.0 - ADAM_B1 ** ADAM_STEP)
    c2 = 1.0 / (1.0 - ADAM_B2 ** ADAM_STEP)

    def body(w_ref, m_ref, v_ref, g_ref, *rest):
        go_ref, d_ref, mo_ref, vo_ref = rest[len(extra):]
        if parts:
            gg = g_ref[0].astype(f32)
            for p in range(1, NDEV):
                gg = gg + g_ref[p].astype(f32)
        else:
            gg = g_ref[...]
        mn = ADAM_B1 * m_ref[...] + (1.0 - ADAM_B1) * gg
        vn = ADAM_B2 * v_ref[...] + (1.0 - ADAM_B2) * (gg * gg)
        go_ref[...] = gg
        mo_ref[...] = mn
        vo_ref[...] = vn
        d_ref[...] = -ADAM_LR * ((mn * c1) / (jnp.sqrt(vn * c2) + ADAM_EPS) + ADAM_WD * w_ref[...])

    sh = jax.ShapeDtypeStruct((nrows, width), f32)
    return pl.pallas_call(body, name=name, grid=(nrows // tr,), in_specs=[blk, blk, blk, gblk] + [pl.BlockSpec(memory_space=pl.ANY)] * len(extra),
                          out_specs=[blk] * 4, out_shape=[sh] * 4, compiler_params=_cparams(("parallel",)))(w, m, v, g, *extra)


def kernel(x, c, ctx, c_ctx, ln_in_g, ln_in_b, w_mod, b_mod, w_in, w_qkv_conv, a_log_f, dt_bias_f, a_log_b, dt_bias_b, dn_norm_g, conf_dw_w, conf_dw_b, conf_ln_g, conf_ln_b, w_out, ln1_g, ln1_b, w_mlp1, b_mlp1, w_mlp2, b_mlp2, ln2_g, ln2_b, loss_target, m_c_ctx, m_ln_in_g, m_ln_in_b, m_w_mod, m_b_mod, m_w_in, m_w_qkv_conv, m_a_log_f, m_dt_bias_f, m_a_log_b, m_dt_bias_b, m_dn_norm_g, m_conf_dw_w, m_conf_dw_b, m_conf_ln_g, m_conf_ln_b, m_w_out, m_ln1_g, m_ln1_b, m_w_mlp1, m_b_mlp1, m_w_mlp2, m_b_mlp2, m_ln2_g, m_ln2_b, v_c_ctx, v_ln_in_g, v_ln_in_b, v_w_mod, v_b_mod, v_w_in, v_w_qkv_conv, v_a_log_f, v_dt_bias_f, v_a_log_b, v_dt_bias_b, v_dn_norm_g, v_conf_dw_w, v_conf_dw_b, v_conf_ln_g, v_conf_ln_b, v_w_out, v_ln1_g, v_ln1_b, v_w_mlp1, v_b_mlp1, v_w_mlp2, v_b_mlp2, v_ln2_g, v_ln2_b):
    weights = dict(c_ctx=c_ctx, ln_in_g=ln_in_g, ln_in_b=ln_in_b, w_mod=w_mod, b_mod=b_mod, w_in=w_in, w_qkv_conv=w_qkv_conv, a_log_f=a_log_f, dt_bias_f=dt_bias_f, a_log_b=a_log_b, dt_bias_b=dt_bias_b, dn_norm_g=dn_norm_g, conf_dw_w=conf_dw_w, conf_dw_b=conf_dw_b, conf_ln_g=conf_ln_g, conf_ln_b=conf_ln_b, w_out=w_out, ln1_g=ln1_g, ln1_b=ln1_b, w_mlp1=w_mlp1, b_mlp1=b_mlp1, w_mlp2=w_mlp2, b_mlp2=b_mlp2, ln2_g=ln2_g, ln2_b=ln2_b)
    mom1 = dict(c_ctx=m_c_ctx, ln_in_g=m_ln_in_g, ln_in_b=m_ln_in_b, w_mod=m_w_mod, b_mod=m_b_mod, w_in=m_w_in, w_qkv_conv=m_w_qkv_conv, a_log_f=m_a_log_f, dt_bias_f=m_dt_bias_f, a_log_b=m_a_log_b, dt_bias_b=m_dt_bias_b, dn_norm_g=m_dn_norm_g, conf_dw_w=m_conf_dw_w, conf_dw_b=m_conf_dw_b, conf_ln_g=m_conf_ln_g, conf_ln_b=m_conf_ln_b, w_out=m_w_out, ln1_g=m_ln1_g, ln1_b=m_ln1_b, w_mlp1=m_w_mlp1, b_mlp1=m_b_mlp1, w_mlp2=m_w_mlp2, b_mlp2=m_b_mlp2, ln2_g=m_ln2_g, ln2_b=m_ln2_b)
    mom2 = dict(c_ctx=v_c_ctx, ln_in_g=v_ln_in_g, ln_in_b=v_ln_in_b, w_mod=v_w_mod, b_mod=v_b_mod, w_in=v_w_in, w_qkv_conv=v_w_qkv_conv, a_log_f=v_a_log_f, dt_bias_f=v_dt_bias_f, a_log_b=v_a_log_b, dt_bias_b=v_dt_bias_b, dn_norm_g=v_dn_norm_g, conf_dw_w=v_conf_dw_w, conf_dw_b=v_conf_dw_b, conf_ln_g=v_conf_ln_g, conf_ln_b=v_conf_ln_b, w_out=v_w_out, ln1_g=v_ln1_g, ln1_b=v_ln1_b, w_mlp1=v_w_mlp1, b_mlp1=v_b_mlp1, w_mlp2=v_w_mlp2, b_mlp2=v_b_mlp2, ln2_g=v_ln2_g, ln2_b=v_ln2_b)
    names = list(weights)

    me = _lin(_mesh_pos())
    S, D = x.shape[1], x.shape[2]
    T = ctx.shape[1]
    R = S + T
    DN = D // 2
    NH = DN // HD
    CONF = D - DN
    K7, K31 = w_qkv_conv.shape[1], conf_dw_w.shape[1]
    DFF = w_mlp1.shape[2] * NDEV
    INC = w_in.shape[2] * NDEV
    CONF_OFF = 4 * DN + 4 * NH
    NC = 4 * DN + 2 * CONF + GPAD
    QB, ZB, VB, GB = 0, 3 * DN // LANES, 4 * DN // LANES, (4 * DN + 2 * CONF) // LANES
    MODC = w_mod.shape[2]
    x2d, ctx2d, tgt = x[0], ctx[0], loss_target[0]
    row = lambda a: a.reshape(1, -1).astype(f32)

    c_all, w7_all, w31_all = _all_gather("ag_small", [c.astype(f32), w_qkv_conv[0], conf_dw_w[0]])
    w7 = jnp.pad(jnp.transpose(w7_all, (1, 0, 2)).reshape(K7, 3 * DN), ((0, (-K7) % 8), (0, 0)))
    w31 = jnp.pad(jnp.transpose(w31_all, (1, 0, 2)).reshape(K31, CONF), ((0, (-K31) % 8), (0, 0)))
    c16 = jnp.concatenate([c_all.reshape(NDEV, D), c_ctx.reshape(1, D).astype(f32), jnp.zeros((7, D), f32)], axis=0)
    (sc16,) = _rowcall("silu_c", lambda j, a: (_silu(a),), 16, 16, [(c16, D, 0, 0, False)], [], [(D, f32, False)], [])
    bmod_mine = lax.dynamic_slice_in_dim(b_mod.astype(f32), me * MODC, MODC, axis=1)
    mod_part = _mm("mod_fwd", sc16, w_mod[0], "nn", f32, extras=[("row", bmod_mine)], epilogue=lambda r, b: (r + b,))
    (mod_g,) = _all_gather("ag_mod", [mod_part])
    mod_all = jnp.transpose(mod_g, (1, 0, 2)).reshape(16, 6 * D)
    mod_me = lax.dynamic_slice_in_dim(mod_all, me, 1, axis=0)
    sh_a, sc_a, g_a, sh_m, sc_m, g_m = [mod_me[:, i * D:(i + 1) * D] for i in range(6)]
    csh_a, csc_a = mod_all[8:9, 0:D], mod_all[8:9, D:2 * D]

    g_win, g_wout = _all_gather("ag_w_in_out", [w_in[0].astype(bf16), w_out[0].astype(bf16)], after=mod_g)
    ag_w1 = _xchg_start("ag_w1_start", w_mlp1[0].astype(bf16), True, g_win)
    ag_w2 = _xchg_start("ag_w2_start", w_mlp2[0].astype(bf16), True, g_win)
    win_full = jnp.transpose(g_win, (1, 0, 2)).reshape(D, INC)
    w_cat = jnp.concatenate([win_full[:, :4 * DN], win_full[:, CONF_OFF:], win_full[:, 4 * DN:CONF_OFF],
                             jnp.zeros((D, GPAD - 4 * NH), bf16)], axis=1)
    wout_full = g_wout.reshape(DN + CONF, D)
    g0, b0 = row(ln_in_g), row(ln_in_b)
    g0_fwd = _behind(ag_w2[4], _behind(ag_w1[4], g0))

    def ln_mod(j, xt, g, b, sh, sc):
        xh, _ = _ln_stats(xt)
        xn = xh * g + b
        return xn, xn * (1.0 + sc) + sh

    tr = _row_tile(T, D)
    vec = lambda a: (a, a.shape[1], 0, False)
    xn, xm = _rowcall("ln_in_lat", ln_mod, S, tr, [(x2d, D, 0, 0, False)], [vec(g0_fwd), vec(b0), vec(sh_a), vec(sc_a)], [(D, f32, False), (D, bf16, False)], [])
    xcn, xcm = _rowcall("ln_in_ctx", ln_mod, T, tr, [(ctx2d, D, 0, 0, False)], [vec(g0), vec(b0), vec(csh_a), vec(csc_a)], [(D, f32, False), (D, bf16, False)], [])
    xm_all = jnp.concatenate([xm, xcm], axis=0)

    h_all = _mm("in_proj", xm_all, w_cat, "nn", f32, tm=1088, tn=1280)

    qkv = _qkv_conv_fwd(h_all, w7, R, S, NH, K7).reshape(3, NH, R, HD)
    lane = jnp.arange(LANES)
    is_a = ((lane < 4 * NH) & ((lane // NH) % 2 == 0)).astype(f32).reshape(1, LANES)
    pad_g = lambda a, b: jnp.concatenate([a.reshape(-1), jnp.zeros((NH,), f32), b.reshape(-1), jnp.zeros((LANES - 3 * NH,), f32)]).reshape(1, LANES)
    neg_a = pad_g(-jnp.exp(a_log_f.astype(f32)), -jnp.exp(a_log_b.astype(f32)))
    dt_v = pad_g(dt_bias_f.astype(f32), dt_bias_b.astype(f32))

    def gates_f(j, hg, isa, na, dt):
        return (jnp.where(isa > 0.5, na * _softplus(hg + dt), _sigmoid(hg)),)

    trg = _row_tile(T, LANES)
    (gates,) = _rowcall("gates_fwd", gates_f, R, trg, [(h_all, LANES, 0, GB, False)], [vec(is_a), vec(neg_a), vec(dt_v)], [(LANES, f32, False)], [])
    gpad = jnp.zeros((R, LANES - 2 * NH), f32)
    gd = jnp.stack([jnp.concatenate([gates[:, :2 * NH], gpad], axis=1), jnp.concatenate([gates[:, 2 * NH:4 * NH], gpad], axis=1)])

    o5, states = _gdn_fwd(qkv, gd, NH, S, T)
    dng = row(dn_norm_g)
    trh = _row_tile(S, HD, cap=1024)
    dn_out = _gate_norm_fwd(o5, h_all, dng, NH, S, ZB, trh)

    yconv = _conf_conv_fwd(h_all, w31, S, CONF, VB, K31)
    bdw, clg, clb = row(conf_dw_b), row(conf_ln_g), row(conf_ln_b)

    def conf_ln(j, yc, b, g, bb):
        xh, _ = _ln_stats(yc + b)
        return (_silu(xh * g + bb),)

    trc = _row_tile(S, CONF)
    (conf_out,) = _rowcall("conf_ln_fwd", conf_ln, S, trc, [(yconv, CONF, 0, 0, False)], [vec(bdw), vec(clg), vec(clb)], [(CONF, bf16, False)], [])
    mix = jnp.concatenate([dn_out, conf_out], axis=1)
    y = _mm("out_proj", mix, wout_full, "nn", f32)

    l1g, l1b, l2g, l2b = row(ln1_g), row(ln1_b), row(ln2_g), row(ln2_b)

    def ln1_mod(j, xnt, yt, ga, g, b, sh, sc):
        xh, _ = _ln_stats(ALPHA * xnt + ga * yt)
        x1 = xh * g + b
        return x1, x1 * (1.0 + sc) + sh

    trd = _row_tile(S, D)
    x1, u = _rowcall("ln1_fwd", ln1_mod, S, trd, [(xn, D, 0, 0, False), (y, D, 0, 0, False)],
                     [vec(g_a), vec(l1g), vec(l1b), vec(sh_m), vec(sc_m)], [(D, f32, False), (D, bf16, False)], [])
    b1, b2 = row(b_mlp1), row(b_mlp2)
    g_w1 = _xchg_wait("ag_w1_wait", ag_w1, True, u)
    hh, act = _mm("mlp1", u, g_w1, "nn", (f32, bf16), extras=[("row", b1)], epilogue=lambda r, b: (r, jnp.square(jnp.maximum(r + b, 0.0))))
    w2_full = _xchg_wait("ag_w2_wait", ag_w2, True, act).reshape(DFF, D)
    y2 = _mm("mlp2", act, w2_full, "nn", f32)

    def ln2_loss(j, x1t, y2t, tg, bb2, gm, g, b):
        y2b = y2t + bb2
        xh, rstd = _ln_stats(ALPHA * x1t + gm * y2b)
        err = xh * g + b - tg
        dx2 = err * (1.0 / D)
        dr2 = _ln_bwd(dx2, xh, rstd, g)
        dy2 = dr2 * gm
        lsum = jnp.broadcast_to(jnp.sum(err * err).reshape(1, 1), (1, LANES))
        return dr2, dy2, _csum(dx2 * xh), _csum(dx2), _csum(dr2 * y2b), _csum(dy2), lsum

    dr2, dy2, d_l2g, d_l2b, d_gm, d_b2, lsum = _rowcall(
        "ln2_loss", ln2_loss, S, trd, [(x1, D, 0, 0, False), (y2, D, 0, 0, False), (tgt, D, 0, 0, False)],
        [vec(b2), vec(g_m), vec(l2g), vec(l2b)], [(D, f32, False), (D, bf16, False)], [(1, D, False)] * 4 + [(1, LANES, False)])
    loss = lax.psum(0.5 * lsum[0, 0] / D, ("x", "y", "c"))

    dw2_p = _mm("mlp2_dw", act, dy2, "tn", bf16)
    a2a_w2 = _xchg_start("a2a_w2_start", dw2_p.reshape(NDEV, DFF // NDEV, D), False, dw2_p)
    dhh = _mm("mlp2_dx", dy2, w2_full, "nt", bf16, extras=[("tile", hh), ("row", _behind(a2a_w2[4], b1))],
              epilogue=lambda r, h, b: (r * (2.0 * jnp.maximum(h + b, 0.0)),))
    (d_b1,) = _rowcall("b1_grad", lambda j, a: (_csum(a.astype(f32)),), S, _row_tile(S, DFF // 8), [(dhh, DFF, 0, 0, False)], [], [], [(1, DFF, False)])
    dw1_p = _mm("mlp1_dw", u, dhh, "tn", bf16, out_split=NDEV)
    a2a_w1 = _xchg_start("a2a_w1_start", dw1_p, False, dw1_p)
    du = _mm("mlp1_dx", dhh, g_w1, "nt", f32)

    def ln1_bwd(j, dr2t, dut, xnt, yt, ga, g, b, sc):
        xh, rstd = _ln_stats(ALPHA * xnt + ga * yt)
        x1t = xh * g + b
        dx1 = ALPHA * dr2t + dut * (1.0 + sc)
        dr1 = _ln_bwd(dx1, xh, rstd, g)
        return dr1, dr1 * ga, _csum(dut * x1t), _csum(dut), _csum(dx1 * xh), _csum(dx1), _csum(dr1 * yt)

    dr1, dy, d_scm, d_shm, d_l1g, d_l1b, d_ga = _rowcall(
        "ln1_bwd", ln1_bwd, S, trd, [(dr2, D, 0, 0, False), (du, D, 0, 0, False), (xn, D, 0, 0, False), (y, D, 0, 0, False)],
        [vec(g_a), vec(l1g), vec(l1b), vec(_behind(a2a_w1[4], sc_m))], [(D, f32, False), (D, bf16, False)], [(1, D, False)] * 5)
    dwout_p = _mm("out_proj_dw", mix, dy, "tn", bf16)
    a2a_wout = _xchg_start("a2a_wout_start", dwout_p.reshape(NDEV, (DN + CONF) // NDEV, D), False, dwout_p)
    dmix = _mm("out_proj_dx", dy, wout_full, "nt", f32)

    def conf_ln_b(j, dm, yc, b, g, bb):
        xh, rstd = _ln_stats(yc + b)
        dln = dm * _dsilu(xh * g + bb)
        dyc = _ln_bwd(dln, xh, rstd, g)
        return dyc, _csum(dln * xh), _csum(dln), _csum(dyc)

    dyc, d_clg, d_clb, d_bdw = _rowcall("conf_ln_bwd", conf_ln_b, S, trc, [(dmix, CONF, 0, DN // CONF, False), (yconv, CONF, 0, 0, False)],
                                        [vec(_behind(a2a_wout[4], bdw)), vec(clg), vec(clb)], [(CONF, f32, False)], [(1, CONF, False)] * 3)
    dval, dgate, dw31 = _conf_conv_bwd(h_all, w31, dyc, S, CONF, VB, K31)

    do, dz, d_dng = _gate_norm_bwd(dmix, o5, h_all, dng, NH, S, ZB, trh)
    dqkv, dgd = _gdn_bwd(qkv, gd, states, do, NH, S, T)
    dh_qkv, dw7 = _qkv_conv_bwd(h_all, w7, dqkv.reshape(2, 3 * NH, R, HD), R, S, NH, K7)
    dgate_cols = jnp.concatenate([dgd[0][:, :2 * NH], dgd[1][:, :2 * NH], jnp.zeros((R, LANES - 4 * NH), f32)], axis=1)

    def gates_b(j, hg, dgt, isa, na, dt):
        sg = _sigmoid(hg)
        sp = _softplus(hg + dt)
        dpre = jnp.where(isa > 0.5, dgt * na * _sigmoid(hg + dt), dgt * sg * (1.0 - sg))
        return dpre, _csum(jnp.where(isa > 0.5, dgt * na * sp, 0.0)), _csum(jnp.where(isa > 0.5, dpre, 0.0))

    dh_g, d_alog, d_dt = _rowcall("gates_bwd", gates_b, R, trg, [(h_all, LANES, 0, GB, False), (dgate_cols, LANES, 0, 0, False)],
                                  [vec(is_a), vec(neg_a), vec(dt_v)], [(LANES, bf16, False)], [(1, LANES, False)] * 2)

    zpad = lambda a: jnp.concatenate([a, jnp.zeros((T, a.shape[1]), bf16)], axis=0)
    dh_all = jnp.concatenate([dh_qkv, zpad(dz), zpad(dval), zpad(dgate), dh_g, jnp.zeros((R, GPAD - LANES), bf16)], axis=1)
    dxm_all = _mm("in_proj_dx", dh_all, w_cat, "nt", f32, tm=1088, tk=2560)

    def ln_in_bwd_lat(j, xt, dr1t, dxm, g, b, sc):
        xh, rstd = _ln_stats(xt)
        xnt = xh * g + b
        dxn = ALPHA * dr1t + dxm * (1.0 + sc)
        return _ln_bwd(dxn, xh, rstd, g), _csum(dxm * xnt), _csum(dxm), _csum(dxn * xh), _csum(dxn)

    def ln_in_bwd_ctx(j, xt, dxm, g, b, sc):
        xh, rstd = _ln_stats(xt)
        xnt = xh * g + b
        dxn = dxm * (1.0 + sc)
        return _csum(dxm * xnt), _csum(dxm), _csum(dxn * xh), _csum(dxn)

    grad_x, d_sca, d_sha, d_g0a, d_b0a = _rowcall(
        "ln_in_bwd_lat", ln_in_bwd_lat, S, trd, [(x2d, D, 0, 0, False), (dr1, D, 0, 0, False), (dxm_all, D, 0, 0, False)],
        [vec(g0), vec(b0), vec(sc_a)], [(D, f32, False)], [(1, D, False)] * 4)
    d_csca, d_csha, d_g0b, d_b0b = _rowcall(
        "ln_in_bwd_ctx", ln_in_bwd_ctx, T, tr, [(ctx2d, D, 0, 0, False), (dxm_all, D, S // tr, 0, False)],
        [vec(g0), vec(b0), vec(csc_a)], [], [(1, D, False)] * 4)

    zD = jnp.zeros((1, D), f32)
    dmod_me = jnp.concatenate([d_sha, d_sca, d_ga, d_shm, d_scm, d_gm], axis=1)
    dmodc_me = jnp.concatenate([d_csha, d_csca, zD, zD, zD, zD], axis=1)
    small_names = ["ln_in_g", "ln_in_b", "a_log_f", "dt_bias_f", "a_log_b", "dt_bias_b", "dn_norm_g", "conf_dw_b", "conf_ln_g", "conf_ln_b",
                   "ln1_g", "ln1_b", "b_mlp1", "b_mlp2", "ln2_g", "ln2_b"]
    small_parts = [d_g0a + d_g0b, d_b0a + d_b0b, d_alog[:, 0:NH], d_dt[:, 0:NH], d_alog[:, 2 * NH:3 * NH], d_dt[:, 2 * NH:3 * NH], d_dng, d_bdw, d_clg, d_clb,
                   d_l1g, d_l1b, d_b1, d_b2, d_l2g, d_l2b]
    conv_parts = [dw7[:K7], dw31[:K31]]
    packed = _pack(small_parts + conv_parts + [dmodc_me])
    (pk_all, dmod_all) = _all_gather("ag_small_grads", [packed, dmod_me])
    summed = _sum8("sum_small_grads", pk_all.reshape(NDEV, -1))
    parts_sh = [a.shape for a in small_parts + conv_parts + [dmodc_me]]
    un = _unpack(summed, parts_sh)
    gsmall = dict(zip(small_names, un[:len(small_names)]))
    g_w7_full, g_w31_full, dmodc = un[len(small_names):]
    gsmall["w_qkv_conv"] = lax.dynamic_slice_in_dim(g_w7_full, me * w_qkv_conv.shape[2], w_qkv_conv.shape[2], axis=1)
    gsmall["conf_dw_w"] = lax.dynamic_slice_in_dim(g_w31_full, me * conf_dw_w.shape[2], conf_dw_w.shape[2], axis=1)

    dm16 = jnp.concatenate([dmod_all.reshape(NDEV, 6 * D), dmodc, jnp.zeros((7, 6 * D), f32)], axis=0)
    (gbmod,) = _rowcall("bmod_grad", lambda j, a: (_csum(a),), 16, 16, [(dm16, 6 * D, 0, 0, False)], [], [], [(1, 6 * D, False)])
    gsmall["b_mod"] = gbmod
    dm16_mine = lax.dynamic_slice_in_dim(dm16, me * MODC, MODC, axis=1)
    g_wmod = _mm("mod_dw", sc16, dm16_mine, "tn", f32)
    dsc16_part = _mm("mod_dx", dm16_mine, w_mod[0], "nt", f32)
    (dsc_all,) = _all_gather("ag_cctx", [dsc16_part[8:9]])
    dsilu_cctx = _sum8("sum_cctx", dsc_all.reshape(NDEV, D))
    (g_cctx,) = _rowcall("cctx_grad", lambda j, ds, cc: (ds * _dsilu(cc),), 1, 1, [(dsilu_cctx, D, 0, 0, False), (c_ctx.reshape(1, D).astype(f32), D, 0, 0, False)],
                         [], [(D, f32, False)], [])
    gsmall["c_ctx"] = g_cctx

    dwcat_p = _mm("in_proj_dw", xm_all, dh_all, "tn", bf16, tn=1280, tk=2176, after=g_cctx)
    dwin = jnp.concatenate([dwcat_p[:, :4 * DN], dwcat_p[:, 4 * DN + 2 * CONF:4 * DN + 2 * CONF + 4 * NH], dwcat_p[:, 4 * DN:4 * DN + 2 * CONF]], axis=1)
    dwin_p = jnp.transpose(dwin.reshape(D, NDEV, INC // NDEV), (1, 0, 2))
    a2a_win = _xchg_start("a2a_win_start", dwin_p, False, dwin_p)

    grads, deltas, new_m, new_v = {}, {}, {}, {}
    res = _adam("adam_w_mod", w_mod[0], m_w_mod[0], v_w_mod[0], g_wmod, False, after=a2a_win[4])
    grads["w_mod"], deltas["w_mod"], new_m["w_mod"], new_v["w_mod"] = [a.reshape(w_mod.shape) for a in res]
    after = res[1]
    for nm, started in (("w_mlp2", a2a_w2), ("w_mlp1", a2a_w1), ("w_out", a2a_wout), ("w_in", a2a_win)):
        w3 = weights[nm]
        g = _xchg_wait("a2a_" + nm + "_wait", started, False, after)
        res = _adam("adam_" + nm, w3[0], mom1[nm][0], mom2[nm][0], g, True)
        grads[nm], deltas[nm], new_m[nm], new_v[nm] = [a.reshape(w3.shape) for a in res]
        after = res[1]
    snames = [n for n in names if n not in grads]
    res = _adam("adam_small", _pack([weights[n] for n in snames]), _pack([mom1[n] for n in snames]), _pack([mom2[n] for n in snames]),
                _pack([gsmall[n] for n in snames]), False)
    shapes = [weights[n].shape for n in snames]
    for dst, packed_out in zip((grads, deltas, new_m, new_v), res):
        for n, a in zip(snames, _unpack(packed_out, shapes)):
            dst[n] = a

    return (loss, grad_x.reshape(x.shape), *[grads[n] for n in names], *[deltas[n] for n in names],
            *[new_m[n] for n in names], *[new_v[n] for n in names])
```

```python
import functools

import jax
import jax.numpy as jnp
from jax import lax
from jax.experimental import pallas as pl
from jax.experimental.pallas import tpu as pltpu

f32 = jnp.float32
bf16 = jnp.bfloat16
MESH = pl.DeviceIdType.MESH

NDEV = 8
HD = 128
CH = 64
GW = 64
LANES = 128
GPAD = 512
LN_EPS = 1e-5
ALPHA = 2.0 ** 0.25
ADAM_LR, ADAM_B1, ADAM_B2, ADAM_EPS, ADAM_WD, ADAM_STEP = 0.001, 0.9, 0.999, 1e-08, 0.01, 10
VMEM_LIMIT = 56 * 1024 * 1024
ROW_BLOCK_BYTES = 1 << 20


def _cparams(sem):
    return pltpu.CompilerParams(dimension_semantics=sem, vmem_limit_bytes=VMEM_LIMIT)


def _tile(dim, pref, align):
    t = min(pref, dim)
    t -= t % align
    while t >= align:
        if dim % t == 0:
            return t
        t -= align
    return dim


def _row_tile(nrows, width, cap=256):
    t = max(16, min(cap, ROW_BLOCK_BYTES // (4 * width)))
    t = 1 << (t.bit_length() - 1)
    while nrows % t:
        t //= 2
    return max(t, 1)


def _sigmoid(x):
    return 1.0 / (1.0 + jnp.exp(-x))


def _silu(x):
    return x * _sigmoid(x)


def _dsilu(x):
    s = _sigmoid(x)
    return s * (1.0 + x * (1.0 - s))


def _softplus(x):
    return jnp.maximum(x, 0.0) + jnp.log(1.0 + jnp.exp(-jnp.abs(x)))


def _ln_stats(r):
    mu = jnp.mean(r, axis=-1, keepdims=True)
    xc = r - mu
    rstd = lax.rsqrt(jnp.mean(xc * xc, axis=-1, keepdims=True) + LN_EPS)
    return xc * rstd, rstd


def _ln_bwd(dy, xhat, rstd, g):
    dxh = dy * g
    return rstd * (dxh - jnp.mean(dxh, axis=-1, keepdims=True) - xhat * jnp.mean(dxh * xhat, axis=-1, keepdims=True))


def _csum(a):
    return jnp.sum(a, axis=0, keepdims=True)


def _mm(name, a, b, mode, out_dtype, out_split=1, tm=1024, tn=1024, tk=2048, extras=(), epilogue=None, after=None):
    if mode == "tn":
        K, M = a.shape
    else:
        M, K = a.shape
    bp = b.shape[0] if b.ndim == 3 else 1
    brows, bcols = b.shape[-2], b.shape[-1] * bp
    N = brows if mode == "nt" else bcols
    assert K == (bcols if mode == "nt" else brows), (name, a.shape, b.shape)
    tm = _tile(M, tm, LANES if mode == "tn" else 16)
    nsplit = max(out_split, bp if mode != "nt" else 1)
    tn = _tile(N // nsplit, tn, LANES)
    tk = _tile(K // (bp if mode == "nt" else 1), tk, LANES)
    nk = K // tk
    dims = {"nn": (((1,), (0,)), ((), ())), "nt": (((1,), (1,)), ((), ())), "tn": (((0,), (0,)), ((), ()))}[mode]
    out_dtypes = out_dtype if isinstance(out_dtype, tuple) else (out_dtype,)

    a_spec = pl.BlockSpec((tk, tm), lambda i, j, k: (k, i)) if mode == "tn" else pl.BlockSpec((tm, tk), lambda i, j, k: (i, k))
    if b.ndim == 3:
        if mode == "nt":
            per = (K // bp) // tk
            b_spec = pl.BlockSpec((None, tn, tk), lambda i, j, k: (k // per, j, k % per))
        else:
            per = (N // bp) // tn
            b_spec = pl.BlockSpec((None, tk, tn), lambda i, j, k: (j // per, k, j % per))
    elif mode == "nt":
        b_spec = pl.BlockSpec((tn, tk), lambda i, j, k: (j, k))
    else:
        b_spec = pl.BlockSpec((tk, tn), lambda i, j, k: (k, j))
    if out_split > 1:
        pero = (N // out_split) // tn
        o_spec = pl.BlockSpec((None, tm, tn), lambda i, j, k: (j // pero, i, j % pero))
        o_shapes = [jax.ShapeDtypeStruct((out_split, M, N // out_split), dt) for dt in out_dtypes]
    else:
        o_spec = pl.BlockSpec((tm, tn), lambda i, j, k: (i, j))
        o_shapes = [jax.ShapeDtypeStruct((M, N), dt) for dt in out_dtypes]
    in_specs = [a_spec, b_spec]
    args = [a, b]
    for kind, arr in extras:
        in_specs.append(pl.BlockSpec((1, tn), lambda i, j, k: (0, j)) if kind == "row" else pl.BlockSpec((tm, tn), lambda i, j, k: (i, j)))
        args.append(arr)
    n_in, n_out = len(args), len(out_dtypes)
    if after is not None:
        in_specs.append(pl.BlockSpec(memory_space=pl.ANY))
        args.append(after)
    n_all = len(args)

    def finish(refs, r):
        outs = (r,) if epilogue is None else epilogue(r, *[e[...] for e in refs[2:n_in]])
        for o_ref, val in zip(refs[n_all:n_all + n_out], outs):
            o_ref[...] = val.astype(o_ref.dtype)

    def body(*refs):
        part = lax.dot_general(refs[0][...].astype(bf16), refs[1][...].astype(bf16), dims, preferred_element_type=f32)
        if nk == 1:
            finish(refs, part)
            return
        acc = refs[-1]
        k = pl.program_id(2)

        @pl.when(k == 0)
        def _():
            acc[...] = part

        @pl.when(jnp.logical_and(k > 0, k < nk - 1))
        def _():
            acc[...] += part

        @pl.when(k == nk - 1)
        def _():
            finish(refs, acc[...] + part)

    res = pl.pallas_call(
        body, name=name, grid=(M // tm, N // tn, nk), in_specs=in_specs, out_specs=[o_spec] * n_out, out_shape=o_shapes,
        scratch_shapes=[pltpu.VMEM((tm, tn), f32)] if nk > 1 else [], compiler_params=_cparams(("parallel", "parallel", "arbitrary")),
    )(*args)
    return res[0] if n_out == 1 else res


def _rowcall(name, fn, nrows, tr, rows_in, vecs_in, rows_out, accs_out, ncol=1):
    nrt = nrows // tr
    in_specs, args = [], []
    for arr, w, ro, co, pc in rows_in:
        in_specs.append(pl.BlockSpec((tr, w), functools.partial(lambda j, i, ro, co, pc: (i + ro, co + (j if pc else 0)), ro=ro, co=co, pc=pc)))
        args.append(arr)
    for arr, w, co, pc in vecs_in:
        in_specs.append(pl.BlockSpec((arr.shape[0], w), functools.partial(lambda j, i, co, pc: (0, co + (j if pc else 0)), co=co, pc=pc)))
        args.append(arr)
    out_specs, out_shape = [], []
    for w, dt, pc in rows_out:
        out_specs.append(pl.BlockSpec((tr, w), functools.partial(lambda j, i, pc: (i, j if pc else 0), pc=pc)))
        out_shape.append(jax.ShapeDtypeStruct((nrows, w * (ncol if pc else 1)), dt))
    for k, w, pc in accs_out:
        out_specs.append(pl.BlockSpec((k, w), functools.partial(lambda j, i, pc: (0, j if pc else 0), pc=pc)))
        out_shape.append(jax.ShapeDtypeStruct((k, w * (ncol if pc else 1)), f32))
    n_in, n_ro = len(args), len(rows_out)

    def body(*refs):
        j, i = pl.program_id(0), pl.program_id(1)
        outs = fn(j, *[r[...] for r in refs[:n_in]])
        for r, val in zip(refs[n_in:n_in + n_ro], outs[:n_ro]):
            r[...] = val.astype(r.dtype)
        for (k, w, pc), r, val in zip(accs_out, refs[n_in + n_ro:], outs[n_ro:]):
            first = (i == 0) if pc else jnp.logical_and(i == 0, j == 0)

            @pl.when(first)
            def _(r=r):
                r[...] = jnp.zeros_like(r)

            r[...] += val

    res = pl.pallas_call(
        body, name=name, grid=(ncol, nrt), in_specs=in_specs, out_specs=out_specs, out_shape=out_shape,
        compiler_params=_cparams(("arbitrary", "arbitrary")),
    )(*args)
    return res


def _tap_valid(mode, t, d, nrows, nlat):
    if mode == "seg":
        tp = t + d
        return (tp >= 0) & (tp < nrows) & ((t < nlat) == (tp < nlat))
    if mode == "row":
        p = (t & (GW - 1)) + d
        return (p >= 0) & (p < GW)
    tp = t + d * GW
    return (tp >= 0) & (tp < nrows)


def _conv(x, w_ref, ktaps, mode, nlat, flip=False):
    nrows = x.shape[0]
    stride = GW if mode == "col" else 1
    t = lax.broadcasted_iota(jnp.int32, (nrows, 1), 0)
    acc = jnp.zeros_like(x)
    for j in range(ktaps):
        d = j - ktaps // 2
        jj = ktaps - 1 - j if flip else j
        wj = w_ref[jj:jj + 1, :]
        if d == 0:
            acc = acc + x * wj
        else:
            xs = pltpu.roll(x, (-d * stride) % nrows, 0)
            acc = acc + jnp.where(_tap_valid(mode, t, d, nrows, nlat), xs * wj, 0.0)
    return acc


def _conv_wgrad(dy, x, dw_ref, ktaps, mode, nlat):
    nrows = x.shape[0]
    stride = GW if mode == "col" else 1
    t = lax.broadcasted_iota(jnp.int32, (nrows, 1), 0)
    dw_ref[...] = jnp.zeros_like(dw_ref)
    for j in range(ktaps):
        d = j - ktaps // 2
        if d == 0:
            prod = x * dy
        else:
            xs = pltpu.roll(x, (-d * stride) % nrows, 0)
            prod = jnp.where(_tap_valid(mode, t, d, nrows, nlat), xs * dy, 0.0)
        dw_ref[j:j + 1, :] = _csum(prod)


def _qkv_post(j, pre, nh):
    a = _silu(pre)
    inv = lax.rsqrt(jnp.sum(a * a, axis=-1, keepdims=True) + 1e-6)
    scale = jnp.where(j < nh, HD ** -0.5, 1.0).astype(f32)
    nrm = jnp.where(j < 2 * nh, inv, 1.0) * scale
    return a, inv, nrm


def _qkv_conv_fwd(h_all, w7, nrows, nlat, nh, ktaps):
    ntile = 3 * nh

    def body(h_ref, w_ref, o_ref):
        j = pl.program_id(0)
        pre = _conv(h_ref[...], w_ref, ktaps, "seg", nlat)
        a, _, nrm = _qkv_post(j, pre, nh)
        o_ref[...] = a * nrm

    return pl.pallas_call(
        body, name="qkv_conv_fwd", grid=(ntile,),
        in_specs=[pl.BlockSpec((nrows, HD), lambda j: (0, j)), pl.BlockSpec((w7.shape[0], HD), lambda j: (0, j))],
        out_specs=pl.BlockSpec((None, nrows, HD), lambda j: (j, 0, 0)),
        out_shape=jax.ShapeDtypeStruct((ntile, nrows, HD), f32), compiler_params=_cparams(("parallel",)),
    )(h_all, w7)


def _qkv_conv_bwd(h_all, w7, dqkv, nrows, nlat, nh, ktaps):
    ntile = 3 * nh

    def body(h_ref, w_ref, d0_ref, d1_ref, dh_ref, dw_ref):
        j = pl.program_id(0)
        hx = h_ref[...]
        pre = _conv(hx, w_ref, ktaps, "seg", nlat)
        a, inv, nrm = _qkv_post(j, pre, nh)
        dn = (d0_ref[...] + d1_ref[...]) * jnp.where(j < nh, HD ** -0.5, 1.0).astype(f32)
        n = a * inv
        da_norm = inv * (dn - n * jnp.sum(dn * n, axis=-1, keepdims=True))
        da = jnp.where(j < 2 * nh, da_norm, dn)
        dpre = da * _dsilu(pre)
        dh_ref[...] = _conv(dpre, w_ref, ktaps, "seg", nlat, flip=True).astype(dh_ref.dtype)
        _conv_wgrad(dpre, hx, dw_ref, ktaps, "seg", nlat)

    blk = pl.BlockSpec((nrows, HD), lambda j: (0, j))
    wblk = pl.BlockSpec((w7.shape[0], HD), lambda j: (0, j))
    dblk = lambda d: pl.BlockSpec((None, None, nrows, HD), lambda j: (d, j, 0, 0))
    return pl.pallas_call(
        body, name="qkv_conv_bwd", grid=(ntile,), in_specs=[blk, wblk, dblk(0), dblk(1)], out_specs=[blk, wblk],
        out_shape=[jax.ShapeDtypeStruct((nrows, ntile * HD), bf16), jax.ShapeDtypeStruct((w7.shape[0], ntile * HD), f32)],
        compiler_params=_cparams(("parallel",)),
    )(h_all, w7, dqkv, dqkv)


def _conf_conv_fwd(h_all, w31, nlat, conf, val_blk, ktaps):
    nt = conf // LANES
    nhalf = nt // 2

    def body(v_ref, g_ref, w_ref, o_ref):
        j = pl.program_id(0)
        glu = v_ref[...] * _sigmoid(g_ref[...])

        @pl.when(j < nhalf)
        def _():
            o_ref[...] = _conv(glu, w_ref, ktaps, "row", nlat)

        @pl.when(j >= nhalf)
        def _():
            o_ref[...] = _conv(glu, w_ref, ktaps, "col", nlat)

    return pl.pallas_call(
        body, name="conf_conv_fwd", grid=(nt,),
        in_specs=[pl.BlockSpec((nlat, LANES), lambda j: (0, val_blk + j)), pl.BlockSpec((nlat, LANES), lambda j: (0, val_blk + nt + j)),
                  pl.BlockSpec((w31.shape[0], LANES), lambda j: (0, j))],
        out_specs=pl.BlockSpec((nlat, LANES), lambda j: (0, j)),
        out_shape=jax.ShapeDtypeStruct((nlat, conf), f32), compiler_params=_cparams(("parallel",)),
    )(h_all, h_all, w31)


def _conf_conv_bwd(h_all, w31, dyc, nlat, conf, val_blk, ktaps):
    nt = conf // LANES
    nhalf = nt // 2

    def body(v_ref, g_ref, w_ref, dy_ref, dv_ref, dg_ref, dw_ref):
        j = pl.program_id(0)
        val, sg = v_ref[...], _sigmoid(g_ref[...])
        glu = val * sg
        dy = dy_ref[...]

        def run(mode):
            dglu = _conv(dy, w_ref, ktaps, mode, nlat, flip=True)
            dv_ref[...] = (dglu * sg).astype(dv_ref.dtype)
            dg_ref[...] = (dglu * val * sg * (1.0 - sg)).astype(dg_ref.dtype)
            _conv_wgrad(dy, glu, dw_ref, ktaps, mode, nlat)

        @pl.when(j < nhalf)
        def _():
            run("row")

        @pl.when(j >= nhalf)
        def _():
            run("col")

    blk = pl.BlockSpec((nlat, LANES), lambda j: (0, j))
    wblk = pl.BlockSpec((w31.shape[0], LANES), lambda j: (0, j))
    return pl.pallas_call(
        body, name="conf_conv_bwd", grid=(nt,),
        in_specs=[pl.BlockSpec((nlat, LANES), lambda j: (0, val_blk + j)), pl.BlockSpec((nlat, LANES), lambda j: (0, val_blk + nt + j)), wblk, blk],
        out_specs=[blk, blk, wblk],
        out_shape=[jax.ShapeDtypeStruct((nlat, conf), bf16), jax.ShapeDtypeStruct((nlat, conf), bf16),
                   jax.ShapeDtypeStruct((w31.shape[0], conf), f32)],
        compiler_params=_cparams(("parallel",)),
    )(h_all, h_all, w31, dyc)


def _bd(eq, a, b):
    return jnp.einsum(eq, a.astype(bf16), b.astype(bf16), preferred_element_type=f32)


def _split(x, pieces):
    out = []
    for _ in range(pieces - 1):
        p = x.astype(bf16)
        out.append(p)
        x = x - p.astype(f32)
    return out + [x.astype(bf16)]


def _h3(eq, a, b):
    (ah, al), (bh, bl) = _split(a, 2), _split(b, 2)
    d = lambda p, q: jnp.einsum(eq, p, q, preferred_element_type=f32)
    return d(ah, bh) + (d(ah, bl) + d(al, bh))


def _x3(eq, a, b):
    d = lambda p, q: jnp.einsum(eq, p, q, preferred_element_type=f32)
    ae, (b1, b2, b3) = a.astype(bf16), _split(b, 3)
    return d(ae, b1) + (d(ae, b2) + d(ae, b3))


def _gdn_chunk(q, k, v, gt, nh, rev, solved=None):
    g = jnp.stack([jnp.broadcast_to(gt[:, h:h + 1], (CH, LANES)) for h in range(nh)])
    beta = jnp.stack([jnp.broadcast_to(gt[:, nh + h:nh + h + 1], (CH, LANES)) for h in range(nh)])
    ii = lax.broadcasted_iota(jnp.int32, (CH, CH), 0)
    jj = lax.broadcasted_iota(jnp.int32, (CH, CH), 1)
    si, sj = jnp.where(rev, jj, ii), jnp.where(rev, ii, jj)
    tril, stril = (si >= sj)[None], (si > sj)[None]
    trilf = jnp.broadcast_to(tril.astype(f32), (nh, CH, CH))
    triuf = jnp.broadcast_to((sj >= si)[None].astype(f32), (nh, CH, CH))
    gc = _x3("hit,htl->hil", trilf, g)
    gc_row = _x3("hil,hjl->hij", jnp.full((nh, CH, LANES), 1.0 / LANES, f32), gc)
    diff = gc[:, :, :CH] - gc_row
    gam = jnp.where(tril, jnp.exp(jnp.where(tril, diff, 0.0)), 0.0)
    e = jnp.exp(gc)
    gl = jnp.where(rev, gc[:, 0:1, :], gc[:, CH - 1:CH, :])
    el = jnp.exp(gl)
    r = jnp.exp(gl - gc)
    kb, vb = k * beta, v * beta
    kbe = kb * e
    amat = jnp.where(stril, _bd("hik,hjk->hij", kb, k) * gam, 0.0)
    if solved is None:
        xp = -amat
        tinv = (ii == jj).astype(f32)[None] + xp
        for _ in range(5):
            xp = _h3("hij,hjk->hik", xp, xp)
            tinv = tinv + _h3("hij,hjk->hik", tinv, xp)
        u = _h3("hij,hjv->hiv", tinv, vb)
        w = _h3("hij,hjk->hik", tinv, kbe)
    else:
        tinv, u, w = solved
    pmat = jnp.where(tril, _bd("hik,hjk->hij", q, k) * gam, 0.0)
    return dict(beta=beta, gam=gam, e=e, el=el, r=r, kb=kb, vb=vb, kbe=kbe, amat=amat, tinv=tinv, u=u, w=w, pmat=pmat,
                qd=q * e, kd=k * r, tril=tril, stril=stril, triuf=triuf)


def _scan_row_chunk(d, n, ns, nt):
    fwd = jnp.where(n < nt, ns + n, n - nt)
    return jnp.where(d == 0, fwd, ns + nt - 1 - n)


def _gdn_fwd(qkv, gd, nh, nlat, nctx):
    nrows = nlat + nctx
    ns, nt = nlat // CH, nctx // CH
    nch = ns + nt

    def body(qkv_ref, g_ref, o_ref, st_ref, ti_ref, u_ref, w_ref, state):
        d, n = pl.program_id(0), pl.program_id(1)

        @pl.when(n == 0)
        def _():
            state[...] = jnp.zeros_like(state)

        c = _gdn_chunk(qkv_ref[0], qkv_ref[1], qkv_ref[2], g_ref[...], nh, d == 1)
        s = state[...]
        vn = c["u"] - _bd("hck,hkv->hcv", c["w"], s)
        o_ref[...] = _bd("hck,hkv->hcv", c["qd"], s) + _bd("hcd,hdv->hcv", c["pmat"], vn)
        st_ref[...] = s
        ti_ref[...], u_ref[...], w_ref[...] = c["tinv"], c["u"], c["w"]
        state[...] = s * c["el"] + _bd("hck,hcv->hkv", c["kd"], vn)

    hblk = pl.BlockSpec((None, nh, CH, HD), lambda d, n: (d, 0, _scan_row_chunk(d, n, ns, nt), 0))
    hshape = jax.ShapeDtypeStruct((2, nh, nrows, HD), f32)
    return pl.pallas_call(
        body, name="gdn_fwd", grid=(2, nch),
        in_specs=[pl.BlockSpec((3, nh, CH, HD), lambda d, n: (0, 0, _scan_row_chunk(d, n, ns, nt), 0)),
                  pl.BlockSpec((None, CH, LANES), lambda d, n: (d, _scan_row_chunk(d, n, ns, nt), 0))],
        out_specs=[hblk, pl.BlockSpec((None, None, nh, HD, HD), lambda d, n: (d, n, 0, 0, 0)),
                   pl.BlockSpec((None, None, nh, CH, CH), lambda d, n: (d, n, 0, 0, 0)), hblk, hblk],
        out_shape=[hshape, jax.ShapeDtypeStruct((2, nch, nh, HD, HD), f32), jax.ShapeDtypeStruct((2, nch, nh, CH, CH), f32), hshape, hshape],
        scratch_shapes=[pltpu.VMEM((nh, HD, HD), f32)], compiler_params=_cparams(("arbitrary", "arbitrary")),
    )(qkv, gd)


def _gdn_bwd(qkv, gd, states, solved, do, nh, nlat, nctx):
    nrows = nlat + nctx
    ns, nt = nlat // CH, nctx // CH
    nch = ns + nt

    def body(qkv_ref, g_ref, st_ref, ti_ref, u_ref, w_ref, do_ref, dqkv_ref, dgd_ref, dstate):
        d, step = pl.program_id(0), pl.program_id(1)

        @pl.when(step == 0)
        def _():
            dstate[...] = jnp.zeros_like(dstate)

        q, k, v = qkv_ref[0], qkv_ref[1], qkv_ref[2]
        c = _gdn_chunk(q, k, v, g_ref[...], nh, d == 1, solved=(ti_ref[...], u_ref[...], w_ref[...]))
        in_ctx = (nch - 1 - step) < nt
        s, dsp = st_ref[...], dstate[...]
        dout = jnp.where(in_ctx, 0.0, do_ref[...])
        beta, gam, e, el, r = c["beta"], c["gam"], c["e"], c["el"], c["r"]
        tinv, u, w, pmat, amat = c["tinv"], c["u"], c["w"], c["pmat"], c["amat"]
        vn = u - _bd("hck,hkv->hcv", w, s)
        dvn = _bd("hdc,hdv->hcv", pmat, dout) + _bd("hck,hkv->hcv", c["kd"], dsp)
        dp = jnp.where(c["tril"], _bd("hcv,hdv->hcd", dout, vn), 0.0)
        dqd = _bd("hcv,hkv->hck", dout, s)
        dkd = _bd("hcv,hkv->hck", vn, dsp)
        dstate[...] = _bd("hck,hcv->hkv", c["qd"], dout) + dsp * el - _bd("hck,hcv->hkv", w, dvn)
        del_ = jnp.sum(jnp.sum(s * dsp, axis=2, keepdims=True), axis=1, keepdims=True)
        dw = -_bd("hcv,hkv->hck", dvn, s)
        dvb = _h3("hji,hjv->hiv", tinv, dvn)
        dkbe = _h3("hji,hjk->hik", tinv, dw)
        z = _bd("hiv,hjv->hij", dvn, u) + _bd("hik,hjk->hij", dw, w)
        da = jnp.where(c["stril"], -_h3("hji,hjl->hil", tinv, z), 0.0)
        dm = da * gam
        dkb = _bd("hij,hjk->hik", dm, k) + dkbe * e
        dn = dp * gam
        dqkv_ref[0] = _bd("hij,hjk->hik", dn, k) + dqd * e
        dqkv_ref[1] = _bd("hji,hjk->hik", dm, c["kb"]) + _bd("hji,hjk->hik", dn, q) + dkd * r + dkb * beta
        dqkv_ref[2] = dvb * beta
        gmat = da * amat + dp * pmat
        rows_minus_cols = jnp.sum(gmat, axis=-1, keepdims=True) - jnp.sum(jnp.swapaxes(gmat, 1, 2), axis=-1, keepdims=True)
        de = jnp.sum(dqd * q + dkbe * c["kb"], axis=-1, keepdims=True)
        drr = jnp.sum(dkd * k, axis=-1, keepdims=True) * r
        dgc = rows_minus_cols + de * e - drr
        dgl = jnp.sum(drr, axis=1, keepdims=True) + del_ * el
        row = lax.broadcasted_iota(jnp.int32, (1, CH, 1), 1)
        total_row = row == jnp.where(d == 1, 0, CH - 1)
        dgc = dgc + jnp.where(total_row, dgl, 0.0)
        dg = _x3("hij,hjl->hil", c["triuf"], dgc)
        db = jnp.sum(dvb * v + dkb * k, axis=-1, keepdims=True)
        lane = lax.broadcasted_iota(jnp.int32, (CH, LANES), 1)
        packed = jnp.zeros((CH, LANES), f32)
        for h in range(nh):
            packed = packed + jnp.where(lane == h, dg[h], 0.0) + jnp.where(lane == nh + h, db[h], 0.0)
        dgd_ref[...] = packed

    def rc(d, n):
        return _scan_row_chunk(d, nch - 1 - n, ns, nt)

    gblk = pl.BlockSpec((None, CH, LANES), lambda d, n: (d, rc(d, n), 0))
    return pl.pallas_call(
        body, name="gdn_bwd", grid=(2, nch),
        in_specs=[pl.BlockSpec((3, nh, CH, HD), lambda d, n: (0, 0, rc(d, n), 0)),
                  gblk,
                  pl.BlockSpec((None, None, nh, HD, HD), lambda d, n: (d, nch - 1 - n, 0, 0, 0)),
                  pl.BlockSpec((None, None, nh, CH, CH), lambda d, n: (d, nch - 1 - n, 0, 0, 0)),
                  pl.BlockSpec((None, nh, CH, HD), lambda d, n: (d, 0, rc(d, n), 0)),
                  pl.BlockSpec((None, nh, CH, HD), lambda d, n: (d, 0, rc(d, n), 0)),
                  pl.BlockSpec((nh, CH, HD), lambda d, n: (0, jnp.minimum(rc(d, n), ns - 1), 0))],
        out_specs=[pl.BlockSpec((None, 3, nh, CH, HD), lambda d, n: (d, 0, 0, rc(d, n), 0)), gblk],
        out_shape=[jax.ShapeDtypeStruct((2, 3, nh, nrows, HD), f32), jax.ShapeDtypeStruct((2, nrows, LANES), f32)],
        scratch_shapes=[pltpu.VMEM((nh, HD, HD), f32)], compiler_params=_cparams(("arbitrary", "arbitrary")),
    )(qkv, gd, states, *solved, do)


def _gate_norm_fwd(o5, h_all, dng, nh, nlat, zblk, tr):
    def body(of_ref, ob_ref, z_ref, g_ref, out_ref):
        o = of_ref[...] + ob_ref[...]
        rs = lax.rsqrt(jnp.mean(o * o, axis=-1, keepdims=True) + 1e-6)
        out_ref[...] = ((o * rs * g_ref[...]) * _silu(z_ref[...])).astype(out_ref.dtype)

    return pl.pallas_call(
        body, name="gate_norm_fwd", grid=(nh, nlat // tr),
        in_specs=[pl.BlockSpec((None, None, tr, HD), lambda j, i: (0, j, i, 0)), pl.BlockSpec((None, None, tr, HD), lambda j, i: (1, j, i, 0)),
                  pl.BlockSpec((tr, HD), lambda j, i: (i, zblk + j)), pl.BlockSpec((1, HD), lambda j, i: (0, 0))],
        out_specs=pl.BlockSpec((tr, HD), lambda j, i: (i, j)), out_shape=jax.ShapeDtypeStruct((nlat, nh * HD), bf16),
        compiler_params=_cparams(("parallel", "parallel")),
    )(o5, o5, h_all, dng)


def _gate_norm_bwd(dmix, o5, h_all, dng, nh, nlat, zblk, tr):
    def body(dm_ref, of_ref, ob_ref, z_ref, g_ref, do_ref, dz_ref, dg_ref):
        o = of_ref[...] + ob_ref[...]
        z, g, dm = z_ref[...], g_ref[...], dm_ref[...]
        rs = lax.rsqrt(jnp.mean(o * o, axis=-1, keepdims=True) + 1e-6)
        don = dm * _silu(z)
        dz_ref[...] = (dm * (o * rs * g) * _dsilu(z)).astype(dz_ref.dtype)
        dog = don * g
        do_ref[...] = rs * (dog - o * rs * rs * jnp.mean(dog * o, axis=-1, keepdims=True))

        @pl.when(jnp.logical_and(pl.program_id(0) == 0, pl.program_id(1) == 0))
        def _():
            dg_ref[...] = jnp.zeros_like(dg_ref)

        dg_ref[...] += _csum(don * o * rs)

    return pl.pallas_call(
        body, name="gate_norm_bwd", grid=(nh, nlat // tr),
        in_specs=[pl.BlockSpec((tr, HD), lambda j, i: (i, j)),
                  pl.BlockSpec((None, None, tr, HD), lambda j, i: (0, j, i, 0)), pl.BlockSpec((None, None, tr, HD), lambda j, i: (1, j, i, 0)),
                  pl.BlockSpec((tr, HD), lambda j, i: (i, zblk + j)), pl.BlockSpec((1, HD), lambda j, i: (0, 0))],
        out_specs=[pl.BlockSpec((None, tr, HD), lambda j, i: (j, i, 0)), pl.BlockSpec((tr, HD), lambda j, i: (i, j)),
                   pl.BlockSpec((1, HD), lambda j, i: (0, 0))],
        out_shape=[jax.ShapeDtypeStruct((nh, nlat, HD), f32), jax.ShapeDtypeStruct((nlat, nh * HD), bf16), jax.ShapeDtypeStruct((1, HD), f32)],
        compiler_params=_cparams(("arbitrary", "arbitrary")),
    )(dmix, o5, o5, h_all, dng)


def _mesh_pos():
    return lax.axis_index("x"), lax.axis_index("y"), lax.axis_index("c")


def _lin(p):
    return 4 * p[0] + 2 * p[1] + p[2]


def _all_gather(name, xs, after=None):
    nx = len(xs)
    extra = [] if after is None else [after]

    def body(*refs):
        xr, outr = refs[:nx], refs[nx + len(extra):2 * nx + len(extra)]
        send, recv, loc = refs[2 * nx + len(extra):]
        x, y, c = _mesh_pos()
        me, sib = (x, y, c), (x, y, 1 - c)
        chips = [(1 - x, y), (x, 1 - y), (1 - x, 1 - y)]

        def cp(l, k, block, to, src=None):
            rows = outr[l].at[_lin(block)]
            return pltpu.make_async_remote_copy(src_ref=rows if src is None else src, dst_ref=rows, send_sem=send.at[l, k],
                                                recv_sem=recv.at[l, k], device_id=to, device_id_type=MESH)

        mine = [pltpu.make_async_copy(xr[l], outr[l].at[_lin(me)], loc.at[l]) for l in range(nx)]
        for m in mine:
            m.start()
        first = []
        for l in range(nx):
            first.append(cp(l, 0, me, sib, src=xr[l]))
            first += [cp(l, 1 + j, me, (*chip, c), src=xr[l]) for j, chip in enumerate(chips)]
        for f in first:
            f.start()
        passed = []
        for l in range(nx):
            for j, chip in enumerate(chips):
                cp(l, 1 + j, (*chip, c), me).wait_recv()
                fwd = cp(l, 4 + j, (*chip, c), sib)
                fwd.start()
                passed.append(fwd)
        for l in range(nx):
            cp(l, 0, sib, me).wait_recv()
            for j, chip in enumerate(chips):
                cp(l, 4 + j, (*chip, 1 - c), me).wait_recv()
        for f in first + passed:
            f.wait_send()
        for m in mine:
            m.wait()

    anyspec = pl.BlockSpec(memory_space=pl.ANY)
    return pl.pallas_call(
        body, name=name, in_specs=[anyspec] * (nx + len(extra)), out_specs=[anyspec] * nx,
        out_shape=[jax.ShapeDtypeStruct((NDEV,) + a.shape, a.dtype) for a in xs],
        scratch_shapes=[pltpu.SemaphoreType.DMA((nx, 7)), pltpu.SemaphoreType.DMA((nx, 7)), pltpu.SemaphoreType.DMA((nx,))],
    )(*xs, *extra)


HBM_SPEC = pl.BlockSpec(memory_space=pltpu.HBM)
SEM_SPEC = pl.BlockSpec(memory_space=pltpu.SEMAPHORE)
ANY_SPEC = pl.BlockSpec(memory_space=pl.ANY)
EFFECT = pltpu.SideEffectType.DATAFLOW_SIDE_EFFECTING


def _peer(rel):
    x, y, c = _mesh_pos()
    return (1 - x if rel & 4 else x, 1 - y if rel & 2 else y, 1 - c if rel & 1 else c)


ALL_PEERS = (1, 2, 3, 4, 5, 6, 7)
SAME_CORE = (2, 4, 6)


def _slot(p, rels):
    return _lin(p) if rels == ALL_PEERS else 2 * p[0] + p[1]


def _xchg_start(name, x, gather, after, rels=ALL_PEERS):
    me = _slot(_mesh_pos(), rels)
    shape = (len(rels) + 1,) + x.shape if gather else x.shape
    own = x if gather else lax.dynamic_index_in_dim(x, me, 0, keepdims=False)
    land = lax.dynamic_update_index_in_dim(lax.empty(shape, x.dtype), own, me, 0)

    def body(x_ref, land_ref, after_ref, send, recv, x_thru, land_thru, token):
        mine = _slot(_mesh_pos(), rels)
        for k, rel in enumerate(rels):
            p = _peer(rel)
            pltpu.make_async_remote_copy(src_ref=x_ref if gather else x_ref.at[_slot(p, rels)], dst_ref=land_ref.at[mine],
                                         send_sem=send.at[k], recv_sem=recv.at[k], device_id=p, device_id_type=MESH).start()
        token[...] = jnp.zeros_like(token)

    return pl.pallas_call(
        body, name=name,
        out_shape=(pltpu.SemaphoreType.DMA((len(rels),)), pltpu.SemaphoreType.DMA((len(rels),)), pltpu.HBM(x.shape, x.dtype),
                   pltpu.HBM(shape, x.dtype), jax.ShapeDtypeStruct((8, LANES), f32)),
        in_specs=(HBM_SPEC, HBM_SPEC, ANY_SPEC), out_specs=(SEM_SPEC, SEM_SPEC, HBM_SPEC, HBM_SPEC, pl.BlockSpec(memory_space=pltpu.VMEM)),
        input_output_aliases={0: 2, 1: 3}, compiler_params=pltpu.CompilerParams(has_side_effects=EFFECT),
    )(pltpu.with_memory_space_constraint(x, pltpu.HBM), pltpu.with_memory_space_constraint(land, pltpu.HBM), after)


def _xchg_wait(name, started, gather, after, rels=ALL_PEERS):
    send, recv, x_thru, land_thru, _ = started

    def body(x_ref, land_ref, send, recv, after_ref, x_dead, got_ref):
        for k, rel in enumerate(rels):
            p = _peer(rel)
            cp = pltpu.make_async_remote_copy(src_ref=x_ref if gather else x_ref.at[_slot(p, rels)], dst_ref=land_ref.at[_slot(p, rels)],
                                              send_sem=send.at[k], recv_sem=recv.at[k], device_id=p, device_id_type=MESH)
            cp.wait_send()
            cp.wait_recv()

    return pl.pallas_call(
        body, name=name, out_shape=(pltpu.HBM(x_thru.shape, x_thru.dtype), pltpu.HBM(land_thru.shape, land_thru.dtype)),
        in_specs=(HBM_SPEC, HBM_SPEC, SEM_SPEC, SEM_SPEC, ANY_SPEC), out_specs=(HBM_SPEC, HBM_SPEC), input_output_aliases={0: 0, 1: 1},
        compiler_params=pltpu.CompilerParams(has_side_effects=EFFECT),
    )(x_thru, land_thru, send, recv, after)[1]


def _sib_swap(name, x):
    def body(x_ref, out_ref, send, recv):
        mx, my, mc = _mesh_pos()
        sib = (mx, my, 1 - mc)
        cps = [pltpu.make_async_remote_copy(src_ref=x_ref.at[2 * q + 1 - mc], dst_ref=out_ref.at[q], send_sem=send.at[q], recv_sem=recv.at[q],
                                            device_id=sib, device_id_type=MESH) for q in range(4)]
        for cp in cps:
            cp.start()
        for cp in cps:
            cp.wait_recv()
        for cp in cps:
            cp.wait_send()

    return pl.pallas_call(
        body, name=name, in_specs=[ANY_SPEC], out_specs=ANY_SPEC, out_shape=jax.ShapeDtypeStruct((4,) + x.shape[1:], x.dtype),
        scratch_shapes=[pltpu.SemaphoreType.DMA((4,)), pltpu.SemaphoreType.DMA((4,))],
    )(x)


def _behind(token, a):
    return a + token[0:1, 0:1].astype(a.dtype)


def _sum8(name, g):
    def body(g_ref, o_ref):
        acc = g_ref[0:1, :]
        for p in range(1, NDEV):
            acc = acc + g_ref[p:p + 1, :]
        o_ref[...] = acc

    return pl.pallas_call(body, name=name, out_shape=jax.ShapeDtypeStruct((1, g.shape[1]), f32),
                          compiler_params=pltpu.CompilerParams(vmem_limit_bytes=VMEM_LIMIT))(g)


def _pack(arrs):
    flat = jnp.concatenate([a.reshape(-1).astype(f32) for a in arrs])
    pad = (-flat.shape[0]) % (8 * LANES)
    return jnp.pad(flat, (0, pad)).reshape(1, -1)


def _unpack(vec, shapes):
    out, off = [], 0
    flat = vec.reshape(-1)
    for s in shapes:
        n = 1
        for d in s:
            n *= d
        out.append(flat[off:off + n].reshape(s))
        off += n
    return out


def _adam(name, w, m, v, g, parts, after=None):
    extra = [] if after is None else [after]
    nrows, width = w.shape
    tr = _row_tile(nrows, width * (3 if parts else 1), cap=512) if nrows >= 16 else nrows
    blk = pl.BlockSpec((tr, width), lambda i: (i, 0))
    gblk = pl.BlockSpec((parts, tr, width), lambda i: (0, i, 0)) if parts else blk
    c1 = 1.0 / (1.0 - ADAM_B1 ** ADAM_STEP)
    c2 = 1.0 / (1.0 - ADAM_B2 ** ADAM_STEP)

    def body(w_ref, m_ref, v_ref, g_ref, *rest):
        go_ref, d_ref, mo_ref, vo_ref = rest[len(extra):]
        if parts:
            gg = g_ref[0].astype(f32)
            for p in range(1, parts):
                gg = gg + g_ref[p].astype(f32)
        else:
            gg = g_ref[...]
        mn = ADAM_B1 * m_ref[...] + (1.0 - ADAM_B1) * gg
        vn = ADAM_B2 * v_ref[...] + (1.0 - ADAM_B2) * (gg * gg)
        go_ref[...] = gg
        mo_ref[...] = mn
        vo_ref[...] = vn
        d_ref[...] = -ADAM_LR * ((mn * c1) / (jnp.sqrt(vn * c2) + ADAM_EPS) + ADAM_WD * w_ref[...])

    sh = jax.ShapeDtypeStruct((nrows, width), f32)
    return pl.pallas_call(body, name=name, grid=(nrows // tr,), in_specs=[blk, blk, blk, gblk] + [pl.BlockSpec(memory_space=pl.ANY)] * len(extra),
                          out_specs=[blk] * 4, out_shape=[sh] * 4, compiler_params=_cparams(("parallel",)))(w, m, v, g, *extra)


def kernel(x, c, ctx, c_ctx, ln_in_g, ln_in_b, w_mod, b_mod, w_in, w_qkv_conv, a_log_f, dt_bias_f, a_log_b, dt_bias_b, dn_norm_g, conf_dw_w, conf_dw_b, conf_ln_g, conf_ln_b, w_out, ln1_g, ln1_b, w_mlp1, b_mlp1, w_mlp2, b_mlp2, ln2_g, ln2_b, loss_target, m_c_ctx, m_ln_in_g, m_ln_in_b, m_w_mod, m_b_mod, m_w_in, m_w_qkv_conv, m_a_log_f, m_dt_bias_f, m_a_log_b, m_dt_bias_b, m_dn_norm_g, m_conf_dw_w, m_conf_dw_b, m_conf_ln_g, m_conf_ln_b, m_w_out, m_ln1_g, m_ln1_b, m_w_mlp1, m_b_mlp1, m_w_mlp2, m_b_mlp2, m_ln2_g, m_ln2_b, v_c_ctx, v_ln_in_g, v_ln_in_b, v_w_mod, v_b_mod, v_w_in, v_w_qkv_conv, v_a_log_f, v_dt_bias_f, v_a_log_b, v_dt_bias_b, v_dn_norm_g, v_conf_dw_w, v_conf_dw_b, v_conf_ln_g, v_conf_ln_b, v_w_out, v_ln1_g, v_ln1_b, v_w_mlp1, v_b_mlp1, v_w_mlp2, v_b_mlp2, v_ln2_g, v_ln2_b):
    weights = dict(c_ctx=c_ctx, ln_in_g=ln_in_g, ln_in_b=ln_in_b, w_mod=w_mod, b_mod=b_mod, w_in=w_in, w_qkv_conv=w_qkv_conv, a_log_f=a_log_f, dt_bias_f=dt_bias_f, a_log_b=a_log_b, dt_bias_b=dt_bias_b, dn_norm_g=dn_norm_g, conf_dw_w=conf_dw_w, conf_dw_b=conf_dw_b, conf_ln_g=conf_ln_g, conf_ln_b=conf_ln_b, w_out=w_out, ln1_g=ln1_g, ln1_b=ln1_b, w_mlp1=w_mlp1, b_mlp1=b_mlp1, w_mlp2=w_mlp2, b_mlp2=b_mlp2, ln2_g=ln2_g, ln2_b=ln2_b)
    mom1 = dict(c_ctx=m_c_ctx, ln_in_g=m_ln_in_g, ln_in_b=m_ln_in_b, w_mod=m_w_mod, b_mod=m_b_mod, w_in=m_w_in, w_qkv_conv=m_w_qkv_conv, a_log_f=m_a_log_f, dt_bias_f=m_dt_bias_f, a_log_b=m_a_log_b, dt_bias_b=m_dt_bias_b, dn_norm_g=m_dn_norm_g, conf_dw_w=m_conf_dw_w, conf_dw_b=m_conf_dw_b, conf_ln_g=m_conf_ln_g, conf_ln_b=m_conf_ln_b, w_out=m_w_out, ln1_g=m_ln1_g, ln1_b=m_ln1_b, w_mlp1=m_w_mlp1, b_mlp1=m_b_mlp1, w_mlp2=m_w_mlp2, b_mlp2=m_b_mlp2, ln2_g=m_ln2_g, ln2_b=m_ln2_b)
    mom2 = dict(c_ctx=v_c_ctx, ln_in_g=v_ln_in_g, ln_in_b=v_ln_in_b, w_mod=v_w_mod, b_mod=v_b_mod, w_in=v_w_in, w_qkv_conv=v_w_qkv_conv, a_log_f=v_a_log_f, dt_bias_f=v_dt_bias_f, a_log_b=v_a_log_b, dt_bias_b=v_dt_bias_b, dn_norm_g=v_dn_norm_g, conf_dw_w=v_conf_dw_w, conf_dw_b=v_conf_dw_b, conf_ln_g=v_conf_ln_g, conf_ln_b=v_conf_ln_b, w_out=v_w_out, ln1_g=v_ln1_g, ln1_b=v_ln1_b, w_mlp1=v_w_mlp1, b_mlp1=v_b_mlp1, w_mlp2=v_w_mlp2, b_mlp2=v_b_mlp2, ln2_g=v_ln2_g, ln2_b=v_ln2_b)
    names = list(weights)

    me = _lin(_mesh_pos())
    S, D = x.shape[1], x.shape[2]
    T = ctx.shape[1]
    R = S + T
    DN = D // 2
    NH = DN // HD
    CONF = D - DN
    K7, K31 = w_qkv_conv.shape[1], conf_dw_w.shape[1]
    DFF = w_mlp1.shape[2] * NDEV
    INC = w_in.shape[2] * NDEV
    CONF_OFF = 4 * DN + 4 * NH
    NC = 4 * DN + 2 * CONF + GPAD
    QB, ZB, VB, GB = 0, 3 * DN // LANES, 4 * DN // LANES, (4 * DN + 2 * CONF) // LANES
    MODC = w_mod.shape[2]
    x2d, ctx2d, tgt = x[0], ctx[0], loss_target[0]
    row = lambda a: a.reshape(1, -1).astype(f32)

    c_all, w7_all, w31_all = _all_gather("ag_small", [c.astype(f32), w_qkv_conv[0], conf_dw_w[0]])
    w7 = jnp.pad(jnp.transpose(w7_all, (1, 0, 2)).reshape(K7, 3 * DN), ((0, (-K7) % 8), (0, 0)))
    w31 = jnp.pad(jnp.transpose(w31_all, (1, 0, 2)).reshape(K31, CONF), ((0, (-K31) % 8), (0, 0)))
    c16 = jnp.concatenate([c_all.reshape(NDEV, D), c_ctx.reshape(1, D).astype(f32), jnp.zeros((7, D), f32)], axis=0)
    (sc16,) = _rowcall("silu_c", lambda j, a: (_silu(a),), 16, 16, [(c16, D, 0, 0, False)], [], [(D, f32, False)], [])
    bmod_mine = lax.dynamic_slice_in_dim(b_mod.astype(f32), me * MODC, MODC, axis=1)
    mod_part = _mm("mod_fwd", sc16, w_mod[0], "nn", f32, extras=[("row", bmod_mine)], epilogue=lambda r, b: (r + b,))
    (mod_g,) = _all_gather("ag_mod", [mod_part])
    mod_all = jnp.transpose(mod_g, (1, 0, 2)).reshape(16, 6 * D)
    mod_me = lax.dynamic_slice_in_dim(mod_all, me, 1, axis=0)
    sh_a, sc_a, g_a, sh_m, sc_m, g_m = [mod_me[:, i * D:(i + 1) * D] for i in range(6)]
    csh_a, csc_a = mod_all[8:9, 0:D], mod_all[8:9, D:2 * D]

    (g_win,) = _all_gather("ag_w_in", [w_in[0].astype(bf16)], after=mod_g)
    ag_wout = _xchg_start("ag_wout_start", w_out[0].astype(bf16), True, g_win)
    ag_w1 = _xchg_start("ag_w1_start", w_mlp1[0].astype(bf16), True, ag_wout[4])
    win_full = jnp.transpose(g_win, (1, 0, 2)).reshape(D, INC)
    w_cat = jnp.concatenate([win_full[:, :4 * DN], win_full[:, CONF_OFF:], win_full[:, 4 * DN:CONF_OFF],
                             jnp.zeros((D, GPAD - 4 * NH), bf16)], axis=1)
    g0, b0 = row(ln_in_g), row(ln_in_b)
    g0_fwd = _behind(ag_w1[4], g0)

    def ln_mod(j, xt, g, b, sh, sc):
        xh, _ = _ln_stats(xt)
        xn = xh * g + b
        return xn, xn * (1.0 + sc) + sh

    tr = _row_tile(T, D)
    vec = lambda a: (a, a.shape[1], 0, False)
    xn, xm = _rowcall("ln_in_lat", ln_mod, S, tr, [(x2d, D, 0, 0, False)], [vec(g0_fwd), vec(b0), vec(sh_a), vec(sc_a)], [(D, f32, False), (D, bf16, False)], [])
    xcn, xcm = _rowcall("ln_in_ctx", ln_mod, T, tr, [(ctx2d, D, 0, 0, False)], [vec(g0), vec(b0), vec(csh_a), vec(csc_a)], [(D, f32, False), (D, bf16, False)], [])
    xm_all = jnp.concatenate([xm, xcm], axis=0)

    h_all = _mm("in_proj", xm_all, w_cat, "nn", f32, tm=1088, tn=1280)

    qkv = _qkv_conv_fwd(h_all, w7, R, S, NH, K7).reshape(3, NH, R, HD)
    lane = jnp.arange(LANES)
    is_a = ((lane < 4 * NH) & ((lane // NH) % 2 == 0)).astype(f32).reshape(1, LANES)
    pad_g = lambda a, b: jnp.concatenate([a.reshape(-1), jnp.zeros((NH,), f32), b.reshape(-1), jnp.zeros((LANES - 3 * NH,), f32)]).reshape(1, LANES)
    neg_a = pad_g(-jnp.exp(a_log_f.astype(f32)), -jnp.exp(a_log_b.astype(f32)))
    dt_v = pad_g(dt_bias_f.astype(f32), dt_bias_b.astype(f32))

    def gates_f(j, hg, isa, na, dt):
        return (jnp.where(isa > 0.5, na * _softplus(hg + dt), _sigmoid(hg)),)

    trg = _row_tile(T, LANES)
    (gates,) = _rowcall("gates_fwd", gates_f, R, trg, [(h_all, LANES, 0, GB, False)], [vec(is_a), vec(neg_a), vec(dt_v)], [(LANES, f32, False)], [])
    gpad = jnp.zeros((R, LANES - 2 * NH), f32)
    gd = jnp.stack([jnp.concatenate([gates[:, :2 * NH], gpad], axis=1), jnp.concatenate([gates[:, 2 * NH:4 * NH], gpad], axis=1)])

    o5, states, *solved = _gdn_fwd(qkv, gd, NH, S, T)
    ag_w2 = _xchg_start("ag_w2_start", w_mlp2[0].astype(bf16), True, o5)
    dng = _behind(ag_w2[4], row(dn_norm_g))
    trh = _row_tile(S, HD, cap=1024)
    dn_out = _gate_norm_fwd(o5, h_all, dng, NH, S, ZB, trh)

    yconv = _conf_conv_fwd(h_all, w31, S, CONF, VB, K31)
    bdw, clg, clb = row(conf_dw_b), row(conf_ln_g), row(conf_ln_b)

    def conf_ln(j, yc, b, g, bb):
        xh, _ = _ln_stats(yc + b)
        return (_silu(xh * g + bb),)

    trc = _row_tile(S, CONF)
    (conf_out,) = _rowcall("conf_ln_fwd", conf_ln, S, trc, [(yconv, CONF, 0, 0, False)], [vec(bdw), vec(clg), vec(clb)], [(CONF, bf16, False)], [])
    mix = jnp.concatenate([dn_out, conf_out], axis=1)
    wout_full = _xchg_wait("ag_wout_wait", ag_wout, True, mix).reshape(DN + CONF, D)
    y = _mm("out_proj", mix, wout_full, "nn", f32)

    l1g, l1b, l2g, l2b = row(ln1_g), row(ln1_b), row(ln2_g), row(ln2_b)

    def ln1_mod(j, xnt, yt, ga, g, b, sh, sc):
        xh, _ = _ln_stats(ALPHA * xnt + ga * yt)
        x1 = xh * g + b
        return x1, x1 * (1.0 + sc) + sh

    trd = _row_tile(S, D)
    x1, u = _rowcall("ln1_fwd", ln1_mod, S, trd, [(xn, D, 0, 0, False), (y, D, 0, 0, False)],
                     [vec(g_a), vec(l1g), vec(l1b), vec(sh_m), vec(sc_m)], [(D, f32, False), (D, bf16, False)], [])
    b1, b2 = row(b_mlp1), row(b_mlp2)
    g_w1 = _xchg_wait("ag_w1_wait", ag_w1, True, u)
    hh, act = _mm("mlp1", u, g_w1, "nn", (f32, bf16), extras=[("row", b1)], epilogue=lambda r, b: (r, jnp.square(jnp.maximum(r + b, 0.0))))
    w2_full = _xchg_wait("ag_w2_wait", ag_w2, True, act).reshape(DFF, D)
    y2 = _mm("mlp2", act, w2_full, "nn", f32)

    def ln2_loss(j, x1t, y2t, tg, bb2, gm, g, b):
        y2b = y2t + bb2
        xh, rstd = _ln_stats(ALPHA * x1t + gm * y2b)
        err = xh * g + b - tg
        dx2 = err * (1.0 / D)
        dr2 = _ln_bwd(dx2, xh, rstd, g)
        dy2 = dr2 * gm
        lsum = jnp.broadcast_to(jnp.sum(err * err).reshape(1, 1), (1, LANES))
        return dr2, dy2, _csum(dx2 * xh), _csum(dx2), _csum(dr2 * y2b), _csum(dy2), lsum

    dr2, dy2, d_l2g, d_l2b, d_gm, d_b2, lsum = _rowcall(
        "ln2_loss", ln2_loss, S, trd, [(x1, D, 0, 0, False), (y2, D, 0, 0, False), (tgt, D, 0, 0, False)],
        [vec(b2), vec(g_m), vec(l2g), vec(l2b)], [(D, f32, False), (D, bf16, False)], [(1, D, False)] * 4 + [(1, LANES, False)])
    loss = lax.psum(0.5 * lsum[0, 0] / D, ("x", "y", "c"))

    dw2_p = _mm("mlp2_dw", act, dy2, "tn", bf16)
    a2a_w2 = _xchg_start("a2a_w2_start", dw2_p.reshape(NDEV, DFF // NDEV, D), False, dw2_p)
    dhh = _mm("mlp2_dx", dy2, w2_full, "nt", bf16, extras=[("tile", hh), ("row", _behind(a2a_w2[4], b1))],
              epilogue=lambda r, h, b: (r * (2.0 * jnp.maximum(h + b, 0.0)),))
    (d_b1,) = _rowcall("b1_grad", lambda j, a: (_csum(a.astype(f32)),), S, _row_tile(S, DFF // 8), [(dhh, DFF, 0, 0, False)], [], [], [(1, DFF, False)])
    dw1_p = _mm("mlp1_dw", u, dhh, "tn", bf16, out_split=NDEV)
    a2a_w1 = _xchg_start("a2a_w1_start", dw1_p, False, dw1_p)
    du = _mm("mlp1_dx", dhh, g_w1, "nt", f32)

    def ln1_bwd(j, dr2t, dut, xnt, yt, ga, g, b, sc):
        xh, rstd = _ln_stats(ALPHA * xnt + ga * yt)
        x1t = xh * g + b
        dx1 = ALPHA * dr2t + dut * (1.0 + sc)
        dr1 = _ln_bwd(dx1, xh, rstd, g)
        return dr1, dr1 * ga, _csum(dut * x1t), _csum(dut), _csum(dx1 * xh), _csum(dx1), _csum(dr1 * yt)

    dr1, dy, d_scm, d_shm, d_l1g, d_l1b, d_ga = _rowcall(
        "ln1_bwd", ln1_bwd, S, trd, [(dr2, D, 0, 0, False), (du, D, 0, 0, False), (xn, D, 0, 0, False), (y, D, 0, 0, False)],
        [vec(g_a), vec(l1g), vec(l1b), vec(_behind(a2a_w1[4], sc_m))], [(D, f32, False), (D, bf16, False)], [(1, D, False)] * 5)
    dwout_p = _mm("out_proj_dw", mix, dy, "tn", bf16)
    a2a_wout = _xchg_start("a2a_wout_start", dwout_p.reshape(NDEV, (DN + CONF) // NDEV, D), False, dwout_p)
    dmix = _mm("out_proj_dx", dy, wout_full, "nt", f32)

    def conf_ln_b(j, dm, yc, b, g, bb):
        xh, rstd = _ln_stats(yc + b)
        dln = dm * _dsilu(xh * g + bb)
        dyc = _ln_bwd(dln, xh, rstd, g)
        return dyc, _csum(dln * xh), _csum(dln), _csum(dyc)

    dyc, d_clg, d_clb, d_bdw = _rowcall("conf_ln_bwd", conf_ln_b, S, trc, [(dmix, CONF, 0, DN // CONF, False), (yconv, CONF, 0, 0, False)],
                                        [vec(_behind(a2a_wout[4], bdw)), vec(clg), vec(clb)], [(CONF, f32, False)], [(1, CONF, False)] * 3)
    dval, dgate, dw31 = _conf_conv_bwd(h_all, w31, dyc, S, CONF, VB, K31)

    do, dz, d_dng = _gate_norm_bwd(dmix, o5, h_all, dng, NH, S, ZB, trh)
    dqkv, dgd = _gdn_bwd(qkv, gd, states, solved, do, NH, S, T)
    dh_qkv, dw7 = _qkv_conv_bwd(h_all, w7, dqkv.reshape(2, 3 * NH, R, HD), R, S, NH, K7)
    dgate_cols = jnp.concatenate([dgd[0][:, :2 * NH], dgd[1][:, :2 * NH], jnp.zeros((R, LANES - 4 * NH), f32)], axis=1)

    def gates_b(j, hg, dgt, isa, na, dt):
        sg = _sigmoid(hg)
        sp = _softplus(hg + dt)
        dpre = jnp.where(isa > 0.5, dgt * na * _sigmoid(hg + dt), dgt * sg * (1.0 - sg))
        return dpre, _csum(jnp.where(isa > 0.5, dgt * na * sp, 0.0)), _csum(jnp.where(isa > 0.5, dpre, 0.0))

    dh_g, d_alog, d_dt = _rowcall("gates_bwd", gates_b, R, trg, [(h_all, LANES, 0, GB, False), (dgate_cols, LANES, 0, 0, False)],
                                  [vec(is_a), vec(neg_a), vec(dt_v)], [(LANES, bf16, False)], [(1, LANES, False)] * 2)

    zpad = lambda a: jnp.concatenate([a, jnp.zeros((T, a.shape[1]), bf16)], axis=0)
    dh_all = jnp.concatenate([dh_qkv, zpad(dz), zpad(dval), zpad(dgate), dh_g, jnp.zeros((R, GPAD - LANES), bf16)], axis=1)
    dxm_all = _mm("in_proj_dx", dh_all, w_cat, "nt", f32, tm=1088, tk=2560)

    def ln_in_bwd_lat(j, xt, dr1t, dxm, g, b, sc):
        xh, rstd = _ln_stats(xt)
        xnt = xh * g + b
        dxn = ALPHA * dr1t + dxm * (1.0 + sc)
        return _ln_bwd(dxn, xh, rstd, g), _csum(dxm * xnt), _csum(dxm), _csum(dxn * xh), _csum(dxn)

    def ln_in_bwd_ctx(j, xt, dxm, g, b, sc):
        xh, rstd = _ln_stats(xt)
        xnt = xh * g + b
        dxn = dxm * (1.0 + sc)
        return _csum(dxm * xnt), _csum(dxm), _csum(dxn * xh), _csum(dxn)

    grad_x, d_sca, d_sha, d_g0a, d_b0a = _rowcall(
        "ln_in_bwd_lat", ln_in_bwd_lat, S, trd, [(x2d, D, 0, 0, False), (dr1, D, 0, 0, False), (dxm_all, D, 0, 0, False)],
        [vec(g0), vec(b0), vec(sc_a)], [(D, f32, False)], [(1, D, False)] * 4)
    d_csca, d_csha, d_g0b, d_b0b = _rowcall(
        "ln_in_bwd_ctx", ln_in_bwd_ctx, T, tr, [(ctx2d, D, 0, 0, False), (dxm_all, D, S // tr, 0, False)],
        [vec(g0), vec(b0), vec(csc_a)], [], [(1, D, False)] * 4)

    zD = jnp.zeros((1, D), f32)
    dmod_me = jnp.concatenate([d_sha, d_sca, d_ga, d_shm, d_scm, d_gm], axis=1)
    dmodc_me = jnp.concatenate([d_csha, d_csca, zD, zD, zD, zD], axis=1)
    small_names = ["ln_in_g", "ln_in_b", "a_log_f", "dt_bias_f", "a_log_b", "dt_bias_b", "dn_norm_g", "conf_dw_b", "conf_ln_g", "conf_ln_b",
                   "ln1_g", "ln1_b", "b_mlp1", "b_mlp2", "ln2_g", "ln2_b"]
    small_parts = [d_g0a + d_g0b, d_b0a + d_b0b, d_alog[:, 0:NH], d_dt[:, 0:NH], d_alog[:, 2 * NH:3 * NH], d_dt[:, 2 * NH:3 * NH], d_dng, d_bdw, d_clg, d_clb,
                   d_l1g, d_l1b, d_b1, d_b2, d_l2g, d_l2b]
    conv_parts = [dw7[:K7], dw31[:K31]]
    packed = _pack(small_parts + conv_parts + [dmodc_me])
    (pk_all, dmod_all) = _all_gather("ag_small_grads", [packed, dmod_me])
    summed = _sum8("sum_small_grads", pk_all.reshape(NDEV, -1))
    parts_sh = [a.shape for a in small_parts + conv_parts + [dmodc_me]]
    un = _unpack(summed, parts_sh)
    gsmall = dict(zip(small_names, un[:len(small_names)]))
    g_w7_full, g_w31_full, dmodc = un[len(small_names):]
    gsmall["w_qkv_conv"] = lax.dynamic_slice_in_dim(g_w7_full, me * w_qkv_conv.shape[2], w_qkv_conv.shape[2], axis=1)
    gsmall["conf_dw_w"] = lax.dynamic_slice_in_dim(g_w31_full, me * conf_dw_w.shape[2], conf_dw_w.shape[2], axis=1)

    dm16 = jnp.concatenate([dmod_all.reshape(NDEV, 6 * D), dmodc, jnp.zeros((7, 6 * D), f32)], axis=0)
    (gbmod,) = _rowcall("bmod_grad", lambda j, a: (_csum(a),), 16, 16, [(dm16, 6 * D, 0, 0, False)], [], [], [(1, 6 * D, False)])
    gsmall["b_mod"] = gbmod
    dm16_mine = lax.dynamic_slice_in_dim(dm16, me * MODC, MODC, axis=1)
    g_wmod = _mm("mod_dw", sc16, dm16_mine, "tn", f32)
    dsc16_part = _mm("mod_dx", dm16_mine, w_mod[0], "nt", f32)
    (dsc_all,) = _all_gather("ag_cctx", [dsc16_part[8:9]])
    dsilu_cctx = _sum8("sum_cctx", dsc_all.reshape(NDEV, D))
    (g_cctx,) = _rowcall("cctx_grad", lambda j, ds, cc: (ds * _dsilu(cc),), 1, 1, [(dsilu_cctx, D, 0, 0, False), (c_ctx.reshape(1, D).astype(f32), D, 0, 0, False)],
                         [], [(D, f32, False)], [])
    gsmall["c_ctx"] = g_cctx

    dwcat_p = _mm("in_proj_dw", xm_all, dh_all, "tn", bf16, tn=1280, tk=2176, after=g_cctx)
    dwin = jnp.concatenate([dwcat_p[:, :4 * DN], dwcat_p[:, 4 * DN + 2 * CONF:4 * DN + 2 * CONF + 4 * NH], dwcat_p[:, 4 * DN:4 * DN + 2 * CONF]], axis=1)
    dwin_p = jnp.transpose(dwin.reshape(D, NDEV, INC // NDEV), (1, 0, 2))
    from_sib = _sib_swap("sib_win", dwin_p)
    mine4 = lax.dynamic_index_in_dim(dwin_p.reshape(4, 2, D, INC // NDEV), lax.axis_index("c"), 1, keepdims=False)
    (chip_sum,) = _rowcall("add_win", lambda j, a, b: (a.astype(f32) + b.astype(f32),), 4 * D, _row_tile(4 * D, INC // NDEV),
                           [(mine4.reshape(4 * D, INC // NDEV), INC // NDEV, 0, 0, False), (from_sib.reshape(4 * D, INC // NDEV), INC // NDEV, 0, 0, False)],
                           [], [(INC // NDEV, bf16, False)], [])
    a2a_win = _xchg_start("a2a_win_start", chip_sum.reshape(4, D, INC // NDEV), False, chip_sum, SAME_CORE)

    grads, deltas, new_m, new_v = {}, {}, {}, {}
    res = _adam("adam_w_mod", w_mod[0], m_w_mod[0], v_w_mod[0], g_wmod, 0, after=a2a_win[4])
    grads["w_mod"], deltas["w_mod"], new_m["w_mod"], new_v["w_mod"] = [a.reshape(w_mod.shape) for a in res]
    after = res[1]
    for nm, started in (("w_mlp2", a2a_w2), ("w_mlp1", a2a_w1), ("w_out", a2a_wout), ("w_in", a2a_win)):
        w3 = weights[nm]
        rels = SAME_CORE if nm == "w_in" else ALL_PEERS
        g = _xchg_wait("a2a_" + nm + "_wait", started, False, after, rels)
        res = _adam("adam_" + nm, w3[0], mom1[nm][0], mom2[nm][0], g, len(rels) + 1)
        grads[nm], deltas[nm], new_m[nm], new_v[nm] = [a.reshape(w3.shape) for a in res]
        after = res[1]
    snames = [n for n in names if n not in grads]
    res = _adam("adam_small", _pack([weights[n] for n in snames]), _pack([mom1[n] for n in snames]), _pack([mom2[n] for n in snames]),
                _pack([gsmall[n] for n in snames]), 0)
    shapes = [weights[n].shape for n in snames]
    for dst, packed_out in zip((grads, deltas, new_m, new_v), res):
        for n, a in zip(snames, _unpack(packed_out, shapes)):
            dst[n] = a

    return (loss, grad_x.reshape(x.shape), *[grads[n] for n in names], *[deltas[n] for n in names],
            *[new_m[n] for n in names], *[new_v[n] for n in names])
```

```python
import functools

import jax
import jax.numpy as jnp
from jax import lax
from jax.experimental import pallas as pl
from jax.experimental.pallas import tpu as pltpu

f32 = jnp.float32
bf16 = jnp.bfloat16
MESH = pl.DeviceIdType.MESH

NDEV = 8
HD = 128
CH = 64
GW = 64
LANES = 128
GPAD = 512
LN_EPS = 1e-5
ALPHA = 2.0 ** 0.25
ADAM_LR, ADAM_B1, ADAM_B2, ADAM_EPS, ADAM_WD, ADAM_STEP = 0.001, 0.9, 0.999, 1e-08, 0.01, 10
VMEM_LIMIT = 56 * 1024 * 1024
ROW_BLOCK_BYTES = 1 << 20


def _cparams(sem):
    return pltpu.CompilerParams(dimension_semantics=sem, vmem_limit_bytes=VMEM_LIMIT)


def _tile(dim, pref, align):
    t = min(pref, dim)
    t -= t % align
    while t >= align:
        if dim % t == 0:
            return t
        t -= align
    return dim


def _row_tile(nrows, width, cap=256):
    t = max(16, min(cap, ROW_BLOCK_BYTES // (4 * width)))
    t = 1 << (t.bit_length() - 1)
    while nrows % t:
        t //= 2
    return max(t, 1)


def _sigmoid(x):
    return 1.0 / (1.0 + jnp.exp(-x))


def _silu(x):
    return x * _sigmoid(x)


def _dsilu(x):
    s = _sigmoid(x)
    return s * (1.0 + x * (1.0 - s))


def _softplus(x):
    return jnp.maximum(x, 0.0) + jnp.log(1.0 + jnp.exp(-jnp.abs(x)))


def _ln_stats(r):
    mu = jnp.mean(r, axis=-1, keepdims=True)
    xc = r - mu
    rstd = lax.rsqrt(jnp.mean(xc * xc, axis=-1, keepdims=True) + LN_EPS)
    return xc * rstd, rstd


def _ln_bwd(dy, xhat, rstd, g):
    dxh = dy * g
    return rstd * (dxh - jnp.mean(dxh, axis=-1, keepdims=True) - xhat * jnp.mean(dxh * xhat, axis=-1, keepdims=True))


def _csum(a):
    return jnp.sum(a, axis=0, keepdims=True)


def _mm(name, a, b, mode, out_dtype, out_split=1, tm=1024, tn=1024, tk=4096, extras=(), epilogue=None, after=None):
    if mode == "tn":
        K, M = a.shape
    else:
        M, K = a.shape
    bp = b.shape[0] if b.ndim == 3 else 1
    brows, bcols = b.shape[-2], b.shape[-1] * bp
    N = brows if mode == "nt" else bcols
    assert K == (bcols if mode == "nt" else brows), (name, a.shape, b.shape)
    tm = _tile(M, tm, LANES if mode == "tn" else 16)
    nsplit = max(out_split, bp if mode != "nt" else 1)
    tn = _tile(N // nsplit, tn, LANES)
    tk = _tile(K // (bp if mode == "nt" else 1), tk, LANES)
    nk = K // tk
    dims = {"nn": (((1,), (0,)), ((), ())), "nt": (((1,), (1,)), ((), ())), "tn": (((0,), (0,)), ((), ()))}[mode]
    out_dtypes = out_dtype if isinstance(out_dtype, tuple) else (out_dtype,)

    a_spec = pl.BlockSpec((tk, tm), lambda i, j, k: (k, i)) if mode == "tn" else pl.BlockSpec((tm, tk), lambda i, j, k: (i, k))
    if b.ndim == 3:
        if mode == "nt":
            per = (K // bp) // tk
            b_spec = pl.BlockSpec((None, tn, tk), lambda i, j, k: (k // per, j, k % per))
        else:
            per = (N // bp) // tn
            b_spec = pl.BlockSpec((None, tk, tn), lambda i, j, k: (j // per, k, j % per))
    elif mode == "nt":
        b_spec = pl.BlockSpec((tn, tk), lambda i, j, k: (j, k))
    else:
        b_spec = pl.BlockSpec((tk, tn), lambda i, j, k: (k, j))
    if out_split > 1:
        pero = (N // out_split) // tn
        o_spec = pl.BlockSpec((None, tm, tn), lambda i, j, k: (j // pero, i, j % pero))
        o_shapes = [jax.ShapeDtypeStruct((out_split, M, N // out_split), dt) for dt in out_dtypes]
    else:
        o_spec = pl.BlockSpec((tm, tn), lambda i, j, k: (i, j))
        o_shapes = [jax.ShapeDtypeStruct((M, N), dt) for dt in out_dtypes]
    in_specs = [a_spec, b_spec]
    args = [a, b]
    for kind, arr in extras:
        in_specs.append(pl.BlockSpec((1, tn), lambda i, j, k: (0, j)) if kind == "row" else pl.BlockSpec((tm, tn), lambda i, j, k: (i, j)))
        args.append(arr)
    n_in, n_out = len(args), len(out_dtypes)
    if after is not None:
        in_specs.append(pl.BlockSpec(memory_space=pl.ANY))
        args.append(after)
    n_all = len(args)

    def finish(refs, r):
        outs = (r,) if epilogue is None else epilogue(r, *[e[...] for e in refs[2:n_in]])
        for o_ref, val in zip(refs[n_all:n_all + n_out], outs):
            o_ref[...] = val.astype(o_ref.dtype)

    def body(*refs):
        part = lax.dot_general(refs[0][...].astype(bf16), refs[1][...].astype(bf16), dims, preferred_element_type=f32)
        if nk == 1:
            finish(refs, part)
            return
        acc = refs[-1]
        k = pl.program_id(2)

        @pl.when(k == 0)
        def _():
            acc[...] = part

        @pl.when(jnp.logical_and(k > 0, k < nk - 1))
        def _():
            acc[...] += part

        @pl.when(k == nk - 1)
        def _():
            finish(refs, acc[...] + part)

    res = pl.pallas_call(
        body, name=name, grid=(M // tm, N // tn, nk), in_specs=in_specs, out_specs=[o_spec] * n_out, out_shape=o_shapes,
        scratch_shapes=[pltpu.VMEM((tm, tn), f32)] if nk > 1 else [], compiler_params=_cparams(("parallel", "parallel", "arbitrary")),
    )(*args)
    return res[0] if n_out == 1 else res


def _rowcall(name, fn, nrows, tr, rows_in, vecs_in, rows_out, accs_out, ncol=1):
    nrt = nrows // tr
    in_specs, args = [], []
    for arr, w, ro, co, pc in rows_in:
        in_specs.append(pl.BlockSpec((tr, w), functools.partial(lambda j, i, ro, co, pc: (i + ro, co + (j if pc else 0)), ro=ro, co=co, pc=pc)))
        args.append(arr)
    for arr, w, co, pc in vecs_in:
        in_specs.append(pl.BlockSpec((arr.shape[0], w), functools.partial(lambda j, i, co, pc: (0, co + (j if pc else 0)), co=co, pc=pc)))
        args.append(arr)
    out_specs, out_shape = [], []
    for w, dt, pc in rows_out:
        out_specs.append(pl.BlockSpec((tr, w), functools.partial(lambda j, i, pc: (i, j if pc else 0), pc=pc)))
        out_shape.append(jax.ShapeDtypeStruct((nrows, w * (ncol if pc else 1)), dt))
    for k, w, pc in accs_out:
        out_specs.append(pl.BlockSpec((k, w), functools.partial(lambda j, i, pc: (0, j if pc else 0), pc=pc)))
        out_shape.append(jax.ShapeDtypeStruct((k, w * (ncol if pc else 1)), f32))
    n_in, n_ro = len(args), len(rows_out)

    def body(*refs):
        j, i = pl.program_id(0), pl.program_id(1)
        outs = fn(j, *[r[...] for r in refs[:n_in]])
        for r, val in zip(refs[n_in:n_in + n_ro], outs[:n_ro]):
            r[...] = val.astype(r.dtype)
        for (k, w, pc), r, val in zip(accs_out, refs[n_in + n_ro:], outs[n_ro:]):
            first = (i == 0) if pc else jnp.logical_and(i == 0, j == 0)

            @pl.when(first)
            def _(r=r):
                r[...] = jnp.zeros_like(r)

            r[...] += val

    res = pl.pallas_call(
        body, name=name, grid=(ncol, nrt), in_specs=in_specs, out_specs=out_specs, out_shape=out_shape,
        compiler_params=_cparams(("arbitrary", "arbitrary")),
    )(*args)
    return res


def _tap_valid(mode, t, d, nrows, nlat):
    if mode == "seg":
        tp = t + d
        return (tp >= 0) & (tp < nrows) & ((t < nlat) == (tp < nlat))
    if mode == "row":
        p = (t & (GW - 1)) + d
        return (p >= 0) & (p < GW)
    tp = t + d * GW
    return (tp >= 0) & (tp < nrows)


def _conv(x, w_ref, ktaps, mode, nlat, flip=False):
    nrows = x.shape[0]
    stride = GW if mode == "col" else 1
    t = lax.broadcasted_iota(jnp.int32, (nrows, 1), 0)
    acc = jnp.zeros_like(x)
    for j in range(ktaps):
        d = j - ktaps // 2
        jj = ktaps - 1 - j if flip else j
        wj = w_ref[jj:jj + 1, :]
        if d == 0:
            acc = acc + x * wj
        else:
            xs = pltpu.roll(x, (-d * stride) % nrows, 0)
            acc = acc + jnp.where(_tap_valid(mode, t, d, nrows, nlat), xs * wj, 0.0)
    return acc


def _conv_wgrad(dy, x, dw_ref, ktaps, mode, nlat):
    nrows = x.shape[0]
    stride = GW if mode == "col" else 1
    t = lax.broadcasted_iota(jnp.int32, (nrows, 1), 0)
    dw_ref[...] = jnp.zeros_like(dw_ref)
    for j in range(ktaps):
        d = j - ktaps // 2
        if d == 0:
            prod = x * dy
        else:
            xs = pltpu.roll(x, (-d * stride) % nrows, 0)
            prod = jnp.where(_tap_valid(mode, t, d, nrows, nlat), xs * dy, 0.0)
        dw_ref[j:j + 1, :] = _csum(prod)


def _qkv_post(j, pre, nh):
    a = _silu(pre)
    inv = lax.rsqrt(jnp.sum(a * a, axis=-1, keepdims=True) + 1e-6)
    scale = jnp.where(j < nh, HD ** -0.5, 1.0).astype(f32)
    nrm = jnp.where(j < 2 * nh, inv, 1.0) * scale
    return a, inv, nrm


def _qkv_conv_fwd(h_all, w7, nrows, nlat, nh, ktaps):
    ntile = 3 * nh

    def body(h_ref, w_ref, o_ref):
        j = pl.program_id(0)
        pre = _conv(h_ref[...], w_ref, ktaps, "seg", nlat)
        a, _, nrm = _qkv_post(j, pre, nh)
        o_ref[...] = a * nrm

    return pl.pallas_call(
        body, name="qkv_conv_fwd", grid=(ntile,),
        in_specs=[pl.BlockSpec((nrows, HD), lambda j: (0, j)), pl.BlockSpec((w7.shape[0], HD), lambda j: (0, j))],
        out_specs=pl.BlockSpec((None, nrows, HD), lambda j: (j, 0, 0)),
        out_shape=jax.ShapeDtypeStruct((ntile, nrows, HD), f32), compiler_params=_cparams(("parallel",)),
    )(h_all, w7)


def _qkv_conv_bwd(h_all, w7, dqkv, nrows, nlat, nh, ktaps):
    ntile = 3 * nh

    def body(h_ref, w_ref, d0_ref, d1_ref, dh_ref, dw_ref):
        j = pl.program_id(0)
        hx = h_ref[...]
        pre = _conv(hx, w_ref, ktaps, "seg", nlat)
        a, inv, nrm = _qkv_post(j, pre, nh)
        dn = (d0_ref[...] + d1_ref[...]) * jnp.where(j < nh, HD ** -0.5, 1.0).astype(f32)
        n = a * inv
        da_norm = inv * (dn - n * jnp.sum(dn * n, axis=-1, keepdims=True))
        da = jnp.where(j < 2 * nh, da_norm, dn)
        dpre = da * _dsilu(pre)
        dh_ref[...] = _conv(dpre, w_ref, ktaps, "seg", nlat, flip=True).astype(dh_ref.dtype)
        _conv_wgrad(dpre, hx, dw_ref, ktaps, "seg", nlat)

    blk = pl.BlockSpec((nrows, HD), lambda j: (0, j))
    wblk = pl.BlockSpec((w7.shape[0], HD), lambda j: (0, j))
    dblk = lambda d: pl.BlockSpec((None, None, nrows, HD), lambda j: (d, j, 0, 0))
    return pl.pallas_call(
        body, name="qkv_conv_bwd", grid=(ntile,), in_specs=[blk, wblk, dblk(0), dblk(1)], out_specs=[blk, wblk],
        out_shape=[jax.ShapeDtypeStruct((nrows, ntile * HD), bf16), jax.ShapeDtypeStruct((w7.shape[0], ntile * HD), f32)],
        compiler_params=_cparams(("parallel",)),
    )(h_all, w7, dqkv, dqkv)


def _conf_conv_fwd(h_all, w31, nlat, conf, val_blk, ktaps):
    nt = conf // LANES
    nhalf = nt // 2

    def body(v_ref, g_ref, w_ref, o_ref):
        j = pl.program_id(0)
        glu = v_ref[...] * _sigmoid(g_ref[...])

        @pl.when(j < nhalf)
        def _():
            o_ref[...] = _conv(glu, w_ref, ktaps, "row", nlat)

        @pl.when(j >= nhalf)
        def _():
            o_ref[...] = _conv(glu, w_ref, ktaps, "col", nlat)

    return pl.pallas_call(
        body, name="conf_conv_fwd", grid=(nt,),
        in_specs=[pl.BlockSpec((nlat, LANES), lambda j: (0, val_blk + j)), pl.BlockSpec((nlat, LANES), lambda j: (0, val_blk + nt + j)),
                  pl.BlockSpec((w31.shape[0], LANES), lambda j: (0, j))],
        out_specs=pl.BlockSpec((nlat, LANES), lambda j: (0, j)),
        out_shape=jax.ShapeDtypeStruct((nlat, conf), f32), compiler_params=_cparams(("parallel",)),
    )(h_all, h_all, w31)


def _conf_conv_bwd(h_all, w31, dyc, nlat, conf, val_blk, ktaps):
    nt = conf // LANES
    nhalf = nt // 2

    def body(v_ref, g_ref, w_ref, dy_ref, dv_ref, dg_ref, dw_ref):
        j = pl.program_id(0)
        val, sg = v_ref[...], _sigmoid(g_ref[...])
        glu = val * sg
        dy = dy_ref[...]

        def run(mode):
            dglu = _conv(dy, w_ref, ktaps, mode, nlat, flip=True)
            dv_ref[...] = (dglu * sg).astype(dv_ref.dtype)
            dg_ref[...] = (dglu * val * sg * (1.0 - sg)).astype(dg_ref.dtype)
            _conv_wgrad(dy, glu, dw_ref, ktaps, mode, nlat)

        @pl.when(j < nhalf)
        def _():
            run("row")

        @pl.when(j >= nhalf)
        def _():
            run("col")

    blk = pl.BlockSpec((nlat, LANES), lambda j: (0, j))
    wblk = pl.BlockSpec((w31.shape[0], LANES), lambda j: (0, j))
    return pl.pallas_call(
        body, name="conf_conv_bwd", grid=(nt,),
        in_specs=[pl.BlockSpec((nlat, LANES), lambda j: (0, val_blk + j)), pl.BlockSpec((nlat, LANES), lambda j: (0, val_blk + nt + j)), wblk, blk],
        out_specs=[blk, blk, wblk],
        out_shape=[jax.ShapeDtypeStruct((nlat, conf), bf16), jax.ShapeDtypeStruct((nlat, conf), bf16),
                   jax.ShapeDtypeStruct((w31.shape[0], conf), f32)],
        compiler_params=_cparams(("parallel",)),
    )(h_all, h_all, w31, dyc)


def _bd(eq, a, b):
    return jnp.einsum(eq, a.astype(bf16), b.astype(bf16), preferred_element_type=f32)


def _split(x, pieces):
    out = []
    for _ in range(pieces - 1):
        p = x.astype(bf16)
        out.append(p)
        x = x - p.astype(f32)
    return out + [x.astype(bf16)]


def _h3(eq, a, b):
    (ah, al), (bh, bl) = _split(a, 2), _split(b, 2)
    d = lambda p, q: jnp.einsum(eq, p, q, preferred_element_type=f32)
    return d(ah, bh) + (d(ah, bl) + d(al, bh))


def _x3(eq, a, b):
    d = lambda p, q: jnp.einsum(eq, p, q, preferred_element_type=f32)
    ae, (b1, b2, b3) = a.astype(bf16), _split(b, 3)
    return d(ae, b1) + (d(ae, b2) + d(ae, b3))


def _gdn_chunk(q, k, v, gt, nh, rev, solved=None):
    g = jnp.stack([jnp.broadcast_to(gt[:, h:h + 1], (CH, LANES)) for h in range(nh)])
    beta = jnp.stack([jnp.broadcast_to(gt[:, nh + h:nh + h + 1], (CH, LANES)) for h in range(nh)])
    ii = lax.broadcasted_iota(jnp.int32, (CH, CH), 0)
    jj = lax.broadcasted_iota(jnp.int32, (CH, CH), 1)
    si, sj = jnp.where(rev, jj, ii), jnp.where(rev, ii, jj)
    tril, stril = (si >= sj)[None], (si > sj)[None]
    trilf = jnp.broadcast_to(tril.astype(f32), (nh, CH, CH))
    triuf = jnp.broadcast_to((sj >= si)[None].astype(f32), (nh, CH, CH))
    gc = _x3("hit,htl->hil", trilf, g)
    gc_row = _x3("hil,hjl->hij", jnp.full((nh, CH, LANES), 1.0 / LANES, f32), gc)
    diff = gc[:, :, :CH] - gc_row
    gam = jnp.where(tril, jnp.exp(jnp.where(tril, diff, 0.0)), 0.0)
    e = jnp.exp(gc)
    gl = jnp.where(rev, gc[:, 0:1, :], gc[:, CH - 1:CH, :])
    el = jnp.exp(gl)
    r = jnp.exp(gl - gc)
    kb, vb = k * beta, v * beta
    kbe = kb * e
    amat = jnp.where(stril, _bd("hik,hjk->hij", kb, k) * gam, 0.0)
    if solved is None:
        xp = -amat
        tinv = (ii == jj).astype(f32)[None] + xp
        for _ in range(5):
            xp = _h3("hij,hjk->hik", xp, xp)
            tinv = tinv + _h3("hij,hjk->hik", tinv, xp)
        u = _h3("hij,hjv->hiv", tinv, vb)
        w = _h3("hij,hjk->hik", tinv, kbe)
    else:
        tinv, u, w = solved
    pmat = jnp.where(tril, _bd("hik,hjk->hij", q, k) * gam, 0.0)
    return dict(beta=beta, gam=gam, e=e, el=el, r=r, kb=kb, vb=vb, kbe=kbe, amat=amat, tinv=tinv, u=u, w=w, pmat=pmat,
                qd=q * e, kd=k * r, tril=tril, stril=stril, triuf=triuf)


def _scan_row_chunk(d, n, ns, nt):
    fwd = jnp.where(n < nt, ns + n, n - nt)
    return jnp.where(d == 0, fwd, ns + nt - 1 - n)


def _gdn_fwd(qkv, gd, nh, nlat, nctx):
    nrows = nlat + nctx
    ns, nt = nlat // CH, nctx // CH
    nch = ns + nt

    def body(qkv_ref, g_ref, o_ref, st_ref, ti_ref, u_ref, w_ref, state):
        d, n = pl.program_id(0), pl.program_id(1)

        @pl.when(n == 0)
        def _():
            state[...] = jnp.zeros_like(state)

        c = _gdn_chunk(qkv_ref[0], qkv_ref[1], qkv_ref[2], g_ref[...], nh, d == 1)
        s = state[...]
        vn = c["u"] - _bd("hck,hkv->hcv", c["w"], s)
        o_ref[...] = _bd("hck,hkv->hcv", c["qd"], s) + _bd("hcd,hdv->hcv", c["pmat"], vn)
        st_ref[...] = s
        ti_ref[...], u_ref[...], w_ref[...] = c["tinv"], c["u"], c["w"]
        state[...] = s * c["el"] + _bd("hck,hcv->hkv", c["kd"], vn)

    hblk = pl.BlockSpec((None, nh, CH, HD), lambda d, n: (d, 0, _scan_row_chunk(d, n, ns, nt), 0))
    hshape = jax.ShapeDtypeStruct((2, nh, nrows, HD), f32)
    return pl.pallas_call(
        body, name="gdn_fwd", grid=(2, nch),
        in_specs=[pl.BlockSpec((3, nh, CH, HD), lambda d, n: (0, 0, _scan_row_chunk(d, n, ns, nt), 0)),
                  pl.BlockSpec((None, CH, LANES), lambda d, n: (d, _scan_row_chunk(d, n, ns, nt), 0))],
        out_specs=[hblk, pl.BlockSpec((None, None, nh, HD, HD), lambda d, n: (d, n, 0, 0, 0)),
                   pl.BlockSpec((None, None, nh, CH, CH), lambda d, n: (d, n, 0, 0, 0)), hblk, hblk],
        out_shape=[hshape, jax.ShapeDtypeStruct((2, nch, nh, HD, HD), f32), jax.ShapeDtypeStruct((2, nch, nh, CH, CH), f32), hshape, hshape],
        scratch_shapes=[pltpu.VMEM((nh, HD, HD), f32)], compiler_params=_cparams(("arbitrary", "arbitrary")),
    )(qkv, gd)


def _gdn_bwd(qkv, gd, states, solved, do, nh, nlat, nctx):
    nrows = nlat + nctx
    ns, nt = nlat // CH, nctx // CH
    nch = ns + nt

    def body(qkv_ref, g_ref, st_ref, ti_ref, u_ref, w_ref, do_ref, dqkv_ref, dgd_ref, dstate):
        d, step = pl.program_id(0), pl.program_id(1)

        @pl.when(step == 0)
        def _():
            dstate[...] = jnp.zeros_like(dstate)

        q, k, v = qkv_ref[0], qkv_ref[1], qkv_ref[2]
        c = _gdn_chunk(q, k, v, g_ref[...], nh, d == 1, solved=(ti_ref[...], u_ref[...], w_ref[...]))
        in_ctx = (nch - 1 - step) < nt
        s, dsp = st_ref[...], dstate[...]
        dout = jnp.where(in_ctx, 0.0, do_ref[...])
        beta, gam, e, el, r = c["beta"], c["gam"], c["e"], c["el"], c["r"]
        tinv, u, w, pmat, amat = c["tinv"], c["u"], c["w"], c["pmat"], c["amat"]
        vn = u - _bd("hck,hkv->hcv", w, s)
        dvn = _bd("hdc,hdv->hcv", pmat, dout) + _bd("hck,hkv->hcv", c["kd"], dsp)
        dp = jnp.where(c["tril"], _bd("hcv,hdv->hcd", dout, vn), 0.0)
        dqd = _bd("hcv,hkv->hck", dout, s)
        dkd = _bd("hcv,hkv->hck", vn, dsp)
        dstate[...] = _bd("hck,hcv->hkv", c["qd"], dout) + dsp * el - _bd("hck,hcv->hkv", w, dvn)
        del_ = jnp.sum(jnp.sum(s * dsp, axis=2, keepdims=True), axis=1, keepdims=True)
        dw = -_bd("hcv,hkv->hck", dvn, s)
        dvb = _h3("hji,hjv->hiv", tinv, dvn)
        dkbe = _h3("hji,hjk->hik", tinv, dw)
        z = _bd("hiv,hjv->hij", dvn, u) + _bd("hik,hjk->hij", dw, w)
        da = jnp.where(c["stril"], -_h3("hji,hjl->hil", tinv, z), 0.0)
        dm = da * gam
        dkb = _bd("hij,hjk->hik", dm, k) + dkbe * e
        dn = dp * gam
        dqkv_ref[0] = _bd("hij,hjk->hik", dn, k) + dqd * e
        dqkv_ref[1] = _bd("hji,hjk->hik", dm, c["kb"]) + _bd("hji,hjk->hik", dn, q) + dkd * r + dkb * beta
        dqkv_ref[2] = dvb * beta
        gmat = da * amat + dp * pmat
        rows_minus_cols = jnp.sum(gmat, axis=-1, keepdims=True) - jnp.sum(jnp.swapaxes(gmat, 1, 2), axis=-1, keepdims=True)
        de = jnp.sum(dqd * q + dkbe * c["kb"], axis=-1, keepdims=True)
        drr = jnp.sum(dkd * k, axis=-1, keepdims=True) * r
        dgc = rows_minus_cols + de * e - drr
        dgl = jnp.sum(drr, axis=1, keepdims=True) + del_ * el
        row = lax.broadcasted_iota(jnp.int32, (1, CH, 1), 1)
        total_row = row == jnp.where(d == 1, 0, CH - 1)
        dgc = dgc + jnp.where(total_row, dgl, 0.0)
        dg = _x3("hij,hjl->hil", c["triuf"], dgc)
        db = jnp.sum(dvb * v + dkb * k, axis=-1, keepdims=True)
        lane = lax.broadcasted_iota(jnp.int32, (CH, LANES), 1)
        packed = jnp.zeros((CH, LANES), f32)
        for h in range(nh):
            packed = packed + jnp.where(lane == h, dg[h], 0.0) + jnp.where(lane == nh + h, db[h], 0.0)
        dgd_ref[...] = packed

    def rc(d, n):
        return _scan_row_chunk(d, nch - 1 - n, ns, nt)

    gblk = pl.BlockSpec((None, CH, LANES), lambda d, n: (d, rc(d, n), 0))
    return pl.pallas_call(
        body, name="gdn_bwd", grid=(2, nch),
        in_specs=[pl.BlockSpec((3, nh, CH, HD), lambda d, n: (0, 0, rc(d, n), 0)),
                  gblk,
                  pl.BlockSpec((None, None, nh, HD, HD), lambda d, n: (d, nch - 1 - n, 0, 0, 0)),
                  pl.BlockSpec((None, None, nh, CH, CH), lambda d, n: (d, nch - 1 - n, 0, 0, 0)),
                  pl.BlockSpec((None, nh, CH, HD), lambda d, n: (d, 0, rc(d, n), 0)),
                  pl.BlockSpec((None, nh, CH, HD), lambda d, n: (d, 0, rc(d, n), 0)),
                  pl.BlockSpec((nh, CH, HD), lambda d, n: (0, jnp.minimum(rc(d, n), ns - 1), 0))],
        out_specs=[pl.BlockSpec((None, 3, nh, CH, HD), lambda d, n: (d, 0, 0, rc(d, n), 0)), gblk],
        out_shape=[jax.ShapeDtypeStruct((2, 3, nh, nrows, HD), f32), jax.ShapeDtypeStruct((2, nrows, LANES), f32)],
        scratch_shapes=[pltpu.VMEM((nh, HD, HD), f32)], compiler_params=_cparams(("arbitrary", "arbitrary")),
    )(qkv, gd, states, *solved, do)


def _gate_norm_fwd(o5, h_all, dng, nh, nlat, zblk, tr):
    def body(of_ref, ob_ref, z_ref, g_ref, out_ref):
        o = of_ref[...] + ob_ref[...]
        rs = lax.rsqrt(jnp.mean(o * o, axis=-1, keepdims=True) + 1e-6)
        out_ref[...] = ((o * rs * g_ref[...]) * _silu(z_ref[...])).astype(out_ref.dtype)

    return pl.pallas_call(
        body, name="gate_norm_fwd", grid=(nh, nlat // tr),
        in_specs=[pl.BlockSpec((None, None, tr, HD), lambda j, i: (0, j, i, 0)), pl.BlockSpec((None, None, tr, HD), lambda j, i: (1, j, i, 0)),
                  pl.BlockSpec((tr, HD), lambda j, i: (i, zblk + j)), pl.BlockSpec((1, HD), lambda j, i: (0, 0))],
        out_specs=pl.BlockSpec((tr, HD), lambda j, i: (i, j)), out_shape=jax.ShapeDtypeStruct((nlat, nh * HD), bf16),
        compiler_params=_cparams(("parallel", "parallel")),
    )(o5, o5, h_all, dng)


def _gate_norm_bwd(dmix, o5, h_all, dng, nh, nlat, zblk, tr):
    def body(dm_ref, of_ref, ob_ref, z_ref, g_ref, do_ref, dz_ref, dg_ref):
        o = of_ref[...] + ob_ref[...]
        z, g, dm = z_ref[...], g_ref[...], dm_ref[...]
        rs = lax.rsqrt(jnp.mean(o * o, axis=-1, keepdims=True) + 1e-6)
        don = dm * _silu(z)
        dz_ref[...] = (dm * (o * rs * g) * _dsilu(z)).astype(dz_ref.dtype)
        dog = don * g
        do_ref[...] = rs * (dog - o * rs * rs * jnp.mean(dog * o, axis=-1, keepdims=True))

        @pl.when(jnp.logical_and(pl.program_id(0) == 0, pl.program_id(1) == 0))
        def _():
            dg_ref[...] = jnp.zeros_like(dg_ref)

        dg_ref[...] += _csum(don * o * rs)

    return pl.pallas_call(
        body, name="gate_norm_bwd", grid=(nh, nlat // tr),
        in_specs=[pl.BlockSpec((tr, HD), lambda j, i: (i, j)),
                  pl.BlockSpec((None, None, tr, HD), lambda j, i: (0, j, i, 0)), pl.BlockSpec((None, None, tr, HD), lambda j, i: (1, j, i, 0)),
                  pl.BlockSpec((tr, HD), lambda j, i: (i, zblk + j)), pl.BlockSpec((1, HD), lambda j, i: (0, 0))],
        out_specs=[pl.BlockSpec((None, tr, HD), lambda j, i: (j, i, 0)), pl.BlockSpec((tr, HD), lambda j, i: (i, j)),
                   pl.BlockSpec((1, HD), lambda j, i: (0, 0))],
        out_shape=[jax.ShapeDtypeStruct((nh, nlat, HD), f32), jax.ShapeDtypeStruct((nlat, nh * HD), bf16), jax.ShapeDtypeStruct((1, HD), f32)],
        compiler_params=_cparams(("arbitrary", "arbitrary")),
    )(dmix, o5, o5, h_all, dng)


def _mesh_pos():
    return lax.axis_index("x"), lax.axis_index("y"), lax.axis_index("c")


def _lin(p):
    return 4 * p[0] + 2 * p[1] + p[2]


def _all_gather(name, xs, after=None):
    nx = len(xs)
    extra = [] if after is None else [after]

    def body(*refs):
        xr, outr = refs[:nx], refs[nx + len(extra):2 * nx + len(extra)]
        send, recv, loc = refs[2 * nx + len(extra):]
        x, y, c = _mesh_pos()
        me, sib = (x, y, c), (x, y, 1 - c)
        chips = [(1 - x, y), (x, 1 - y), (1 - x, 1 - y)]

        def cp(l, k, block, to, src=None):
            rows = outr[l].at[_lin(block)]
            return pltpu.make_async_remote_copy(src_ref=rows if src is None else src, dst_ref=rows, send_sem=send.at[l, k],
                                                recv_sem=recv.at[l, k], device_id=to, device_id_type=MESH)

        mine = [pltpu.make_async_copy(xr[l], outr[l].at[_lin(me)], loc.at[l]) for l in range(nx)]
        for m in mine:
            m.start()
        first = []
        for l in range(nx):
            first.append(cp(l, 0, me, sib, src=xr[l]))
            first += [cp(l, 1 + j, me, (*chip, c), src=xr[l]) for j, chip in enumerate(chips)]
        for f in first:
            f.start()
        passed = []
        for l in range(nx):
            for j, chip in enumerate(chips):
                cp(l, 1 + j, (*chip, c), me).wait_recv()
                fwd = cp(l, 4 + j, (*chip, c), sib)
                fwd.start()
                passed.append(fwd)
        for l in range(nx):
            cp(l, 0, sib, me).wait_recv()
            for j, chip in enumerate(chips):
                cp(l, 4 + j, (*chip, 1 - c), me).wait_recv()
        for f in first + passed:
            f.wait_send()
        for m in mine:
            m.wait()

    anyspec = pl.BlockSpec(memory_space=pl.ANY)
    return pl.pallas_call(
        body, name=name, in_specs=[anyspec] * (nx + len(extra)), out_specs=[anyspec] * nx,
        out_shape=[jax.ShapeDtypeStruct((NDEV,) + a.shape, a.dtype) for a in xs],
        scratch_shapes=[pltpu.SemaphoreType.DMA((nx, 7)), pltpu.SemaphoreType.DMA((nx, 7)), pltpu.SemaphoreType.DMA((nx,))],
    )(*xs, *extra)


HBM_SPEC = pl.BlockSpec(memory_space=pltpu.HBM)
SEM_SPEC = pl.BlockSpec(memory_space=pltpu.SEMAPHORE)
ANY_SPEC = pl.BlockSpec(memory_space=pl.ANY)
EFFECT = pltpu.SideEffectType.DATAFLOW_SIDE_EFFECTING


def _peer(rel):
    x, y, c = _mesh_pos()
    return (1 - x if rel & 4 else x, 1 - y if rel & 2 else y, 1 - c if rel & 1 else c)


ALL_PEERS = (1, 2, 3, 4, 5, 6, 7)
SAME_CORE = (2, 4, 6)


def _slot(p, rels):
    return _lin(p) if rels == ALL_PEERS else 2 * p[0] + p[1]


def _xchg_start(name, x, gather, after, rels=ALL_PEERS):
    me = _slot(_mesh_pos(), rels)
    shape = (len(rels) + 1,) + x.shape if gather else x.shape
    own = x if gather else lax.dynamic_index_in_dim(x, me, 0, keepdims=False)
    land = lax.dynamic_update_index_in_dim(lax.empty(shape, x.dtype), own, me, 0)

    def body(x_ref, land_ref, after_ref, send, recv, x_thru, land_thru, token):
        mine = _slot(_mesh_pos(), rels)
        for k, rel in enumerate(rels):
            p = _peer(rel)
            pltpu.make_async_remote_copy(src_ref=x_ref if gather else x_ref.at[_slot(p, rels)], dst_ref=land_ref.at[mine],
                                         send_sem=send.at[k], recv_sem=recv.at[k], device_id=p, device_id_type=MESH).start()
        token[...] = jnp.zeros_like(token)

    return pl.pallas_call(
        body, name=name,
        out_shape=(pltpu.SemaphoreType.DMA((len(rels),)), pltpu.SemaphoreType.DMA((len(rels),)), pltpu.HBM(x.shape, x.dtype),
                   pltpu.HBM(shape, x.dtype), jax.ShapeDtypeStruct((8, LANES), f32)),
        in_specs=(HBM_SPEC, HBM_SPEC, ANY_SPEC), out_specs=(SEM_SPEC, SEM_SPEC, HBM_SPEC, HBM_SPEC, pl.BlockSpec(memory_space=pltpu.VMEM)),
        input_output_aliases={0: 2, 1: 3}, compiler_params=pltpu.CompilerParams(has_side_effects=EFFECT),
    )(pltpu.with_memory_space_constraint(x, pltpu.HBM), pltpu.with_memory_space_constraint(land, pltpu.HBM), after)


def _xchg_wait(name, started, gather, after, rels=ALL_PEERS):
    send, recv, x_thru, land_thru, _ = started

    def body(x_ref, land_ref, send, recv, after_ref, x_dead, got_ref):
        for k, rel in enumerate(rels):
            p = _peer(rel)
            cp = pltpu.make_async_remote_copy(src_ref=x_ref if gather else x_ref.at[_slot(p, rels)], dst_ref=land_ref.at[_slot(p, rels)],
                                              send_sem=send.at[k], recv_sem=recv.at[k], device_id=p, device_id_type=MESH)
            cp.wait_send()
            cp.wait_recv()

    return pl.pallas_call(
        body, name=name, out_shape=(pltpu.HBM(x_thru.shape, x_thru.dtype), pltpu.HBM(land_thru.shape, land_thru.dtype)),
        in_specs=(HBM_SPEC, HBM_SPEC, SEM_SPEC, SEM_SPEC, ANY_SPEC), out_specs=(HBM_SPEC, HBM_SPEC), input_output_aliases={0: 0, 1: 1},
        compiler_params=pltpu.CompilerParams(has_side_effects=EFFECT),
    )(x_thru, land_thru, send, recv, after)[1]


def _sib_swap(name, x):
    def body(x_ref, out_ref, send, recv):
        mx, my, mc = _mesh_pos()
        sib = (mx, my, 1 - mc)
        cps = [pltpu.make_async_remote_copy(src_ref=x_ref.at[2 * q + 1 - mc], dst_ref=out_ref.at[q], send_sem=send.at[q], recv_sem=recv.at[q],
                                            device_id=sib, device_id_type=MESH) for q in range(4)]
        for cp in cps:
            cp.start()
        for cp in cps:
            cp.wait_recv()
        for cp in cps:
            cp.wait_send()

    return pl.pallas_call(
        body, name=name, in_specs=[ANY_SPEC], out_specs=ANY_SPEC, out_shape=jax.ShapeDtypeStruct((4,) + x.shape[1:], x.dtype),
        scratch_shapes=[pltpu.SemaphoreType.DMA((4,)), pltpu.SemaphoreType.DMA((4,))],
    )(x)


def _behind(token, a):
    return a + token[0:1, 0:1].astype(a.dtype)


def _sum8(name, g):
    def body(g_ref, o_ref):
        acc = g_ref[0:1, :]
        for p in range(1, NDEV):
            acc = acc + g_ref[p:p + 1, :]
        o_ref[...] = acc

    return pl.pallas_call(body, name=name, out_shape=jax.ShapeDtypeStruct((1, g.shape[1]), f32),
                          compiler_params=pltpu.CompilerParams(vmem_limit_bytes=VMEM_LIMIT))(g)


def _pack(arrs):
    flat = jnp.concatenate([a.reshape(-1).astype(f32) for a in arrs])
    pad = (-flat.shape[0]) % (8 * LANES)
    return jnp.pad(flat, (0, pad)).reshape(1, -1)


def _unpack(vec, shapes):
    out, off = [], 0
    flat = vec.reshape(-1)
    for s in shapes:
        n = 1
        for d in s:
            n *= d
        out.append(flat[off:off + n].reshape(s))
        off += n
    return out


def _adam(name, w, m, v, g, parts, after=None):
    extra = [] if after is None else [after]
    nrows, width = w.shape
    tr = _row_tile(nrows, width * (3 if parts else 1), cap=512) if nrows >= 16 else nrows
    blk = pl.BlockSpec((tr, width), lambda i: (i, 0))
    gblk = pl.BlockSpec((parts, tr, width), lambda i: (0, i, 0)) if parts else blk
    c1 = 1.0 / (1.0 - ADAM_B1 ** ADAM_STEP)
    c2 = 1.0 / (1.0 - ADAM_B2 ** ADAM_STEP)

    def body(w_ref, m_ref, v_ref, g_ref, *rest):
        go_ref, d_ref, mo_ref, vo_ref = rest[len(extra):]
        if parts:
            gg = g_ref[0].astype(f32)
            for p in range(1, parts):
                gg = gg + g_ref[p].astype(f32)
        else:
            gg = g_ref[...]
        mn = ADAM_B1 * m_ref[...] + (1.0 - ADAM_B1) * gg
        vn = ADAM_B2 * v_ref[...] + (1.0 - ADAM_B2) * (gg * gg)
        go_ref[...] = gg
        mo_ref[...] = mn
        vo_ref[...] = vn
        d_ref[...] = -ADAM_LR * ((mn * c1) / (jnp.sqrt(vn * c2) + ADAM_EPS) + ADAM_WD * w_ref[...])

    sh = jax.ShapeDtypeStruct((nrows, width), f32)
    return pl.pallas_call(body, name=name, grid=(nrows // tr,), in_specs=[blk, blk, blk, gblk] + [pl.BlockSpec(memory_space=pl.ANY)] * len(extra),
                          out_specs=[blk] * 4, out_shape=[sh] * 4, compiler_params=_cparams(("parallel",)))(w, m, v, g, *extra)


def kernel(x, c, ctx, c_ctx, ln_in_g, ln_in_b, w_mod, b_mod, w_in, w_qkv_conv, a_log_f, dt_bias_f, a_log_b, dt_bias_b, dn_norm_g, conf_dw_w, conf_dw_b, conf_ln_g, conf_ln_b, w_out, ln1_g, ln1_b, w_mlp1, b_mlp1, w_mlp2, b_mlp2, ln2_g, ln2_b, loss_target, m_c_ctx, m_ln_in_g, m_ln_in_b, m_w_mod, m_b_mod, m_w_in, m_w_qkv_conv, m_a_log_f, m_dt_bias_f, m_a_log_b, m_dt_bias_b, m_dn_norm_g, m_conf_dw_w, m_conf_dw_b, m_conf_ln_g, m_conf_ln_b, m_w_out, m_ln1_g, m_ln1_b, m_w_mlp1, m_b_mlp1, m_w_mlp2, m_b_mlp2, m_ln2_g, m_ln2_b, v_c_ctx, v_ln_in_g, v_ln_in_b, v_w_mod, v_b_mod, v_w_in, v_w_qkv_conv, v_a_log_f, v_dt_bias_f, v_a_log_b, v_dt_bias_b, v_dn_norm_g, v_conf_dw_w, v_conf_dw_b, v_conf_ln_g, v_conf_ln_b, v_w_out, v_ln1_g, v_ln1_b, v_w_mlp1, v_b_mlp1, v_w_mlp2, v_b_mlp2, v_ln2_g, v_ln2_b):
    weights = dict(c_ctx=c_ctx, ln_in_g=ln_in_g, ln_in_b=ln_in_b, w_mod=w_mod, b_mod=b_mod, w_in=w_in, w_qkv_conv=w_qkv_conv, a_log_f=a_log_f, dt_bias_f=dt_bias_f, a_log_b=a_log_b, dt_bias_b=dt_bias_b, dn_norm_g=dn_norm_g, conf_dw_w=conf_dw_w, conf_dw_b=conf_dw_b, conf_ln_g=conf_ln_g, conf_ln_b=conf_ln_b, w_out=w_out, ln1_g=ln1_g, ln1_b=ln1_b, w_mlp1=w_mlp1, b_mlp1=b_mlp1, w_mlp2=w_mlp2, b_mlp2=b_mlp2, ln2_g=ln2_g, ln2_b=ln2_b)
    mom1 = dict(c_ctx=m_c_ctx, ln_in_g=m_ln_in_g, ln_in_b=m_ln_in_b, w_mod=m_w_mod, b_mod=m_b_mod, w_in=m_w_in, w_qkv_conv=m_w_qkv_conv, a_log_f=m_a_log_f, dt_bias_f=m_dt_bias_f, a_log_b=m_a_log_b, dt_bias_b=m_dt_bias_b, dn_norm_g=m_dn_norm_g, conf_dw_w=m_conf_dw_w, conf_dw_b=m_conf_dw_b, conf_ln_g=m_conf_ln_g, conf_ln_b=m_conf_ln_b, w_out=m_w_out, ln1_g=m_ln1_g, ln1_b=m_ln1_b, w_mlp1=m_w_mlp1, b_mlp1=m_b_mlp1, w_mlp2=m_w_mlp2, b_mlp2=m_b_mlp2, ln2_g=m_ln2_g, ln2_b=m_ln2_b)
    mom2 = dict(c_ctx=v_c_ctx, ln_in_g=v_ln_in_g, ln_in_b=v_ln_in_b, w_mod=v_w_mod, b_mod=v_b_mod, w_in=v_w_in, w_qkv_conv=v_w_qkv_conv, a_log_f=v_a_log_f, dt_bias_f=v_dt_bias_f, a_log_b=v_a_log_b, dt_bias_b=v_dt_bias_b, dn_norm_g=v_dn_norm_g, conf_dw_w=v_conf_dw_w, conf_dw_b=v_conf_dw_b, conf_ln_g=v_conf_ln_g, conf_ln_b=v_conf_ln_b, w_out=v_w_out, ln1_g=v_ln1_g, ln1_b=v_ln1_b, w_mlp1=v_w_mlp1, b_mlp1=v_b_mlp1, w_mlp2=v_w_mlp2, b_mlp2=v_b_mlp2, ln2_g=v_ln2_g, ln2_b=v_ln2_b)
    names = list(weights)

    me = _lin(_mesh_pos())
    S, D = x.shape[1], x.shape[2]
    T = ctx.shape[1]
    R = S + T
    DN = D // 2
    NH = DN // HD
    CONF = D - DN
    K7, K31 = w_qkv_conv.shape[1], conf_dw_w.shape[1]
    DFF = w_mlp1.shape[2] * NDEV
    INC = w_in.shape[2] * NDEV
    CONF_OFF = 4 * DN + 4 * NH
    NC = 4 * DN + 2 * CONF + GPAD
    QB, ZB, VB, GB = 0, 3 * DN // LANES, 4 * DN // LANES, (4 * DN + 2 * CONF) // LANES
    MODC = w_mod.shape[2]
    x2d, ctx2d, tgt = x[0], ctx[0], loss_target[0]
    row = lambda a: a.reshape(1, -1).astype(f32)

    c_all, w7_all, w31_all = _all_gather("ag_small", [c.astype(f32), w_qkv_conv[0], conf_dw_w[0]])
    w7 = jnp.pad(jnp.transpose(w7_all, (1, 0, 2)).reshape(K7, 3 * DN), ((0, (-K7) % 8), (0, 0)))
    w31 = jnp.pad(jnp.transpose(w31_all, (1, 0, 2)).reshape(K31, CONF), ((0, (-K31) % 8), (0, 0)))
    c16 = jnp.concatenate([c_all.reshape(NDEV, D), c_ctx.reshape(1, D).astype(f32), jnp.zeros((7, D), f32)], axis=0)
    (sc16,) = _rowcall("silu_c", lambda j, a: (_silu(a),), 16, 16, [(c16, D, 0, 0, False)], [], [(D, f32, False)], [])
    bmod_mine = lax.dynamic_slice_in_dim(b_mod.astype(f32), me * MODC, MODC, axis=1)
    mod_part = _mm("mod_fwd", sc16, w_mod[0], "nn", f32, extras=[("row", bmod_mine)], epilogue=lambda r, b: (r + b,))
    (mod_g,) = _all_gather("ag_mod", [mod_part])
    mod_all = jnp.transpose(mod_g, (1, 0, 2)).reshape(16, 6 * D)
    mod_me = lax.dynamic_slice_in_dim(mod_all, me, 1, axis=0)
    sh_a, sc_a, g_a, sh_m, sc_m, g_m = [mod_me[:, i * D:(i + 1) * D] for i in range(6)]
    csh_a, csc_a = mod_all[8:9, 0:D], mod_all[8:9, D:2 * D]

    (g_win,) = _all_gather("ag_w_in", [w_in[0].astype(bf16)], after=mod_g)
    ag_wout = _xchg_start("ag_wout_start", w_out[0].astype(bf16), True, g_win)
    ag_w1 = _xchg_start("ag_w1_start", w_mlp1[0].astype(bf16), True, ag_wout[4])
    win_full = jnp.transpose(g_win, (1, 0, 2)).reshape(D, INC)
    w_cat = jnp.concatenate([win_full[:, :4 * DN], win_full[:, CONF_OFF:], win_full[:, 4 * DN:CONF_OFF],
                             jnp.zeros((D, GPAD - 4 * NH), bf16)], axis=1)
    g0, b0 = row(ln_in_g), row(ln_in_b)
    g0_fwd = _behind(ag_w1[4], g0)

    def ln_mod(j, xt, g, b, sh, sc):
        xh, _ = _ln_stats(xt)
        xn = xh * g + b
        return xn, xn * (1.0 + sc) + sh

    tr = _row_tile(T, D)
    vec = lambda a: (a, a.shape[1], 0, False)
    xn, xm = _rowcall("ln_in_lat", ln_mod, S, tr, [(x2d, D, 0, 0, False)], [vec(g0_fwd), vec(b0), vec(sh_a), vec(sc_a)], [(D, f32, False), (D, bf16, False)], [])
    xcn, xcm = _rowcall("ln_in_ctx", ln_mod, T, tr, [(ctx2d, D, 0, 0, False)], [vec(g0), vec(b0), vec(csh_a), vec(csc_a)], [(D, f32, False), (D, bf16, False)], [])
    xm_all = jnp.concatenate([xm, xcm], axis=0)

    h_all = _mm("in_proj", xm_all, w_cat, "nn", f32, tm=1088, tn=640)

    qkv = _qkv_conv_fwd(h_all, w7, R, S, NH, K7).reshape(3, NH, R, HD)
    lane = jnp.arange(LANES)
    is_a = ((lane < 4 * NH) & ((lane // NH) % 2 == 0)).astype(f32).reshape(1, LANES)
    pad_g = lambda a, b: jnp.concatenate([a.reshape(-1), jnp.zeros((NH,), f32), b.reshape(-1), jnp.zeros((LANES - 3 * NH,), f32)]).reshape(1, LANES)
    neg_a = pad_g(-jnp.exp(a_log_f.astype(f32)), -jnp.exp(a_log_b.astype(f32)))
    dt_v = pad_g(dt_bias_f.astype(f32), dt_bias_b.astype(f32))

    def gates_f(j, hg, isa, na, dt):
        return (jnp.where(isa > 0.5, na * _softplus(hg + dt), _sigmoid(hg)),)

    trg = _row_tile(T, LANES)
    ag_w2 = _xchg_start("ag_w2_start", w_mlp2[0].astype(bf16), True, qkv)
    (gates,) = _rowcall("gates_fwd", gates_f, R, trg, [(h_all, LANES, 0, GB, False)], [vec(_behind(ag_w2[4], is_a)), vec(neg_a), vec(dt_v)], [(LANES, f32, False)], [])
    gpad = jnp.zeros((R, LANES - 2 * NH), f32)
    gd = jnp.stack([jnp.concatenate([gates[:, :2 * NH], gpad], axis=1), jnp.concatenate([gates[:, 2 * NH:4 * NH], gpad], axis=1)])

    o5, states, *solved = _gdn_fwd(qkv, gd, NH, S, T)
    dng = row(dn_norm_g)
    trh = _row_tile(S, HD, cap=1024)
    dn_out = _gate_norm_fwd(o5, h_all, dng, NH, S, ZB, trh)

    yconv = _conf_conv_fwd(h_all, w31, S, CONF, VB, K31)
    bdw, clg, clb = row(conf_dw_b), row(conf_ln_g), row(conf_ln_b)

    def conf_ln(j, yc, b, g, bb):
        xh, _ = _ln_stats(yc + b)
        return (_silu(xh * g + bb),)

    trc = _row_tile(S, CONF)
    (conf_out,) = _rowcall("conf_ln_fwd", conf_ln, S, trc, [(yconv, CONF, 0, 0, False)], [vec(bdw), vec(clg), vec(clb)], [(CONF, bf16, False)], [])
    mix = jnp.concatenate([dn_out, conf_out], axis=1)
    wout_full = _xchg_wait("ag_wout_wait", ag_wout, True, mix).reshape(DN + CONF, D)
    y = _mm("out_proj", mix, wout_full, "nn", f32)

    l1g, l1b, l2g, l2b = row(ln1_g), row(ln1_b), row(ln2_g), row(ln2_b)

    def ln1_mod(j, xnt, yt, ga, g, b, sh, sc):
        xh, _ = _ln_stats(ALPHA * xnt + ga * yt)
        x1 = xh * g + b
        return x1, x1 * (1.0 + sc) + sh

    trd = _row_tile(S, D)
    x1, u = _rowcall("ln1_fwd", ln1_mod, S, trd, [(xn, D, 0, 0, False), (y, D, 0, 0, False)],
                     [vec(g_a), vec(l1g), vec(l1b), vec(sh_m), vec(sc_m)], [(D, f32, False), (D, bf16, False)], [])
    b1, b2 = row(b_mlp1), row(b_mlp2)
    g_w1 = _xchg_wait("ag_w1_wait", ag_w1, True, u)
    hh, act = _mm("mlp1", u, g_w1, "nn", (f32, bf16), extras=[("row", b1)], epilogue=lambda r, b: (r, jnp.square(jnp.maximum(r + b, 0.0))))
    w2_full = _xchg_wait("ag_w2_wait", ag_w2, True, act).reshape(DFF, D)
    y2 = _mm("mlp2", act, w2_full, "nn", f32)

    def ln2_loss(j, x1t, y2t, tg, bb2, gm, g, b):
        y2b = y2t + bb2
        xh, rstd = _ln_stats(ALPHA * x1t + gm * y2b)
        err = xh * g + b - tg
        dx2 = err * (1.0 / D)
        dr2 = _ln_bwd(dx2, xh, rstd, g)
        dy2 = dr2 * gm
        lsum = jnp.broadcast_to(jnp.sum(err * err).reshape(1, 1), (1, LANES))
        return dr2, dy2, _csum(dx2 * xh), _csum(dx2), _csum(dr2 * y2b), _csum(dy2), lsum

    dr2, dy2, d_l2g, d_l2b, d_gm, d_b2, lsum = _rowcall(
        "ln2_loss", ln2_loss, S, trd, [(x1, D, 0, 0, False), (y2, D, 0, 0, False), (tgt, D, 0, 0, False)],
        [vec(b2), vec(g_m), vec(l2g), vec(l2b)], [(D, f32, False), (D, bf16, False)], [(1, D, False)] * 4 + [(1, LANES, False)])
    loss = lax.psum(0.5 * lsum[0, 0] / D, ("x", "y", "c"))

    dw2_p = _mm("mlp2_dw", act, dy2, "tn", bf16)
    a2a_w2 = _xchg_start("a2a_w2_start", dw2_p.reshape(NDEV, DFF // NDEV, D), False, dw2_p)
    dhh = _mm("mlp2_dx", dy2, w2_full, "nt", bf16, extras=[("tile", hh), ("row", _behind(a2a_w2[4], b1))],
              epilogue=lambda r, h, b: (r * (2.0 * jnp.maximum(h + b, 0.0)),))
    (d_b1,) = _rowcall("b1_grad", lambda j, a: (_csum(a.astype(f32)),), S, _row_tile(S, DFF // 8), [(dhh, DFF, 0, 0, False)], [], [], [(1, DFF, False)])
    dw1_p = _mm("mlp1_dw", u, dhh, "tn", bf16, out_split=NDEV)
    a2a_w1 = _xchg_start("a2a_w1_start", dw1_p, False, dw1_p)
    du = _mm("mlp1_dx", dhh, g_w1, "nt", f32)

    def ln1_bwd(j, dr2t, dut, xnt, yt, ga, g, b, sc):
        xh, rstd = _ln_stats(ALPHA * xnt + ga * yt)
        x1t = xh * g + b
        dx1 = ALPHA * dr2t + dut * (1.0 + sc)
        dr1 = _ln_bwd(dx1, xh, rstd, g)
        return dr1, dr1 * ga, _csum(dut * x1t), _csum(dut), _csum(dx1 * xh), _csum(dx1), _csum(dr1 * yt)

    dr1, dy, d_scm, d_shm, d_l1g, d_l1b, d_ga = _rowcall(
        "ln1_bwd", ln1_bwd, S, trd, [(dr2, D, 0, 0, False), (du, D, 0, 0, False), (xn, D, 0, 0, False), (y, D, 0, 0, False)],
        [vec(g_a), vec(l1g), vec(l1b), vec(_behind(a2a_w1[4], sc_m))], [(D, f32, False), (D, bf16, False)], [(1, D, False)] * 5)
    dwout_p = _mm("out_proj_dw", mix, dy, "tn", bf16)
    a2a_wout = _xchg_start("a2a_wout_start", dwout_p.reshape(NDEV, (DN + CONF) // NDEV, D), False, dwout_p)
    dmix = _mm("out_proj_dx", dy, wout_full, "nt", f32)

    def conf_ln_b(j, dm, yc, b, g, bb):
        xh, rstd = _ln_stats(yc + b)
        dln = dm * _dsilu(xh * g + bb)
        dyc = _ln_bwd(dln, xh, rstd, g)
        return dyc, _csum(dln * xh), _csum(dln), _csum(dyc)

    dyc, d_clg, d_clb, d_bdw = _rowcall("conf_ln_bwd", conf_ln_b, S, trc, [(dmix, CONF, 0, DN // CONF, False), (yconv, CONF, 0, 0, False)],
                                        [vec(_behind(a2a_wout[4], bdw)), vec(clg), vec(clb)], [(CONF, f32, False)], [(1, CONF, False)] * 3)
    dval, dgate, dw31 = _conf_conv_bwd(h_all, w31, dyc, S, CONF, VB, K31)

    do, dz, d_dng = _gate_norm_bwd(dmix, o5, h_all, dng, NH, S, ZB, trh)
    dqkv, dgd = _gdn_bwd(qkv, gd, states, solved, do, NH, S, T)
    dh_qkv, dw7 = _qkv_conv_bwd(h_all, w7, dqkv.reshape(2, 3 * NH, R, HD), R, S, NH, K7)
    dgate_cols = jnp.concatenate([dgd[0][:, :2 * NH], dgd[1][:, :2 * NH], jnp.zeros((R, LANES - 4 * NH), f32)], axis=1)

    def gates_b(j, hg, dgt, isa, na, dt):
        sg = _sigmoid(hg)
        sp = _softplus(hg + dt)
        dpre = jnp.where(isa > 0.5, dgt * na * _sigmoid(hg + dt), dgt * sg * (1.0 - sg))
        return dpre, _csum(jnp.where(isa > 0.5, dgt * na * sp, 0.0)), _csum(jnp.where(isa > 0.5, dpre, 0.0))

    dh_g, d_alog, d_dt = _rowcall("gates_bwd", gates_b, R, trg, [(h_all, LANES, 0, GB, False), (dgate_cols, LANES, 0, 0, False)],
                                  [vec(is_a), vec(neg_a), vec(dt_v)], [(LANES, bf16, False)], [(1, LANES, False)] * 2)

    zpad = lambda a: jnp.concatenate([a, jnp.zeros((T, a.shape[1]), bf16)], axis=0)
    dh_all = jnp.concatenate([dh_qkv, zpad(dz), zpad(dval), zpad(dgate), dh_g, jnp.zeros((R, GPAD - LANES), bf16)], axis=1)
    dxm_all = _mm("in_proj_dx", dh_all, w_cat, "nt", f32, tm=1088, tk=2560)

    def ln_in_bwd_lat(j, xt, dr1t, dxm, g, b, sc):
        xh, rstd = _ln_stats(xt)
        xnt = xh * g + b
        dxn = ALPHA * dr1t + dxm * (1.0 + sc)
        return _ln_bwd(dxn, xh, rstd, g), _csum(dxm * xnt), _csum(dxm), _csum(dxn * xh), _csum(dxn)

    def ln_in_bwd_ctx(j, xt, dxm, g, b, sc):
        xh, rstd = _ln_stats(xt)
        xnt = xh * g + b
        dxn = dxm * (1.0 + sc)
        return _csum(dxm * xnt), _csum(dxm), _csum(dxn * xh), _csum(dxn)

    grad_x, d_sca, d_sha, d_g0a, d_b0a = _rowcall(
        "ln_in_bwd_lat", ln_in_bwd_lat, S, trd, [(x2d, D, 0, 0, False), (dr1, D, 0, 0, False), (dxm_all, D, 0, 0, False)],
        [vec(g0), vec(b0), vec(sc_a)], [(D, f32, False)], [(1, D, False)] * 4)
    d_csca, d_csha, d_g0b, d_b0b = _rowcall(
        "ln_in_bwd_ctx", ln_in_bwd_ctx, T, tr, [(ctx2d, D, 0, 0, False), (dxm_all, D, S // tr, 0, False)],
        [vec(g0), vec(b0), vec(csc_a)], [], [(1, D, False)] * 4)

    zD = jnp.zeros((1, D), f32)
    dmod_me = jnp.concatenate([d_sha, d_sca, d_ga, d_shm, d_scm, d_gm], axis=1)
    dmodc_me = jnp.concatenate([d_csha, d_csca, zD, zD, zD, zD], axis=1)
    small_names = ["ln_in_g", "ln_in_b", "a_log_f", "dt_bias_f", "a_log_b", "dt_bias_b", "dn_norm_g", "conf_dw_b", "conf_ln_g", "conf_ln_b",
                   "ln1_g", "ln1_b", "b_mlp1", "b_mlp2", "ln2_g", "ln2_b"]
    small_parts = [d_g0a + d_g0b, d_b0a + d_b0b, d_alog[:, 0:NH], d_dt[:, 0:NH], d_alog[:, 2 * NH:3 * NH], d_dt[:, 2 * NH:3 * NH], d_dng, d_bdw, d_clg, d_clb,
                   d_l1g, d_l1b, d_b1, d_b2, d_l2g, d_l2b]
    conv_parts = [dw7[:K7], dw31[:K31]]
    packed = _pack(small_parts + conv_parts + [dmodc_me])
    (pk_all, dmod_all) = _all_gather("ag_small_grads", [packed, dmod_me])
    summed = _sum8("sum_small_grads", pk_all.reshape(NDEV, -1))
    parts_sh = [a.shape for a in small_parts + conv_parts + [dmodc_me]]
    un = _unpack(summed, parts_sh)
    gsmall = dict(zip(small_names, un[:len(small_names)]))
    g_w7_full, g_w31_full, dmodc = un[len(small_names):]
    gsmall["w_qkv_conv"] = lax.dynamic_slice_in_dim(g_w7_full, me * w_qkv_conv.shape[2], w_qkv_conv.shape[2], axis=1)
    gsmall["conf_dw_w"] = lax.dynamic_slice_in_dim(g_w31_full, me * conf_dw_w.shape[2], conf_dw_w.shape[2], axis=1)

    dm16 = jnp.concatenate([dmod_all.reshape(NDEV, 6 * D), dmodc, jnp.zeros((7, 6 * D), f32)], axis=0)
    (gbmod,) = _rowcall("bmod_grad", lambda j, a: (_csum(a),), 16, 16, [(dm16, 6 * D, 0, 0, False)], [], [], [(1, 6 * D, False)])
    gsmall["b_mod"] = gbmod
    dm16_mine = lax.dynamic_slice_in_dim(dm16, me * MODC, MODC, axis=1)
    g_wmod = _mm("mod_dw", sc16, dm16_mine, "tn", f32)
    dsc16_part = _mm("mod_dx", dm16_mine, w_mod[0], "nt", f32)
    (dsc_all,) = _all_gather("ag_cctx", [dsc16_part[8:9]])
    dsilu_cctx = _sum8("sum_cctx", dsc_all.reshape(NDEV, D))
    (g_cctx,) = _rowcall("cctx_grad", lambda j, ds, cc: (ds * _dsilu(cc),), 1, 1, [(dsilu_cctx, D, 0, 0, False), (c_ctx.reshape(1, D).astype(f32), D, 0, 0, False)],
                         [], [(D, f32, False)], [])
    gsmall["c_ctx"] = g_cctx

    dwcat_p = _mm("in_proj_dw", xm_all, dh_all, "tn", bf16, tn=640, tk=4352, after=g_cctx)
    dwin = jnp.concatenate([dwcat_p[:, :4 * DN], dwcat_p[:, 4 * DN + 2 * CONF:4 * DN + 2 * CONF + 4 * NH], dwcat_p[:, 4 * DN:4 * DN + 2 * CONF]], axis=1)
    dwin_p = jnp.transpose(dwin.reshape(D, NDEV, INC // NDEV), (1, 0, 2))
    from_sib = _sib_swap("sib_win", dwin_p)
    mine4 = lax.dynamic_index_in_dim(dwin_p.reshape(4, 2, D, INC // NDEV), lax.axis_index("c"), 1, keepdims=False)
    (chip_sum,) = _rowcall("add_win", lambda j, a, b: (a.astype(f32) + b.astype(f32),), 4 * D, _row_tile(4 * D, INC // NDEV),
                           [(mine4.reshape(4 * D, INC // NDEV), INC // NDEV, 0, 0, False), (from_sib.reshape(4 * D, INC // NDEV), INC // NDEV, 0, 0, False)],
                           [], [(INC // NDEV, bf16, False)], [])
    a2a_win = _xchg_start("a2a_win_start", chip_sum.reshape(4, D, INC // NDEV), False, chip_sum, SAME_CORE)

    grads, deltas, new_m, new_v = {}, {}, {}, {}
    res = _adam("adam_w_mod", w_mod[0], m_w_mod[0], v_w_mod[0], g_wmod, 0, after=a2a_win[4])
    grads["w_mod"], deltas["w_mod"], new_m["w_mod"], new_v["w_mod"] = [a.reshape(w_mod.shape) for a in res]
    after = res[1]
    for nm, started in (("w_mlp2", a2a_w2), ("w_mlp1", a2a_w1), ("w_out", a2a_wout), ("w_in", a2a_win)):
        w3 = weights[nm]
        rels = SAME_CORE if nm == "w_in" else ALL_PEERS
        g = _xchg_wait("a2a_" + nm + "_wait", started, False, after, rels)
        res = _adam("adam_" + nm, w3[0], mom1[nm][0], mom2[nm][0], g, len(rels) + 1)
        grads[nm], deltas[nm], new_m[nm], new_v[nm] = [a.reshape(w3.shape) for a in res]
        after = res[1]
    snames = [n for n in names if n not in grads]
    res = _adam("adam_small", _pack([weights[n] for n in snames]), _pack([mom1[n] for n in snames]), _pack([mom2[n] for n in snames]),
                _pack([gsmall[n] for n in snames]), 0)
    shapes = [weights[n].shape for n in snames]
    for dst, packed_out in zip((grads, deltas, new_m, new_v), res):
        for n, a in zip(snames, _unpack(packed_out, shapes)):
            dst[n] = a

    return (loss, grad_x.reshape(x.shape), *[grads[n] for n in names], *[deltas[n] for n in names],
            *[new_m[n] for n in names], *[new_v[n] for n in names])
```

```python
import functools

import jax
import jax.numpy as jnp
from jax import lax
from jax.experimental import pallas as pl
from jax.experimental.pallas import tpu as pltpu

f32 = jnp.float32
bf16 = jnp.bfloat16
MESH = pl.DeviceIdType.MESH

NDEV = 8
HD = 128
CH = 64
GW = 64
LANES = 128
GPAD = 512
LN_EPS = 1e-5
ALPHA = 2.0 ** 0.25
ADAM_LR, ADAM_B1, ADAM_B2, ADAM_EPS, ADAM_WD, ADAM_STEP = 0.001, 0.9, 0.999, 1e-08, 0.01, 10
VMEM_LIMIT = 56 * 1024 * 1024
ROW_BLOCK_BYTES = 1 << 21


def _cparams(sem):
    return pltpu.CompilerParams(dimension_semantics=sem, vmem_limit_bytes=VMEM_LIMIT)


def _tile(dim, pref, align):
    t = min(pref, dim)
    t -= t % align
    while t >= align:
        if dim % t == 0:
            return t
        t -= align
    return dim


def _row_tile(nrows, width, cap=256):
    t = max(16, min(cap, ROW_BLOCK_BYTES // (4 * width)))
    t = 1 << (t.bit_length() - 1)
    while nrows % t:
        t //= 2
    return max(t, 1)


def _sigmoid(x):
    return 1.0 / (1.0 + jnp.exp(-x))


def _silu(x):
    return x * _sigmoid(x)


def _dsilu(x):
    s = _sigmoid(x)
    return s * (1.0 + x * (1.0 - s))


def _softplus(x):
    return jnp.maximum(x, 0.0) + jnp.log(1.0 + jnp.exp(-jnp.abs(x)))


def _ln_stats(r):
    mu = jnp.mean(r, axis=-1, keepdims=True)
    xc = r - mu
    rstd = lax.rsqrt(jnp.mean(xc * xc, axis=-1, keepdims=True) + LN_EPS)
    return xc * rstd, rstd


def _ln_bwd(dy, xhat, rstd, g):
    dxh = dy * g
    return rstd * (dxh - jnp.mean(dxh, axis=-1, keepdims=True) - xhat * jnp.mean(dxh * xhat, axis=-1, keepdims=True))


def _csum(a):
    return jnp.sum(a, axis=0, keepdims=True)


def _mm(name, a, b, mode, out_dtype, out_split=1, tm=1024, tn=1024, tk=4096, extras=(), epilogue=None, after=None):
    if mode == "tn":
        K, M = a.shape
    else:
        M, K = a.shape
    bp = b.shape[0] if b.ndim == 3 else 1
    brows, bcols = b.shape[-2], b.shape[-1] * bp
    N = brows if mode == "nt" else bcols
    assert K == (bcols if mode == "nt" else brows), (name, a.shape, b.shape)
    tm = _tile(M, tm, LANES if mode == "tn" else 16)
    nsplit = max(out_split, bp if mode != "nt" else 1)
    tn = _tile(N // nsplit, tn, LANES)
    tk = _tile(K // (bp if mode == "nt" else 1), tk, LANES)
    nk = K // tk
    dims = {"nn": (((1,), (0,)), ((), ())), "nt": (((1,), (1,)), ((), ())), "tn": (((0,), (0,)), ((), ()))}[mode]
    out_dtypes = out_dtype if isinstance(out_dtype, tuple) else (out_dtype,)

    a_spec = pl.BlockSpec((tk, tm), lambda i, j, k: (k, i)) if mode == "tn" else pl.BlockSpec((tm, tk), lambda i, j, k: (i, k))
    if b.ndim == 3:
        if mode == "nt":
            per = (K // bp) // tk
            b_spec = pl.BlockSpec((None, tn, tk), lambda i, j, k: (k // per, j, k % per))
        else:
            per = (N // bp) // tn
            b_spec = pl.BlockSpec((None, tk, tn), lambda i, j, k: (j // per, k, j % per))
    elif mode == "nt":
        b_spec = pl.BlockSpec((tn, tk), lambda i, j, k: (j, k))
    else:
        b_spec = pl.BlockSpec((tk, tn), lambda i, j, k: (k, j))
    if out_split > 1:
        pero = (N // out_split) // tn
        o_spec = pl.BlockSpec((None, tm, tn), lambda i, j, k: (j // pero, i, j % pero))
        o_shapes = [jax.ShapeDtypeStruct((out_split, M, N // out_split), dt) for dt in out_dtypes]
    else:
        o_spec = pl.BlockSpec((tm, tn), lambda i, j, k: (i, j))
        o_shapes = [jax.ShapeDtypeStruct((M, N), dt) for dt in out_dtypes]
    in_specs = [a_spec, b_spec]
    args = [a, b]
    for kind, arr in extras:
        in_specs.append(pl.BlockSpec((1, tn), lambda i, j, k: (0, j)) if kind == "row" else pl.BlockSpec((tm, tn), lambda i, j, k: (i, j)))
        args.append(arr)
    n_in, n_out = len(args), len(out_dtypes)
    if after is not None:
        in_specs.append(pl.BlockSpec(memory_space=pl.ANY))
        args.append(after)
    n_all = len(args)

    def finish(refs, r):
        outs = (r,) if epilogue is None else epilogue(r, *[e[...] for e in refs[2:n_in]])
        for o_ref, val in zip(refs[n_all:n_all + n_out], outs):
            o_ref[...] = val.astype(o_ref.dtype)

    def body(*refs):
        part = lax.dot_general(refs[0][...].astype(bf16), refs[1][...].astype(bf16), dims, preferred_element_type=f32)
        if nk == 1:
            finish(refs, part)
            return
        acc = refs[-1]
        k = pl.program_id(2)

        @pl.when(k == 0)
        def _():
            acc[...] = part

        @pl.when(jnp.logical_and(k > 0, k < nk - 1))
        def _():
            acc[...] += part

        @pl.when(k == nk - 1)
        def _():
            finish(refs, acc[...] + part)

    res = pl.pallas_call(
        body, name=name, grid=(M // tm, N // tn, nk), in_specs=in_specs, out_specs=[o_spec] * n_out, out_shape=o_shapes,
        scratch_shapes=[pltpu.VMEM((tm, tn), f32)] if nk > 1 else [], compiler_params=_cparams(("parallel", "parallel", "arbitrary")),
    )(*args)
    return res[0] if n_out == 1 else res


def _rowcall(name, fn, nrows, tr, rows_in, vecs_in, rows_out, accs_out, ncol=1):
    nrt = nrows // tr
    in_specs, args = [], []
    for arr, w, ro, co, pc in rows_in:
        in_specs.append(pl.BlockSpec((tr, w), functools.partial(lambda j, i, ro, co, pc: (i + ro, co + (j if pc else 0)), ro=ro, co=co, pc=pc)))
        args.append(arr)
    for arr, w, co, pc in vecs_in:
        in_specs.append(pl.BlockSpec((arr.shape[0], w), functools.partial(lambda j, i, co, pc: (0, co + (j if pc else 0)), co=co, pc=pc)))
        args.append(arr)
    out_specs, out_shape = [], []
    for w, dt, pc in rows_out:
        out_specs.append(pl.BlockSpec((tr, w), functools.partial(lambda j, i, pc: (i, j if pc else 0), pc=pc)))
        out_shape.append(jax.ShapeDtypeStruct((nrows, w * (ncol if pc else 1)), dt))
    for k, w, pc in accs_out:
        out_specs.append(pl.BlockSpec((k, w), functools.partial(lambda j, i, pc: (0, j if pc else 0), pc=pc)))
        out_shape.append(jax.ShapeDtypeStruct((k, w * (ncol if pc else 1)), f32))
    n_in, n_ro = len(args), len(rows_out)

    def body(*refs):
        j, i = pl.program_id(0), pl.program_id(1)
        outs = fn(j, *[r[...] for r in refs[:n_in]])
        for r, val in zip(refs[n_in:n_in + n_ro], outs[:n_ro]):
            r[...] = val.astype(r.dtype)
        for (k, w, pc), r, val in zip(accs_out, refs[n_in + n_ro:], outs[n_ro:]):
            first = (i == 0) if pc else jnp.logical_and(i == 0, j == 0)

            @pl.when(first)
            def _(r=r):
                r[...] = jnp.zeros_like(r)

            r[...] += val

    res = pl.pallas_call(
        body, name=name, grid=(ncol, nrt), in_specs=in_specs, out_specs=out_specs, out_shape=out_shape,
        compiler_params=_cparams(("arbitrary", "arbitrary")),
    )(*args)
    return res


def _tap_valid(mode, t, d, nrows, nlat):
    if mode == "seg":
        tp = t + d
        return (tp >= 0) & (tp < nrows) & ((t < nlat) == (tp < nlat))
    if mode == "row":
        p = (t & (GW - 1)) + d
        return (p >= 0) & (p < GW)
    tp = t + d * GW
    return (tp >= 0) & (tp < nrows)


def _conv(x, w_ref, ktaps, mode, nlat, flip=False):
    nrows = x.shape[0]
    stride = GW if mode == "col" else 1
    t = lax.broadcasted_iota(jnp.int32, (nrows, 1), 0)
    acc = jnp.zeros_like(x)
    for j in range(ktaps):
        d = j - ktaps // 2
        jj = ktaps - 1 - j if flip else j
        wj = w_ref[jj:jj + 1, :]
        if d == 0:
            acc = acc + x * wj
        else:
            xs = pltpu.roll(x, (-d * stride) % nrows, 0)
            acc = acc + jnp.where(_tap_valid(mode, t, d, nrows, nlat), xs * wj, 0.0)
    return acc


def _conv_wgrad(dy, x, dw_ref, ktaps, mode, nlat):
    nrows = x.shape[0]
    stride = GW if mode == "col" else 1
    t = lax.broadcasted_iota(jnp.int32, (nrows, 1), 0)
    dw_ref[...] = jnp.zeros_like(dw_ref)
    for j in range(ktaps):
        d = j - ktaps // 2
        if d == 0:
            prod = x * dy
        else:
            xs = pltpu.roll(x, (-d * stride) % nrows, 0)
            prod = jnp.where(_tap_valid(mode, t, d, nrows, nlat), xs * dy, 0.0)
        dw_ref[j:j + 1, :] = _csum(prod)


def _qkv_post(j, pre, nh):
    a = _silu(pre)
    inv = lax.rsqrt(jnp.sum(a * a, axis=-1, keepdims=True) + 1e-6)
    scale = jnp.where(j < nh, HD ** -0.5, 1.0).astype(f32)
    nrm = jnp.where(j < 2 * nh, inv, 1.0) * scale
    return a, inv, nrm


def _qkv_conv_fwd(h_all, w7, nrows, nlat, nh, ktaps):
    ntile = 3 * nh

    def body(h_ref, w_ref, o_ref):
        j = pl.program_id(0)
        pre = _conv(h_ref[...], w_ref, ktaps, "seg", nlat)
        a, _, nrm = _qkv_post(j, pre, nh)
        o_ref[...] = a * nrm

    return pl.pallas_call(
        body, name="qkv_conv_fwd", grid=(ntile,),
        in_specs=[pl.BlockSpec((nrows, HD), lambda j: (0, j)), pl.BlockSpec((w7.shape[0], HD), lambda j: (0, j))],
        out_specs=pl.BlockSpec((None, nrows, HD), lambda j: (j, 0, 0)),
        out_shape=jax.ShapeDtypeStruct((ntile, nrows, HD), f32), compiler_params=_cparams(("parallel",)),
    )(h_all, w7)


def _qkv_conv_bwd(h_all, w7, dqkv, nrows, nlat, nh, ktaps):
    ntile = 3 * nh

    def body(h_ref, w_ref, d0_ref, d1_ref, dh_ref, dw_ref):
        j = pl.program_id(0)
        hx = h_ref[...]
        pre = _conv(hx, w_ref, ktaps, "seg", nlat)
        a, inv, nrm = _qkv_post(j, pre, nh)
        dn = (d0_ref[...] + d1_ref[...]) * jnp.where(j < nh, HD ** -0.5, 1.0).astype(f32)
        n = a * inv
        da_norm = inv * (dn - n * jnp.sum(dn * n, axis=-1, keepdims=True))
        da = jnp.where(j < 2 * nh, da_norm, dn)
        dpre = da * _dsilu(pre)
        dh_ref[...] = _conv(dpre, w_ref, ktaps, "seg", nlat, flip=True).astype(dh_ref.dtype)
        _conv_wgrad(dpre, hx, dw_ref, ktaps, "seg", nlat)

    blk = pl.BlockSpec((nrows, HD), lambda j: (0, j))
    wblk = pl.BlockSpec((w7.shape[0], HD), lambda j: (0, j))
    dblk = lambda d: pl.BlockSpec((None, None, nrows, HD), lambda j: (d, j, 0, 0))
    return pl.pallas_call(
        body, name="qkv_conv_bwd", grid=(ntile,), in_specs=[blk, wblk, dblk(0), dblk(1)], out_specs=[blk, wblk],
        out_shape=[jax.ShapeDtypeStruct((nrows, ntile * HD), bf16), jax.ShapeDtypeStruct((w7.shape[0], ntile * HD), f32)],
        compiler_params=_cparams(("parallel",)),
    )(h_all, w7, dqkv, dqkv)


def _conf_conv_fwd(h_all, w31, nlat, conf, val_blk, ktaps):
    nt = conf // LANES
    nhalf = nt // 2

    def body(v_ref, g_ref, w_ref, o_ref):
        j = pl.program_id(0)
        glu = v_ref[...] * _sigmoid(g_ref[...])

        @pl.when(j < nhalf)
        def _():
            o_ref[...] = _conv(glu, w_ref, ktaps, "row", nlat)

        @pl.when(j >= nhalf)
        def _():
            o_ref[...] = _conv(glu, w_ref, ktaps, "col", nlat)

    return pl.pallas_call(
        body, name="conf_conv_fwd", grid=(nt,),
        in_specs=[pl.BlockSpec((nlat, LANES), lambda j: (0, val_blk + j)), pl.BlockSpec((nlat, LANES), lambda j: (0, val_blk + nt + j)),
                  pl.BlockSpec((w31.shape[0], LANES), lambda j: (0, j))],
        out_specs=pl.BlockSpec((nlat, LANES), lambda j: (0, j)),
        out_shape=jax.ShapeDtypeStruct((nlat, conf), f32), compiler_params=_cparams(("parallel",)),
    )(h_all, h_all, w31)


def _conf_conv_bwd(h_all, w31, dyc, nlat, conf, val_blk, ktaps):
    nt = conf // LANES
    nhalf = nt // 2

    def body(v_ref, g_ref, w_ref, dy_ref, dv_ref, dg_ref, dw_ref):
        j = pl.program_id(0)
        val, sg = v_ref[...], _sigmoid(g_ref[...])
        glu = val * sg
        dy = dy_ref[...]

        def run(mode):
            dglu = _conv(dy, w_ref, ktaps, mode, nlat, flip=True)
            dv_ref[...] = (dglu * sg).astype(dv_ref.dtype)
            dg_ref[...] = (dglu * val * sg * (1.0 - sg)).astype(dg_ref.dtype)
            _conv_wgrad(dy, glu, dw_ref, ktaps, mode, nlat)

        @pl.when(j < nhalf)
        def _():
            run("row")

        @pl.when(j >= nhalf)
        def _():
            run("col")

    blk = pl.BlockSpec((nlat, LANES), lambda j: (0, j))
    wblk = pl.BlockSpec((w31.shape[0], LANES), lambda j: (0, j))
    return pl.pallas_call(
        body, name="conf_conv_bwd", grid=(nt,),
        in_specs=[pl.BlockSpec((nlat, LANES), lambda j: (0, val_blk + j)), pl.BlockSpec((nlat, LANES), lambda j: (0, val_blk + nt + j)), wblk, blk],
        out_specs=[blk, blk, wblk],
        out_shape=[jax.ShapeDtypeStruct((nlat, conf), bf16), jax.ShapeDtypeStruct((nlat, conf), bf16),
                   jax.ShapeDtypeStruct((w31.shape[0], conf), f32)],
        compiler_params=_cparams(("parallel",)),
    )(h_all, h_all, w31, dyc)


def _bd(eq, a, b):
    return jnp.einsum(eq, a.astype(bf16), b.astype(bf16), preferred_element_type=f32)


def _split(x, pieces):
    out = []
    for _ in range(pieces - 1):
        p = x.astype(bf16)
        out.append(p)
        x = x - p.astype(f32)
    return out + [x.astype(bf16)]


def _h3(eq, a, b):
    (ah, al), (bh, bl) = _split(a, 2), _split(b, 2)
    d = lambda p, q: jnp.einsum(eq, p, q, preferred_element_type=f32)
    return d(ah, bh) + (d(ah, bl) + d(al, bh))


def _cumsum_rows(x, rev):
    row = lax.broadcasted_iota(jnp.int32, x.shape, 0)
    down, up, s = x, x, 1
    while s < CH:
        down = down + jnp.where(row >= s, pltpu.roll(down, s, 0), 0.0)
        up = up + jnp.where(row < CH - s, pltpu.roll(up, CH - s, 0), 0.0)
        s *= 2
    return jnp.where(rev, up, down)


def _gdn_chunk(q, k, v, gt, nh, rev, solved=None):
    beta = jnp.stack([jnp.broadcast_to(gt[:, nh + h:nh + h + 1], (CH, LANES)) for h in range(nh)])
    ii = lax.broadcasted_iota(jnp.int32, (CH, CH), 0)
    jj = lax.broadcasted_iota(jnp.int32, (CH, CH), 1)
    si, sj = jnp.where(rev, jj, ii), jnp.where(rev, ii, jj)
    tril, stril = (si >= sj)[None], (si > sj)[None]
    gct = _cumsum_rows(gt, rev)
    gct_t = gct.T
    gc = jnp.stack([jnp.broadcast_to(gct[:, h:h + 1], (CH, LANES)) for h in range(nh)])
    gc_row = jnp.stack([jnp.broadcast_to(gct_t[h:h + 1, :], (CH, CH)) for h in range(nh)])
    diff = gc[:, :, :CH] - gc_row
    gam = jnp.where(tril, jnp.exp(jnp.where(tril, diff, 0.0)), 0.0)
    e = jnp.exp(gc)
    gl = jnp.where(rev, gc[:, 0:1, :], gc[:, CH - 1:CH, :])
    el = jnp.exp(gl)
    r = jnp.exp(gl - gc)
    kb, vb = k * beta, v * beta
    kbe = kb * e
    amat = jnp.where(stril, _bd("hik,hjk->hij", kb, k) * gam, 0.0)
    if solved is None:
        xp = -amat
        tinv = (ii == jj).astype(f32)[None] + xp
        for _ in range(5):
            xp = _h3("hij,hjk->hik", xp, xp)
            tinv = tinv + _h3("hij,hjk->hik", tinv, xp)
        u = _h3("hij,hjv->hiv", tinv, vb)
        w = _h3("hij,hjk->hik", tinv, kbe)
    else:
        tinv, u, w = solved
    pmat = jnp.where(tril, _bd("hik,hjk->hij", q, k) * gam, 0.0)
    return dict(beta=beta, gam=gam, e=e, el=el, r=r, kb=kb, vb=vb, kbe=kbe, amat=amat, tinv=tinv, u=u, w=w, pmat=pmat,
                qd=q * e, kd=k * r, tril=tril, stril=stril)


def _scan_row_chunk(d, n, ns, nt):
    fwd = jnp.where(n < nt, ns + n, n - nt)
    return jnp.where(d == 0, fwd, ns + nt - 1 - n)


def _gdn_fwd(qkv, gd, nh, nlat, nctx):
    nrows = nlat + nctx
    ns, nt = nlat // CH, nctx // CH
    nch = ns + nt

    def body(qkv_ref, g_ref, o_ref, st_ref, ti_ref, u_ref, w_ref, state):
        d, n = pl.program_id(0), pl.program_id(1)

        @pl.when(n == 0)
        def _():
            state[...] = jnp.zeros_like(state)

        c = _gdn_chunk(qkv_ref[0], qkv_ref[1], qkv_ref[2], g_ref[...], nh, d == 1)
        s = state[...]
        vn = c["u"] - _bd("hck,hkv->hcv", c["w"], s)
        o_ref[...] = _bd("hck,hkv->hcv", c["qd"], s) + _bd("hcd,hdv->hcv", c["pmat"], vn)
        st_ref[...] = s
        ti_ref[...], u_ref[...], w_ref[...] = c["tinv"], c["u"], c["w"]
        state[...] = s * c["el"] + _bd("hck,hcv->hkv", c["kd"], vn)

    hblk = pl.BlockSpec((None, nh, CH, HD), lambda d, n: (d, 0, _scan_row_chunk(d, n, ns, nt), 0))
    hshape = jax.ShapeDtypeStruct((2, nh, nrows, HD), f32)
    return pl.pallas_call(
        body, name="gdn_fwd", grid=(2, nch),
        in_specs=[pl.BlockSpec((3, nh, CH, HD), lambda d, n: (0, 0, _scan_row_chunk(d, n, ns, nt), 0)),
                  pl.BlockSpec((None, CH, LANES), lambda d, n: (d, _scan_row_chunk(d, n, ns, nt), 0))],
        out_specs=[hblk, pl.BlockSpec((None, None, nh, HD, HD), lambda d, n: (d, n, 0, 0, 0)),
                   pl.BlockSpec((None, None, nh, CH, CH), lambda d, n: (d, n, 0, 0, 0)), hblk, hblk],
        out_shape=[hshape, jax.ShapeDtypeStruct((2, nch, nh, HD, HD), f32), jax.ShapeDtypeStruct((2, nch, nh, CH, CH), f32), hshape, hshape],
        scratch_shapes=[pltpu.VMEM((nh, HD, HD), f32)], compiler_params=_cparams(("arbitrary", "arbitrary")),
    )(qkv, gd)


def _gdn_bwd(qkv, gd, states, solved, do, nh, nlat, nctx):
    nrows = nlat + nctx
    ns, nt = nlat // CH, nctx // CH
    nch = ns + nt

    def body(qkv_ref, g_ref, st_ref, ti_ref, u_ref, w_ref, do_ref, dqkv_ref, dgd_ref, dstate):
        d, step = pl.program_id(0), pl.program_id(1)

        @pl.when(step == 0)
        def _():
            dstate[...] = jnp.zeros_like(dstate)

        q, k, v = qkv_ref[0], qkv_ref[1], qkv_ref[2]
        c = _gdn_chunk(q, k, v, g_ref[...], nh, d == 1, solved=(ti_ref[...], u_ref[...], w_ref[...]))
        in_ctx = (nch - 1 - step) < nt
        s, dsp = st_ref[...], dstate[...]
        dout = jnp.where(in_ctx, 0.0, do_ref[...])
        beta, gam, e, el, r = c["beta"], c["gam"], c["e"], c["el"], c["r"]
        tinv, u, w, pmat, amat = c["tinv"], c["u"], c["w"], c["pmat"], c["amat"]
        vn = u - _bd("hck,hkv->hcv", w, s)
        dvn = _bd("hdc,hdv->hcv", pmat, dout) + _bd("hck,hkv->hcv", c["kd"], dsp)
        dp = jnp.where(c["tril"], _bd("hcv,hdv->hcd", dout, vn), 0.0)
        dqd = _bd("hcv,hkv->hck", dout, s)
        dkd = _bd("hcv,hkv->hck", vn, dsp)
        dstate[...] = _bd("hck,hcv->hkv", c["qd"], dout) + dsp * el - _bd("hck,hcv->hkv", w, dvn)
        del_ = jnp.sum(jnp.sum(s * dsp, axis=2, keepdims=True), axis=1, keepdims=True)
        dw = -_bd("hcv,hkv->hck", dvn, s)
        dvb = _h3("hji,hjv->hiv", tinv, dvn)
        dkbe = _h3("hji,hjk->hik", tinv, dw)
        z = _bd("hiv,hjv->hij", dvn, u) + _bd("hik,hjk->hij", dw, w)
        da = jnp.where(c["stril"], -_h3("hji,hjl->hil", tinv, z), 0.0)
        dm = da * gam
        dkb = _bd("hij,hjk->hik", dm, k) + dkbe * e
        dn = dp * gam
        dqkv_ref[0] = _bd("hij,hjk->hik", dn, k) + dqd * e
        dqkv_ref[1] = _bd("hji,hjk->hik", dm, c["kb"]) + _bd("hji,hjk->hik", dn, q) + dkd * r + dkb * beta
        dqkv_ref[2] = dvb * beta
        gmat = da * amat + dp * pmat
        rows_minus_cols = jnp.sum(gmat, axis=-1, keepdims=True) - jnp.sum(jnp.swapaxes(gmat, 1, 2), axis=-1, keepdims=True)
        de = jnp.sum(dqd * q + dkbe * c["kb"], axis=-1, keepdims=True)
        drr = jnp.sum(dkd * k, axis=-1, keepdims=True) * r
        dgc = rows_minus_cols + de * e - drr
        dgl = jnp.sum(drr, axis=1, keepdims=True) + del_ * el
        row = lax.broadcasted_iota(jnp.int32, (1, CH, 1), 1)
        total_row = row == jnp.where(d == 1, 0, CH - 1)
        dgc = dgc + jnp.where(total_row, dgl, 0.0)
        db = jnp.sum(dvb * v + dkb * k, axis=-1, keepdims=True)
        lane = lax.broadcasted_iota(jnp.int32, (CH, LANES), 1)
        dgc_cols, db_cols = jnp.zeros((CH, LANES), f32), jnp.zeros((CH, LANES), f32)
        for h in range(nh):
            dgc_cols = dgc_cols + jnp.where(lane == h, dgc[h], 0.0)
            db_cols = db_cols + jnp.where(lane == nh + h, db[h], 0.0)
        dgd_ref[...] = _cumsum_rows(dgc_cols, d == 0) + db_cols

    def rc(d, n):
        return _scan_row_chunk(d, nch - 1 - n, ns, nt)

    gblk = pl.BlockSpec((None, CH, LANES), lambda d, n: (d, rc(d, n), 0))
    return pl.pallas_call(
        body, name="gdn_bwd", grid=(2, nch),
        in_specs=[pl.BlockSpec((3, nh, CH, HD), lambda d, n: (0, 0, rc(d, n), 0)),
                  gblk,
                  pl.BlockSpec((None, None, nh, HD, HD), lambda d, n: (d, nch - 1 - n, 0, 0, 0)),
                  pl.BlockSpec((None, None, nh, CH, CH), lambda d, n: (d, nch - 1 - n, 0, 0, 0)),
                  pl.BlockSpec((None, nh, CH, HD), lambda d, n: (d, 0, rc(d, n), 0)),
                  pl.BlockSpec((None, nh, CH, HD), lambda d, n: (d, 0, rc(d, n), 0)),
                  pl.BlockSpec((nh, CH, HD), lambda d, n: (0, jnp.minimum(rc(d, n), ns - 1), 0))],
        out_specs=[pl.BlockSpec((None, 3, nh, CH, HD), lambda d, n: (d, 0, 0, rc(d, n), 0)), gblk],
        out_shape=[jax.ShapeDtypeStruct((2, 3, nh, nrows, HD), f32), jax.ShapeDtypeStruct((2, nrows, LANES), f32)],
        scratch_shapes=[pltpu.VMEM((nh, HD, HD), f32)], compiler_params=_cparams(("arbitrary", "arbitrary")),
    )(qkv, gd, states, *solved, do)


def _gate_norm_fwd(o5, h_all, dng, nh, nlat, zblk, tr):
    def body(of_ref, ob_ref, z_ref, g_ref, out_ref):
        o = of_ref[...] + ob_ref[...]
        rs = lax.rsqrt(jnp.mean(o * o, axis=-1, keepdims=True) + 1e-6)
        out_ref[...] = ((o * rs * g_ref[...]) * _silu(z_ref[...])).astype(out_ref.dtype)

    return pl.pallas_call(
        body, name="gate_norm_fwd", grid=(nh, nlat // tr),
        in_specs=[pl.BlockSpec((None, None, tr, HD), lambda j, i: (0, j, i, 0)), pl.BlockSpec((None, None, tr, HD), lambda j, i: (1, j, i, 0)),
                  pl.BlockSpec((tr, HD), lambda j, i: (i, zblk + j)), pl.BlockSpec((1, HD), lambda j, i: (0, 0))],
        out_specs=pl.BlockSpec((tr, HD), lambda j, i: (i, j)), out_shape=jax.ShapeDtypeStruct((nlat, nh * HD), bf16),
        compiler_params=_cparams(("parallel", "parallel")),
    )(o5, o5, h_all, dng)


def _gate_norm_bwd(dmix, o5, h_all, dng, nh, nlat, zblk, tr):
    def body(dm_ref, of_ref, ob_ref, z_ref, g_ref, do_ref, dz_ref, dg_ref):
        o = of_ref[...] + ob_ref[...]
        z, g, dm = z_ref[...], g_ref[...], dm_ref[...]
        rs = lax.rsqrt(jnp.mean(o * o, axis=-1, keepdims=True) + 1e-6)
        don = dm * _silu(z)
        dz_ref[...] = (dm * (o * rs * g) * _dsilu(z)).astype(dz_ref.dtype)
        dog = don * g
        do_ref[...] = rs * (dog - o * rs * rs * jnp.mean(dog * o, axis=-1, keepdims=True))

        @pl.when(jnp.logical_and(pl.program_id(0) == 0, pl.program_id(1) == 0))
        def _():
            dg_ref[...] = jnp.zeros_like(dg_ref)

        dg_ref[...] += _csum(don * o * rs)

    return pl.pallas_call(
        body, name="gate_norm_bwd", grid=(nh, nlat // tr),
        in_specs=[pl.BlockSpec((tr, HD), lambda j, i: (i, j)),
                  pl.BlockSpec((None, None, tr, HD), lambda j, i: (0, j, i, 0)), pl.BlockSpec((None, None, tr, HD), lambda j, i: (1, j, i, 0)),
                  pl.BlockSpec((tr, HD), lambda j, i: (i, zblk + j)), pl.BlockSpec((1, HD), lambda j, i: (0, 0))],
        out_specs=[pl.BlockSpec((None, tr, HD), lambda j, i: (j, i, 0)), pl.BlockSpec((tr, HD), lambda j, i: (i, j)),
                   pl.BlockSpec((1, HD), lambda j, i: (0, 0))],
        out_shape=[jax.ShapeDtypeStruct((nh, nlat, HD), f32), jax.ShapeDtypeStruct((nlat, nh * HD), bf16), jax.ShapeDtypeStruct((1, HD), f32)],
        compiler_params=_cparams(("arbitrary", "arbitrary")),
    )(dmix, o5, o5, h_all, dng)


def _mesh_pos():
    return lax.axis_index("x"), lax.axis_index("y"), lax.axis_index("c")


def _lin(p):
    return 4 * p[0] + 2 * p[1] + p[2]


def _all_gather(name, xs, after=None):
    nx = len(xs)
    extra = [] if after is None else [after]

    def body(*refs):
        xr, outr = refs[:nx], refs[nx + len(extra):2 * nx + len(extra)]
        send, recv, loc = refs[2 * nx + len(extra):]
        x, y, c = _mesh_pos()
        me, sib = (x, y, c), (x, y, 1 - c)
        chips = [(1 - x, y), (x, 1 - y), (1 - x, 1 - y)]

        def cp(l, k, block, to, src=None):
            rows = outr[l].at[_lin(block)]
            return pltpu.make_async_remote_copy(src_ref=rows if src is None else src, dst_ref=rows, send_sem=send.at[l, k],
                                                recv_sem=recv.at[l, k], device_id=to, device_id_type=MESH)

        mine = [pltpu.make_async_copy(xr[l], outr[l].at[_lin(me)], loc.at[l]) for l in range(nx)]
        for m in mine:
            m.start()
        first = []
        for l in range(nx):
            first.append(cp(l, 0, me, sib, src=xr[l]))
            first += [cp(l, 1 + j, me, (*chip, c), src=xr[l]) for j, chip in enumerate(chips)]
        for f in first:
            f.start()
        passed = []
        for l in range(nx):
            for j, chip in enumerate(chips):
                cp(l, 1 + j, (*chip, c), me).wait_recv()
                fwd = cp(l, 4 + j, (*chip, c), sib)
                fwd.start()
                passed.append(fwd)
        for l in range(nx):
            cp(l, 0, sib, me).wait_recv()
            for j, chip in enumerate(chips):
                cp(l, 4 + j, (*chip, 1 - c), me).wait_recv()
        for f in first + passed:
            f.wait_send()
        for m in mine:
            m.wait()

    anyspec = pl.BlockSpec(memory_space=pl.ANY)
    return pl.pallas_call(
        body, name=name, in_specs=[anyspec] * (nx + len(extra)), out_specs=[anyspec] * nx,
        out_shape=[jax.ShapeDtypeStruct((NDEV,) + a.shape, a.dtype) for a in xs],
        scratch_shapes=[pltpu.SemaphoreType.DMA((nx, 7)), pltpu.SemaphoreType.DMA((nx, 7)), pltpu.SemaphoreType.DMA((nx,))],
    )(*xs, *extra)


HBM_SPEC = pl.BlockSpec(memory_space=pltpu.HBM)
SEM_SPEC = pl.BlockSpec(memory_space=pltpu.SEMAPHORE)
ANY_SPEC = pl.BlockSpec(memory_space=pl.ANY)
EFFECT = pltpu.SideEffectType.DATAFLOW_SIDE_EFFECTING


def _peer(rel):
    x, y, c = _mesh_pos()
    return (1 - x if rel & 4 else x, 1 - y if rel & 2 else y, 1 - c if rel & 1 else c)


ALL_PEERS = (1, 2, 3, 4, 5, 6, 7)
SAME_CORE = (2, 4, 6)


def _slot(p, rels):
    return _lin(p) if rels == ALL_PEERS else 2 * p[0] + p[1]


def _xchg_start(name, x, gather, after, rels=ALL_PEERS):
    me = _slot(_mesh_pos(), rels)
    shape = (len(rels) + 1,) + x.shape if gather else x.shape
    own = x if gather else lax.dynamic_index_in_dim(x, me, 0, keepdims=False)
    land = lax.dynamic_update_index_in_dim(lax.empty(shape, x.dtype), own, me, 0)

    def body(x_ref, land_ref, after_ref, send, recv, x_thru, land_thru, token):
        mine = _slot(_mesh_pos(), rels)
        for k, rel in enumerate(rels):
            p = _peer(rel)
            pltpu.make_async_remote_copy(src_ref=x_ref if gather else x_ref.at[_slot(p, rels)], dst_ref=land_ref.at[mine],
                                         send_sem=send.at[k], recv_sem=recv.at[k], device_id=p, device_id_type=MESH).start()
        token[...] = jnp.zeros_like(token)

    return pl.pallas_call(
        body, name=name,
        out_shape=(pltpu.SemaphoreType.DMA((len(rels),)), pltpu.SemaphoreType.DMA((len(rels),)), pltpu.HBM(x.shape, x.dtype),
                   pltpu.HBM(shape, x.dtype), jax.ShapeDtypeStruct((8, LANES), f32)),
        in_specs=(HBM_SPEC, HBM_SPEC, ANY_SPEC), out_specs=(SEM_SPEC, SEM_SPEC, HBM_SPEC, HBM_SPEC, pl.BlockSpec(memory_space=pltpu.VMEM)),
        input_output_aliases={0: 2, 1: 3}, compiler_params=pltpu.CompilerParams(has_side_effects=EFFECT),
    )(pltpu.with_memory_space_constraint(x, pltpu.HBM), pltpu.with_memory_space_constraint(land, pltpu.HBM), after)


def _xchg_wait(name, started, gather, after, rels=ALL_PEERS):
    send, recv, x_thru, land_thru, _ = started

    def body(x_ref, land_ref, send, recv, after_ref, x_dead, got_ref):
        for k, rel in enumerate(rels):
            p = _peer(rel)
            cp = pltpu.make_async_remote_copy(src_ref=x_ref if gather else x_ref.at[_slot(p, rels)], dst_ref=land_ref.at[_slot(p, rels)],
                                              send_sem=send.at[k], recv_sem=recv.at[k], device_id=p, device_id_type=MESH)
            cp.wait_send()
            cp.wait_recv()

    return pl.pallas_call(
        body, name=name, out_shape=(pltpu.HBM(x_thru.shape, x_thru.dtype), pltpu.HBM(land_thru.shape, land_thru.dtype)),
        in_specs=(HBM_SPEC, HBM_SPEC, SEM_SPEC, SEM_SPEC, ANY_SPEC), out_specs=(HBM_SPEC, HBM_SPEC), input_output_aliases={0: 0, 1: 1},
        compiler_params=pltpu.CompilerParams(has_side_effects=EFFECT),
    )(x_thru, land_thru, send, recv, after)[1]


def _sib_swap(name, x):
    def body(x_ref, out_ref, send, recv):
        mx, my, mc = _mesh_pos()
        sib = (mx, my, 1 - mc)
        cps = [pltpu.make_async_remote_copy(src_ref=x_ref.at[2 * q + 1 - mc], dst_ref=out_ref.at[q], send_sem=send.at[q], recv_sem=recv.at[q],
                                            device_id=sib, device_id_type=MESH) for q in range(4)]
        for cp in cps:
            cp.start()
        for cp in cps:
            cp.wait_recv()
        for cp in cps:
            cp.wait_send()

    return pl.pallas_call(
        body, name=name, in_specs=[ANY_SPEC], out_specs=ANY_SPEC, out_shape=jax.ShapeDtypeStruct((4,) + x.shape[1:], x.dtype),
        scratch_shapes=[pltpu.SemaphoreType.DMA((4,)), pltpu.SemaphoreType.DMA((4,))],
    )(x)


def _behind(token, a):
    return a + token[0:1, 0:1].astype(a.dtype)


def _sum8(name, g):
    def body(g_ref, o_ref):
        acc = g_ref[0:1, :]
        for p in range(1, NDEV):
            acc = acc + g_ref[p:p + 1, :]
        o_ref[...] = acc

    return pl.pallas_call(body, name=name, out_shape=jax.ShapeDtypeStruct((1, g.shape[1]), f32),
                          compiler_params=pltpu.CompilerParams(vmem_limit_bytes=VMEM_LIMIT))(g)


def _pack(arrs):
    flat = jnp.concatenate([a.reshape(-1).astype(f32) for a in arrs])
    pad = (-flat.shape[0]) % (8 * LANES)
    return jnp.pad(flat, (0, pad)).reshape(1, -1)


def _unpack(vec, shapes):
    out, off = [], 0
    flat = vec.reshape(-1)
    for s in shapes:
        n = 1
        for d in s:
            n *= d
        out.append(flat[off:off + n].reshape(s))
        off += n
    return out


def _adam(name, w, m, v, g, parts, after=None):
    extra = [] if after is None else [after]
    nrows, width = w.shape
    tr = _row_tile(nrows, width) if nrows >= 16 else nrows
    blk = pl.BlockSpec((tr, width), lambda i: (i, 0))
    gblk = pl.BlockSpec((parts, tr, width), lambda i: (0, i, 0)) if parts else blk
    c1 = 1.0 / (1.0 - ADAM_B1 ** ADAM_STEP)
    c2 = 1.0 / (1.0 - ADAM_B2 ** ADAM_STEP)

    def body(w_ref, m_ref, v_ref, g_ref, *rest):
        go_ref, d_ref, mo_ref, vo_ref = rest[len(extra):]
        if parts:
            gg = g_ref[0].astype(f32)
            for p in range(1, parts):
                gg = gg + g_ref[p].astype(f32)
        else:
            gg = g_ref[...]
        mn = ADAM_B1 * m_ref[...] + (1.0 - ADAM_B1) * gg
        vn = ADAM_B2 * v_ref[...] + (1.0 - ADAM_B2) * (gg * gg)
        go_ref[...] = gg
        mo_ref[...] = mn
        vo_ref[...] = vn
        d_ref[...] = -ADAM_LR * ((mn * c1) / (jnp.sqrt(vn * c2) + ADAM_EPS) + ADAM_WD * w_ref[...])

    sh = jax.ShapeDtypeStruct((nrows, width), f32)
    return pl.pallas_call(body, name=name, grid=(nrows // tr,), in_specs=[blk, blk, blk, gblk] + [pl.BlockSpec(memory_space=pl.ANY)] * len(extra),
                          out_specs=[blk] * 4, out_shape=[sh] * 4, compiler_params=_cparams(("parallel",)))(w, m, v, g, *extra)


def kernel(x, c, ctx, c_ctx, ln_in_g, ln_in_b, w_mod, b_mod, w_in, w_qkv_conv, a_log_f, dt_bias_f, a_log_b, dt_bias_b, dn_norm_g, conf_dw_w, conf_dw_b, conf_ln_g, conf_ln_b, w_out, ln1_g, ln1_b, w_mlp1, b_mlp1, w_mlp2, b_mlp2, ln2_g, ln2_b, loss_target, m_c_ctx, m_ln_in_g, m_ln_in_b, m_w_mod, m_b_mod, m_w_in, m_w_qkv_conv, m_a_log_f, m_dt_bias_f, m_a_log_b, m_dt_bias_b, m_dn_norm_g, m_conf_dw_w, m_conf_dw_b, m_conf_ln_g, m_conf_ln_b, m_w_out, m_ln1_g, m_ln1_b, m_w_mlp1, m_b_mlp1, m_w_mlp2, m_b_mlp2, m_ln2_g, m_ln2_b, v_c_ctx, v_ln_in_g, v_ln_in_b, v_w_mod, v_b_mod, v_w_in, v_w_qkv_conv, v_a_log_f, v_dt_bias_f, v_a_log_b, v_dt_bias_b, v_dn_norm_g, v_conf_dw_w, v_conf_dw_b, v_conf_ln_g, v_conf_ln_b, v_w_out, v_ln1_g, v_ln1_b, v_w_mlp1, v_b_mlp1, v_w_mlp2, v_b_mlp2, v_ln2_g, v_ln2_b):
    weights = dict(c_ctx=c_ctx, ln_in_g=ln_in_g, ln_in_b=ln_in_b, w_mod=w_mod, b_mod=b_mod, w_in=w_in, w_qkv_conv=w_qkv_conv, a_log_f=a_log_f, dt_bias_f=dt_bias_f, a_log_b=a_log_b, dt_bias_b=dt_bias_b, dn_norm_g=dn_norm_g, conf_dw_w=conf_dw_w, conf_dw_b=conf_dw_b, conf_ln_g=conf_ln_g, conf_ln_b=conf_ln_b, w_out=w_out, ln1_g=ln1_g, ln1_b=ln1_b, w_mlp1=w_mlp1, b_mlp1=b_mlp1, w_mlp2=w_mlp2, b_mlp2=b_mlp2, ln2_g=ln2_g, ln2_b=ln2_b)
    mom1 = dict(c_ctx=m_c_ctx, ln_in_g=m_ln_in_g, ln_in_b=m_ln_in_b, w_mod=m_w_mod, b_mod=m_b_mod, w_in=m_w_in, w_qkv_conv=m_w_qkv_conv, a_log_f=m_a_log_f, dt_bias_f=m_dt_bias_f, a_log_b=m_a_log_b, dt_bias_b=m_dt_bias_b, dn_norm_g=m_dn_norm_g, conf_dw_w=m_conf_dw_w, conf_dw_b=m_conf_dw_b, conf_ln_g=m_conf_ln_g, conf_ln_b=m_conf_ln_b, w_out=m_w_out, ln1_g=m_ln1_g, ln1_b=m_ln1_b, w_mlp1=m_w_mlp1, b_mlp1=m_b_mlp1, w_mlp2=m_w_mlp2, b_mlp2=m_b_mlp2, ln2_g=m_ln2_g, ln2_b=m_ln2_b)
    mom2 = dict(c_ctx=v_c_ctx, ln_in_g=v_ln_in_g, ln_in_b=v_ln_in_b, w_mod=v_w_mod, b_mod=v_b_mod, w_in=v_w_in, w_qkv_conv=v_w_qkv_conv, a_log_f=v_a_log_f, dt_bias_f=v_dt_bias_f, a_log_b=v_a_log_b, dt_bias_b=v_dt_bias_b, dn_norm_g=v_dn_norm_g, conf_dw_w=v_conf_dw_w, conf_dw_b=v_conf_dw_b, conf_ln_g=v_conf_ln_g, conf_ln_b=v_conf_ln_b, w_out=v_w_out, ln1_g=v_ln1_g, ln1_b=v_ln1_b, w_mlp1=v_w_mlp1, b_mlp1=v_b_mlp1, w_mlp2=v_w_mlp2, b_mlp2=v_b_mlp2, ln2_g=v_ln2_g, ln2_b=v_ln2_b)
    names = list(weights)

    me = _lin(_mesh_pos())
    S, D = x.shape[1], x.shape[2]
    T = ctx.shape[1]
    R = S + T
    DN = D // 2
    NH = DN // HD
    CONF = D - DN
    K7, K31 = w_qkv_conv.shape[1], conf_dw_w.shape[1]
    DFF = w_mlp1.shape[2] * NDEV
    INC = w_in.shape[2] * NDEV
    CONF_OFF = 4 * DN + 4 * NH
    NC = 4 * DN + 2 * CONF + GPAD
    QB, ZB, VB, GB = 0, 3 * DN // LANES, 4 * DN // LANES, (4 * DN + 2 * CONF) // LANES
    MODC = w_mod.shape[2]
    x2d, ctx2d, tgt = x[0], ctx[0], loss_target[0]
    row = lambda a: a.reshape(1, -1).astype(f32)

    c_all, w7_all, w31_all = _all_gather("ag_small", [c.astype(f32), w_qkv_conv[0], conf_dw_w[0]])
    w7 = jnp.pad(jnp.transpose(w7_all, (1, 0, 2)).reshape(K7, 3 * DN), ((0, (-K7) % 8), (0, 0)))
    w31 = jnp.pad(jnp.transpose(w31_all, (1, 0, 2)).reshape(K31, CONF), ((0, (-K31) % 8), (0, 0)))
    c16 = jnp.concatenate([c_all.reshape(NDEV, D), c_ctx.reshape(1, D).astype(f32), jnp.zeros((7, D), f32)], axis=0)
    (sc16,) = _rowcall("silu_c", lambda j, a: (_silu(a),), 16, 16, [(c16, D, 0, 0, False)], [], [(D, f32, False)], [])
    bmod_mine = lax.dynamic_slice_in_dim(b_mod.astype(f32), me * MODC, MODC, axis=1)
    mod_part = _mm("mod_fwd", sc16, w_mod[0], "nn", f32, extras=[("row", bmod_mine)], epilogue=lambda r, b: (r + b,))
    (mod_g,) = _all_gather("ag_mod", [mod_part])
    mod_all = jnp.transpose(mod_g, (1, 0, 2)).reshape(16, 6 * D)
    mod_me = lax.dynamic_slice_in_dim(mod_all, me, 1, axis=0)
    sh_a, sc_a, g_a, sh_m, sc_m, g_m = [mod_me[:, i * D:(i + 1) * D] for i in range(6)]
    csh_a, csc_a = mod_all[8:9, 0:D], mod_all[8:9, D:2 * D]

    (g_win,) = _all_gather("ag_w_in", [w_in[0].astype(bf16)], after=mod_g)
    ag_wout = _xchg_start("ag_wout_start", w_out[0].astype(bf16), True, g_win)
    ag_w1 = _xchg_start("ag_w1_start", w_mlp1[0].astype(bf16), True, ag_wout[4])
    win_full = jnp.transpose(g_win, (1, 0, 2)).reshape(D, INC)
    w_cat = jnp.concatenate([win_full[:, :4 * DN], win_full[:, CONF_OFF:], win_full[:, 4 * DN:CONF_OFF],
                             jnp.zeros((D, GPAD - 4 * NH), bf16)], axis=1)
    g0, b0 = row(ln_in_g), row(ln_in_b)
    g0_fwd = _behind(ag_w1[4], g0)

    def ln_mod(j, xt, g, b, sh, sc):
        xh, _ = _ln_stats(xt)
        xn = xh * g + b
        return xn, xn * (1.0 + sc) + sh

    tr = _row_tile(T, D)
    vec = lambda a: (a, a.shape[1], 0, False)
    xn, xm = _rowcall("ln_in_lat", ln_mod, S, tr, [(x2d, D, 0, 0, False)], [vec(g0_fwd), vec(b0), vec(sh_a), vec(sc_a)], [(D, f32, False), (D, bf16, False)], [])
    xcn, xcm = _rowcall("ln_in_ctx", ln_mod, T, tr, [(ctx2d, D, 0, 0, False)], [vec(g0), vec(b0), vec(csh_a), vec(csc_a)], [(D, f32, False), (D, bf16, False)], [])
    xm_all = jnp.concatenate([xm, xcm], axis=0)

    h_all = _mm("in_proj", xm_all, w_cat, "nn", f32, tm=1088, tn=1280, tk=2048)

    qkv = _qkv_conv_fwd(h_all, w7, R, S, NH, K7).reshape(3, NH, R, HD)
    lane = jnp.arange(LANES)
    is_a = ((lane < 4 * NH) & ((lane // NH) % 2 == 0)).astype(f32).reshape(1, LANES)
    pad_g = lambda a, b: jnp.concatenate([a.reshape(-1), jnp.zeros((NH,), f32), b.reshape(-1), jnp.zeros((LANES - 3 * NH,), f32)]).reshape(1, LANES)
    neg_a = pad_g(-jnp.exp(a_log_f.astype(f32)), -jnp.exp(a_log_b.astype(f32)))
    dt_v = pad_g(dt_bias_f.astype(f32), dt_bias_b.astype(f32))

    def gates_f(j, hg, isa, na, dt):
        return (jnp.where(isa > 0.5, na * _softplus(hg + dt), _sigmoid(hg)),)

    trg = _row_tile(T, LANES)
    ag_w2 = _xchg_start("ag_w2_start", w_mlp2[0].astype(bf16), True, qkv)
    (gates,) = _rowcall("gates_fwd", gates_f, R, trg, [(h_all, LANES, 0, GB, False)], [vec(_behind(ag_w2[4], is_a)), vec(neg_a), vec(dt_v)], [(LANES, f32, False)], [])
    gpad = jnp.zeros((R, LANES - 2 * NH), f32)
    gd = jnp.stack([jnp.concatenate([gates[:, :2 * NH], gpad], axis=1), jnp.concatenate([gates[:, 2 * NH:4 * NH], gpad], axis=1)])

    o5, states, *solved = _gdn_fwd(qkv, gd, NH, S, T)
    dng = row(dn_norm_g)
    trh = _row_tile(S, HD, cap=1024)
    dn_out = _gate_norm_fwd(o5, h_all, dng, NH, S, ZB, trh)

    yconv = _conf_conv_fwd(h_all, w31, S, CONF, VB, K31)
    bdw, clg, clb = row(conf_dw_b), row(conf_ln_g), row(conf_ln_b)

    def conf_ln(j, yc, b, g, bb):
        xh, _ = _ln_stats(yc + b)
        return (_silu(xh * g + bb),)

    trc = _row_tile(S, CONF)
    (conf_out,) = _rowcall("conf_ln_fwd", conf_ln, S, trc, [(yconv, CONF, 0, 0, False)], [vec(bdw), vec(clg), vec(clb)], [(CONF, bf16, False)], [])
    mix = jnp.concatenate([dn_out, conf_out], axis=1)
    wout_full = _xchg_wait("ag_wout_wait", ag_wout, True, mix).reshape(DN + CONF, D)
    y = _mm("out_proj", mix, wout_full, "nn", f32)

    l1g, l1b, l2g, l2b = row(ln1_g), row(ln1_b), row(ln2_g), row(ln2_b)

    def ln1_mod(j, xnt, yt, ga, g, b, sh, sc):
        xh, _ = _ln_stats(ALPHA * xnt + ga * yt)
        x1 = xh * g + b
        return x1, x1 * (1.0 + sc) + sh

    trd = _row_tile(S, D)
    x1, u = _rowcall("ln1_fwd", ln1_mod, S, trd, [(xn, D, 0, 0, False), (y, D, 0, 0, False)],
                     [vec(g_a), vec(l1g), vec(l1b), vec(sh_m), vec(sc_m)], [(D, f32, False), (D, bf16, False)], [])
    b1, b2 = row(b_mlp1), row(b_mlp2)
    g_w1 = _xchg_wait("ag_w1_wait", ag_w1, True, u)
    hh, act = _mm("mlp1", u, g_w1, "nn", (f32, bf16), extras=[("row", b1)], epilogue=lambda r, b: (r, jnp.square(jnp.maximum(r + b, 0.0))))
    w2_full = _xchg_wait("ag_w2_wait", ag_w2, True, act).reshape(DFF, D)
    y2 = _mm("mlp2", act, w2_full, "nn", f32)

    def ln2_loss(j, x1t, y2t, tg, bb2, gm, g, b):
        y2b = y2t + bb2
        xh, rstd = _ln_stats(ALPHA * x1t + gm * y2b)
        err = xh * g + b - tg
        dx2 = err * (1.0 / D)
        dr2 = _ln_bwd(dx2, xh, rstd, g)
        dy2 = dr2 * gm
        lsum = jnp.broadcast_to(jnp.sum(err * err).reshape(1, 1), (1, LANES))
        return dr2, dy2, _csum(dx2 * xh), _csum(dx2), _csum(dr2 * y2b), _csum(dy2), lsum

    dr2, dy2, d_l2g, d_l2b, d_gm, d_b2, lsum = _rowcall(
        "ln2_loss", ln2_loss, S, trd, [(x1, D, 0, 0, False), (y2, D, 0, 0, False), (tgt, D, 0, 0, False)],
        [vec(b2), vec(g_m), vec(l2g), vec(l2b)], [(D, f32, False), (D, bf16, False)], [(1, D, False)] * 4 + [(1, LANES, False)])
    loss = lax.psum(0.5 * lsum[0, 0] / D, ("x", "y", "c"))

    dw2_p = _mm("mlp2_dw", act, dy2, "tn", bf16)
    a2a_w2 = _xchg_start("a2a_w2_start", dw2_p.reshape(NDEV, DFF // NDEV, D), False, dw2_p)
    dhh = _mm("mlp2_dx", dy2, w2_full, "nt", bf16, extras=[("tile", hh), ("row", _behind(a2a_w2[4], b1))],
              epilogue=lambda r, h, b: (r * (2.0 * jnp.maximum(h + b, 0.0)),))
    (d_b1,) = _rowcall("b1_grad", lambda j, a: (_csum(a.astype(f32)),), S, _row_tile(S, DFF // 8), [(dhh, DFF, 0, 0, False)], [], [], [(1, DFF, False)])
    dw1_p = _mm("mlp1_dw", u, dhh, "tn", bf16, out_split=NDEV)
    a2a_w1 = _xchg_start("a2a_w1_start", dw1_p, False, dw1_p)
    du = _mm("mlp1_dx", dhh, g_w1, "nt", f32)

    def ln1_bwd(j, dr2t, dut, xnt, yt, ga, g, b, sc):
        xh, rstd = _ln_stats(ALPHA * xnt + ga * yt)
        x1t = xh * g + b
        dx1 = ALPHA * dr2t + dut * (1.0 + sc)
        dr1 = _ln_bwd(dx1, xh, rstd, g)
        return dr1, dr1 * ga, _csum(dut * x1t), _csum(dut), _csum(dx1 * xh), _csum(dx1), _csum(dr1 * yt)

    dr1, dy, d_scm, d_shm, d_l1g, d_l1b, d_ga = _rowcall(
        "ln1_bwd", ln1_bwd, S, trd, [(dr2, D, 0, 0, False), (du, D, 0, 0, False), (xn, D, 0, 0, False), (y, D, 0, 0, False)],
        [vec(g_a), vec(l1g), vec(l1b), vec(_behind(a2a_w1[4], sc_m))], [(D, f32, False), (D, bf16, False)], [(1, D, False)] * 5)
    dwout_p = _mm("out_proj_dw", mix, dy, "tn", bf16)
    a2a_wout = _xchg_start("a2a_wout_start", dwout_p.reshape(NDEV, (DN + CONF) // NDEV, D), False, dwout_p)
    dmix = _mm("out_proj_dx", dy, wout_full, "nt", f32)

    def conf_ln_b(j, dm, yc, b, g, bb):
        xh, rstd = _ln_stats(yc + b)
        dln = dm * _dsilu(xh * g + bb)
        dyc = _ln_bwd(dln, xh, rstd, g)
        return dyc, _csum(dln * xh), _csum(dln), _csum(dyc)

    dyc, d_clg, d_clb, d_bdw = _rowcall("conf_ln_bwd", conf_ln_b, S, trc, [(dmix, CONF, 0, DN // CONF, False), (yconv, CONF, 0, 0, False)],
                                        [vec(_behind(a2a_wout[4], bdw)), vec(clg), vec(clb)], [(CONF, f32, False)], [(1, CONF, False)] * 3)
    dval, dgate, dw31 = _conf_conv_bwd(h_all, w31, dyc, S, CONF, VB, K31)

    do, dz, d_dng = _gate_norm_bwd(dmix, o5, h_all, dng, NH, S, ZB, trh)
    dqkv, dgd = _gdn_bwd(qkv, gd, states, solved, do, NH, S, T)
    dh_qkv, dw7 = _qkv_conv_bwd(h_all, w7, dqkv.reshape(2, 3 * NH, R, HD), R, S, NH, K7)
    dgate_cols = jnp.concatenate([dgd[0][:, :2 * NH], dgd[1][:, :2 * NH], jnp.zeros((R, LANES - 4 * NH), f32)], axis=1)

    def gates_b(j, hg, dgt, isa, na, dt):
        sg = _sigmoid(hg)
        sp = _softplus(hg + dt)
        dpre = jnp.where(isa > 0.5, dgt * na * _sigmoid(hg + dt), dgt * sg * (1.0 - sg))
        return dpre, _csum(jnp.where(isa > 0.5, dgt * na * sp, 0.0)), _csum(jnp.where(isa > 0.5, dpre, 0.0))

    dh_g, d_alog, d_dt = _rowcall("gates_bwd", gates_b, R, trg, [(h_all, LANES, 0, GB, False), (dgate_cols, LANES, 0, 0, False)],
                                  [vec(is_a), vec(neg_a), vec(dt_v)], [(LANES, bf16, False)], [(1, LANES, False)] * 2)

    zpad = lambda a: jnp.concatenate([a, jnp.zeros((T, a.shape[1]), bf16)], axis=0)
    dh_all = jnp.concatenate([dh_qkv, zpad(dz), zpad(dval), zpad(dgate), dh_g, jnp.zeros((R, GPAD - LANES), bf16)], axis=1)
    dxm_all = _mm("in_proj_dx", dh_all, w_cat, "nt", f32, tm=1088, tk=2560)

    def ln_in_bwd_lat(j, xt, dr1t, dxm, g, b, sc):
        xh, rstd = _ln_stats(xt)
        xnt = xh * g + b
        dxn = ALPHA * dr1t + dxm * (1.0 + sc)
        return _ln_bwd(dxn, xh, rstd, g), _csum(dxm * xnt), _csum(dxm), _csum(dxn * xh), _csum(dxn)

    def ln_in_bwd_ctx(j, xt, dxm, g, b, sc):
        xh, rstd = _ln_stats(xt)
        xnt = xh * g + b
        dxn = dxm * (1.0 + sc)
        return _csum(dxm * xnt), _csum(dxm), _csum(dxn * xh), _csum(dxn)

    grad_x, d_sca, d_sha, d_g0a, d_b0a = _rowcall(
        "ln_in_bwd_lat", ln_in_bwd_lat, S, trd, [(x2d, D, 0, 0, False), (dr1, D, 0, 0, False), (dxm_all, D, 0, 0, False)],
        [vec(g0), vec(b0), vec(sc_a)], [(D, f32, False)], [(1, D, False)] * 4)
    d_csca, d_csha, d_g0b, d_b0b = _rowcall(
        "ln_in_bwd_ctx", ln_in_bwd_ctx, T, tr, [(ctx2d, D, 0, 0, False), (dxm_all, D, S // tr, 0, False)],
        [vec(g0), vec(b0), vec(csc_a)], [], [(1, D, False)] * 4)

    zD = jnp.zeros((1, D), f32)
    dmod_me = jnp.concatenate([d_sha, d_sca, d_ga, d_shm, d_scm, d_gm], axis=1)
    dmodc_me = jnp.concatenate([d_csha, d_csca, zD, zD, zD, zD], axis=1)
    small_names = ["ln_in_g", "ln_in_b", "a_log_f", "dt_bias_f", "a_log_b", "dt_bias_b", "dn_norm_g", "conf_dw_b", "conf_ln_g", "conf_ln_b",
                   "ln1_g", "ln1_b", "b_mlp1", "b_mlp2", "ln2_g", "ln2_b"]
    small_parts = [d_g0a + d_g0b, d_b0a + d_b0b, d_alog[:, 0:NH], d_dt[:, 0:NH], d_alog[:, 2 * NH:3 * NH], d_dt[:, 2 * NH:3 * NH], d_dng, d_bdw, d_clg, d_clb,
                   d_l1g, d_l1b, d_b1, d_b2, d_l2g, d_l2b]
    conv_parts = [dw7[:K7], dw31[:K31]]
    packed = _pack(small_parts + conv_parts + [dmodc_me])
    (pk_all, dmod_all) = _all_gather("ag_small_grads", [packed, dmod_me])
    summed = _sum8("sum_small_grads", pk_all.reshape(NDEV, -1))
    parts_sh = [a.shape for a in small_parts + conv_parts + [dmodc_me]]
    un = _unpack(summed, parts_sh)
    gsmall = dict(zip(small_names, un[:len(small_names)]))
    g_w7_full, g_w31_full, dmodc = un[len(small_names):]
    gsmall["w_qkv_conv"] = lax.dynamic_slice_in_dim(g_w7_full, me * w_qkv_conv.shape[2], w_qkv_conv.shape[2], axis=1)
    gsmall["conf_dw_w"] = lax.dynamic_slice_in_dim(g_w31_full, me * conf_dw_w.shape[2], conf_dw_w.shape[2], axis=1)

    dm16 = jnp.concatenate([dmod_all.reshape(NDEV, 6 * D), dmodc, jnp.zeros((7, 6 * D), f32)], axis=0)
    (gbmod,) = _rowcall("bmod_grad", lambda j, a: (_csum(a),), 16, 16, [(dm16, 6 * D, 0, 0, False)], [], [], [(1, 6 * D, False)])
    gsmall["b_mod"] = gbmod
    dm16_mine = lax.dynamic_slice_in_dim(dm16, me * MODC, MODC, axis=1)
    g_wmod = _mm("mod_dw", sc16, dm16_mine, "tn", f32)
    dsc16_part = _mm("mod_dx", dm16_mine, w_mod[0], "nt", f32)
    (dsc_all,) = _all_gather("ag_cctx", [dsc16_part[8:9]])
    dsilu_cctx = _sum8("sum_cctx", dsc_all.reshape(NDEV, D))
    (g_cctx,) = _rowcall("cctx_grad", lambda j, ds, cc: (ds * _dsilu(cc),), 1, 1, [(dsilu_cctx, D, 0, 0, False), (c_ctx.reshape(1, D).astype(f32), D, 0, 0, False)],
                         [], [(D, f32, False)], [])
    gsmall["c_ctx"] = g_cctx

    dwcat_p = _mm("in_proj_dw", xm_all, dh_all, "tn", bf16, tn=1280, tk=4352, after=g_cctx)
    dwin = jnp.concatenate([dwcat_p[:, :4 * DN], dwcat_p[:, 4 * DN + 2 * CONF:4 * DN + 2 * CONF + 4 * NH], dwcat_p[:, 4 * DN:4 * DN + 2 * CONF]], axis=1)
    dwin_p = jnp.transpose(dwin.reshape(D, NDEV, INC // NDEV), (1, 0, 2))
    from_sib = _sib_swap("sib_win", dwin_p)
    mine4 = lax.dynamic_index_in_dim(dwin_p.reshape(4, 2, D, INC // NDEV), lax.axis_index("c"), 1, keepdims=False)
    (chip_sum,) = _rowcall("add_win", lambda j, a, b: (a.astype(f32) + b.astype(f32),), 4 * D, _row_tile(4 * D, INC // NDEV),
                           [(mine4.reshape(4 * D, INC // NDEV), INC // NDEV, 0, 0, False), (from_sib.reshape(4 * D, INC // NDEV), INC // NDEV, 0, 0, False)],
                           [], [(INC // NDEV, bf16, False)], [])
    a2a_win = _xchg_start("a2a_win_start", chip_sum.reshape(4, D, INC // NDEV), False, chip_sum, SAME_CORE)

    grads, deltas, new_m, new_v = {}, {}, {}, {}
    res = _adam("adam_w_mod", w_mod[0], m_w_mod[0], v_w_mod[0], g_wmod, 0, after=a2a_win[4])
    grads["w_mod"], deltas["w_mod"], new_m["w_mod"], new_v["w_mod"] = [a.reshape(w_mod.shape) for a in res]
    after = res[1]
    for nm, started in (("w_mlp2", a2a_w2), ("w_mlp1", a2a_w1), ("w_out", a2a_wout), ("w_in", a2a_win)):
        w3 = weights[nm]
        rels = SAME_CORE if nm == "w_in" else ALL_PEERS
        g = _xchg_wait("a2a_" + nm + "_wait", started, False, after, rels)
        res = _adam("adam_" + nm, w3[0], mom1[nm][0], mom2[nm][0], g, len(rels) + 1)
        grads[nm], deltas[nm], new_m[nm], new_v[nm] = [a.reshape(w3.shape) for a in res]
        after = res[1]
    snames = [n for n in names if n not in grads]
    res = _adam("adam_small", _pack([weights[n] for n in snames]), _pack([mom1[n] for n in snames]), _pack([mom2[n] for n in snames]),
                _pack([gsmall[n] for n in snames]), 0)
    shapes = [weights[n].shape for n in snames]
    for dst, packed_out in zip((grads, deltas, new_m, new_v), res):
        for n, a in zip(snames, _unpack(packed_out, shapes)):
            dst[n] = a

    return (loss, grad_x.reshape(x.shape), *[grads[n] for n in names], *[deltas[n] for n in names],
            *[new_m[n] for n in names], *[new_v[n] for n in names])
```

```python
import functools

import jax
import jax.numpy as jnp
from jax import lax
from jax.experimental import pallas as pl
from jax.experimental.pallas import tpu as pltpu

f32 = jnp.float32
bf16 = jnp.bfloat16
MESH = pl.DeviceIdType.MESH

NDEV = 8
HD = 128
CH = 64
GW = 64
LANES = 128
GPAD = 512
LN_EPS = 1e-5
ALPHA = 2.0 ** 0.25
ADAM_LR, ADAM_B1, ADAM_B2, ADAM_EPS, ADAM_WD, ADAM_STEP = 0.001, 0.9, 0.999, 1e-08, 0.01, 10
VMEM_LIMIT = 56 * 1024 * 1024
ROW_BLOCK_BYTES = 1 << 21


def _cparams(sem):
    return pltpu.CompilerParams(dimension_semantics=sem, vmem_limit_bytes=VMEM_LIMIT)


def _tile(dim, pref, align):
    t = min(pref, dim)
    t -= t % align
    while t >= align:
        if dim % t == 0:
            return t
        t -= align
    return dim


def _row_tile(nrows, width, cap=256):
    t = max(16, min(cap, ROW_BLOCK_BYTES // (4 * width)))
    t = 1 << (t.bit_length() - 1)
    while nrows % t:
        t //= 2
    return max(t, 1)


def _sigmoid(x):
    return 1.0 / (1.0 + jnp.exp(-x))


def _silu(x):
    return x * _sigmoid(x)


def _dsilu(x):
    s = _sigmoid(x)
    return s * (1.0 + x * (1.0 - s))


def _softplus(x):
    return jnp.maximum(x, 0.0) + jnp.log(1.0 + jnp.exp(-jnp.abs(x)))


def _ln_stats(r):
    mu = jnp.mean(r, axis=-1, keepdims=True)
    xc = r - mu
    rstd = lax.rsqrt(jnp.mean(xc * xc, axis=-1, keepdims=True) + LN_EPS)
    return xc * rstd, rstd


def _ln_bwd(dy, xhat, rstd, g):
    dxh = dy * g
    return rstd * (dxh - jnp.mean(dxh, axis=-1, keepdims=True) - xhat * jnp.mean(dxh * xhat, axis=-1, keepdims=True))


def _csum(a):
    return jnp.sum(a, axis=0, keepdims=True)


def _mm(name, a, b, mode, out_dtype, out_split=1, tm=1024, tn=1024, tk=4096, extras=(), epilogue=None, after=None):
    if mode == "tn":
        K, M = a.shape
    else:
        M, K = a.shape
    bp = b.shape[0] if b.ndim == 3 else 1
    brows, bcols = b.shape[-2], b.shape[-1] * bp
    N = brows if mode == "nt" else bcols
    assert K == (bcols if mode == "nt" else brows), (name, a.shape, b.shape)
    tm = _tile(M, tm, LANES if mode == "tn" else 16)
    nsplit = max(out_split, bp if mode != "nt" else 1)
    tn = _tile(N // nsplit, tn, LANES)
    kgroup = 1
    if mode == "nt" and bp > 1 and tk >= 2 * (K // bp):
        kgroup = min(tk // (K // bp), bp)
        tk = kgroup * (K // bp)
    else:
        tk = _tile(K // (bp if mode == "nt" else 1), tk, LANES)
    nk = K // tk
    dims = {"nn": (((1,), (0,)), ((), ())), "nt": (((1,), (1,)), ((), ())), "tn": (((0,), (0,)), ((), ()))}[mode]
    out_dtypes = out_dtype if isinstance(out_dtype, tuple) else (out_dtype,)

    a_spec = pl.BlockSpec((tk, tm), lambda i, j, k: (k, i)) if mode == "tn" else pl.BlockSpec((tm, tk), lambda i, j, k: (i, k))
    if b.ndim == 3:
        if mode == "nt" and kgroup > 1:
            b_spec = pl.BlockSpec((kgroup, tn, K // bp), lambda i, j, k: (k, j, 0))
        elif mode == "nt":
            per = (K // bp) // tk
            b_spec = pl.BlockSpec((None, tn, tk), lambda i, j, k: (k // per, j, k % per))
        else:
            per = (N // bp) // tn
            b_spec = pl.BlockSpec((None, tk, tn), lambda i, j, k: (j // per, k, j % per))
    elif mode == "nt":
        b_spec = pl.BlockSpec((tn, tk), lambda i, j, k: (j, k))
    else:
        b_spec = pl.BlockSpec((tk, tn), lambda i, j, k: (k, j))
    if out_split > 1:
        pero = (N // out_split) // tn
        o_spec = pl.BlockSpec((None, tm, tn), lambda i, j, k: (j // pero, i, j % pero))
        o_shapes = [jax.ShapeDtypeStruct((out_split, M, N // out_split), dt) for dt in out_dtypes]
    else:
        o_spec = pl.BlockSpec((tm, tn), lambda i, j, k: (i, j))
        o_shapes = [jax.ShapeDtypeStruct((M, N), dt) for dt in out_dtypes]
    in_specs = [a_spec, b_spec]
    args = [a, b]
    for kind, arr in extras:
        in_specs.append(pl.BlockSpec((1, tn), lambda i, j, k: (0, j)) if kind == "row" else pl.BlockSpec((tm, tn), lambda i, j, k: (i, j)))
        args.append(arr)
    n_in, n_out = len(args), len(out_dtypes)
    if after is not None:
        in_specs.append(pl.BlockSpec(memory_space=pl.ANY))
        args.append(after)
    n_all = len(args)

    def finish(refs, r):
        outs = (r,) if epilogue is None else epilogue(r, *[e[...] for e in refs[2:n_in]])
        for o_ref, val in zip(refs[n_all:n_all + n_out], outs):
            o_ref[...] = val.astype(o_ref.dtype)

    def body(*refs):
        if kgroup > 1:
            kp = K // bp
            part = lax.dot_general(refs[0][:, 0:kp].astype(bf16), refs[1][0].astype(bf16), dims, preferred_element_type=f32)
            for gi in range(1, kgroup):
                part = part + lax.dot_general(refs[0][:, gi * kp:(gi + 1) * kp].astype(bf16), refs[1][gi].astype(bf16), dims, preferred_element_type=f32)
        else:
            part = lax.dot_general(refs[0][...].astype(bf16), refs[1][...].astype(bf16), dims, preferred_element_type=f32)
        if nk == 1:
            finish(refs, part)
            return
        acc = refs[-1]
        k = pl.program_id(2)

        @pl.when(k == 0)
        def _():
            acc[...] = part

        @pl.when(jnp.logical_and(k > 0, k < nk - 1))
        def _():
            acc[...] += part

        @pl.when(k == nk - 1)
        def _():
            finish(refs, acc[...] + part)

    res = pl.pallas_call(
        body, name=name, grid=(M // tm, N // tn, nk), in_specs=in_specs, out_specs=[o_spec] * n_out, out_shape=o_shapes,
        scratch_shapes=[pltpu.VMEM((tm, tn), f32)] if nk > 1 else [], compiler_params=_cparams(("parallel", "parallel", "arbitrary")),
    )(*args)
    return res[0] if n_out == 1 else res


def _rowcall(name, fn, nrows, tr, rows_in, vecs_in, rows_out, accs_out, ncol=1):
    nrt = nrows // tr
    in_specs, args = [], []
    for arr, w, ro, co, pc in rows_in:
        in_specs.append(pl.BlockSpec((tr, w), functools.partial(lambda j, i, ro, co, pc: (i + ro, co + (j if pc else 0)), ro=ro, co=co, pc=pc)))
        args.append(arr)
    for arr, w, co, pc in vecs_in:
        in_specs.append(pl.BlockSpec((arr.shape[0], w), functools.partial(lambda j, i, co, pc: (0, co + (j if pc else 0)), co=co, pc=pc)))
        args.append(arr)
    out_specs, out_shape = [], []
    for w, dt, pc in rows_out:
        out_specs.append(pl.BlockSpec((tr, w), functools.partial(lambda j, i, pc: (i, j if pc else 0), pc=pc)))
        out_shape.append(jax.ShapeDtypeStruct((nrows, w * (ncol if pc else 1)), dt))
    for k, w, pc in accs_out:
        out_specs.append(pl.BlockSpec((k, w), functools.partial(lambda j, i, pc: (0, j if pc else 0), pc=pc)))
        out_shape.append(jax.ShapeDtypeStruct((k, w * (ncol if pc else 1)), f32))
    n_in, n_ro = len(args), len(rows_out)

    def body(*refs):
        j, i = pl.program_id(0), pl.program_id(1)
        outs = fn(j, *[r[...] for r in refs[:n_in]])
        for r, val in zip(refs[n_in:n_in + n_ro], outs[:n_ro]):
            r[...] = val.astype(r.dtype)
        for (k, w, pc), r, val in zip(accs_out, refs[n_in + n_ro:], outs[n_ro:]):
            first = (i == 0) if pc else jnp.logical_and(i == 0, j == 0)

            @pl.when(first)
            def _(r=r):
                r[...] = jnp.zeros_like(r)

            r[...] += val

    res = pl.pallas_call(
        body, name=name, grid=(ncol, nrt), in_specs=in_specs, out_specs=out_specs, out_shape=out_shape,
        compiler_params=_cparams(("arbitrary", "arbitrary")),
    )(*args)
    return res


def _tap_valid(mode, t, d, nrows, nlat):
    if mode == "seg":
        tp = t + d
        return (tp >= 0) & (tp < nrows) & ((t < nlat) == (tp < nlat))
    if mode == "row":
        p = (t & (GW - 1)) + d
        return (p >= 0) & (p < GW)
    tp = t + d * GW
    return (tp >= 0) & (tp < nrows)


def _conv(x, w_ref, ktaps, mode, nlat, flip=False):
    nrows = x.shape[0]
    stride = GW if mode == "col" else 1
    t = lax.broadcasted_iota(jnp.int32, (nrows, 1), 0)
    acc = jnp.zeros_like(x)
    for j in range(ktaps):
        d = j - ktaps // 2
        jj = ktaps - 1 - j if flip else j
        wj = w_ref[jj:jj + 1, :]
        if d == 0:
            acc = acc + x * wj
        else:
            xs = pltpu.roll(x, (-d * stride) % nrows, 0)
            acc = acc + jnp.where(_tap_valid(mode, t, d, nrows, nlat), xs * wj, 0.0)
    return acc


def _conv_wgrad(dy, x, dw_ref, ktaps, mode, nlat):
    nrows = x.shape[0]
    stride = GW if mode == "col" else 1
    t = lax.broadcasted_iota(jnp.int32, (nrows, 1), 0)
    dw_ref[...] = jnp.zeros_like(dw_ref)
    for j in range(ktaps):
        d = j - ktaps // 2
        if d == 0:
            prod = x * dy
        else:
            xs = pltpu.roll(x, (-d * stride) % nrows, 0)
            prod = jnp.where(_tap_valid(mode, t, d, nrows, nlat), xs * dy, 0.0)
        dw_ref[j:j + 1, :] = _csum(prod)


def _qkv_post(j, pre, nh):
    a = _silu(pre)
    inv = lax.rsqrt(jnp.sum(a * a, axis=-1, keepdims=True) + 1e-6)
    scale = jnp.where(j < nh, HD ** -0.5, 1.0).astype(f32)
    nrm = jnp.where(j < 2 * nh, inv, 1.0) * scale
    return a, inv, nrm


def _qkv_conv_fwd(h_all, w7, nrows, nlat, nh, ktaps):
    ntile = 3 * nh

    def body(h_ref, w_ref, o_ref):
        j = pl.program_id(0)
        pre = _conv(h_ref[...], w_ref, ktaps, "seg", nlat)
        a, _, nrm = _qkv_post(j, pre, nh)
        o_ref[...] = a * nrm

    return pl.pallas_call(
        body, name="qkv_conv_fwd", grid=(ntile,),
        in_specs=[pl.BlockSpec((nrows, HD), lambda j: (0, j)), pl.BlockSpec((w7.shape[0], HD), lambda j: (0, j))],
        out_specs=pl.BlockSpec((None, nrows, HD), lambda j: (j, 0, 0)),
        out_shape=jax.ShapeDtypeStruct((ntile, nrows, HD), f32), compiler_params=_cparams(("parallel",)),
    )(h_all, w7)


def _qkv_conv_bwd(h_all, w7, dqkv, nrows, nlat, nh, ktaps):
    ntile = 3 * nh

    def body(h_ref, w_ref, d0_ref, d1_ref, dh_ref, dw_ref):
        j = pl.program_id(0)
        hx = h_ref[...]
        pre = _conv(hx, w_ref, ktaps, "seg", nlat)
        a, inv, nrm = _qkv_post(j, pre, nh)
        dn = (d0_ref[...] + d1_ref[...]) * jnp.where(j < nh, HD ** -0.5, 1.0).astype(f32)
        n = a * inv
        da_norm = inv * (dn - n * jnp.sum(dn * n, axis=-1, keepdims=True))
        da = jnp.where(j < 2 * nh, da_norm, dn)
        dpre = da * _dsilu(pre)
        dh_ref[...] = _conv(dpre, w_ref, ktaps, "seg", nlat, flip=True).astype(dh_ref.dtype)
        _conv_wgrad(dpre, hx, dw_ref, ktaps, "seg", nlat)

    blk = pl.BlockSpec((nrows, HD), lambda j: (0, j))
    wblk = pl.BlockSpec((w7.shape[0], HD), lambda j: (0, j))
    dblk = lambda d: pl.BlockSpec((None, None, nrows, HD), lambda j: (d, j, 0, 0))
    return pl.pallas_call(
        body, name="qkv_conv_bwd", grid=(ntile,), in_specs=[blk, wblk, dblk(0), dblk(1)], out_specs=[blk, wblk],
        out_shape=[jax.ShapeDtypeStruct((nrows, ntile * HD), bf16), jax.ShapeDtypeStruct((w7.shape[0], ntile * HD), f32)],
        compiler_params=_cparams(("parallel",)),
    )(h_all, w7, dqkv, dqkv)


def _conf_conv_fwd(h_all, w31, nlat, conf, val_blk, ktaps):
    nt = conf // LANES
    nhalf = nt // 2

    def body(v_ref, g_ref, w_ref, o_ref):
        j = pl.program_id(0)
        glu = v_ref[...] * _sigmoid(g_ref[...])

        @pl.when(j < nhalf)
        def _():
            o_ref[...] = _conv(glu, w_ref, ktaps, "row", nlat)

        @pl.when(j >= nhalf)
        def _():
            o_ref[...] = _conv(glu, w_ref, ktaps, "col", nlat)

    return pl.pallas_call(
        body, name="conf_conv_fwd", grid=(nt,),
        in_specs=[pl.BlockSpec((nlat, LANES), lambda j: (0, val_blk + j)), pl.BlockSpec((nlat, LANES), lambda j: (0, val_blk + nt + j)),
                  pl.BlockSpec((w31.shape[0], LANES), lambda j: (0, j))],
        out_specs=pl.BlockSpec((nlat, LANES), lambda j: (0, j)),
        out_shape=jax.ShapeDtypeStruct((nlat, conf), f32), compiler_params=_cparams(("parallel",)),
    )(h_all, h_all, w31)


def _conf_conv_bwd(h_all, w31, dyc, nlat, conf, val_blk, ktaps):
    nt = conf // LANES
    nhalf = nt // 2

    def body(v_ref, g_ref, w_ref, dy_ref, dv_ref, dg_ref, dw_ref):
        j = pl.program_id(0)
        val, sg = v_ref[...], _sigmoid(g_ref[...])
        glu = val * sg
        dy = dy_ref[...]

        def run(mode):
            dglu = _conv(dy, w_ref, ktaps, mode, nlat, flip=True)
            dv_ref[...] = (dglu * sg).astype(dv_ref.dtype)
            dg_ref[...] = (dglu * val * sg * (1.0 - sg)).astype(dg_ref.dtype)
            _conv_wgrad(dy, glu, dw_ref, ktaps, mode, nlat)

        @pl.when(j < nhalf)
        def _():
            run("row")

        @pl.when(j >= nhalf)
        def _():
            run("col")

    blk = pl.BlockSpec((nlat, LANES), lambda j: (0, j))
    wblk = pl.BlockSpec((w31.shape[0], LANES), lambda j: (0, j))
    return pl.pallas_call(
        body, name="conf_conv_bwd", grid=(nt,),
        in_specs=[pl.BlockSpec((nlat, LANES), lambda j: (0, val_blk + j)), pl.BlockSpec((nlat, LANES), lambda j: (0, val_blk + nt + j)), wblk, blk],
        out_specs=[blk, blk, wblk],
        out_shape=[jax.ShapeDtypeStruct((nlat, conf), bf16), jax.ShapeDtypeStruct((nlat, conf), bf16),
                   jax.ShapeDtypeStruct((w31.shape[0], conf), f32)],
        compiler_params=_cparams(("parallel",)),
    )(h_all, h_all, w31, dyc)


def _bd(eq, a, b):
    return jnp.einsum(eq, a.astype(bf16), b.astype(bf16), preferred_element_type=f32)


def _split(x, pieces):
    out = []
    for _ in range(pieces - 1):
        p = x.astype(bf16)
        out.append(p)
        x = x - p.astype(f32)
    return out + [x.astype(bf16)]


def _h3(eq, a, b):
    (ah, al), (bh, bl) = _split(a, 2), _split(b, 2)
    d = lambda p, q: jnp.einsum(eq, p, q, preferred_element_type=f32)
    return d(ah, bh) + (d(ah, bl) + d(al, bh))


def _cumsum_rows(x, rev):
    row = lax.broadcasted_iota(jnp.int32, x.shape, 0)
    down, up, s = x, x, 1
    while s < CH:
        down = down + jnp.where(row >= s, pltpu.roll(down, s, 0), 0.0)
        up = up + jnp.where(row < CH - s, pltpu.roll(up, CH - s, 0), 0.0)
        s *= 2
    return jnp.where(rev, up, down)


def _gdn_chunk(q, k, v, gt, nh, rev, solved=None):
    beta = jnp.stack([jnp.broadcast_to(gt[:, nh + h:nh + h + 1], (CH, LANES)) for h in range(nh)])
    ii = lax.broadcasted_iota(jnp.int32, (CH, CH), 0)
    jj = lax.broadcasted_iota(jnp.int32, (CH, CH), 1)
    si, sj = jnp.where(rev, jj, ii), jnp.where(rev, ii, jj)
    tril, stril = (si >= sj)[None], (si > sj)[None]
    gct = _cumsum_rows(gt, rev)
    gct_t = gct.T
    gc = jnp.stack([jnp.broadcast_to(gct[:, h:h + 1], (CH, LANES)) for h in range(nh)])
    gc_row = jnp.stack([jnp.broadcast_to(gct_t[h:h + 1, :], (CH, CH)) for h in range(nh)])
    diff = gc[:, :, :CH] - gc_row
    gam = jnp.where(tril, jnp.exp(jnp.where(tril, diff, 0.0)), 0.0)
    e = jnp.exp(gc)
    gl = jnp.where(rev, gc[:, 0:1, :], gc[:, CH - 1:CH, :])
    el = jnp.exp(gl)
    r = jnp.exp(gl - gc)
    kb, vb = k * beta, v * beta
    kbe = kb * e
    amat = jnp.where(stril, _bd("hik,hjk->hij", kb, k) * gam, 0.0)
    if solved is None:
        xp = -amat
        tinv = (ii == jj).astype(f32)[None] + xp
        for _ in range(5):
            xp = _h3("hij,hjk->hik", xp, xp)
            tinv = tinv + _h3("hij,hjk->hik", tinv, xp)
        u = _h3("hij,hjv->hiv", tinv, vb)
        w = _h3("hij,hjk->hik", tinv, kbe)
    else:
        tinv, u, w = solved
    pmat = jnp.where(tril, _bd("hik,hjk->hij", q, k) * gam, 0.0)
    return dict(beta=beta, gam=gam, e=e, el=el, r=r, kb=kb, vb=vb, kbe=kbe, amat=amat, tinv=tinv, u=u, w=w, pmat=pmat,
                qd=q * e, kd=k * r, tril=tril, stril=stril)


def _scan_row_chunk(d, n, ns, nt):
    fwd = jnp.where(n < nt, ns + n, n - nt)
    return jnp.where(d == 0, fwd, ns + nt - 1 - n)


def _gdn_fwd(qkv, gd, nh, nlat, nctx):
    nrows = nlat + nctx
    ns, nt = nlat // CH, nctx // CH
    nch = ns + nt

    def body(qkv_ref, g_ref, o_ref, st_ref, ti_ref, u_ref, w_ref, state):
        d, n = pl.program_id(0), pl.program_id(1)

        @pl.when(n == 0)
        def _():
            state[...] = jnp.zeros_like(state)

        c = _gdn_chunk(qkv_ref[0], qkv_ref[1], qkv_ref[2], g_ref[...], nh, d == 1)
        s = state[...]
        vn = c["u"] - _bd("hck,hkv->hcv", c["w"], s)
        o_ref[...] = _bd("hck,hkv->hcv", c["qd"], s) + _bd("hcd,hdv->hcv", c["pmat"], vn)
        st_ref[...] = s
        ti_ref[...], u_ref[...], w_ref[...] = c["tinv"], c["u"], c["w"]
        state[...] = s * c["el"] + _bd("hck,hcv->hkv", c["kd"], vn)

    hblk = pl.BlockSpec((None, nh, CH, HD), lambda d, n: (d, 0, _scan_row_chunk(d, n, ns, nt), 0))
    hshape = jax.ShapeDtypeStruct((2, nh, nrows, HD), f32)
    return pl.pallas_call(
        body, name="gdn_fwd", grid=(2, nch),
        in_specs=[pl.BlockSpec((3, nh, CH, HD), lambda d, n: (0, 0, _scan_row_chunk(d, n, ns, nt), 0)),
                  pl.BlockSpec((None, CH, LANES), lambda d, n: (d, _scan_row_chunk(d, n, ns, nt), 0))],
        out_specs=[hblk, pl.BlockSpec((None, None, nh, HD, HD), lambda d, n: (d, n, 0, 0, 0)),
                   pl.BlockSpec((None, None, nh, CH, CH), lambda d, n: (d, n, 0, 0, 0)), hblk, hblk],
        out_shape=[hshape, jax.ShapeDtypeStruct((2, nch, nh, HD, HD), f32), jax.ShapeDtypeStruct((2, nch, nh, CH, CH), f32), hshape, hshape],
        scratch_shapes=[pltpu.VMEM((nh, HD, HD), f32)], compiler_params=_cparams(("arbitrary", "arbitrary")),
    )(qkv, gd)


def _gdn_bwd(qkv, gd, states, solved, do, nh, nlat, nctx):
    nrows = nlat + nctx
    ns, nt = nlat // CH, nctx // CH
    nch = ns + nt

    def body(qkv_ref, g_ref, st_ref, ti_ref, u_ref, w_ref, do_ref, dqkv_ref, dgd_ref, dstate):
        d, step = pl.program_id(0), pl.program_id(1)

        @pl.when(step == 0)
        def _():
            dstate[...] = jnp.zeros_like(dstate)

        q, k, v = qkv_ref[0], qkv_ref[1], qkv_ref[2]
        c = _gdn_chunk(q, k, v, g_ref[...], nh, d == 1, solved=(ti_ref[...], u_ref[...], w_ref[...]))
        in_ctx = (nch - 1 - step) < nt
        s, dsp = st_ref[...], dstate[...]
        dout = jnp.where(in_ctx, 0.0, do_ref[...])
        beta, gam, e, el, r = c["beta"], c["gam"], c["e"], c["el"], c["r"]
        tinv, u, w, pmat, amat = c["tinv"], c["u"], c["w"], c["pmat"], c["amat"]
        vn = u - _bd("hck,hkv->hcv", w, s)
        dvn = _bd("hdc,hdv->hcv", pmat, dout) + _bd("hck,hkv->hcv", c["kd"], dsp)
        dp = jnp.where(c["tril"], _bd("hcv,hdv->hcd", dout, vn), 0.0)
        dqd = _bd("hcv,hkv->hck", dout, s)
        dkd = _bd("hcv,hkv->hck", vn, dsp)
        dstate[...] = _bd("hck,hcv->hkv", c["qd"], dout) + dsp * el - _bd("hck,hcv->hkv", w, dvn)
        del_ = jnp.sum(jnp.sum(s * dsp, axis=2, keepdims=True), axis=1, keepdims=True)
        dw = -_bd("hcv,hkv->hck", dvn, s)
        dvb = _h3("hji,hjv->hiv", tinv, dvn)
        dkbe = _h3("hji,hjk->hik", tinv, dw)
        z = _bd("hiv,hjv->hij", dvn, u) + _bd("hik,hjk->hij", dw, w)
        da = jnp.where(c["stril"], -_h3("hji,hjl->hil", tinv, z), 0.0)
        dm = da * gam
        dkb = _bd("hij,hjk->hik", dm, k) + dkbe * e
        dn = dp * gam
        dqkv_ref[0] = _bd("hij,hjk->hik", dn, k) + dqd * e
        dqkv_ref[1] = _bd("hji,hjk->hik", dm, c["kb"]) + _bd("hji,hjk->hik", dn, q) + dkd * r + dkb * beta
        dqkv_ref[2] = dvb * beta
        gmat = da * amat + dp * pmat
        rows_minus_cols = jnp.sum(gmat, axis=-1, keepdims=True) - jnp.sum(jnp.swapaxes(gmat, 1, 2), axis=-1, keepdims=True)
        de = jnp.sum(dqd * q + dkbe * c["kb"], axis=-1, keepdims=True)
        drr = jnp.sum(dkd * k, axis=-1, keepdims=True) * r
        dgc = rows_minus_cols + de * e - drr
        dgl = jnp.sum(drr, axis=1, keepdims=True) + del_ * el
        row = lax.broadcasted_iota(jnp.int32, (1, CH, 1), 1)
        total_row = row == jnp.where(d == 1, 0, CH - 1)
        dgc = dgc + jnp.where(total_row, dgl, 0.0)
        db = jnp.sum(dvb * v + dkb * k, axis=-1, keepdims=True)
        lane = lax.broadcasted_iota(jnp.int32, (CH, LANES), 1)
        dgc_cols, db_cols = jnp.zeros((CH, LANES), f32), jnp.zeros((CH, LANES), f32)
        for h in range(nh):
            dgc_cols = dgc_cols + jnp.where(lane == h, dgc[h], 0.0)
            db_cols = db_cols + jnp.where(lane == nh + h, db[h], 0.0)
        dgd_ref[...] = _cumsum_rows(dgc_cols, d == 0) + db_cols

    def rc(d, n):
        return _scan_row_chunk(d, nch - 1 - n, ns, nt)

    gblk = pl.BlockSpec((None, CH, LANES), lambda d, n: (d, rc(d, n), 0))
    return pl.pallas_call(
        body, name="gdn_bwd", grid=(2, nch),
        in_specs=[pl.BlockSpec((3, nh, CH, HD), lambda d, n: (0, 0, rc(d, n), 0)),
                  gblk,
                  pl.BlockSpec((None, None, nh, HD, HD), lambda d, n: (d, nch - 1 - n, 0, 0, 0)),
                  pl.BlockSpec((None, None, nh, CH, CH), lambda d, n: (d, nch - 1 - n, 0, 0, 0)),
                  pl.BlockSpec((None, nh, CH, HD), lambda d, n: (d, 0, rc(d, n), 0)),
                  pl.BlockSpec((None, nh, CH, HD), lambda d, n: (d, 0, rc(d, n), 0)),
                  pl.BlockSpec((nh, CH, HD), lambda d, n: (0, jnp.minimum(rc(d, n), ns - 1), 0))],
        out_specs=[pl.BlockSpec((None, 3, nh, CH, HD), lambda d, n: (d, 0, 0, rc(d, n), 0)), gblk],
        out_shape=[jax.ShapeDtypeStruct((2, 3, nh, nrows, HD), f32), jax.ShapeDtypeStruct((2, nrows, LANES), f32)],
        scratch_shapes=[pltpu.VMEM((nh, HD, HD), f32)], compiler_params=_cparams(("arbitrary", "arbitrary")),
    )(qkv, gd, states, *solved, do)


def _gate_norm_fwd(o5, h_all, dng, nh, nlat, zblk, tr):
    def body(of_ref, ob_ref, z_ref, g_ref, out_ref):
        o = of_ref[...] + ob_ref[...]
        rs = lax.rsqrt(jnp.mean(o * o, axis=-1, keepdims=True) + 1e-6)
        out_ref[...] = ((o * rs * g_ref[...]) * _silu(z_ref[...])).astype(out_ref.dtype)

    return pl.pallas_call(
        body, name="gate_norm_fwd", grid=(nh, nlat // tr),
        in_specs=[pl.BlockSpec((None, None, tr, HD), lambda j, i: (0, j, i, 0)), pl.BlockSpec((None, None, tr, HD), lambda j, i: (1, j, i, 0)),
                  pl.BlockSpec((tr, HD), lambda j, i: (i, zblk + j)), pl.BlockSpec((1, HD), lambda j, i: (0, 0))],
        out_specs=pl.BlockSpec((tr, HD), lambda j, i: (i, j)), out_shape=jax.ShapeDtypeStruct((nlat, nh * HD), bf16),
        compiler_params=_cparams(("parallel", "parallel")),
    )(o5, o5, h_all, dng)


def _gate_norm_bwd(dmix, o5, h_all, dng, nh, nlat, zblk, tr):
    def body(dm_ref, of_ref, ob_ref, z_ref, g_ref, do_ref, dz_ref, dg_ref):
        o = of_ref[...] + ob_ref[...]
        z, g, dm = z_ref[...], g_ref[...], dm_ref[...]
        rs = lax.rsqrt(jnp.mean(o * o, axis=-1, keepdims=True) + 1e-6)
        don = dm * _silu(z)
        dz_ref[...] = (dm * (o * rs * g) * _dsilu(z)).astype(dz_ref.dtype)
        dog = don * g
        do_ref[...] = rs * (dog - o * rs * rs * jnp.mean(dog * o, axis=-1, keepdims=True))

        @pl.when(jnp.logical_and(pl.program_id(0) == 0, pl.program_id(1) == 0))
        def _():
            dg_ref[...] = jnp.zeros_like(dg_ref)

        dg_ref[...] += _csum(don * o * rs)

    return pl.pallas_call(
        body, name="gate_norm_bwd", grid=(nh, nlat // tr),
        in_specs=[pl.BlockSpec((tr, HD), lambda j, i: (i, j)),
                  pl.BlockSpec((None, None, tr, HD), lambda j, i: (0, j, i, 0)), pl.BlockSpec((None, None, tr, HD), lambda j, i: (1, j, i, 0)),
                  pl.BlockSpec((tr, HD), lambda j, i: (i, zblk + j)), pl.BlockSpec((1, HD), lambda j, i: (0, 0))],
        out_specs=[pl.BlockSpec((None, tr, HD), lambda j, i: (j, i, 0)), pl.BlockSpec((tr, HD), lambda j, i: (i, j)),
                   pl.BlockSpec((1, HD), lambda j, i: (0, 0))],
        out_shape=[jax.ShapeDtypeStruct((nh, nlat, HD), f32), jax.ShapeDtypeStruct((nlat, nh * HD), bf16), jax.ShapeDtypeStruct((1, HD), f32)],
        compiler_params=_cparams(("arbitrary", "arbitrary")),
    )(dmix, o5, o5, h_all, dng)


def _mesh_pos():
    return lax.axis_index("x"), lax.axis_index("y"), lax.axis_index("c")


def _lin(p):
    return 4 * p[0] + 2 * p[1] + p[2]


def _all_gather(name, xs, after=None):
    nx = len(xs)
    extra = [] if after is None else [after]

    def body(*refs):
        xr, outr = refs[:nx], refs[nx + len(extra):2 * nx + len(extra)]
        send, recv, loc = refs[2 * nx + len(extra):]
        x, y, c = _mesh_pos()
        me, sib = (x, y, c), (x, y, 1 - c)
        chips = [(1 - x, y), (x, 1 - y), (1 - x, 1 - y)]

        def cp(l, k, block, to, src=None):
            rows = outr[l].at[_lin(block)]
            return pltpu.make_async_remote_copy(src_ref=rows if src is None else src, dst_ref=rows, send_sem=send.at[l, k],
                                                recv_sem=recv.at[l, k], device_id=to, device_id_type=MESH)

        mine = [pltpu.make_async_copy(xr[l], outr[l].at[_lin(me)], loc.at[l]) for l in range(nx)]
        for m in mine:
            m.start()
        first = []
        for l in range(nx):
            first.append(cp(l, 0, me, sib, src=xr[l]))
            first += [cp(l, 1 + j, me, (*chip, c), src=xr[l]) for j, chip in enumerate(chips)]
        for f in first:
            f.start()
        passed = []
        for l in range(nx):
            for j, chip in enumerate(chips):
                cp(l, 1 + j, (*chip, c), me).wait_recv()
                fwd = cp(l, 4 + j, (*chip, c), sib)
                fwd.start()
                passed.append(fwd)
        for l in range(nx):
            cp(l, 0, sib, me).wait_recv()
            for j, chip in enumerate(chips):
                cp(l, 4 + j, (*chip, 1 - c), me).wait_recv()
        for f in first + passed:
            f.wait_send()
        for m in mine:
            m.wait()

    anyspec = pl.BlockSpec(memory_space=pl.ANY)
    return pl.pallas_call(
        body, name=name, in_specs=[anyspec] * (nx + len(extra)), out_specs=[anyspec] * nx,
        out_shape=[jax.ShapeDtypeStruct((NDEV,) + a.shape, a.dtype) for a in xs],
        scratch_shapes=[pltpu.SemaphoreType.DMA((nx, 7)), pltpu.SemaphoreType.DMA((nx, 7)), pltpu.SemaphoreType.DMA((nx,))],
    )(*xs, *extra)


HBM_SPEC = pl.BlockSpec(memory_space=pltpu.HBM)
SEM_SPEC = pl.BlockSpec(memory_space=pltpu.SEMAPHORE)
ANY_SPEC = pl.BlockSpec(memory_space=pl.ANY)
EFFECT = pltpu.SideEffectType.DATAFLOW_SIDE_EFFECTING


def _peer(rel):
    x, y, c = _mesh_pos()
    return (1 - x if rel & 4 else x, 1 - y if rel & 2 else y, 1 - c if rel & 1 else c)


ALL_PEERS = (1, 2, 3, 4, 5, 6, 7)
SAME_CORE = (2, 4, 6)
CORE_AND_SIB = (1, 2, 4, 6)


def _slot(p, rels):
    return 2 * p[0] + p[1] if rels == SAME_CORE else _lin(p)


def _xchg_start(name, x, gather, after, rels=ALL_PEERS):
    me = _slot(_mesh_pos(), rels)
    shape = ((4 if rels == SAME_CORE else NDEV),) + x.shape if gather else x.shape
    own = x if gather else lax.dynamic_index_in_dim(x, me, 0, keepdims=False)
    land = lax.dynamic_update_index_in_dim(lax.empty(shape, x.dtype), own, me, 0)

    def body(x_ref, land_ref, after_ref, send, recv, x_thru, land_thru, token):
        mine = _slot(_mesh_pos(), rels)
        for k, rel in enumerate(rels):
            p = _peer(rel)
            pltpu.make_async_remote_copy(src_ref=x_ref if gather else x_ref.at[_slot(p, rels)], dst_ref=land_ref.at[mine],
                                         send_sem=send.at[k], recv_sem=recv.at[k], device_id=p, device_id_type=MESH).start()
        token[...] = jnp.zeros_like(token)

    return pl.pallas_call(
        body, name=name,
        out_shape=(pltpu.SemaphoreType.DMA((len(rels),)), pltpu.SemaphoreType.DMA((len(rels),)), pltpu.HBM(x.shape, x.dtype),
                   pltpu.HBM(shape, x.dtype), jax.ShapeDtypeStruct((8, LANES), f32)),
        in_specs=(HBM_SPEC, HBM_SPEC, ANY_SPEC), out_specs=(SEM_SPEC, SEM_SPEC, HBM_SPEC, HBM_SPEC, pl.BlockSpec(memory_space=pltpu.VMEM)),
        input_output_aliases={0: 2, 1: 3}, compiler_params=pltpu.CompilerParams(has_side_effects=EFFECT),
    )(pltpu.with_memory_space_constraint(x, pltpu.HBM), pltpu.with_memory_space_constraint(land, pltpu.HBM), after)


def _xchg_wait(name, started, gather, after, rels=ALL_PEERS):
    send, recv, x_thru, land_thru, _ = started

    def body(x_ref, land_ref, send, recv, after_ref, x_dead, got_ref):
        for k, rel in enumerate(rels):
            p = _peer(rel)
            cp = pltpu.make_async_remote_copy(src_ref=x_ref if gather else x_ref.at[_slot(p, rels)], dst_ref=land_ref.at[_slot(p, rels)],
                                              send_sem=send.at[k], recv_sem=recv.at[k], device_id=p, device_id_type=MESH)
            cp.wait_send()
            cp.wait_recv()

    return pl.pallas_call(
        body, name=name, out_shape=(pltpu.HBM(x_thru.shape, x_thru.dtype), pltpu.HBM(land_thru.shape, land_thru.dtype)),
        in_specs=(HBM_SPEC, HBM_SPEC, SEM_SPEC, SEM_SPEC, ANY_SPEC), out_specs=(HBM_SPEC, HBM_SPEC), input_output_aliases={0: 0, 1: 1},
        compiler_params=pltpu.CompilerParams(has_side_effects=EFFECT),
    )(x_thru, land_thru, send, recv, after)[1]


def _sib_forward(name, land):
    def body(land_ref, out_ref, send, recv):
        mx, my, mc = _mesh_pos()
        sib = (mx, my, 1 - mc)

        def cp(k, block):
            return pltpu.make_async_remote_copy(src_ref=out_ref.at[_lin(block)], dst_ref=out_ref.at[_lin(block)], send_sem=send.at[k],
                                                recv_sem=recv.at[k], device_id=sib, device_id_type=MESH)

        chips = [_peer(rel)[:2] for rel in SAME_CORE]
        sends = [cp(k, (*chip, mc)) for k, chip in enumerate(chips)]
        for s in sends:
            s.start()
        for k, chip in enumerate(chips):
            cp(k, (*chip, 1 - mc)).wait_recv()
        for s in sends:
            s.wait_send()

    return pl.pallas_call(
        body, name=name, in_specs=[ANY_SPEC], out_specs=ANY_SPEC, out_shape=jax.ShapeDtypeStruct(land.shape, land.dtype),
        input_output_aliases={0: 0}, scratch_shapes=[pltpu.SemaphoreType.DMA((3,)), pltpu.SemaphoreType.DMA((3,))],
    )(land)


def _sib_swap(name, x):
    def body(x_ref, out_ref, send, recv):
        mx, my, mc = _mesh_pos()
        sib = (mx, my, 1 - mc)
        cps = [pltpu.make_async_remote_copy(src_ref=x_ref.at[2 * q + 1 - mc], dst_ref=out_ref.at[q], send_sem=send.at[q], recv_sem=recv.at[q],
                                            device_id=sib, device_id_type=MESH) for q in range(4)]
        for cp in cps:
            cp.start()
        for cp in cps:
            cp.wait_recv()
        for cp in cps:
            cp.wait_send()

    return pl.pallas_call(
        body, name=name, in_specs=[ANY_SPEC], out_specs=ANY_SPEC, out_shape=jax.ShapeDtypeStruct((4,) + x.shape[1:], x.dtype),
        scratch_shapes=[pltpu.SemaphoreType.DMA((4,)), pltpu.SemaphoreType.DMA((4,))],
    )(x)


def _behind(token, a):
    return a + token[0:1, 0:1].astype(a.dtype)


def _sum8(name, g):
    def body(g_ref, o_ref):
        acc = g_ref[0:1, :]
        for p in range(1, NDEV):
            acc = acc + g_ref[p:p + 1, :]
        o_ref[...] = acc

    return pl.pallas_call(body, name=name, out_shape=jax.ShapeDtypeStruct((1, g.shape[1]), f32),
                          compiler_params=pltpu.CompilerParams(vmem_limit_bytes=VMEM_LIMIT))(g)


def _pack(arrs):
    flat = jnp.concatenate([a.reshape(-1).astype(f32) for a in arrs])
    pad = (-flat.shape[0]) % (8 * LANES)
    return jnp.pad(flat, (0, pad)).reshape(1, -1)


def _unpack(vec, shapes):
    out, off = [], 0
    flat = vec.reshape(-1)
    for s in shapes:
        n = 1
        for d in s:
            n *= d
        out.append(flat[off:off + n].reshape(s))
        off += n
    return out


def _adam(name, w, m, v, g, parts, after=None):
    extra = [] if after is None else [after]
    nrows, width = w.shape
    tr = _row_tile(nrows, width) if nrows >= 16 else nrows
    blk = pl.BlockSpec((tr, width), lambda i: (i, 0))
    gblk = pl.BlockSpec((parts, tr, width), lambda i: (0, i, 0)) if parts else blk
    c1 = 1.0 / (1.0 - ADAM_B1 ** ADAM_STEP)
    c2 = 1.0 / (1.0 - ADAM_B2 ** ADAM_STEP)

    def body(w_ref, m_ref, v_ref, g_ref, *rest):
        go_ref, d_ref, mo_ref, vo_ref = rest[len(extra):]
        if parts:
            gg = g_ref[0].astype(f32)
            for p in range(1, parts):
                gg = gg + g_ref[p].astype(f32)
        else:
            gg = g_ref[...]
        mn = ADAM_B1 * m_ref[...] + (1.0 - ADAM_B1) * gg
        vn = ADAM_B2 * v_ref[...] + (1.0 - ADAM_B2) * (gg * gg)
        go_ref[...] = gg
        mo_ref[...] = mn
        vo_ref[...] = vn
        d_ref[...] = -ADAM_LR * ((mn * c1) / (jnp.sqrt(vn * c2) + ADAM_EPS) + ADAM_WD * w_ref[...])

    sh = jax.ShapeDtypeStruct((nrows, width), f32)
    return pl.pallas_call(body, name=name, grid=(nrows // tr,), in_specs=[blk, blk, blk, gblk] + [pl.BlockSpec(memory_space=pl.ANY)] * len(extra),
                          out_specs=[blk] * 4, out_shape=[sh] * 4, compiler_params=_cparams(("parallel",)))(w, m, v, g, *extra)


def kernel(x, c, ctx, c_ctx, ln_in_g, ln_in_b, w_mod, b_mod, w_in, w_qkv_conv, a_log_f, dt_bias_f, a_log_b, dt_bias_b, dn_norm_g, conf_dw_w, conf_dw_b, conf_ln_g, conf_ln_b, w_out, ln1_g, ln1_b, w_mlp1, b_mlp1, w_mlp2, b_mlp2, ln2_g, ln2_b, loss_target, m_c_ctx, m_ln_in_g, m_ln_in_b, m_w_mod, m_b_mod, m_w_in, m_w_qkv_conv, m_a_log_f, m_dt_bias_f, m_a_log_b, m_dt_bias_b, m_dn_norm_g, m_conf_dw_w, m_conf_dw_b, m_conf_ln_g, m_conf_ln_b, m_w_out, m_ln1_g, m_ln1_b, m_w_mlp1, m_b_mlp1, m_w_mlp2, m_b_mlp2, m_ln2_g, m_ln2_b, v_c_ctx, v_ln_in_g, v_ln_in_b, v_w_mod, v_b_mod, v_w_in, v_w_qkv_conv, v_a_log_f, v_dt_bias_f, v_a_log_b, v_dt_bias_b, v_dn_norm_g, v_conf_dw_w, v_conf_dw_b, v_conf_ln_g, v_conf_ln_b, v_w_out, v_ln1_g, v_ln1_b, v_w_mlp1, v_b_mlp1, v_w_mlp2, v_b_mlp2, v_ln2_g, v_ln2_b):
    weights = dict(c_ctx=c_ctx, ln_in_g=ln_in_g, ln_in_b=ln_in_b, w_mod=w_mod, b_mod=b_mod, w_in=w_in, w_qkv_conv=w_qkv_conv, a_log_f=a_log_f, dt_bias_f=dt_bias_f, a_log_b=a_log_b, dt_bias_b=dt_bias_b, dn_norm_g=dn_norm_g, conf_dw_w=conf_dw_w, conf_dw_b=conf_dw_b, conf_ln_g=conf_ln_g, conf_ln_b=conf_ln_b, w_out=w_out, ln1_g=ln1_g, ln1_b=ln1_b, w_mlp1=w_mlp1, b_mlp1=b_mlp1, w_mlp2=w_mlp2, b_mlp2=b_mlp2, ln2_g=ln2_g, ln2_b=ln2_b)
    mom1 = dict(c_ctx=m_c_ctx, ln_in_g=m_ln_in_g, ln_in_b=m_ln_in_b, w_mod=m_w_mod, b_mod=m_b_mod, w_in=m_w_in, w_qkv_conv=m_w_qkv_conv, a_log_f=m_a_log_f, dt_bias_f=m_dt_bias_f, a_log_b=m_a_log_b, dt_bias_b=m_dt_bias_b, dn_norm_g=m_dn_norm_g, conf_dw_w=m_conf_dw_w, conf_dw_b=m_conf_dw_b, conf_ln_g=m_conf_ln_g, conf_ln_b=m_conf_ln_b, w_out=m_w_out, ln1_g=m_ln1_g, ln1_b=m_ln1_b, w_mlp1=m_w_mlp1, b_mlp1=m_b_mlp1, w_mlp2=m_w_mlp2, b_mlp2=m_b_mlp2, ln2_g=m_ln2_g, ln2_b=m_ln2_b)
    mom2 = dict(c_ctx=v_c_ctx, ln_in_g=v_ln_in_g, ln_in_b=v_ln_in_b, w_mod=v_w_mod, b_mod=v_b_mod, w_in=v_w_in, w_qkv_conv=v_w_qkv_conv, a_log_f=v_a_log_f, dt_bias_f=v_dt_bias_f, a_log_b=v_a_log_b, dt_bias_b=v_dt_bias_b, dn_norm_g=v_dn_norm_g, conf_dw_w=v_conf_dw_w, conf_dw_b=v_conf_dw_b, conf_ln_g=v_conf_ln_g, conf_ln_b=v_conf_ln_b, w_out=v_w_out, ln1_g=v_ln1_g, ln1_b=v_ln1_b, w_mlp1=v_w_mlp1, b_mlp1=v_b_mlp1, w_mlp2=v_w_mlp2, b_mlp2=v_b_mlp2, ln2_g=v_ln2_g, ln2_b=v_ln2_b)
    names = list(weights)

    me = _lin(_mesh_pos())
    S, D = x.shape[1], x.shape[2]
    T = ctx.shape[1]
    R = S + T
    DN = D // 2
    NH = DN // HD
    CONF = D - DN
    K7, K31 = w_qkv_conv.shape[1], conf_dw_w.shape[1]
    DFF = w_mlp1.shape[2] * NDEV
    INC = w_in.shape[2] * NDEV
    CONF_OFF = 4 * DN + 4 * NH
    NC = 4 * DN + 2 * CONF + GPAD
    QB, ZB, VB, GB = 0, 3 * DN // LANES, 4 * DN // LANES, (4 * DN + 2 * CONF) // LANES
    MODC = w_mod.shape[2]
    x2d, ctx2d, tgt = x[0], ctx[0], loss_target[0]
    row = lambda a: a.reshape(1, -1).astype(f32)

    c_all, w7_all, w31_all = _all_gather("ag_small", [c.astype(f32), w_qkv_conv[0], conf_dw_w[0]])
    w7 = jnp.pad(jnp.transpose(w7_all, (1, 0, 2)).reshape(K7, 3 * DN), ((0, (-K7) % 8), (0, 0)))
    w31 = jnp.pad(jnp.transpose(w31_all, (1, 0, 2)).reshape(K31, CONF), ((0, (-K31) % 8), (0, 0)))
    c16 = jnp.concatenate([c_all.reshape(NDEV, D), c_ctx.reshape(1, D).astype(f32), jnp.zeros((7, D), f32)], axis=0)
    (sc16,) = _rowcall("silu_c", lambda j, a: (_silu(a),), 16, 16, [(c16, D, 0, 0, False)], [], [(D, f32, False)], [])
    bmod_mine = lax.dynamic_slice_in_dim(b_mod.astype(f32), me * MODC, MODC, axis=1)
    mod_part = _mm("mod_fwd", sc16, w_mod[0], "nn", f32, extras=[("row", bmod_mine)], epilogue=lambda r, b: (r + b,))
    (mod_g,) = _all_gather("ag_mod", [mod_part])
    mod_all = jnp.transpose(mod_g, (1, 0, 2)).reshape(16, 6 * D)
    mod_me = lax.dynamic_slice_in_dim(mod_all, me, 1, axis=0)
    sh_a, sc_a, g_a, sh_m, sc_m, g_m = [mod_me[:, i * D:(i + 1) * D] for i in range(6)]
    csh_a, csc_a = mod_all[8:9, 0:D], mod_all[8:9, D:2 * D]

    (g_win,) = _all_gather("ag_w_in", [w_in[0].astype(bf16)], after=mod_g)
    ag_wout = _xchg_start("ag_wout_start", w_out[0].astype(bf16), True, g_win)
    ag_w1 = _xchg_start("ag_w1_start", w_mlp1[0].astype(bf16), True, ag_wout[4])
    win_full = jnp.transpose(g_win, (1, 0, 2)).reshape(D, INC)
    w_cat = jnp.concatenate([win_full[:, :4 * DN], win_full[:, CONF_OFF:], win_full[:, 4 * DN:CONF_OFF],
                             jnp.zeros((D, GPAD - 4 * NH), bf16)], axis=1)
    g0, b0 = row(ln_in_g), row(ln_in_b)
    g0_fwd = _behind(ag_w1[4], g0)

    def ln_mod(j, xt, g, b, sh, sc):
        xh, _ = _ln_stats(xt)
        xn = xh * g + b
        return xn, xn * (1.0 + sc) + sh

    tr = _row_tile(T, D)
    vec = lambda a: (a, a.shape[1], 0, False)
    xn, xm = _rowcall("ln_in_lat", ln_mod, S, tr, [(x2d, D, 0, 0, False)], [vec(g0_fwd), vec(b0), vec(sh_a), vec(sc_a)], [(D, f32, False), (D, bf16, False)], [])
    xcn, xcm = _rowcall("ln_in_ctx", ln_mod, T, tr, [(ctx2d, D, 0, 0, False)], [vec(g0), vec(b0), vec(csh_a), vec(csc_a)], [(D, f32, False), (D, bf16, False)], [])
    xm_all = jnp.concatenate([xm, xcm], axis=0)

    h_all = _mm("in_proj", xm_all, w_cat, "nn", f32, tm=1088, tn=1280, tk=2048)

    qkv = _qkv_conv_fwd(h_all, w7, R, S, NH, K7).reshape(3, NH, R, HD)
    lane = jnp.arange(LANES)
    is_a = ((lane < 4 * NH) & ((lane // NH) % 2 == 0)).astype(f32).reshape(1, LANES)
    pad_g = lambda a, b: jnp.concatenate([a.reshape(-1), jnp.zeros((NH,), f32), b.reshape(-1), jnp.zeros((LANES - 3 * NH,), f32)]).reshape(1, LANES)
    neg_a = pad_g(-jnp.exp(a_log_f.astype(f32)), -jnp.exp(a_log_b.astype(f32)))
    dt_v = pad_g(dt_bias_f.astype(f32), dt_bias_b.astype(f32))

    def gates_f(j, hg, isa, na, dt):
        return (jnp.where(isa > 0.5, na * _softplus(hg + dt), _sigmoid(hg)),)

    trg = _row_tile(T, LANES)
    ag_w2 = _xchg_start("ag_w2_start", w_mlp2[0].astype(bf16), True, qkv, CORE_AND_SIB)
    (gates,) = _rowcall("gates_fwd", gates_f, R, trg, [(h_all, LANES, 0, GB, False)], [vec(_behind(ag_w2[4], is_a)), vec(neg_a), vec(dt_v)], [(LANES, f32, False)], [])
    gpad = jnp.zeros((R, LANES - 2 * NH), f32)
    gd = jnp.stack([jnp.concatenate([gates[:, :2 * NH], gpad], axis=1), jnp.concatenate([gates[:, 2 * NH:4 * NH], gpad], axis=1)])

    o5, states, *solved = _gdn_fwd(qkv, gd, NH, S, T)
    dng = row(dn_norm_g)
    trh = _row_tile(S, HD, cap=1024)
    dn_out = _gate_norm_fwd(o5, h_all, dng, NH, S, ZB, trh)

    yconv = _conf_conv_fwd(h_all, w31, S, CONF, VB, K31)
    bdw, clg, clb = row(conf_dw_b), row(conf_ln_g), row(conf_ln_b)

    def conf_ln(j, yc, b, g, bb):
        xh, _ = _ln_stats(yc + b)
        return (_silu(xh * g + bb),)

    trc = _row_tile(S, CONF)
    (conf_out,) = _rowcall("conf_ln_fwd", conf_ln, S, trc, [(yconv, CONF, 0, 0, False)], [vec(bdw), vec(clg), vec(clb)], [(CONF, bf16, False)], [])
    mix = jnp.concatenate([dn_out, conf_out], axis=1)
    wout_full = _xchg_wait("ag_wout_wait", ag_wout, True, mix).reshape(DN + CONF, D)
    y = _mm("out_proj", mix, wout_full, "nn", f32)

    l1g, l1b, l2g, l2b = row(ln1_g), row(ln1_b), row(ln2_g), row(ln2_b)

    def ln1_mod(j, xnt, yt, ga, g, b, sh, sc):
        xh, _ = _ln_stats(ALPHA * xnt + ga * yt)
        x1 = xh * g + b
        return x1, x1 * (1.0 + sc) + sh

    trd = _row_tile(S, D)
    x1, u = _rowcall("ln1_fwd", ln1_mod, S, trd, [(xn, D, 0, 0, False), (y, D, 0, 0, False)],
                     [vec(g_a), vec(l1g), vec(l1b), vec(sh_m), vec(sc_m)], [(D, f32, False), (D, bf16, False)], [])
    b1, b2 = row(b_mlp1), row(b_mlp2)
    g_w1 = _xchg_wait("ag_w1_wait", ag_w1, True, u)
    hh, act = _mm("mlp1", u, g_w1, "nn", (f32, bf16), extras=[("row", b1)], epilogue=lambda r, b: (r, jnp.square(jnp.maximum(r + b, 0.0))))
    w2_full = _sib_forward("ag_w2_fwd", _xchg_wait("ag_w2_wait", ag_w2, True, act, CORE_AND_SIB)).reshape(DFF, D)
    y2 = _mm("mlp2", act, w2_full, "nn", f32)

    def ln2_loss(j, x1t, y2t, tg, bb2, gm, g, b):
        y2b = y2t + bb2
        xh, rstd = _ln_stats(ALPHA * x1t + gm * y2b)
        err = xh * g + b - tg
        dx2 = err * (1.0 / D)
        dr2 = _ln_bwd(dx2, xh, rstd, g)
        dy2 = dr2 * gm
        lsum = jnp.broadcast_to(jnp.sum(err * err).reshape(1, 1), (1, LANES))
        return dr2, dy2, _csum(dx2 * xh), _csum(dx2), _csum(dr2 * y2b), _csum(dy2), lsum

    dr2, dy2, d_l2g, d_l2b, d_gm, d_b2, lsum = _rowcall(
        "ln2_loss", ln2_loss, S, trd, [(x1, D, 0, 0, False), (y2, D, 0, 0, False), (tgt, D, 0, 0, False)],
        [vec(b2), vec(g_m), vec(l2g), vec(l2b)], [(D, f32, False), (D, bf16, False)], [(1, D, False)] * 4 + [(1, LANES, False)])
    loss = lax.psum(0.5 * lsum[0, 0] / D, ("x", "y", "c"))

    dw2_p = _mm("mlp2_dw", act, dy2, "tn", bf16)
    a2a_w2 = _xchg_start("a2a_w2_start", dw2_p.reshape(NDEV, DFF // NDEV, D), False, dw2_p)
    dhh = _mm("mlp2_dx", dy2, w2_full, "nt", bf16, extras=[("tile", hh), ("row", _behind(a2a_w2[4], b1))],
              epilogue=lambda r, h, b: (r * (2.0 * jnp.maximum(h + b, 0.0)),))
    (d_b1,) = _rowcall("b1_grad", lambda j, a: (_csum(a.astype(f32)),), S, _row_tile(S, DFF // 8), [(dhh, DFF, 0, 0, False)], [], [], [(1, DFF, False)])
    dw1_p = _mm("mlp1_dw", u, dhh, "tn", bf16, out_split=NDEV)
    a2a_w1 = _xchg_start("a2a_w1_start", dw1_p, False, dw1_p)
    du = _mm("mlp1_dx", dhh, g_w1, "nt", f32)

    def ln1_bwd(j, dr2t, dut, xnt, yt, ga, g, b, sc):
        xh, rstd = _ln_stats(ALPHA * xnt + ga * yt)
        x1t = xh * g + b
        dx1 = ALPHA * dr2t + dut * (1.0 + sc)
        dr1 = _ln_bwd(dx1, xh, rstd, g)
        return dr1, dr1 * ga, _csum(dut * x1t), _csum(dut), _csum(dx1 * xh), _csum(dx1), _csum(dr1 * yt)

    dr1, dy, d_scm, d_shm, d_l1g, d_l1b, d_ga = _rowcall(
        "ln1_bwd", ln1_bwd, S, trd, [(dr2, D, 0, 0, False), (du, D, 0, 0, False), (xn, D, 0, 0, False), (y, D, 0, 0, False)],
        [vec(g_a), vec(l1g), vec(l1b), vec(_behind(a2a_w1[4], sc_m))], [(D, f32, False), (D, bf16, False)], [(1, D, False)] * 5)
    dwout_p = _mm("out_proj_dw", mix, dy, "tn", bf16)
    a2a_wout = _xchg_start("a2a_wout_start", dwout_p.reshape(NDEV, (DN + CONF) // NDEV, D), False, dwout_p)
    dmix = _mm("out_proj_dx", dy, wout_full, "nt", f32)

    def conf_ln_b(j, dm, yc, b, g, bb):
        xh, rstd = _ln_stats(yc + b)
        dln = dm * _dsilu(xh * g + bb)
        dyc = _ln_bwd(dln, xh, rstd, g)
        return dyc, _csum(dln * xh), _csum(dln), _csum(dyc)

    dyc, d_clg, d_clb, d_bdw = _rowcall("conf_ln_bwd", conf_ln_b, S, trc, [(dmix, CONF, 0, DN // CONF, False), (yconv, CONF, 0, 0, False)],
                                        [vec(_behind(a2a_wout[4], bdw)), vec(clg), vec(clb)], [(CONF, f32, False)], [(1, CONF, False)] * 3)
    dval, dgate, dw31 = _conf_conv_bwd(h_all, w31, dyc, S, CONF, VB, K31)

    do, dz, d_dng = _gate_norm_bwd(dmix, o5, h_all, dng, NH, S, ZB, trh)
    dqkv, dgd = _gdn_bwd(qkv, gd, states, solved, do, NH, S, T)
    dh_qkv, dw7 = _qkv_conv_bwd(h_all, w7, dqkv.reshape(2, 3 * NH, R, HD), R, S, NH, K7)
    dgate_cols = jnp.concatenate([dgd[0][:, :2 * NH], dgd[1][:, :2 * NH], jnp.zeros((R, LANES - 4 * NH), f32)], axis=1)

    def gates_b(j, hg, dgt, isa, na, dt):
        sg = _sigmoid(hg)
        sp = _softplus(hg + dt)
        dpre = jnp.where(isa > 0.5, dgt * na * _sigmoid(hg + dt), dgt * sg * (1.0 - sg))
        return dpre, _csum(jnp.where(isa > 0.5, dgt * na * sp, 0.0)), _csum(jnp.where(isa > 0.5, dpre, 0.0))

    dh_g, d_alog, d_dt = _rowcall("gates_bwd", gates_b, R, trg, [(h_all, LANES, 0, GB, False), (dgate_cols, LANES, 0, 0, False)],
                                  [vec(is_a), vec(neg_a), vec(dt_v)], [(LANES, bf16, False)], [(1, LANES, False)] * 2)

    zpad = lambda a: jnp.concatenate([a, jnp.zeros((T, a.shape[1]), bf16)], axis=0)
    dh_all = jnp.concatenate([dh_qkv, zpad(dz), zpad(dval), zpad(dgate), dh_g, jnp.zeros((R, GPAD - LANES), bf16)], axis=1)
    dxm_all = _mm("in_proj_dx", dh_all, w_cat, "nt", f32, tm=1088, tk=2560)

    def ln_in_bwd_lat(j, xt, dr1t, dxm, g, b, sc):
        xh, rstd = _ln_stats(xt)
        xnt = xh * g + b
        dxn = ALPHA * dr1t + dxm * (1.0 + sc)
        return _ln_bwd(dxn, xh, rstd, g), _csum(dxm * xnt), _csum(dxm), _csum(dxn * xh), _csum(dxn)

    def ln_in_bwd_ctx(j, xt, dxm, g, b, sc):
        xh, rstd = _ln_stats(xt)
        xnt = xh * g + b
        dxn = dxm * (1.0 + sc)
        return _csum(dxm * xnt), _csum(dxm), _csum(dxn * xh), _csum(dxn)

    grad_x, d_sca, d_sha, d_g0a, d_b0a = _rowcall(
        "ln_in_bwd_lat", ln_in_bwd_lat, S, trd, [(x2d, D, 0, 0, False), (dr1, D, 0, 0, False), (dxm_all, D, 0, 0, False)],
        [vec(g0), vec(b0), vec(sc_a)], [(D, f32, False)], [(1, D, False)] * 4)
    d_csca, d_csha, d_g0b, d_b0b = _rowcall(
        "ln_in_bwd_ctx", ln_in_bwd_ctx, T, tr, [(ctx2d, D, 0, 0, False), (dxm_all, D, S // tr, 0, False)],
        [vec(g0), vec(b0), vec(csc_a)], [], [(1, D, False)] * 4)

    zD = jnp.zeros((1, D), f32)
    dmod_me = jnp.concatenate([d_sha, d_sca, d_ga, d_shm, d_scm, d_gm], axis=1)
    dmodc_me = jnp.concatenate([d_csha, d_csca, zD, zD, zD, zD], axis=1)
    small_names = ["ln_in_g", "ln_in_b", "a_log_f", "dt_bias_f", "a_log_b", "dt_bias_b", "dn_norm_g", "conf_dw_b", "conf_ln_g", "conf_ln_b",
                   "ln1_g", "ln1_b", "b_mlp1", "b_mlp2", "ln2_g", "ln2_b"]
    small_parts = [d_g0a + d_g0b, d_b0a + d_b0b, d_alog[:, 0:NH], d_dt[:, 0:NH], d_alog[:, 2 * NH:3 * NH], d_dt[:, 2 * NH:3 * NH], d_dng, d_bdw, d_clg, d_clb,
                   d_l1g, d_l1b, d_b1, d_b2, d_l2g, d_l2b]
    conv_parts = [dw7[:K7], dw31[:K31]]
    packed = _pack(small_parts + conv_parts + [dmodc_me])
    (pk_all, dmod_all) = _all_gather("ag_small_grads", [packed, dmod_me])
    summed = _sum8("sum_small_grads", pk_all.reshape(NDEV, -1))
    parts_sh = [a.shape for a in small_parts + conv_parts + [dmodc_me]]
    un = _unpack(summed, parts_sh)
    gsmall = dict(zip(small_names, un[:len(small_names)]))
    g_w7_full, g_w31_full, dmodc = un[len(small_names):]
    gsmall["w_qkv_conv"] = lax.dynamic_slice_in_dim(g_w7_full, me * w_qkv_conv.shape[2], w_qkv_conv.shape[2], axis=1)
    gsmall["conf_dw_w"] = lax.dynamic_slice_in_dim(g_w31_full, me * conf_dw_w.shape[2], conf_dw_w.shape[2], axis=1)

    dm16 = jnp.concatenate([dmod_all.reshape(NDEV, 6 * D), dmodc, jnp.zeros((7, 6 * D), f32)], axis=0)
    (gbmod,) = _rowcall("bmod_grad", lambda j, a: (_csum(a),), 16, 16, [(dm16, 6 * D, 0, 0, False)], [], [], [(1, 6 * D, False)])
    gsmall["b_mod"] = gbmod
    dm16_mine = lax.dynamic_slice_in_dim(dm16, me * MODC, MODC, axis=1)
    g_wmod = _mm("mod_dw", sc16, dm16_mine, "tn", f32)
    dsc16_part = _mm("mod_dx", dm16_mine, w_mod[0], "nt", f32)
    (dsc_all,) = _all_gather("ag_cctx", [dsc16_part[8:9]])
    dsilu_cctx = _sum8("sum_cctx", dsc_all.reshape(NDEV, D))
    (g_cctx,) = _rowcall("cctx_grad", lambda j, ds, cc: (ds * _dsilu(cc),), 1, 1, [(dsilu_cctx, D, 0, 0, False), (c_ctx.reshape(1, D).astype(f32), D, 0, 0, False)],
                         [], [(D, f32, False)], [])
    gsmall["c_ctx"] = g_cctx

    dwcat_p = _mm("in_proj_dw", xm_all, dh_all, "tn", bf16, tn=1280, tk=4352, after=g_cctx)
    dwin = jnp.concatenate([dwcat_p[:, :4 * DN], dwcat_p[:, 4 * DN + 2 * CONF:4 * DN + 2 * CONF + 4 * NH], dwcat_p[:, 4 * DN:4 * DN + 2 * CONF]], axis=1)
    dwin_p = jnp.transpose(dwin.reshape(D, NDEV, INC // NDEV), (1, 0, 2))
    from_sib = _sib_swap("sib_win", dwin_p)
    mine4 = lax.dynamic_index_in_dim(dwin_p.reshape(4, 2, D, INC // NDEV), lax.axis_index("c"), 1, keepdims=False)
    (chip_sum,) = _rowcall("add_win", lambda j, a, b: (a.astype(f32) + b.astype(f32),), 4 * D, _row_tile(4 * D, INC // NDEV),
                           [(mine4.reshape(4 * D, INC // NDEV), INC // NDEV, 0, 0, False), (from_sib.reshape(4 * D, INC // NDEV), INC // NDEV, 0, 0, False)],
                           [], [(INC // NDEV, bf16, False)], [])
    a2a_win = _xchg_start("a2a_win_start", chip_sum.reshape(4, D, INC // NDEV), False, chip_sum, SAME_CORE)

    grads, deltas, new_m, new_v = {}, {}, {}, {}
    res = _adam("adam_w_mod", w_mod[0], m_w_mod[0], v_w_mod[0], g_wmod, 0, after=a2a_win[4])
    grads["w_mod"], deltas["w_mod"], new_m["w_mod"], new_v["w_mod"] = [a.reshape(w_mod.shape) for a in res]
    after = res[1]
    for nm, started in (("w_mlp2", a2a_w2), ("w_mlp1", a2a_w1), ("w_out", a2a_wout), ("w_in", a2a_win)):
        w3 = weights[nm]
        rels = SAME_CORE if nm == "w_in" else ALL_PEERS
        g = _xchg_wait("a2a_" + nm + "_wait", started, False, after, rels)
        res = _adam("adam_" + nm, w3[0], mom1[nm][0], mom2[nm][0], g, len(rels) + 1)
        grads[nm], deltas[nm], new_m[nm], new_v[nm] = [a.reshape(w3.shape) for a in res]
        after = res[1]
    snames = [n for n in names if n not in grads]
    res = _adam("adam_small", _pack([weights[n] for n in snames]), _pack([mom1[n] for n in snames]), _pack([mom2[n] for n in snames]),
                _pack([gsmall[n] for n in snames]), 0)
    shapes = [weights[n].shape for n in snames]
    for dst, packed_out in zip((grads, deltas, new_m, new_v), res):
        for n, a in zip(snames, _unpack(packed_out, shapes)):
            dst[n] = a

    return (loss, grad_x.reshape(x.shape), *[grads[n] for n in names], *[deltas[n] for n in names],
            *[new_m[n] for n in names], *[new_v[n] for n in names])
```

```python
import functools

import jax
import jax.numpy as jnp
from jax import lax
from jax.experimental import pallas as pl
from jax.experimental.pallas import tpu as pltpu

f32 = jnp.float32
bf16 = jnp.bfloat16
MESH = pl.DeviceIdType.MESH

NDEV = 8
HD = 128
CH = 64
GW = 64
LANES = 128
GPAD = 512
LN_EPS = 1e-5
ALPHA = 2.0 ** 0.25
ADAM_LR, ADAM_B1, ADAM_B2, ADAM_EPS, ADAM_WD, ADAM_STEP = 0.001, 0.9, 0.999, 1e-08, 0.01, 10
VMEM_LIMIT = 60 * 1024 * 1024
ROW_BLOCK_BYTES = 1 << 21


def _cparams(sem):
    return pltpu.CompilerParams(dimension_semantics=sem, vmem_limit_bytes=VMEM_LIMIT)


def _tile(dim, pref, align):
    t = min(pref, dim)
    t -= t % align
    while t >= align:
        if dim % t == 0:
            return t
        t -= align
    return dim


def _row_tile(nrows, width, cap=256):
    t = max(16, min(cap, ROW_BLOCK_BYTES // (4 * width)))
    t = 1 << (t.bit_length() - 1)
    while nrows % t:
        t //= 2
    return max(t, 1)


def _sigmoid(x):
    return 1.0 / (1.0 + jnp.exp(-x))


def _silu(x):
    return x * _sigmoid(x)


def _dsilu(x):
    s = _sigmoid(x)
    return s * (1.0 + x * (1.0 - s))


def _softplus(x):
    return jnp.maximum(x, 0.0) + jnp.log(1.0 + jnp.exp(-jnp.abs(x)))


def _ln_stats(r):
    mu = jnp.mean(r, axis=-1, keepdims=True)
    xc = r - mu
    rstd = lax.rsqrt(jnp.mean(xc * xc, axis=-1, keepdims=True) + LN_EPS)
    return xc * rstd, rstd


def _ln_bwd(dy, xhat, rstd, g):
    dxh = dy * g
    return rstd * (dxh - jnp.mean(dxh, axis=-1, keepdims=True) - xhat * jnp.mean(dxh * xhat, axis=-1, keepdims=True))


def _csum(a):
    return jnp.sum(a, axis=0, keepdims=True)


def _mm(name, a, b, mode, out_dtype, out_split=1, tm=1024, tn=1024, tk=4096, extras=(), epilogue=None, after=None):
    if mode == "tn":
        K, M = a.shape
    else:
        M, K = a.shape
    bp = b.shape[0] if b.ndim == 3 else 1
    brows, bcols = b.shape[-2], b.shape[-1] * bp
    N = brows if mode == "nt" else bcols
    assert K == (bcols if mode == "nt" else brows), (name, a.shape, b.shape)
    tm = _tile(M, tm, LANES if mode == "tn" else 16)
    nsplit = max(out_split, bp if mode != "nt" else 1)
    tn = _tile(N // nsplit, tn, LANES)
    kgroup = 1
    if mode == "nt" and bp > 1 and tk >= 2 * (K // bp):
        kgroup = min(tk // (K // bp), bp)
        tk = kgroup * (K // bp)
    else:
        tk = _tile(K // (bp if mode == "nt" else 1), tk, LANES)
    nk = K // tk
    dims = {"nn": (((1,), (0,)), ((), ())), "nt": (((1,), (1,)), ((), ())), "tn": (((0,), (0,)), ((), ()))}[mode]
    out_dtypes = out_dtype if isinstance(out_dtype, tuple) else (out_dtype,)

    a_spec = pl.BlockSpec((tk, tm), lambda i, j, k: (k, i)) if mode == "tn" else pl.BlockSpec((tm, tk), lambda i, j, k: (i, k))
    if b.ndim == 3:
        if mode == "nt" and kgroup > 1:
            b_spec = pl.BlockSpec((kgroup, tn, K // bp), lambda i, j, k: (k, j, 0))
        elif mode == "nt":
            per = (K // bp) // tk
            b_spec = pl.BlockSpec((None, tn, tk), lambda i, j, k: (k // per, j, k % per))
        else:
            per = (N // bp) // tn
            b_spec = pl.BlockSpec((None, tk, tn), lambda i, j, k: (j // per, k, j % per))
    elif mode == "nt":
        b_spec = pl.BlockSpec((tn, tk), lambda i, j, k: (j, k))
    else:
        b_spec = pl.BlockSpec((tk, tn), lambda i, j, k: (k, j))
    if out_split > 1:
        pero = (N // out_split) // tn
        o_spec = pl.BlockSpec((None, tm, tn), lambda i, j, k: (j // pero, i, j % pero))
        o_shapes = [jax.ShapeDtypeStruct((out_split, M, N // out_split), dt) for dt in out_dtypes]
    else:
        o_spec = pl.BlockSpec((tm, tn), lambda i, j, k: (i, j))
        o_shapes = [jax.ShapeDtypeStruct((M, N), dt) for dt in out_dtypes]
    in_specs = [a_spec, b_spec]
    args = [a, b]
    for kind, arr in extras:
        in_specs.append(pl.BlockSpec((1, tn), lambda i, j, k: (0, j)) if kind == "row" else pl.BlockSpec((tm, tn), lambda i, j, k: (i, j)))
        args.append(arr)
    n_in, n_out = len(args), len(out_dtypes)
    if after is not None:
        in_specs.append(pl.BlockSpec(memory_space=pl.ANY))
        args.append(after)
    n_all = len(args)

    def finish(refs, r):
        outs = (r,) if epilogue is None else epilogue(r, *[e[...] for e in refs[2:n_in]])
        for o_ref, val in zip(refs[n_all:n_all + n_out], outs):
            o_ref[...] = val.astype(o_ref.dtype)

    def body(*refs):
        if kgroup > 1:
            kp = K // bp
            part = lax.dot_general(refs[0][:, 0:kp].astype(bf16), refs[1][0].astype(bf16), dims, preferred_element_type=f32)
            for gi in range(1, kgroup):
                part = part + lax.dot_general(refs[0][:, gi * kp:(gi + 1) * kp].astype(bf16), refs[1][gi].astype(bf16), dims, preferred_element_type=f32)
        else:
            part = lax.dot_general(refs[0][...].astype(bf16), refs[1][...].astype(bf16), dims, preferred_element_type=f32)
        if nk == 1:
            finish(refs, part)
            return
        acc = refs[-1]
        k = pl.program_id(2)

        @pl.when(k == 0)
        def _():
            acc[...] = part

        @pl.when(jnp.logical_and(k > 0, k < nk - 1))
        def _():
            acc[...] += part

        @pl.when(k == nk - 1)
        def _():
            finish(refs, acc[...] + part)

    res = pl.pallas_call(
        body, name=name, grid=(M // tm, N // tn, nk), in_specs=in_specs, out_specs=[o_spec] * n_out, out_shape=o_shapes,
        scratch_shapes=[pltpu.VMEM((tm, tn), f32)] if nk > 1 else [], compiler_params=_cparams(("parallel", "parallel", "arbitrary")),
    )(*args)
    return res[0] if n_out == 1 else res


def _rowcall(name, fn, nrows, tr, rows_in, vecs_in, rows_out, accs_out, ncol=1):
    nrt = nrows // tr
    in_specs, args = [], []
    for arr, w, ro, co, pc in rows_in:
        in_specs.append(pl.BlockSpec((tr, w), functools.partial(lambda j, i, ro, co, pc: (i + ro, co + (j if pc else 0)), ro=ro, co=co, pc=pc)))
        args.append(arr)
    for arr, w, co, pc in vecs_in:
        in_specs.append(pl.BlockSpec((arr.shape[0], w), functools.partial(lambda j, i, co, pc: (0, co + (j if pc else 0)), co=co, pc=pc)))
        args.append(arr)
    out_specs, out_shape = [], []
    for w, dt, pc in rows_out:
        out_specs.append(pl.BlockSpec((tr, w), functools.partial(lambda j, i, pc: (i, j if pc else 0), pc=pc)))
        out_shape.append(jax.ShapeDtypeStruct((nrows, w * (ncol if pc else 1)), dt))
    for k, w, pc in accs_out:
        out_specs.append(pl.BlockSpec((k, w), functools.partial(lambda j, i, pc: (0, j if pc else 0), pc=pc)))
        out_shape.append(jax.ShapeDtypeStruct((k, w * (ncol if pc else 1)), f32))
    n_in, n_ro = len(args), len(rows_out)

    def body(*refs):
        j, i = pl.program_id(0), pl.program_id(1)
        outs = fn(j, *[r[...] for r in refs[:n_in]])
        for r, val in zip(refs[n_in:n_in + n_ro], outs[:n_ro]):
            r[...] = val.astype(r.dtype)
        for (k, w, pc), r, val in zip(accs_out, refs[n_in + n_ro:], outs[n_ro:]):
            first = (i == 0) if pc else jnp.logical_and(i == 0, j == 0)

            @pl.when(first)
            def _(r=r):
                r[...] = jnp.zeros_like(r)

            r[...] += val

    res = pl.pallas_call(
        body, name=name, grid=(ncol, nrt), in_specs=in_specs, out_specs=out_specs, out_shape=out_shape,
        compiler_params=_cparams(("arbitrary", "arbitrary")),
    )(*args)
    return res


def _tap_valid(mode, t, d, nrows, nlat):
    if mode == "seg":
        tp = t + d
        return (tp >= 0) & (tp < nrows) & ((t < nlat) == (tp < nlat))
    if mode == "row":
        p = (t & (GW - 1)) + d
        return (p >= 0) & (p < GW)
    tp = t + d * GW
    return (tp >= 0) & (tp < nrows)


def _conv(x, w_ref, ktaps, mode, nlat, flip=False):
    nrows = x.shape[0]
    stride = GW if mode == "col" else 1
    t = lax.broadcasted_iota(jnp.int32, (nrows, 1), 0)
    acc = jnp.zeros_like(x)
    for j in range(ktaps):
        d = j - ktaps // 2
        jj = ktaps - 1 - j if flip else j
        wj = w_ref[jj:jj + 1, :]
        if d == 0:
            acc = acc + x * wj
        else:
            xs = pltpu.roll(x, (-d * stride) % nrows, 0)
            acc = acc + jnp.where(_tap_valid(mode, t, d, nrows, nlat), xs * wj, 0.0)
    return acc


def _conv_wgrad(dy, x, dw_ref, ktaps, mode, nlat):
    nrows = x.shape[0]
    stride = GW if mode == "col" else 1
    t = lax.broadcasted_iota(jnp.int32, (nrows, 1), 0)
    dw_ref[...] = jnp.zeros_like(dw_ref)
    for j in range(ktaps):
        d = j - ktaps // 2
        if d == 0:
            prod = x * dy
        else:
            xs = pltpu.roll(x, (-d * stride) % nrows, 0)
            prod = jnp.where(_tap_valid(mode, t, d, nrows, nlat), xs * dy, 0.0)
        dw_ref[j:j + 1, :] = _csum(prod)


def _qkv_post(j, pre, nh):
    a = _silu(pre)
    inv = lax.rsqrt(jnp.sum(a * a, axis=-1, keepdims=True) + 1e-6)
    scale = jnp.where(j < nh, HD ** -0.5, 1.0).astype(f32)
    nrm = jnp.where(j < 2 * nh, inv, 1.0) * scale
    return a, inv, nrm


def _qkv_conv_fwd(h_all, w7, nrows, nlat, nh, ktaps):
    ntile = 3 * nh

    def body(h_ref, w_ref, o_ref):
        j = pl.program_id(0)
        pre = _conv(h_ref[...], w_ref, ktaps, "seg", nlat)
        a, _, nrm = _qkv_post(j, pre, nh)
        o_ref[...] = a * nrm

    return pl.pallas_call(
        body, name="qkv_conv_fwd", grid=(ntile,),
        in_specs=[pl.BlockSpec((nrows, HD), lambda j: (0, j)), pl.BlockSpec((w7.shape[0], HD), lambda j: (0, j))],
        out_specs=pl.BlockSpec((None, nrows, HD), lambda j: (j, 0, 0)),
        out_shape=jax.ShapeDtypeStruct((ntile, nrows, HD), f32), compiler_params=_cparams(("parallel",)),
    )(h_all, w7)


def _qkv_conv_bwd(h_all, w7, dqkv, nrows, nlat, nh, ktaps):
    ntile = 3 * nh

    def body(h_ref, w_ref, d0_ref, d1_ref, dh_ref, dw_ref):
        j = pl.program_id(0)
        hx = h_ref[...]
        pre = _conv(hx, w_ref, ktaps, "seg", nlat)
        a, inv, nrm = _qkv_post(j, pre, nh)
        dn = (d0_ref[...] + d1_ref[...]) * jnp.where(j < nh, HD ** -0.5, 1.0).astype(f32)
        n = a * inv
        da_norm = inv * (dn - n * jnp.sum(dn * n, axis=-1, keepdims=True))
        da = jnp.where(j < 2 * nh, da_norm, dn)
        dpre = da * _dsilu(pre)
        dh_ref[...] = _conv(dpre, w_ref, ktaps, "seg", nlat, flip=True).astype(dh_ref.dtype)
        _conv_wgrad(dpre, hx, dw_ref, ktaps, "seg", nlat)

    blk = pl.BlockSpec((nrows, HD), lambda j: (0, j))
    wblk = pl.BlockSpec((w7.shape[0], HD), lambda j: (0, j))
    dblk = lambda d: pl.BlockSpec((None, None, nrows, HD), lambda j: (d, j, 0, 0))
    return pl.pallas_call(
        body, name="qkv_conv_bwd", grid=(ntile,), in_specs=[blk, wblk, dblk(0), dblk(1)], out_specs=[blk, wblk],
        out_shape=[jax.ShapeDtypeStruct((nrows, ntile * HD), bf16), jax.ShapeDtypeStruct((w7.shape[0], ntile * HD), f32)],
        compiler_params=_cparams(("parallel",)),
    )(h_all, w7, dqkv, dqkv)


def _conf_conv_fwd(h_all, w31, nlat, conf, val_blk, ktaps):
    nt = conf // LANES
    nhalf = nt // 2

    def body(v_ref, g_ref, w_ref, o_ref):
        j = pl.program_id(0)
        glu = v_ref[...] * _sigmoid(g_ref[...])

        @pl.when(j < nhalf)
        def _():
            o_ref[...] = _conv(glu, w_ref, ktaps, "row", nlat)

        @pl.when(j >= nhalf)
        def _():
            o_ref[...] = _conv(glu, w_ref, ktaps, "col", nlat)

    return pl.pallas_call(
        body, name="conf_conv_fwd", grid=(nt,),
        in_specs=[pl.BlockSpec((nlat, LANES), lambda j: (0, val_blk + j)), pl.BlockSpec((nlat, LANES), lambda j: (0, val_blk + nt + j)),
                  pl.BlockSpec((w31.shape[0], LANES), lambda j: (0, j))],
        out_specs=pl.BlockSpec((nlat, LANES), lambda j: (0, j)),
        out_shape=jax.ShapeDtypeStruct((nlat, conf), f32), compiler_params=_cparams(("parallel",)),
    )(h_all, h_all, w31)


def _conf_conv_bwd(h_all, w31, dyc, nlat, conf, val_blk, ktaps):
    nt = conf // LANES
    nhalf = nt // 2

    def body(v_ref, g_ref, w_ref, dy_ref, dv_ref, dg_ref, dw_ref):
        j = pl.program_id(0)
        val, sg = v_ref[...], _sigmoid(g_ref[...])
        glu = val * sg
        dy = dy_ref[...]

        def run(mode):
            dglu = _conv(dy, w_ref, ktaps, mode, nlat, flip=True)
            dv_ref[...] = (dglu * sg).astype(dv_ref.dtype)
            dg_ref[...] = (dglu * val * sg * (1.0 - sg)).astype(dg_ref.dtype)
            _conv_wgrad(dy, glu, dw_ref, ktaps, mode, nlat)

        @pl.when(j < nhalf)
        def _():
            run("row")

        @pl.when(j >= nhalf)
        def _():
            run("col")

    blk = pl.BlockSpec((nlat, LANES), lambda j: (0, j))
    wblk = pl.BlockSpec((w31.shape[0], LANES), lambda j: (0, j))
    return pl.pallas_call(
        body, name="conf_conv_bwd", grid=(nt,),
        in_specs=[pl.BlockSpec((nlat, LANES), lambda j: (0, val_blk + j)), pl.BlockSpec((nlat, LANES), lambda j: (0, val_blk + nt + j)), wblk, blk],
        out_specs=[blk, blk, wblk],
        out_shape=[jax.ShapeDtypeStruct((nlat, conf), bf16), jax.ShapeDtypeStruct((nlat, conf), bf16),
                   jax.ShapeDtypeStruct((w31.shape[0], conf), f32)],
        compiler_params=_cparams(("parallel",)),
    )(h_all, h_all, w31, dyc)


def _bd(eq, a, b):
    return jnp.einsum(eq, a.astype(bf16), b.astype(bf16), preferred_element_type=f32)


def _split(x, pieces):
    out = []
    for _ in range(pieces - 1):
        p = x.astype(bf16)
        out.append(p)
        x = x - p.astype(f32)
    return out + [x.astype(bf16)]


def _h3(eq, a, b):
    (ah, al), (bh, bl) = _split(a, 2), _split(b, 2)
    d = lambda p, q: jnp.einsum(eq, p, q, preferred_element_type=f32)
    return d(ah, bh) + (d(ah, bl) + d(al, bh))


def _cumsum_rows(x, rev):
    row = lax.broadcasted_iota(jnp.int32, x.shape, 0)
    down, up, s = x, x, 1
    while s < CH:
        down = down + jnp.where(row >= s, pltpu.roll(down, s, 0), 0.0)
        up = up + jnp.where(row < CH - s, pltpu.roll(up, CH - s, 0), 0.0)
        s *= 2
    return jnp.where(rev, up, down)


def _gdn_chunk(q, k, v, gt, nh, rev, solved=None):
    beta = jnp.stack([jnp.broadcast_to(gt[:, nh + h:nh + h + 1], (CH, LANES)) for h in range(nh)])
    ii = lax.broadcasted_iota(jnp.int32, (CH, CH), 0)
    jj = lax.broadcasted_iota(jnp.int32, (CH, CH), 1)
    si, sj = jnp.where(rev, jj, ii), jnp.where(rev, ii, jj)
    tril, stril = (si >= sj)[None], (si > sj)[None]
    gct = _cumsum_rows(gt, rev)
    gct_t = gct.T
    gc = jnp.stack([jnp.broadcast_to(gct[:, h:h + 1], (CH, LANES)) for h in range(nh)])
    gc_row = jnp.stack([jnp.broadcast_to(gct_t[h:h + 1, :], (CH, CH)) for h in range(nh)])
    diff = gc[:, :, :CH] - gc_row
    gam = jnp.where(tril, jnp.exp(jnp.where(tril, diff, 0.0)), 0.0)
    e = jnp.exp(gc)
    gl = jnp.where(rev, gc[:, 0:1, :], gc[:, CH - 1:CH, :])
    el = jnp.exp(gl)
    r = jnp.exp(gl - gc)
    kb, vb = k * beta, v * beta
    kbe = kb * e
    amat = jnp.where(stril, _bd("hik,hjk->hij", kb, k) * gam, 0.0)
    if solved is None:
        xp = -amat
        tinv = (ii == jj).astype(f32)[None] + xp
        for _ in range(5):
            xp = _h3("hij,hjk->hik", xp, xp)
            tinv = tinv + _h3("hij,hjk->hik", tinv, xp)
        u = _h3("hij,hjv->hiv", tinv, vb)
        w = _h3("hij,hjk->hik", tinv, kbe)
    else:
        tinv, u, w = solved
    pmat = jnp.where(tril, _bd("hik,hjk->hij", q, k) * gam, 0.0)
    return dict(beta=beta, gam=gam, e=e, el=el, r=r, kb=kb, vb=vb, kbe=kbe, amat=amat, tinv=tinv, u=u, w=w, pmat=pmat,
                qd=q * e, kd=k * r, tril=tril, stril=stril)


def _scan_row_chunk(d, n, ns, nt):
    fwd = jnp.where(n < nt, ns + n, n - nt)
    return jnp.where(d == 0, fwd, ns + nt - 1 - n)


def _gdn_fwd(qkv, gd, nh, nlat, nctx):
    nrows = nlat + nctx
    ns, nt = nlat // CH, nctx // CH
    nch = ns + nt

    def body(qkv_ref, g_ref, o_ref, st_ref, ti_ref, u_ref, w_ref, state):
        d, n = pl.program_id(0), pl.program_id(1)

        @pl.when(n == 0)
        def _():
            state[...] = jnp.zeros_like(state)

        c = _gdn_chunk(qkv_ref[0], qkv_ref[1], qkv_ref[2], g_ref[...], nh, d == 1)
        s = state[...]
        vn = c["u"] - _bd("hck,hkv->hcv", c["w"], s)
        o_ref[...] = _bd("hck,hkv->hcv", c["qd"], s) + _bd("hcd,hdv->hcv", c["pmat"], vn)
        st_ref[...] = s
        ti_ref[...], u_ref[...], w_ref[...] = c["tinv"], c["u"], c["w"]
        state[...] = s * c["el"] + _bd("hck,hcv->hkv", c["kd"], vn)

    hblk = pl.BlockSpec((None, nh, CH, HD), lambda d, n: (d, 0, _scan_row_chunk(d, n, ns, nt), 0))
    hshape = jax.ShapeDtypeStruct((2, nh, nrows, HD), f32)
    return pl.pallas_call(
        body, name="gdn_fwd", grid=(2, nch),
        in_specs=[pl.BlockSpec((3, nh, CH, HD), lambda d, n: (0, 0, _scan_row_chunk(d, n, ns, nt), 0)),
                  pl.BlockSpec((None, CH, LANES), lambda d, n: (d, _scan_row_chunk(d, n, ns, nt), 0))],
        out_specs=[hblk, pl.BlockSpec((None, None, nh, HD, HD), lambda d, n: (d, n, 0, 0, 0)),
                   pl.BlockSpec((None, None, nh, CH, CH), lambda d, n: (d, n, 0, 0, 0)), hblk, hblk],
        out_shape=[hshape, jax.ShapeDtypeStruct((2, nch, nh, HD, HD), f32), jax.ShapeDtypeStruct((2, nch, nh, CH, CH), f32), hshape, hshape],
        scratch_shapes=[pltpu.VMEM((nh, HD, HD), f32)], compiler_params=_cparams(("arbitrary", "arbitrary")),
    )(qkv, gd)


def _gdn_bwd(qkv, gd, states, solved, do, nh, nlat, nctx):
    nrows = nlat + nctx
    ns, nt = nlat // CH, nctx // CH
    nch = ns + nt

    def body(qkv_ref, g_ref, st_ref, ti_ref, u_ref, w_ref, do_ref, dqkv_ref, dgd_ref, dstate):
        d, step = pl.program_id(0), pl.program_id(1)

        @pl.when(step == 0)
        def _():
            dstate[...] = jnp.zeros_like(dstate)

        q, k, v = qkv_ref[0], qkv_ref[1], qkv_ref[2]
        c = _gdn_chunk(q, k, v, g_ref[...], nh, d == 1, solved=(ti_ref[...], u_ref[...], w_ref[...]))
        in_ctx = (nch - 1 - step) < nt
        s, dsp = st_ref[...], dstate[...]
        dout = jnp.where(in_ctx, 0.0, do_ref[...])
        beta, gam, e, el, r = c["beta"], c["gam"], c["e"], c["el"], c["r"]
        tinv, u, w, pmat, amat = c["tinv"], c["u"], c["w"], c["pmat"], c["amat"]
        vn = u - _bd("hck,hkv->hcv", w, s)
        dvn = _bd("hdc,hdv->hcv", pmat, dout) + _bd("hck,hkv->hcv", c["kd"], dsp)
        dp = jnp.where(c["tril"], _bd("hcv,hdv->hcd", dout, vn), 0.0)
        dqd = _bd("hcv,hkv->hck", dout, s)
        dkd = _bd("hcv,hkv->hck", vn, dsp)
        dstate[...] = _bd("hck,hcv->hkv", c["qd"], dout) + dsp * el - _bd("hck,hcv->hkv", w, dvn)
        del_ = jnp.sum(jnp.sum(s * dsp, axis=2, keepdims=True), axis=1, keepdims=True)
        dw = -_bd("hcv,hkv->hck", dvn, s)
        dvb = _h3("hji,hjv->hiv", tinv, dvn)
        dkbe = _h3("hji,hjk->hik", tinv, dw)
        z = _bd("hiv,hjv->hij", dvn, u) + _bd("hik,hjk->hij", dw, w)
        da = jnp.where(c["stril"], -_h3("hji,hjl->hil", tinv, z), 0.0)
        dm = da * gam
        dkb = _bd("hij,hjk->hik", dm, k) + dkbe * e
        dn = dp * gam
        dqkv_ref[0] = _bd("hij,hjk->hik", dn, k) + dqd * e
        dqkv_ref[1] = _bd("hji,hjk->hik", dm, c["kb"]) + _bd("hji,hjk->hik", dn, q) + dkd * r + dkb * beta
        dqkv_ref[2] = dvb * beta
        gmat = da * amat + dp * pmat
        rows_minus_cols = jnp.sum(gmat, axis=-1, keepdims=True) - jnp.sum(jnp.swapaxes(gmat, 1, 2), axis=-1, keepdims=True)
        de = jnp.sum(dqd * q + dkbe * c["kb"], axis=-1, keepdims=True)
        drr = jnp.sum(dkd * k, axis=-1, keepdims=True) * r
        dgc = rows_minus_cols + de * e - drr
        dgl = jnp.sum(drr, axis=1, keepdims=True) + del_ * el
        row = lax.broadcasted_iota(jnp.int32, (1, CH, 1), 1)
        total_row = row == jnp.where(d == 1, 0, CH - 1)
        dgc = dgc + jnp.where(total_row, dgl, 0.0)
        db = jnp.sum(dvb * v + dkb * k, axis=-1, keepdims=True)
        lane = lax.broadcasted_iota(jnp.int32, (CH, LANES), 1)
        dgc_cols, db_cols = jnp.zeros((CH, LANES), f32), jnp.zeros((CH, LANES), f32)
        for h in range(nh):
            dgc_cols = dgc_cols + jnp.where(lane == h, dgc[h], 0.0)
            db_cols = db_cols + jnp.where(lane == nh + h, db[h], 0.0)
        dgd_ref[...] = _cumsum_rows(dgc_cols, d == 0) + db_cols

    def rc(d, n):
        return _scan_row_chunk(d, nch - 1 - n, ns, nt)

    gblk = pl.BlockSpec((None, CH, LANES), lambda d, n: (d, rc(d, n), 0))
    return pl.pallas_call(
        body, name="gdn_bwd", grid=(2, nch),
        in_specs=[pl.BlockSpec((3, nh, CH, HD), lambda d, n: (0, 0, rc(d, n), 0)),
                  gblk,
                  pl.BlockSpec((None, None, nh, HD, HD), lambda d, n: (d, nch - 1 - n, 0, 0, 0)),
                  pl.BlockSpec((None, None, nh, CH, CH), lambda d, n: (d, nch - 1 - n, 0, 0, 0)),
                  pl.BlockSpec((None, nh, CH, HD), lambda d, n: (d, 0, rc(d, n), 0)),
                  pl.BlockSpec((None, nh, CH, HD), lambda d, n: (d, 0, rc(d, n), 0)),
                  pl.BlockSpec((nh, CH, HD), lambda d, n: (0, jnp.minimum(rc(d, n), ns - 1), 0))],
        out_specs=[pl.BlockSpec((None, 3, nh, CH, HD), lambda d, n: (d, 0, 0, rc(d, n), 0)), gblk],
        out_shape=[jax.ShapeDtypeStruct((2, 3, nh, nrows, HD), f32), jax.ShapeDtypeStruct((2, nrows, LANES), f32)],
        scratch_shapes=[pltpu.VMEM((nh, HD, HD), f32)], compiler_params=_cparams(("arbitrary", "arbitrary")),
    )(qkv, gd, states, *solved, do)


def _gate_norm_fwd(o5, h_all, dng, nh, nlat, zblk, tr):
    def body(of_ref, ob_ref, z_ref, g_ref, out_ref):
        o = of_ref[...] + ob_ref[...]
        rs = lax.rsqrt(jnp.mean(o * o, axis=-1, keepdims=True) + 1e-6)
        out_ref[...] = ((o * rs * g_ref[...]) * _silu(z_ref[...])).astype(out_ref.dtype)

    return pl.pallas_call(
        body, name="gate_norm_fwd", grid=(nh, nlat // tr),
        in_specs=[pl.BlockSpec((None, None, tr, HD), lambda j, i: (0, j, i, 0)), pl.BlockSpec((None, None, tr, HD), lambda j, i: (1, j, i, 0)),
                  pl.BlockSpec((tr, HD), lambda j, i: (i, zblk + j)), pl.BlockSpec((1, HD), lambda j, i: (0, 0))],
        out_specs=pl.BlockSpec((tr, HD), lambda j, i: (i, j)), out_shape=jax.ShapeDtypeStruct((nlat, nh * HD), bf16),
        compiler_params=_cparams(("parallel", "parallel")),
    )(o5, o5, h_all, dng)


def _gate_norm_bwd(dmix, o5, h_all, dng, nh, nlat, zblk, tr):
    def body(dm_ref, of_ref, ob_ref, z_ref, g_ref, do_ref, dz_ref, dg_ref):
        o = of_ref[...] + ob_ref[...]
        z, g, dm = z_ref[...], g_ref[...], dm_ref[...]
        rs = lax.rsqrt(jnp.mean(o * o, axis=-1, keepdims=True) + 1e-6)
        don = dm * _silu(z)
        dz_ref[...] = (dm * (o * rs * g) * _dsilu(z)).astype(dz_ref.dtype)
        dog = don * g
        do_ref[...] = rs * (dog - o * rs * rs * jnp.mean(dog * o, axis=-1, keepdims=True))

        @pl.when(jnp.logical_and(pl.program_id(0) == 0, pl.program_id(1) == 0))
        def _():
            dg_ref[...] = jnp.zeros_like(dg_ref)

        dg_ref[...] += _csum(don * o * rs)

    return pl.pallas_call(
        body, name="gate_norm_bwd", grid=(nh, nlat // tr),
        in_specs=[pl.BlockSpec((tr, HD), lambda j, i: (i, j)),
                  pl.BlockSpec((None, None, tr, HD), lambda j, i: (0, j, i, 0)), pl.BlockSpec((None, None, tr, HD), lambda j, i: (1, j, i, 0)),
                  pl.BlockSpec((tr, HD), lambda j, i: (i, zblk + j)), pl.BlockSpec((1, HD), lambda j, i: (0, 0))],
        out_specs=[pl.BlockSpec((None, tr, HD), lambda j, i: (j, i, 0)), pl.BlockSpec((tr, HD), lambda j, i: (i, j)),
                   pl.BlockSpec((1, HD), lambda j, i: (0, 0))],
        out_shape=[jax.ShapeDtypeStruct((nh, nlat, HD), f32), jax.ShapeDtypeStruct((nlat, nh * HD), bf16), jax.ShapeDtypeStruct((1, HD), f32)],
        compiler_params=_cparams(("arbitrary", "arbitrary")),
    )(dmix, o5, o5, h_all, dng)


def _mesh_pos():
    return lax.axis_index("x"), lax.axis_index("y"), lax.axis_index("c")


def _lin(p):
    return 4 * p[0] + 2 * p[1] + p[2]


def _all_gather(name, xs, after=None):
    nx = len(xs)
    extra = [] if after is None else [after]

    def body(*refs):
        xr, outr = refs[:nx], refs[nx + len(extra):2 * nx + len(extra)]
        send, recv, loc = refs[2 * nx + len(extra):]
        x, y, c = _mesh_pos()
        me, sib = (x, y, c), (x, y, 1 - c)
        chips = [(1 - x, y), (x, 1 - y), (1 - x, 1 - y)]

        def cp(l, k, block, to, src=None):
            rows = outr[l].at[_lin(block)]
            return pltpu.make_async_remote_copy(src_ref=rows if src is None else src, dst_ref=rows, send_sem=send.at[l, k],
                                                recv_sem=recv.at[l, k], device_id=to, device_id_type=MESH)

        mine = [pltpu.make_async_copy(xr[l], outr[l].at[_lin(me)], loc.at[l]) for l in range(nx)]
        for m in mine:
            m.start()
        first = []
        for l in range(nx):
            first.append(cp(l, 0, me, sib, src=xr[l]))
            first += [cp(l, 1 + j, me, (*chip, c), src=xr[l]) for j, chip in enumerate(chips)]
        for f in first:
            f.start()
        passed = []
        for l in range(nx):
            for j, chip in enumerate(chips):
                cp(l, 1 + j, (*chip, c), me).wait_recv()
                fwd = cp(l, 4 + j, (*chip, c), sib)
                fwd.start()
                passed.append(fwd)
        for l in range(nx):
            cp(l, 0, sib, me).wait_recv()
            for j, chip in enumerate(chips):
                cp(l, 4 + j, (*chip, 1 - c), me).wait_recv()
        for f in first + passed:
            f.wait_send()
        for m in mine:
            m.wait()

    anyspec = pl.BlockSpec(memory_space=pl.ANY)
    return pl.pallas_call(
        body, name=name, in_specs=[anyspec] * (nx + len(extra)), out_specs=[anyspec] * nx,
        out_shape=[jax.ShapeDtypeStruct((NDEV,) + a.shape, a.dtype) for a in xs],
        scratch_shapes=[pltpu.SemaphoreType.DMA((nx, 7)), pltpu.SemaphoreType.DMA((nx, 7)), pltpu.SemaphoreType.DMA((nx,))],
    )(*xs, *extra)


HBM_SPEC = pl.BlockSpec(memory_space=pltpu.HBM)
SEM_SPEC = pl.BlockSpec(memory_space=pltpu.SEMAPHORE)
ANY_SPEC = pl.BlockSpec(memory_space=pl.ANY)
EFFECT = pltpu.SideEffectType.DATAFLOW_SIDE_EFFECTING


def _peer(rel):
    x, y, c = _mesh_pos()
    return (1 - x if rel & 4 else x, 1 - y if rel & 2 else y, 1 - c if rel & 1 else c)


ALL_PEERS = (1, 2, 3, 4, 5, 6, 7)
SAME_CORE = (2, 4, 6)
CORE_AND_SIB = (1, 2, 4, 6)


def _slot(p, rels):
    return 2 * p[0] + p[1] if rels == SAME_CORE else _lin(p)


def _xchg_start(name, x, gather, after, rels=ALL_PEERS):
    me = _slot(_mesh_pos(), rels)
    shape = ((4 if rels == SAME_CORE else NDEV),) + x.shape if gather else x.shape
    own = x if gather else lax.dynamic_index_in_dim(x, me, 0, keepdims=False)
    land = lax.dynamic_update_index_in_dim(lax.empty(shape, x.dtype), own, me, 0)

    def body(x_ref, land_ref, after_ref, send, recv, x_thru, land_thru, token):
        mine = _slot(_mesh_pos(), rels)
        for k, rel in enumerate(rels):
            p = _peer(rel)
            pltpu.make_async_remote_copy(src_ref=x_ref if gather else x_ref.at[_slot(p, rels)], dst_ref=land_ref.at[mine],
                                         send_sem=send.at[k], recv_sem=recv.at[k], device_id=p, device_id_type=MESH).start()
        token[...] = jnp.zeros_like(token)

    return pl.pallas_call(
        body, name=name,
        out_shape=(pltpu.SemaphoreType.DMA((len(rels),)), pltpu.SemaphoreType.DMA((len(rels),)), pltpu.HBM(x.shape, x.dtype),
                   pltpu.HBM(shape, x.dtype), jax.ShapeDtypeStruct((8, LANES), f32)),
        in_specs=(HBM_SPEC, HBM_SPEC, ANY_SPEC), out_specs=(SEM_SPEC, SEM_SPEC, HBM_SPEC, HBM_SPEC, pl.BlockSpec(memory_space=pltpu.VMEM)),
        input_output_aliases={0: 2, 1: 3}, compiler_params=pltpu.CompilerParams(has_side_effects=EFFECT),
    )(pltpu.with_memory_space_constraint(x, pltpu.HBM), pltpu.with_memory_space_constraint(land, pltpu.HBM), after)


def _xchg_wait(name, started, gather, after, rels=ALL_PEERS):
    send, recv, x_thru, land_thru, _ = started

    def body(x_ref, land_ref, send, recv, after_ref, x_dead, got_ref):
        for k, rel in enumerate(rels):
            p = _peer(rel)
            cp = pltpu.make_async_remote_copy(src_ref=x_ref if gather else x_ref.at[_slot(p, rels)], dst_ref=land_ref.at[_slot(p, rels)],
                                              send_sem=send.at[k], recv_sem=recv.at[k], device_id=p, device_id_type=MESH)
            cp.wait_send()
            cp.wait_recv()

    return pl.pallas_call(
        body, name=name, out_shape=(pltpu.HBM(x_thru.shape, x_thru.dtype), pltpu.HBM(land_thru.shape, land_thru.dtype)),
        in_specs=(HBM_SPEC, HBM_SPEC, SEM_SPEC, SEM_SPEC, ANY_SPEC), out_specs=(HBM_SPEC, HBM_SPEC), input_output_aliases={0: 0, 1: 1},
        compiler_params=pltpu.CompilerParams(has_side_effects=EFFECT),
    )(x_thru, land_thru, send, recv, after)[1]


def _sib_forward(name, land):
    def body(land_ref, out_ref, send, recv):
        mx, my, mc = _mesh_pos()
        sib = (mx, my, 1 - mc)

        def cp(k, block):
            return pltpu.make_async_remote_copy(src_ref=out_ref.at[_lin(block)], dst_ref=out_ref.at[_lin(block)], send_sem=send.at[k],
                                                recv_sem=recv.at[k], device_id=sib, device_id_type=MESH)

        chips = [_peer(rel)[:2] for rel in SAME_CORE]
        sends = [cp(k, (*chip, mc)) for k, chip in enumerate(chips)]
        for s in sends:
            s.start()
        for k, chip in enumerate(chips):
            cp(k, (*chip, 1 - mc)).wait_recv()
        for s in sends:
            s.wait_send()

    return pl.pallas_call(
        body, name=name, in_specs=[ANY_SPEC], out_specs=ANY_SPEC, out_shape=jax.ShapeDtypeStruct(land.shape, land.dtype),
        input_output_aliases={0: 0}, scratch_shapes=[pltpu.SemaphoreType.DMA((3,)), pltpu.SemaphoreType.DMA((3,))],
    )(land)


def _sib_swap(name, x):
    def body(x_ref, out_ref, send, recv):
        mx, my, mc = _mesh_pos()
        sib = (mx, my, 1 - mc)
        cps = [pltpu.make_async_remote_copy(src_ref=x_ref.at[2 * q + 1 - mc], dst_ref=out_ref.at[q], send_sem=send.at[q], recv_sem=recv.at[q],
                                            device_id=sib, device_id_type=MESH) for q in range(4)]
        for cp in cps:
            cp.start()
        for cp in cps:
            cp.wait_recv()
        for cp in cps:
            cp.wait_send()

    return pl.pallas_call(
        body, name=name, in_specs=[ANY_SPEC], out_specs=ANY_SPEC, out_shape=jax.ShapeDtypeStruct((4,) + x.shape[1:], x.dtype),
        scratch_shapes=[pltpu.SemaphoreType.DMA((4,)), pltpu.SemaphoreType.DMA((4,))],
    )(x)


def _behind(token, a):
    return a + token[0:1, 0:1].astype(a.dtype)


def _sum8(name, g):
    def body(g_ref, o_ref):
        acc = g_ref[0:1, :]
        for p in range(1, NDEV):
            acc = acc + g_ref[p:p + 1, :]
        o_ref[...] = acc

    return pl.pallas_call(body, name=name, out_shape=jax.ShapeDtypeStruct((1, g.shape[1]), f32),
                          compiler_params=pltpu.CompilerParams(vmem_limit_bytes=VMEM_LIMIT))(g)


def _pack(arrs):
    flat = jnp.concatenate([a.reshape(-1).astype(f32) for a in arrs])
    pad = (-flat.shape[0]) % (8 * LANES)
    return jnp.pad(flat, (0, pad)).reshape(1, -1)


def _unpack(vec, shapes):
    out, off = [], 0
    flat = vec.reshape(-1)
    for s in shapes:
        n = 1
        for d in s:
            n *= d
        out.append(flat[off:off + n].reshape(s))
        off += n
    return out


def _adam(name, w, m, v, g, parts, after=None):
    extra = [] if after is None else [after]
    nrows, width = w.shape
    tr = _row_tile(nrows, width) if nrows >= 16 else nrows
    blk = pl.BlockSpec((tr, width), lambda i: (i, 0))
    gblk = pl.BlockSpec((parts, tr, width), lambda i: (0, i, 0)) if parts else blk
    c1 = 1.0 / (1.0 - ADAM_B1 ** ADAM_STEP)
    c2 = 1.0 / (1.0 - ADAM_B2 ** ADAM_STEP)

    def body(w_ref, m_ref, v_ref, g_ref, *rest):
        go_ref, d_ref, mo_ref, vo_ref = rest[len(extra):]
        if parts:
            gg = g_ref[0].astype(f32)
            for p in range(1, parts):
                gg = gg + g_ref[p].astype(f32)
        else:
            gg = g_ref[...]
        mn = ADAM_B1 * m_ref[...] + (1.0 - ADAM_B1) * gg
        vn = ADAM_B2 * v_ref[...] + (1.0 - ADAM_B2) * (gg * gg)
        go_ref[...] = gg
        mo_ref[...] = mn
        vo_ref[...] = vn
        d_ref[...] = -ADAM_LR * ((mn * c1) / (jnp.sqrt(vn * c2) + ADAM_EPS) + ADAM_WD * w_ref[...])

    sh = jax.ShapeDtypeStruct((nrows, width), f32)
    return pl.pallas_call(body, name=name, grid=(nrows // tr,), in_specs=[blk, blk, blk, gblk] + [pl.BlockSpec(memory_space=pl.ANY)] * len(extra),
                          out_specs=[blk] * 4, out_shape=[sh] * 4, compiler_params=_cparams(("parallel",)))(w, m, v, g, *extra)


def kernel(x, c, ctx, c_ctx, ln_in_g, ln_in_b, w_mod, b_mod, w_in, w_qkv_conv, a_log_f, dt_bias_f, a_log_b, dt_bias_b, dn_norm_g, conf_dw_w, conf_dw_b, conf_ln_g, conf_ln_b, w_out, ln1_g, ln1_b, w_mlp1, b_mlp1, w_mlp2, b_mlp2, ln2_g, ln2_b, loss_target, m_c_ctx, m_ln_in_g, m_ln_in_b, m_w_mod, m_b_mod, m_w_in, m_w_qkv_conv, m_a_log_f, m_dt_bias_f, m_a_log_b, m_dt_bias_b, m_dn_norm_g, m_conf_dw_w, m_conf_dw_b, m_conf_ln_g, m_conf_ln_b, m_w_out, m_ln1_g, m_ln1_b, m_w_mlp1, m_b_mlp1, m_w_mlp2, m_b_mlp2, m_ln2_g, m_ln2_b, v_c_ctx, v_ln_in_g, v_ln_in_b, v_w_mod, v_b_mod, v_w_in, v_w_qkv_conv, v_a_log_f, v_dt_bias_f, v_a_log_b, v_dt_bias_b, v_dn_norm_g, v_conf_dw_w, v_conf_dw_b, v_conf_ln_g, v_conf_ln_b, v_w_out, v_ln1_g, v_ln1_b, v_w_mlp1, v_b_mlp1, v_w_mlp2, v_b_mlp2, v_ln2_g, v_ln2_b):
    weights = dict(c_ctx=c_ctx, ln_in_g=ln_in_g, ln_in_b=ln_in_b, w_mod=w_mod, b_mod=b_mod, w_in=w_in, w_qkv_conv=w_qkv_conv, a_log_f=a_log_f, dt_bias_f=dt_bias_f, a_log_b=a_log_b, dt_bias_b=dt_bias_b, dn_norm_g=dn_norm_g, conf_dw_w=conf_dw_w, conf_dw_b=conf_dw_b, conf_ln_g=conf_ln_g, conf_ln_b=conf_ln_b, w_out=w_out, ln1_g=ln1_g, ln1_b=ln1_b, w_mlp1=w_mlp1, b_mlp1=b_mlp1, w_mlp2=w_mlp2, b_mlp2=b_mlp2, ln2_g=ln2_g, ln2_b=ln2_b)
    mom1 = dict(c_ctx=m_c_ctx, ln_in_g=m_ln_in_g, ln_in_b=m_ln_in_b, w_mod=m_w_mod, b_mod=m_b_mod, w_in=m_w_in, w_qkv_conv=m_w_qkv_conv, a_log_f=m_a_log_f, dt_bias_f=m_dt_bias_f, a_log_b=m_a_log_b, dt_bias_b=m_dt_bias_b, dn_norm_g=m_dn_norm_g, conf_dw_w=m_conf_dw_w, conf_dw_b=m_conf_dw_b, conf_ln_g=m_conf_ln_g, conf_ln_b=m_conf_ln_b, w_out=m_w_out, ln1_g=m_ln1_g, ln1_b=m_ln1_b, w_mlp1=m_w_mlp1, b_mlp1=m_b_mlp1, w_mlp2=m_w_mlp2, b_mlp2=m_b_mlp2, ln2_g=m_ln2_g, ln2_b=m_ln2_b)
    mom2 = dict(c_ctx=v_c_ctx, ln_in_g=v_ln_in_g, ln_in_b=v_ln_in_b, w_mod=v_w_mod, b_mod=v_b_mod, w_in=v_w_in, w_qkv_conv=v_w_qkv_conv, a_log_f=v_a_log_f, dt_bias_f=v_dt_bias_f, a_log_b=v_a_log_b, dt_bias_b=v_dt_bias_b, dn_norm_g=v_dn_norm_g, conf_dw_w=v_conf_dw_w, conf_dw_b=v_conf_dw_b, conf_ln_g=v_conf_ln_g, conf_ln_b=v_conf_ln_b, w_out=v_w_out, ln1_g=v_ln1_g, ln1_b=v_ln1_b, w_mlp1=v_w_mlp1, b_mlp1=v_b_mlp1, w_mlp2=v_w_mlp2, b_mlp2=v_b_mlp2, ln2_g=v_ln2_g, ln2_b=v_ln2_b)
    names = list(weights)

    me = _lin(_mesh_pos())
    S, D = x.shape[1], x.shape[2]
    T = ctx.shape[1]
    R = S + T
    DN = D // 2
    NH = DN // HD
    CONF = D - DN
    K7, K31 = w_qkv_conv.shape[1], conf_dw_w.shape[1]
    DFF = w_mlp1.shape[2] * NDEV
    INC = w_in.shape[2] * NDEV
    CONF_OFF = 4 * DN + 4 * NH
    NC = 4 * DN + 2 * CONF + GPAD
    QB, ZB, VB, GB = 0, 3 * DN // LANES, 4 * DN // LANES, (4 * DN + 2 * CONF) // LANES
    MODC = w_mod.shape[2]
    x2d, ctx2d, tgt = x[0], ctx[0], loss_target[0]
    row = lambda a: a.reshape(1, -1).astype(f32)

    c_all, w7_all, w31_all = _all_gather("ag_small", [c.astype(f32), w_qkv_conv[0], conf_dw_w[0]])
    w7 = jnp.pad(jnp.transpose(w7_all, (1, 0, 2)).reshape(K7, 3 * DN), ((0, (-K7) % 8), (0, 0)))
    w31 = jnp.pad(jnp.transpose(w31_all, (1, 0, 2)).reshape(K31, CONF), ((0, (-K31) % 8), (0, 0)))
    c16 = jnp.concatenate([c_all.reshape(NDEV, D), c_ctx.reshape(1, D).astype(f32), jnp.zeros((7, D), f32)], axis=0)
    (sc16,) = _rowcall("silu_c", lambda j, a: (_silu(a),), 16, 16, [(c16, D, 0, 0, False)], [], [(D, f32, False)], [])
    bmod_mine = lax.dynamic_slice_in_dim(b_mod.astype(f32), me * MODC, MODC, axis=1)
    mod_part = _mm("mod_fwd", sc16, w_mod[0], "nn", f32, extras=[("row", bmod_mine)], epilogue=lambda r, b: (r + b,))
    (mod_g,) = _all_gather("ag_mod", [mod_part])
    mod_all = jnp.transpose(mod_g, (1, 0, 2)).reshape(16, 6 * D)
    mod_me = lax.dynamic_slice_in_dim(mod_all, me, 1, axis=0)
    sh_a, sc_a, g_a, sh_m, sc_m, g_m = [mod_me[:, i * D:(i + 1) * D] for i in range(6)]
    csh_a, csc_a = mod_all[8:9, 0:D], mod_all[8:9, D:2 * D]

    (g_win,) = _all_gather("ag_w_in", [w_in[0].astype(bf16)], after=mod_g)
    ag_wout = _xchg_start("ag_wout_start", w_out[0].astype(bf16), True, g_win)
    ag_w1 = _xchg_start("ag_w1_start", w_mlp1[0].astype(bf16), True, ag_wout[4])
    win_full = jnp.transpose(g_win, (1, 0, 2)).reshape(D, INC)
    w_cat = jnp.concatenate([win_full[:, :4 * DN], win_full[:, CONF_OFF:], win_full[:, 4 * DN:CONF_OFF],
                             jnp.zeros((D, GPAD - 4 * NH), bf16)], axis=1)
    g0, b0 = row(ln_in_g), row(ln_in_b)
    g0_fwd = _behind(ag_w1[4], g0)

    def ln_mod(j, xt, g, b, sh, sc):
        xh, _ = _ln_stats(xt)
        xn = xh * g + b
        return xn, xn * (1.0 + sc) + sh

    tr = _row_tile(T, D)
    vec = lambda a: (a, a.shape[1], 0, False)
    xn, xm = _rowcall("ln_in_lat", ln_mod, S, tr, [(x2d, D, 0, 0, False)], [vec(g0_fwd), vec(b0), vec(sh_a), vec(sc_a)], [(D, f32, False), (D, bf16, False)], [])
    xcn, xcm = _rowcall("ln_in_ctx", ln_mod, T, tr, [(ctx2d, D, 0, 0, False)], [vec(g0), vec(b0), vec(csh_a), vec(csc_a)], [(D, f32, False), (D, bf16, False)], [])
    xm_all = jnp.concatenate([xm, xcm], axis=0)

    h_all = _mm("in_proj", xm_all, w_cat, "nn", f32, tm=1088, tn=1280)

    qkv = _qkv_conv_fwd(h_all, w7, R, S, NH, K7).reshape(3, NH, R, HD)
    lane = jnp.arange(LANES)
    is_a = ((lane < 4 * NH) & ((lane // NH) % 2 == 0)).astype(f32).reshape(1, LANES)
    pad_g = lambda a, b: jnp.concatenate([a.reshape(-1), jnp.zeros((NH,), f32), b.reshape(-1), jnp.zeros((LANES - 3 * NH,), f32)]).reshape(1, LANES)
    neg_a = pad_g(-jnp.exp(a_log_f.astype(f32)), -jnp.exp(a_log_b.astype(f32)))
    dt_v = pad_g(dt_bias_f.astype(f32), dt_bias_b.astype(f32))

    def gates_f(j, hg, isa, na, dt):
        return (jnp.where(isa > 0.5, na * _softplus(hg + dt), _sigmoid(hg)),)

    trg = _row_tile(T, LANES)
    ag_w2 = _xchg_start("ag_w2_start", w_mlp2[0].astype(bf16), True, qkv, CORE_AND_SIB)
    (gates,) = _rowcall("gates_fwd", gates_f, R, trg, [(h_all, LANES, 0, GB, False)], [vec(_behind(ag_w2[4], is_a)), vec(neg_a), vec(dt_v)], [(LANES, f32, False)], [])
    gpad = jnp.zeros((R, LANES - 2 * NH), f32)
    gd = jnp.stack([jnp.concatenate([gates[:, :2 * NH], gpad], axis=1), jnp.concatenate([gates[:, 2 * NH:4 * NH], gpad], axis=1)])

    o5, states, *solved = _gdn_fwd(qkv, gd, NH, S, T)
    dng = row(dn_norm_g)
    trh = _row_tile(S, HD, cap=1024)
    dn_out = _gate_norm_fwd(o5, h_all, dng, NH, S, ZB, trh)

    yconv = _conf_conv_fwd(h_all, w31, S, CONF, VB, K31)
    bdw, clg, clb = row(conf_dw_b), row(conf_ln_g), row(conf_ln_b)

    def conf_ln(j, yc, b, g, bb):
        xh, _ = _ln_stats(yc + b)
        return (_silu(xh * g + bb),)

    trc = _row_tile(S, CONF)
    (conf_out,) = _rowcall("conf_ln_fwd", conf_ln, S, trc, [(yconv, CONF, 0, 0, False)], [vec(bdw), vec(clg), vec(clb)], [(CONF, bf16, False)], [])
    mix = jnp.concatenate([dn_out, conf_out], axis=1)
    wout_full = _xchg_wait("ag_wout_wait", ag_wout, True, mix).reshape(DN + CONF, D)
    y = _mm("out_proj", mix, wout_full, "nn", f32)

    l1g, l1b, l2g, l2b = row(ln1_g), row(ln1_b), row(ln2_g), row(ln2_b)

    def ln1_mod(j, xnt, yt, ga, g, b, sh, sc):
        xh, _ = _ln_stats(ALPHA * xnt + ga * yt)
        x1 = xh * g + b
        return x1, x1 * (1.0 + sc) + sh

    trd = _row_tile(S, D)
    x1, u = _rowcall("ln1_fwd", ln1_mod, S, trd, [(xn, D, 0, 0, False), (y, D, 0, 0, False)],
                     [vec(g_a), vec(l1g), vec(l1b), vec(sh_m), vec(sc_m)], [(D, f32, False), (D, bf16, False)], [])
    b1, b2 = row(b_mlp1), row(b_mlp2)
    g_w1 = _xchg_wait("ag_w1_wait", ag_w1, True, u)
    hh, act = _mm("mlp1", u, g_w1, "nn", (f32, bf16), extras=[("row", b1)], epilogue=lambda r, b: (r, jnp.square(jnp.maximum(r + b, 0.0))))
    w2_full = _sib_forward("ag_w2_fwd", _xchg_wait("ag_w2_wait", ag_w2, True, act, CORE_AND_SIB)).reshape(DFF, D)
    y2 = _mm("mlp2", act, w2_full, "nn", f32)

    def ln2_loss(j, x1t, y2t, tg, bb2, gm, g, b):
        y2b = y2t + bb2
        xh, rstd = _ln_stats(ALPHA * x1t + gm * y2b)
        err = xh * g + b - tg
        dx2 = err * (1.0 / D)
        dr2 = _ln_bwd(dx2, xh, rstd, g)
        dy2 = dr2 * gm
        lsum = jnp.broadcast_to(jnp.sum(err * err).reshape(1, 1), (1, LANES))
        return dr2, dy2, _csum(dx2 * xh), _csum(dx2), _csum(dr2 * y2b), _csum(dy2), lsum

    dr2, dy2, d_l2g, d_l2b, d_gm, d_b2, lsum = _rowcall(
        "ln2_loss", ln2_loss, S, trd, [(x1, D, 0, 0, False), (y2, D, 0, 0, False), (tgt, D, 0, 0, False)],
        [vec(b2), vec(g_m), vec(l2g), vec(l2b)], [(D, f32, False), (D, bf16, False)], [(1, D, False)] * 4 + [(1, LANES, False)])
    loss = lax.psum(0.5 * lsum[0, 0] / D, ("x", "y", "c"))

    dw2_p = _mm("mlp2_dw", act, dy2, "tn", bf16)
    a2a_w2 = _xchg_start("a2a_w2_start", dw2_p.reshape(NDEV, DFF // NDEV, D), False, b2)
    dhh = _mm("mlp2_dx", dy2, w2_full, "nt", bf16, extras=[("tile", hh), ("row", _behind(a2a_w2[4], b1))],
              epilogue=lambda r, h, b: (r * (2.0 * jnp.maximum(h + b, 0.0)),))
    (d_b1,) = _rowcall("b1_grad", lambda j, a: (_csum(a.astype(f32)),), S, _row_tile(S, DFF // 8), [(dhh, DFF, 0, 0, False)], [], [], [(1, DFF, False)])
    dw1_p = _mm("mlp1_dw", u, dhh, "tn", bf16, out_split=NDEV)
    a2a_w1 = _xchg_start("a2a_w1_start", dw1_p, False, d_b1)
    du = _mm("mlp1_dx", dhh, g_w1, "nt", f32)

    def ln1_bwd(j, dr2t, dut, xnt, yt, ga, g, b, sc):
        xh, rstd = _ln_stats(ALPHA * xnt + ga * yt)
        x1t = xh * g + b
        dx1 = ALPHA * dr2t + dut * (1.0 + sc)
        dr1 = _ln_bwd(dx1, xh, rstd, g)
        return dr1, dr1 * ga, _csum(dut * x1t), _csum(dut), _csum(dx1 * xh), _csum(dx1), _csum(dr1 * yt)

    dr1, dy, d_scm, d_shm, d_l1g, d_l1b, d_ga = _rowcall(
        "ln1_bwd", ln1_bwd, S, trd, [(dr2, D, 0, 0, False), (du, D, 0, 0, False), (xn, D, 0, 0, False), (y, D, 0, 0, False)],
        [vec(g_a), vec(l1g), vec(l1b), vec(_behind(a2a_w1[4], sc_m))], [(D, f32, False), (D, bf16, False)], [(1, D, False)] * 5)
    dwout_p = _mm("out_proj_dw", mix, dy, "tn", bf16)
    a2a_wout = _xchg_start("a2a_wout_start", dwout_p.reshape(NDEV, (DN + CONF) // NDEV, D), False, d_l1g)
    dmix = _mm("out_proj_dx", dy, wout_full, "nt", f32)

    def conf_ln_b(j, dm, yc, b, g, bb):
        xh, rstd = _ln_stats(yc + b)
        dln = dm * _dsilu(xh * g + bb)
        dyc = _ln_bwd(dln, xh, rstd, g)
        return dyc, _csum(dln * xh), _csum(dln), _csum(dyc)

    dyc, d_clg, d_clb, d_bdw = _rowcall("conf_ln_bwd", conf_ln_b, S, trc, [(dmix, CONF, 0, DN // CONF, False), (yconv, CONF, 0, 0, False)],
                                        [vec(_behind(a2a_wout[4], bdw)), vec(clg), vec(clb)], [(CONF, f32, False)], [(1, CONF, False)] * 3)
    dval, dgate, dw31 = _conf_conv_bwd(h_all, w31, dyc, S, CONF, VB, K31)

    do, dz, d_dng = _gate_norm_bwd(dmix, o5, h_all, dng, NH, S, ZB, trh)
    dqkv, dgd = _gdn_bwd(qkv, gd, states, solved, do, NH, S, T)
    dh_qkv, dw7 = _qkv_conv_bwd(h_all, w7, dqkv.reshape(2, 3 * NH, R, HD), R, S, NH, K7)
    dgate_cols = jnp.concatenate([dgd[0][:, :2 * NH], dgd[1][:, :2 * NH], jnp.zeros((R, LANES - 4 * NH), f32)], axis=1)

    def gates_b(j, hg, dgt, isa, na, dt):
        sg = _sigmoid(hg)
        sp = _softplus(hg + dt)
        dpre = jnp.where(isa > 0.5, dgt * na * _sigmoid(hg + dt), dgt * sg * (1.0 - sg))
        return dpre, _csum(jnp.where(isa > 0.5, dgt * na * sp, 0.0)), _csum(jnp.where(isa > 0.5, dpre, 0.0))

    dh_g, d_alog, d_dt = _rowcall("gates_bwd", gates_b, R, trg, [(h_all, LANES, 0, GB, False), (dgate_cols, LANES, 0, 0, False)],
                                  [vec(is_a), vec(neg_a), vec(dt_v)], [(LANES, bf16, False)], [(1, LANES, False)] * 2)

    zpad = lambda a: jnp.concatenate([a, jnp.zeros((T, a.shape[1]), bf16)], axis=0)
    dh_all = jnp.concatenate([dh_qkv, zpad(dz), zpad(dval), zpad(dgate), dh_g, jnp.zeros((R, GPAD - LANES), bf16)], axis=1)
    dxm_all = _mm("in_proj_dx", dh_all, w_cat, "nt", f32, tm=1088, tk=2560)

    def ln_in_bwd_lat(j, xt, dr1t, dxm, g, b, sc):
        xh, rstd = _ln_stats(xt)
        xnt = xh * g + b
        dxn = ALPHA * dr1t + dxm * (1.0 + sc)
        return _ln_bwd(dxn, xh, rstd, g), _csum(dxm * xnt), _csum(dxm), _csum(dxn * xh), _csum(dxn)

    def ln_in_bwd_ctx(j, xt, dxm, g, b, sc):
        xh, rstd = _ln_stats(xt)
        xnt = xh * g + b
        dxn = dxm * (1.0 + sc)
        return _csum(dxm * xnt), _csum(dxm), _csum(dxn * xh), _csum(dxn)

    grad_x, d_sca, d_sha, d_g0a, d_b0a = _rowcall(
        "ln_in_bwd_lat", ln_in_bwd_lat, S, trd, [(x2d, D, 0, 0, False), (dr1, D, 0, 0, False), (dxm_all, D, 0, 0, False)],
        [vec(g0), vec(b0), vec(sc_a)], [(D, f32, False)], [(1, D, False)] * 4)
    d_csca, d_csha, d_g0b, d_b0b = _rowcall(
        "ln_in_bwd_ctx", ln_in_bwd_ctx, T, tr, [(ctx2d, D, 0, 0, False), (dxm_all, D, S // tr, 0, False)],
        [vec(g0), vec(b0), vec(csc_a)], [], [(1, D, False)] * 4)

    zD = jnp.zeros((1, D), f32)
    dmod_me = jnp.concatenate([d_sha, d_sca, d_ga, d_shm, d_scm, d_gm], axis=1)
    dmodc_me = jnp.concatenate([d_csha, d_csca, zD, zD, zD, zD], axis=1)
    small_names = ["ln_in_g", "ln_in_b", "a_log_f", "dt_bias_f", "a_log_b", "dt_bias_b", "dn_norm_g", "conf_dw_b", "conf_ln_g", "conf_ln_b",
                   "ln1_g", "ln1_b", "b_mlp1", "b_mlp2", "ln2_g", "ln2_b"]
    small_parts = [d_g0a + d_g0b, d_b0a + d_b0b, d_alog[:, 0:NH], d_dt[:, 0:NH], d_alog[:, 2 * NH:3 * NH], d_dt[:, 2 * NH:3 * NH], d_dng, d_bdw, d_clg, d_clb,
                   d_l1g, d_l1b, d_b1, d_b2, d_l2g, d_l2b]
    conv_parts = [dw7[:K7], dw31[:K31]]
    packed = _pack(small_parts + conv_parts + [dmodc_me])
    (pk_all, dmod_all) = _all_gather("ag_small_grads", [packed, dmod_me])
    summed = _sum8("sum_small_grads", pk_all.reshape(NDEV, -1))
    parts_sh = [a.shape for a in small_parts + conv_parts + [dmodc_me]]
    un = _unpack(summed, parts_sh)
    gsmall = dict(zip(small_names, un[:len(small_names)]))
    g_w7_full, g_w31_full, dmodc = un[len(small_names):]
    gsmall["w_qkv_conv"] = lax.dynamic_slice_in_dim(g_w7_full, me * w_qkv_conv.shape[2], w_qkv_conv.shape[2], axis=1)
    gsmall["conf_dw_w"] = lax.dynamic_slice_in_dim(g_w31_full, me * conf_dw_w.shape[2], conf_dw_w.shape[2], axis=1)

    dm16 = jnp.concatenate([dmod_all.reshape(NDEV, 6 * D), dmodc, jnp.zeros((7, 6 * D), f32)], axis=0)
    (gbmod,) = _rowcall("bmod_grad", lambda j, a: (_csum(a),), 16, 16, [(dm16, 6 * D, 0, 0, False)], [], [], [(1, 6 * D, False)])
    gsmall["b_mod"] = gbmod
    dm16_mine = lax.dynamic_slice_in_dim(dm16, me * MODC, MODC, axis=1)
    g_wmod = _mm("mod_dw", sc16, dm16_mine, "tn", f32)
    dsc16_part = _mm("mod_dx", dm16_mine, w_mod[0], "nt", f32)
    (dsc_all,) = _all_gather("ag_cctx", [dsc16_part[8:9]])
    dsilu_cctx = _sum8("sum_cctx", dsc_all.reshape(NDEV, D))
    (g_cctx,) = _rowcall("cctx_grad", lambda j, ds, cc: (ds * _dsilu(cc),), 1, 1, [(dsilu_cctx, D, 0, 0, False), (c_ctx.reshape(1, D).astype(f32), D, 0, 0, False)],
                         [], [(D, f32, False)], [])
    gsmall["c_ctx"] = g_cctx

    dwcat_p = _mm("in_proj_dw", xm_all, dh_all, "tn", bf16, tn=1280, tk=4352, after=g_cctx)
    dwin = jnp.concatenate([dwcat_p[:, :4 * DN], dwcat_p[:, 4 * DN + 2 * CONF:4 * DN + 2 * CONF + 4 * NH], dwcat_p[:, 4 * DN:4 * DN + 2 * CONF]], axis=1)
    dwin_p = jnp.transpose(dwin.reshape(D, NDEV, INC // NDEV), (1, 0, 2))
    from_sib = _sib_swap("sib_win", dwin_p)
    mine4 = lax.dynamic_index_in_dim(dwin_p.reshape(4, 2, D, INC // NDEV), lax.axis_index("c"), 1, keepdims=False)
    (chip_sum,) = _rowcall("add_win", lambda j, a, b: (a.astype(f32) + b.astype(f32),), 4 * D, _row_tile(4 * D, INC // NDEV),
                           [(mine4.reshape(4 * D, INC // NDEV), INC // NDEV, 0, 0, False), (from_sib.reshape(4 * D, INC // NDEV), INC // NDEV, 0, 0, False)],
                           [], [(INC // NDEV, bf16, False)], [])
    a2a_win = _xchg_start("a2a_win_start", chip_sum.reshape(4, D, INC // NDEV), False, g_cctx, SAME_CORE)

    grads, deltas, new_m, new_v = {}, {}, {}, {}
    res = _adam("adam_w_mod", w_mod[0], m_w_mod[0], v_w_mod[0], g_wmod, 0, after=a2a_win[4])
    grads["w_mod"], deltas["w_mod"], new_m["w_mod"], new_v["w_mod"] = [a.reshape(w_mod.shape) for a in res]
    after = res[1]
    for nm, started in (("w_mlp2", a2a_w2), ("w_mlp1", a2a_w1), ("w_out", a2a_wout), ("w_in", a2a_win)):
        w3 = weights[nm]
        rels = SAME_CORE if nm == "w_in" else ALL_PEERS
        g = _xchg_wait("a2a_" + nm + "_wait", started, False, after, rels)
        res = _adam("adam_" + nm, w3[0], mom1[nm][0], mom2[nm][0], g, len(rels) + 1)
        grads[nm], deltas[nm], new_m[nm], new_v[nm] = [a.reshape(w3.shape) for a in res]
        after = res[1]
    snames = [n for n in names if n not in grads]
    res = _adam("adam_small", _pack([weights[n] for n in snames]), _pack([mom1[n] for n in snames]), _pack([mom2[n] for n in snames]),
                _pack([gsmall[n] for n in snames]), 0)
    shapes = [weights[n].shape for n in snames]
    for dst, packed_out in zip((grads, deltas, new_m, new_v), res):
        for n, a in zip(snames, _unpack(packed_out, shapes)):
            dst[n] = a

    return (loss, grad_x.reshape(x.shape), *[grads[n] for n in names], *[deltas[n] for n in names],
            *[new_m[n] for n in names], *[new_v[n] for n in names])
```

```python
import functools

import jax
import jax.numpy as jnp
from jax import lax
from jax.experimental import pallas as pl
from jax.experimental.pallas import tpu as pltpu

f32 = jnp.float32
bf16 = jnp.bfloat16
MESH = pl.DeviceIdType.MESH

NDEV = 8
HD = 128
CH = 64
GW = 64
LANES = 128
GPAD = 512
LN_EPS = 1e-5
ALPHA = 2.0 ** 0.25
ADAM_LR, ADAM_B1, ADAM_B2, ADAM_EPS, ADAM_WD, ADAM_STEP = 0.001, 0.9, 0.999, 1e-08, 0.01, 10
VMEM_LIMIT = 60 * 1024 * 1024
ROW_BLOCK_BYTES = 1 << 21


def _cparams(sem):
    return pltpu.CompilerParams(dimension_semantics=sem, vmem_limit_bytes=VMEM_LIMIT)


def _tile(dim, pref, align):
    t = min(pref, dim)
    t -= t % align
    while t >= align:
        if dim % t == 0:
            return t
        t -= align
    return dim


def _row_tile(nrows, width, cap=256):
    t = max(16, min(cap, ROW_BLOCK_BYTES // (4 * width)))
    t = 1 << (t.bit_length() - 1)
    while nrows % t:
        t //= 2
    return max(t, 1)


def _sigmoid(x):
    return 1.0 / (1.0 + jnp.exp(-x))


def _silu(x):
    return x * _sigmoid(x)


def _dsilu(x):
    s = _sigmoid(x)
    return s * (1.0 + x * (1.0 - s))


def _softplus(x):
    return jnp.maximum(x, 0.0) + jnp.log(1.0 + jnp.exp(-jnp.abs(x)))


def _ln_stats(r):
    mu = jnp.mean(r, axis=-1, keepdims=True)
    xc = r - mu
    rstd = lax.rsqrt(jnp.mean(xc * xc, axis=-1, keepdims=True) + LN_EPS)
    return xc * rstd, rstd


def _ln_bwd(dy, xhat, rstd, g):
    dxh = dy * g
    return rstd * (dxh - jnp.mean(dxh, axis=-1, keepdims=True) - xhat * jnp.mean(dxh * xhat, axis=-1, keepdims=True))


def _csum(a):
    return jnp.sum(a, axis=0, keepdims=True)


def _mm(name, a, b, mode, out_dtype, out_split=1, tm=1024, tn=1024, tk=4096, extras=(), epilogue=None, after=None):
    if mode == "tn":
        K, M = a.shape
    else:
        M, K = a.shape
    bp = b.shape[0] if b.ndim == 3 else 1
    brows, bcols = b.shape[-2], b.shape[-1] * bp
    N = brows if mode == "nt" else bcols
    assert K == (bcols if mode == "nt" else brows), (name, a.shape, b.shape)
    tm = _tile(M, tm, LANES if mode == "tn" else 16)
    nsplit = max(out_split, bp if mode != "nt" else 1)
    tn = _tile(N // nsplit, tn, LANES)
    kgroup = 1
    if mode == "nt" and bp > 1 and tk >= 2 * (K // bp):
        kgroup = min(tk // (K // bp), bp)
        tk = kgroup * (K // bp)
    else:
        tk = _tile(K // (bp if mode == "nt" else 1), tk, LANES)
    nk = K // tk
    dims = {"nn": (((1,), (0,)), ((), ())), "nt": (((1,), (1,)), ((), ())), "tn": (((0,), (0,)), ((), ()))}[mode]
    out_dtypes = out_dtype if isinstance(out_dtype, tuple) else (out_dtype,)

    a_spec = pl.BlockSpec((tk, tm), lambda i, j, k: (k, i)) if mode == "tn" else pl.BlockSpec((tm, tk), lambda i, j, k: (i, k))
    if b.ndim == 3:
        if mode == "nt" and kgroup > 1:
            b_spec = pl.BlockSpec((kgroup, tn, K // bp), lambda i, j, k: (k, j, 0))
        elif mode == "nt":
            per = (K // bp) // tk
            b_spec = pl.BlockSpec((None, tn, tk), lambda i, j, k: (k // per, j, k % per))
        else:
            per = (N // bp) // tn
            b_spec = pl.BlockSpec((None, tk, tn), lambda i, j, k: (j // per, k, j % per))
    elif mode == "nt":
        b_spec = pl.BlockSpec((tn, tk), lambda i, j, k: (j, k))
    else:
        b_spec = pl.BlockSpec((tk, tn), lambda i, j, k: (k, j))
    if out_split > 1:
        pero = (N // out_split) // tn
        o_spec = pl.BlockSpec((None, tm, tn), lambda i, j, k: (j // pero, i, j % pero))
        o_shapes = [jax.ShapeDtypeStruct((out_split, M, N // out_split), dt) for dt in out_dtypes]
    else:
        o_spec = pl.BlockSpec((tm, tn), lambda i, j, k: (i, j))
        o_shapes = [jax.ShapeDtypeStruct((M, N), dt) for dt in out_dtypes]
    in_specs = [a_spec, b_spec]
    args = [a, b]
    for kind, arr in extras:
        in_specs.append(pl.BlockSpec((1, tn), lambda i, j, k: (0, j)) if kind == "row" else pl.BlockSpec((tm, tn), lambda i, j, k: (i, j)))
        args.append(arr)
    n_in, n_out = len(args), len(out_dtypes)
    if after is not None:
        in_specs.append(pl.BlockSpec(memory_space=pl.ANY))
        args.append(after)
    n_all = len(args)

    def finish(refs, r):
        outs = (r,) if epilogue is None else epilogue(r, *[e[...] for e in refs[2:n_in]])
        for o_ref, val in zip(refs[n_all:n_all + n_out], outs):
            o_ref[...] = val.astype(o_ref.dtype)

    def body(*refs):
        if kgroup > 1:
            kp = K // bp
            part = lax.dot_general(refs[0][:, 0:kp].astype(bf16), refs[1][0].astype(bf16), dims, preferred_element_type=f32)
            for gi in range(1, kgroup):
                part = part + lax.dot_general(refs[0][:, gi * kp:(gi + 1) * kp].astype(bf16), refs[1][gi].astype(bf16), dims, preferred_element_type=f32)
        else:
            part = lax.dot_general(refs[0][...].astype(bf16), refs[1][...].astype(bf16), dims, preferred_element_type=f32)
        if nk == 1:
            finish(refs, part)
            return
        acc = refs[-1]
        k = pl.program_id(2)

        @pl.when(k == 0)
        def _():
            acc[...] = part

        @pl.when(jnp.logical_and(k > 0, k < nk - 1))
        def _():
            acc[...] += part

        @pl.when(k == nk - 1)
        def _():
            finish(refs, acc[...] + part)

    res = pl.pallas_call(
        body, name=name, grid=(M // tm, N // tn, nk), in_specs=in_specs, out_specs=[o_spec] * n_out, out_shape=o_shapes,
        scratch_shapes=[pltpu.VMEM((tm, tn), f32)] if nk > 1 else [], compiler_params=_cparams(("parallel", "parallel", "arbitrary")),
    )(*args)
    return res[0] if n_out == 1 else res


def _rowcall(name, fn, nrows, tr, rows_in, vecs_in, rows_out, accs_out, ncol=1):
    nrt = nrows // tr
    in_specs, args = [], []
    for arr, w, ro, co, pc in rows_in:
        in_specs.append(pl.BlockSpec((tr, w), functools.partial(lambda j, i, ro, co, pc: (i + ro, co + (j if pc else 0)), ro=ro, co=co, pc=pc)))
        args.append(arr)
    for arr, w, co, pc in vecs_in:
        in_specs.append(pl.BlockSpec((arr.shape[0], w), functools.partial(lambda j, i, co, pc: (0, co + (j if pc else 0)), co=co, pc=pc)))
        args.append(arr)
    out_specs, out_shape = [], []
    for w, dt, pc in rows_out:
        out_specs.append(pl.BlockSpec((tr, w), functools.partial(lambda j, i, pc: (i, j if pc else 0), pc=pc)))
        out_shape.append(jax.ShapeDtypeStruct((nrows, w * (ncol if pc else 1)), dt))
    for k, w, pc in accs_out:
        out_specs.append(pl.BlockSpec((k, w), functools.partial(lambda j, i, pc: (0, j if pc else 0), pc=pc)))
        out_shape.append(jax.ShapeDtypeStruct((k, w * (ncol if pc else 1)), f32))
    n_in, n_ro = len(args), len(rows_out)

    def body(*refs):
        j, i = pl.program_id(0), pl.program_id(1)
        outs = fn(j, *[r[...] for r in refs[:n_in]])
        for r, val in zip(refs[n_in:n_in + n_ro], outs[:n_ro]):
            r[...] = val.astype(r.dtype)
        for (k, w, pc), r, val in zip(accs_out, refs[n_in + n_ro:], outs[n_ro:]):
            first = (i == 0) if pc else jnp.logical_and(i == 0, j == 0)

            @pl.when(first)
            def _(r=r):
                r[...] = jnp.zeros_like(r)

            r[...] += val

    res = pl.pallas_call(
        body, name=name, grid=(ncol, nrt), in_specs=in_specs, out_specs=out_specs, out_shape=out_shape,
        compiler_params=_cparams(("arbitrary", "arbitrary")),
    )(*args)
    return res


def _tap_room(mode, nrows, nlat):
    t = lax.broadcasted_iota(jnp.int32, (nrows, 1), 0)
    if mode == "seg":
        lat = t < nlat
        return t - jnp.where(lat, 0, nlat), jnp.where(lat, nlat, nrows) - 1 - t
    if mode == "row":
        p = t & (GW - 1)
        return p, GW - 1 - p
    return t // GW, (nrows - 1 - t) // GW


def _tap_valid(room, d):
    return room[1] >= d if d > 0 else room[0] >= -d


def _conv(x, w_ref, ktaps, mode, nlat, flip=False):
    nrows = x.shape[0]
    stride = GW if mode == "col" else 1
    room = _tap_room(mode, nrows, nlat)
    acc = jnp.zeros_like(x)
    for j in range(ktaps):
        d = j - ktaps // 2
        jj = ktaps - 1 - j if flip else j
        wj = w_ref[jj:jj + 1, :]
        if d == 0:
            acc = acc + x * wj
        else:
            xs = pltpu.roll(x, (-d * stride) % nrows, 0)
            acc = acc + jnp.where(_tap_valid(room, d), xs * wj, 0.0)
    return acc


def _conv_wgrad(dy, x, dw_ref, ktaps, mode, nlat):
    nrows = x.shape[0]
    stride = GW if mode == "col" else 1
    room = _tap_room(mode, nrows, nlat)
    dw_ref[...] = jnp.zeros_like(dw_ref)
    for j in range(ktaps):
        d = j - ktaps // 2
        if d == 0:
            prod = x * dy
        else:
            xs = pltpu.roll(x, (-d * stride) % nrows, 0)
            prod = jnp.where(_tap_valid(room, d), xs * dy, 0.0)
        dw_ref[j:j + 1, :] = _csum(prod)


def _qkv_post(j, pre, nh):
    a = _silu(pre)
    inv = lax.rsqrt(jnp.sum(a * a, axis=-1, keepdims=True) + 1e-6)
    scale = jnp.where(j < nh, HD ** -0.5, 1.0).astype(f32)
    nrm = jnp.where(j < 2 * nh, inv, 1.0) * scale
    return a, inv, nrm


def _qkv_conv_fwd(h_all, w7, nrows, nlat, nh, ktaps):
    ntile = 3 * nh

    def body(h_ref, w_ref, o_ref):
        j = pl.program_id(0)
        pre = _conv(h_ref[...], w_ref, ktaps, "seg", nlat)
        a, _, nrm = _qkv_post(j, pre, nh)
        o_ref[...] = a * nrm

    return pl.pallas_call(
        body, name="qkv_conv_fwd", grid=(ntile,),
        in_specs=[pl.BlockSpec((nrows, HD), lambda j: (0, j)), pl.BlockSpec((w7.shape[0], HD), lambda j: (0, j))],
        out_specs=pl.BlockSpec((None, nrows, HD), lambda j: (j, 0, 0)),
        out_shape=jax.ShapeDtypeStruct((ntile, nrows, HD), f32), compiler_params=_cparams(("parallel",)),
    )(h_all, w7)


def _qkv_conv_bwd(h_all, w7, dqkv, nrows, nlat, nh, ktaps):
    ntile = 3 * nh

    def body(h_ref, w_ref, d0_ref, d1_ref, dh_ref, dw_ref):
        j = pl.program_id(0)
        hx = h_ref[...]
        pre = _conv(hx, w_ref, ktaps, "seg", nlat)
        a, inv, nrm = _qkv_post(j, pre, nh)
        dn = (d0_ref[...] + d1_ref[...]) * jnp.where(j < nh, HD ** -0.5, 1.0).astype(f32)
        n = a * inv
        da_norm = inv * (dn - n * jnp.sum(dn * n, axis=-1, keepdims=True))
        da = jnp.where(j < 2 * nh, da_norm, dn)
        dpre = da * _dsilu(pre)
        dh_ref[...] = _conv(dpre, w_ref, ktaps, "seg", nlat, flip=True).astype(dh_ref.dtype)
        _conv_wgrad(dpre, hx, dw_ref, ktaps, "seg", nlat)

    blk = pl.BlockSpec((nrows, HD), lambda j: (0, j))
    wblk = pl.BlockSpec((w7.shape[0], HD), lambda j: (0, j))
    dblk = lambda d: pl.BlockSpec((None, None, nrows, HD), lambda j: (d, j, 0, 0))
    return pl.pallas_call(
        body, name="qkv_conv_bwd", grid=(ntile,), in_specs=[blk, wblk, dblk(0), dblk(1)], out_specs=[blk, wblk],
        out_shape=[jax.ShapeDtypeStruct((nrows, ntile * HD), bf16), jax.ShapeDtypeStruct((w7.shape[0], ntile * HD), f32)],
        compiler_params=_cparams(("parallel",)),
    )(h_all, w7, dqkv, dqkv)


def _conf_conv_fwd(h_all, w31, nlat, conf, val_blk, ktaps):
    nt = conf // LANES
    nhalf = nt // 2

    def body(v_ref, g_ref, w_ref, o_ref):
        j = pl.program_id(0)
        glu = v_ref[...] * _sigmoid(g_ref[...])

        @pl.when(j < nhalf)
        def _():
            o_ref[...] = _conv(glu, w_ref, ktaps, "row", nlat)

        @pl.when(j >= nhalf)
        def _():
            o_ref[...] = _conv(glu, w_ref, ktaps, "col", nlat)

    return pl.pallas_call(
        body, name="conf_conv_fwd", grid=(nt,),
        in_specs=[pl.BlockSpec((nlat, LANES), lambda j: (0, val_blk + j)), pl.BlockSpec((nlat, LANES), lambda j: (0, val_blk + nt + j)),
                  pl.BlockSpec((w31.shape[0], LANES), lambda j: (0, j))],
        out_specs=pl.BlockSpec((nlat, LANES), lambda j: (0, j)),
        out_shape=jax.ShapeDtypeStruct((nlat, conf), f32), compiler_params=_cparams(("parallel",)),
    )(h_all, h_all, w31)


def _conf_conv_bwd(h_all, w31, dyc, nlat, conf, val_blk, ktaps):
    nt = conf // LANES
    nhalf = nt // 2

    def body(v_ref, g_ref, w_ref, dy_ref, dv_ref, dg_ref, dw_ref):
        j = pl.program_id(0)
        val, sg = v_ref[...], _sigmoid(g_ref[...])
        glu = val * sg
        dy = dy_ref[...]

        def run(mode):
            dglu = _conv(dy, w_ref, ktaps, mode, nlat, flip=True)
            dv_ref[...] = (dglu * sg).astype(dv_ref.dtype)
            dg_ref[...] = (dglu * val * sg * (1.0 - sg)).astype(dg_ref.dtype)
            _conv_wgrad(dy, glu, dw_ref, ktaps, mode, nlat)

        @pl.when(j < nhalf)
        def _():
            run("row")

        @pl.when(j >= nhalf)
        def _():
            run("col")

    blk = pl.BlockSpec((nlat, LANES), lambda j: (0, j))
    wblk = pl.BlockSpec((w31.shape[0], LANES), lambda j: (0, j))
    return pl.pallas_call(
        body, name="conf_conv_bwd", grid=(nt,),
        in_specs=[pl.BlockSpec((nlat, LANES), lambda j: (0, val_blk + j)), pl.BlockSpec((nlat, LANES), lambda j: (0, val_blk + nt + j)), wblk, blk],
        out_specs=[blk, blk, wblk],
        out_shape=[jax.ShapeDtypeStruct((nlat, conf), bf16), jax.ShapeDtypeStruct((nlat, conf), bf16),
                   jax.ShapeDtypeStruct((w31.shape[0], conf), f32)],
        compiler_params=_cparams(("parallel",)),
    )(h_all, h_all, w31, dyc)


def _bd(eq, a, b):
    return jnp.einsum(eq, a.astype(bf16), b.astype(bf16), preferred_element_type=f32)


def _split(x, pieces):
    out = []
    for _ in range(pieces - 1):
        p = x.astype(bf16)
        out.append(p)
        x = x - p.astype(f32)
    return out + [x.astype(bf16)]


def _h3(eq, a, b):
    (ah, al), (bh, bl) = _split(a, 2), _split(b, 2)
    d = lambda p, q: jnp.einsum(eq, p, q, preferred_element_type=f32)
    return d(ah, bh) + (d(ah, bl) + d(al, bh))


def _cumsum_rows(x, rev):
    row = lax.broadcasted_iota(jnp.int32, x.shape, 0)
    down, up, s = x, x, 1
    while s < CH:
        down = down + jnp.where(row >= s, pltpu.roll(down, s, 0), 0.0)
        up = up + jnp.where(row < CH - s, pltpu.roll(up, CH - s, 0), 0.0)
        s *= 2
    return jnp.where(rev, up, down)


def _gdn_chunk(q, k, v, gt, nh, rev, solved=None):
    beta = jnp.stack([jnp.broadcast_to(gt[:, nh + h:nh + h + 1], (CH, LANES)) for h in range(nh)])
    ii = lax.broadcasted_iota(jnp.int32, (CH, CH), 0)
    jj = lax.broadcasted_iota(jnp.int32, (CH, CH), 1)
    si, sj = jnp.where(rev, jj, ii), jnp.where(rev, ii, jj)
    tril, stril = (si >= sj)[None], (si > sj)[None]
    gct = _cumsum_rows(gt, rev)
    gct_t = gct.T
    gc = jnp.stack([jnp.broadcast_to(gct[:, h:h + 1], (CH, LANES)) for h in range(nh)])
    gc_row = jnp.stack([jnp.broadcast_to(gct_t[h:h + 1, :], (CH, CH)) for h in range(nh)])
    diff = gc[:, :, :CH] - gc_row
    gam = jnp.where(tril, jnp.exp(jnp.where(tril, diff, 0.0)), 0.0)
    e = jnp.exp(gc)
    gl = jnp.where(rev, gc[:, 0:1, :], gc[:, CH - 1:CH, :])
    el = jnp.exp(gl)
    r = jnp.exp(gl - gc)
    kb, vb = k * beta, v * beta
    kbe = kb * e
    amat = jnp.where(stril, _bd("hik,hjk->hij", kb, k) * gam, 0.0)
    if solved is None:
        xp = -amat
        tinv = (ii == jj).astype(f32)[None] + xp
        for _ in range(5):
            xp = _h3("hij,hjk->hik", xp, xp)
            tinv = tinv + _h3("hij,hjk->hik", tinv, xp)
        u = _h3("hij,hjv->hiv", tinv, vb)
        w = _h3("hij,hjk->hik", tinv, kbe)
    else:
        tinv, u, w = solved
    pmat = jnp.where(tril, _bd("hik,hjk->hij", q, k) * gam, 0.0)
    return dict(beta=beta, gam=gam, e=e, el=el, r=r, kb=kb, vb=vb, kbe=kbe, amat=amat, tinv=tinv, u=u, w=w, pmat=pmat,
                qd=q * e, kd=k * r, tril=tril, stril=stril)


def _scan_row_chunk(d, n, ns, nt):
    fwd = jnp.where(n < nt, ns + n, n - nt)
    return jnp.where(d == 0, fwd, ns + nt - 1 - n)


def _gdn_fwd(qkv, gd, nh, nlat, nctx):
    nrows = nlat + nctx
    ns, nt = nlat // CH, nctx // CH
    nch = ns + nt

    def body(qkv_ref, g_ref, o_ref, st_ref, ti_ref, u_ref, w_ref, state):
        d, n = pl.program_id(0), pl.program_id(1)

        @pl.when(n == 0)
        def _():
            state[...] = jnp.zeros_like(state)

        c = _gdn_chunk(qkv_ref[0], qkv_ref[1], qkv_ref[2], g_ref[...], nh, d == 1)
        s = state[...]
        vn = c["u"] - _bd("hck,hkv->hcv", c["w"], s)
        o_ref[...] = _bd("hck,hkv->hcv", c["qd"], s) + _bd("hcd,hdv->hcv", c["pmat"], vn)
        st_ref[...] = s
        ti_ref[...], u_ref[...], w_ref[...] = c["tinv"], c["u"], c["w"]
        state[...] = s * c["el"] + _bd("hck,hcv->hkv", c["kd"], vn)

    hblk = pl.BlockSpec((None, nh, CH, HD), lambda d, n: (d, 0, _scan_row_chunk(d, n, ns, nt), 0))
    hshape = jax.ShapeDtypeStruct((2, nh, nrows, HD), f32)
    return pl.pallas_call(
        body, name="gdn_fwd", grid=(2, nch),
        in_specs=[pl.BlockSpec((3, nh, CH, HD), lambda d, n: (0, 0, _scan_row_chunk(d, n, ns, nt), 0)),
                  pl.BlockSpec((None, CH, LANES), lambda d, n: (d, _scan_row_chunk(d, n, ns, nt), 0))],
        out_specs=[hblk, pl.BlockSpec((None, None, nh, HD, HD), lambda d, n: (d, n, 0, 0, 0)),
                   pl.BlockSpec((None, None, nh, CH, CH), lambda d, n: (d, n, 0, 0, 0)), hblk, hblk],
        out_shape=[hshape, jax.ShapeDtypeStruct((2, nch, nh, HD, HD), f32), jax.ShapeDtypeStruct((2, nch, nh, CH, CH), f32), hshape, hshape],
        scratch_shapes=[pltpu.VMEM((nh, HD, HD), f32)], compiler_params=_cparams(("arbitrary", "arbitrary")),
    )(qkv, gd)


def _gdn_bwd(qkv, gd, states, solved, do, nh, nlat, nctx):
    nrows = nlat + nctx
    ns, nt = nlat // CH, nctx // CH
    nch = ns + nt

    def body(qkv_ref, g_ref, st_ref, ti_ref, u_ref, w_ref, do_ref, dqkv_ref, dgd_ref, dstate):
        d, step = pl.program_id(0), pl.program_id(1)

        @pl.when(step == 0)
        def _():
            dstate[...] = jnp.zeros_like(dstate)

        q, k, v = qkv_ref[0], qkv_ref[1], qkv_ref[2]
        c = _gdn_chunk(q, k, v, g_ref[...], nh, d == 1, solved=(ti_ref[...], u_ref[...], w_ref[...]))
        in_ctx = (nch - 1 - step) < nt
        s, dsp = st_ref[...], dstate[...]
        dout = jnp.where(in_ctx, 0.0, do_ref[...])
        beta, gam, e, el, r = c["beta"], c["gam"], c["e"], c["el"], c["r"]
        tinv, u, w, pmat, amat = c["tinv"], c["u"], c["w"], c["pmat"], c["amat"]
        vn = u - _bd("hck,hkv->hcv", w, s)
        dvn = _bd("hdc,hdv->hcv", pmat, dout) + _bd("hck,hkv->hcv", c["kd"], dsp)
        dp = jnp.where(c["tril"], _bd("hcv,hdv->hcd", dout, vn), 0.0)
        dqd = _bd("hcv,hkv->hck", dout, s)
        dkd = _bd("hcv,hkv->hck", vn, dsp)
        dstate[...] = _bd("hck,hcv->hkv", c["qd"], dout) + dsp * el - _bd("hck,hcv->hkv", w, dvn)
        del_ = jnp.sum(jnp.sum(s * dsp, axis=2, keepdims=True), axis=1, keepdims=True)
        dw = -_bd("hcv,hkv->hck", dvn, s)
        dvb = _h3("hji,hjv->hiv", tinv, dvn)
        dkbe = _h3("hji,hjk->hik", tinv, dw)
        z = _bd("hiv,hjv->hij", dvn, u) + _bd("hik,hjk->hij", dw, w)
        da = jnp.where(c["stril"], -_h3("hji,hjl->hil", tinv, z), 0.0)
        dm = da * gam
        dkb = _bd("hij,hjk->hik", dm, k) + dkbe * e
        dn = dp * gam
        dqkv_ref[0] = _bd("hij,hjk->hik", dn, k) + dqd * e
        dqkv_ref[1] = _bd("hji,hjk->hik", dm, c["kb"]) + _bd("hji,hjk->hik", dn, q) + dkd * r + dkb * beta
        dqkv_ref[2] = dvb * beta
        gmat = da * amat + dp * pmat
        rows_minus_cols = jnp.sum(gmat, axis=-1, keepdims=True) - jnp.sum(jnp.swapaxes(gmat, 1, 2), axis=-1, keepdims=True)
        de = jnp.sum(dqd * q + dkbe * c["kb"], axis=-1, keepdims=True)
        drr = jnp.sum(dkd * k, axis=-1, keepdims=True) * r
        dgc = rows_minus_cols + de * e - drr
        dgl = jnp.sum(drr, axis=1, keepdims=True) + del_ * el
        row = lax.broadcasted_iota(jnp.int32, (1, CH, 1), 1)
        total_row = row == jnp.where(d == 1, 0, CH - 1)
        dgc = dgc + jnp.where(total_row, dgl, 0.0)
        db = jnp.sum(dvb * v + dkb * k, axis=-1, keepdims=True)
        lane = lax.broadcasted_iota(jnp.int32, (CH, LANES), 1)
        dgc_cols, db_cols = jnp.zeros((CH, LANES), f32), jnp.zeros((CH, LANES), f32)
        for h in range(nh):
            dgc_cols = dgc_cols + jnp.where(lane == h, dgc[h], 0.0)
            db_cols = db_cols + jnp.where(lane == nh + h, db[h], 0.0)
        dgd_ref[...] = _cumsum_rows(dgc_cols, d == 0) + db_cols

    def rc(d, n):
        return _scan_row_chunk(d, nch - 1 - n, ns, nt)

    gblk = pl.BlockSpec((None, CH, LANES), lambda d, n: (d, rc(d, n), 0))
    return pl.pallas_call(
        body, name="gdn_bwd", grid=(2, nch),
        in_specs=[pl.BlockSpec((3, nh, CH, HD), lambda d, n: (0, 0, rc(d, n), 0)),
                  gblk,
                  pl.BlockSpec((None, None, nh, HD, HD), lambda d, n: (d, nch - 1 - n, 0, 0, 0)),
                  pl.BlockSpec((None, None, nh, CH, CH), lambda d, n: (d, nch - 1 - n, 0, 0, 0)),
                  pl.BlockSpec((None, nh, CH, HD), lambda d, n: (d, 0, rc(d, n), 0)),
                  pl.BlockSpec((None, nh, CH, HD), lambda d, n: (d, 0, rc(d, n), 0)),
                  pl.BlockSpec((nh, CH, HD), lambda d, n: (0, jnp.minimum(rc(d, n), ns - 1), 0))],
        out_specs=[pl.BlockSpec((None, 3, nh, CH, HD), lambda d, n: (d, 0, 0, rc(d, n), 0)), gblk],
        out_shape=[jax.ShapeDtypeStruct((2, 3, nh, nrows, HD), f32), jax.ShapeDtypeStruct((2, nrows, LANES), f32)],
        scratch_shapes=[pltpu.VMEM((nh, HD, HD), f32)], compiler_params=_cparams(("arbitrary", "arbitrary")),
    )(qkv, gd, states, *solved, do)


def _gate_norm_fwd(o5, h_all, dng, nh, nlat, zblk, tr):
    def body(of_ref, ob_ref, z_ref, g_ref, out_ref):
        o = of_ref[...] + ob_ref[...]
        rs = lax.rsqrt(jnp.mean(o * o, axis=-1, keepdims=True) + 1e-6)
        out_ref[...] = ((o * rs * g_ref[...]) * _silu(z_ref[...])).astype(out_ref.dtype)

    return pl.pallas_call(
        body, name="gate_norm_fwd", grid=(nh, nlat // tr),
        in_specs=[pl.BlockSpec((None, None, tr, HD), lambda j, i: (0, j, i, 0)), pl.BlockSpec((None, None, tr, HD), lambda j, i: (1, j, i, 0)),
                  pl.BlockSpec((tr, HD), lambda j, i: (i, zblk + j)), pl.BlockSpec((1, HD), lambda j, i: (0, 0))],
        out_specs=pl.BlockSpec((tr, HD), lambda j, i: (i, j)), out_shape=jax.ShapeDtypeStruct((nlat, nh * HD), bf16),
        compiler_params=_cparams(("parallel", "parallel")),
    )(o5, o5, h_all, dng)


def _gate_norm_bwd(dmix, o5, h_all, dng, nh, nlat, zblk, tr):
    def body(dm_ref, of_ref, ob_ref, z_ref, g_ref, do_ref, dz_ref, dg_ref):
        o = of_ref[...] + ob_ref[...]
        z, g, dm = z_ref[...], g_ref[...], dm_ref[...]
        rs = lax.rsqrt(jnp.mean(o * o, axis=-1, keepdims=True) + 1e-6)
        don = dm * _silu(z)
        dz_ref[...] = (dm * (o * rs * g) * _dsilu(z)).astype(dz_ref.dtype)
        dog = don * g
        do_ref[...] = rs * (dog - o * rs * rs * jnp.mean(dog * o, axis=-1, keepdims=True))

        @pl.when(jnp.logical_and(pl.program_id(0) == 0, pl.program_id(1) == 0))
        def _():
            dg_ref[...] = jnp.zeros_like(dg_ref)

        dg_ref[...] += _csum(don * o * rs)

    return pl.pallas_call(
        body, name="gate_norm_bwd", grid=(nh, nlat // tr),
        in_specs=[pl.BlockSpec((tr, HD), lambda j, i: (i, j)),
                  pl.BlockSpec((None, None, tr, HD), lambda j, i: (0, j, i, 0)), pl.BlockSpec((None, None, tr, HD), lambda j, i: (1, j, i, 0)),
                  pl.BlockSpec((tr, HD), lambda j, i: (i, zblk + j)), pl.BlockSpec((1, HD), lambda j, i: (0, 0))],
        out_specs=[pl.BlockSpec((None, tr, HD), lambda j, i: (j, i, 0)), pl.BlockSpec((tr, HD), lambda j, i: (i, j)),
                   pl.BlockSpec((1, HD), lambda j, i: (0, 0))],
        out_shape=[jax.ShapeDtypeStruct((nh, nlat, HD), f32), jax.ShapeDtypeStruct((nlat, nh * HD), bf16), jax.ShapeDtypeStruct((1, HD), f32)],
        compiler_params=_cparams(("arbitrary", "arbitrary")),
    )(dmix, o5, o5, h_all, dng)


def _mesh_pos():
    return lax.axis_index("x"), lax.axis_index("y"), lax.axis_index("c")


def _lin(p):
    return 4 * p[0] + 2 * p[1] + p[2]


def _all_gather(name, xs, after=None):
    nx = len(xs)
    extra = [] if after is None else [after]

    def body(*refs):
        xr, outr = refs[:nx], refs[nx + len(extra):2 * nx + len(extra)]
        send, recv, loc = refs[2 * nx + len(extra):]
        x, y, c = _mesh_pos()
        me, sib = (x, y, c), (x, y, 1 - c)
        chips = [(1 - x, y), (x, 1 - y), (1 - x, 1 - y)]

        def cp(l, k, block, to, src=None):
            rows = outr[l].at[_lin(block)]
            return pltpu.make_async_remote_copy(src_ref=rows if src is None else src, dst_ref=rows, send_sem=send.at[l, k],
                                                recv_sem=recv.at[l, k], device_id=to, device_id_type=MESH)

        mine = [pltpu.make_async_copy(xr[l], outr[l].at[_lin(me)], loc.at[l]) for l in range(nx)]
        for m in mine:
            m.start()
        first = []
        for l in range(nx):
            first.append(cp(l, 0, me, sib, src=xr[l]))
            first += [cp(l, 1 + j, me, (*chip, c), src=xr[l]) for j, chip in enumerate(chips)]
        for f in first:
            f.start()
        passed = []
        for l in range(nx):
            for j, chip in enumerate(chips):
                cp(l, 1 + j, (*chip, c), me).wait_recv()
                fwd = cp(l, 4 + j, (*chip, c), sib)
                fwd.start()
                passed.append(fwd)
        for l in range(nx):
            cp(l, 0, sib, me).wait_recv()
            for j, chip in enumerate(chips):
                cp(l, 4 + j, (*chip, 1 - c), me).wait_recv()
        for f in first + passed:
            f.wait_send()
        for m in mine:
            m.wait()

    anyspec = pl.BlockSpec(memory_space=pl.ANY)
    return pl.pallas_call(
        body, name=name, in_specs=[anyspec] * (nx + len(extra)), out_specs=[anyspec] * nx,
        out_shape=[jax.ShapeDtypeStruct((NDEV,) + a.shape, a.dtype) for a in xs],
        scratch_shapes=[pltpu.SemaphoreType.DMA((nx, 7)), pltpu.SemaphoreType.DMA((nx, 7)), pltpu.SemaphoreType.DMA((nx,))],
    )(*xs, *extra)


HBM_SPEC = pl.BlockSpec(memory_space=pltpu.HBM)
SEM_SPEC = pl.BlockSpec(memory_space=pltpu.SEMAPHORE)
ANY_SPEC = pl.BlockSpec(memory_space=pl.ANY)
EFFECT = pltpu.SideEffectType.DATAFLOW_SIDE_EFFECTING


def _peer(rel):
    x, y, c = _mesh_pos()
    return (1 - x if rel & 4 else x, 1 - y if rel & 2 else y, 1 - c if rel & 1 else c)


ALL_PEERS = (1, 2, 3, 4, 5, 6, 7)
SAME_CORE = (2, 4, 6)
CORE_AND_SIB = (1, 2, 4, 6)


def _slot(p, rels):
    return 2 * p[0] + p[1] if rels == SAME_CORE else _lin(p)


def _xchg_start(name, x, gather, after, rels=ALL_PEERS):
    me = _slot(_mesh_pos(), rels)
    shape = ((4 if rels == SAME_CORE else NDEV),) + x.shape if gather else x.shape
    own = x if gather else lax.dynamic_index_in_dim(x, me, 0, keepdims=False)
    land = lax.dynamic_update_index_in_dim(lax.empty(shape, x.dtype), own, me, 0)

    def body(x_ref, land_ref, after_ref, send, recv, x_thru, land_thru, token):
        mine = _slot(_mesh_pos(), rels)
        for k, rel in enumerate(rels):
            p = _peer(rel)
            pltpu.make_async_remote_copy(src_ref=x_ref if gather else x_ref.at[_slot(p, rels)], dst_ref=land_ref.at[mine],
                                         send_sem=send.at[k], recv_sem=recv.at[k], device_id=p, device_id_type=MESH).start()
        token[...] = jnp.zeros_like(token)

    return pl.pallas_call(
        body, name=name,
        out_shape=(pltpu.SemaphoreType.DMA((len(rels),)), pltpu.SemaphoreType.DMA((len(rels),)), pltpu.HBM(x.shape, x.dtype),
                   pltpu.HBM(shape, x.dtype), jax.ShapeDtypeStruct((8, LANES), f32)),
        in_specs=(HBM_SPEC, HBM_SPEC, ANY_SPEC), out_specs=(SEM_SPEC, SEM_SPEC, HBM_SPEC, HBM_SPEC, pl.BlockSpec(memory_space=pltpu.VMEM)),
        input_output_aliases={0: 2, 1: 3}, compiler_params=pltpu.CompilerParams(has_side_effects=EFFECT),
    )(pltpu.with_memory_space_constraint(x, pltpu.HBM), pltpu.with_memory_space_constraint(land, pltpu.HBM), after)


def _xchg_wait(name, started, gather, after, rels=ALL_PEERS):
    send, recv, x_thru, land_thru, _ = started

    def body(x_ref, land_ref, send, recv, after_ref, x_dead, got_ref):
        for k, rel in enumerate(rels):
            p = _peer(rel)
            cp = pltpu.make_async_remote_copy(src_ref=x_ref if gather else x_ref.at[_slot(p, rels)], dst_ref=land_ref.at[_slot(p, rels)],
                                              send_sem=send.at[k], recv_sem=recv.at[k], device_id=p, device_id_type=MESH)
            cp.wait_send()
            cp.wait_recv()

    return pl.pallas_call(
        body, name=name, out_shape=(pltpu.HBM(x_thru.shape, x_thru.dtype), pltpu.HBM(land_thru.shape, land_thru.dtype)),
        in_specs=(HBM_SPEC, HBM_SPEC, SEM_SPEC, SEM_SPEC, ANY_SPEC), out_specs=(HBM_SPEC, HBM_SPEC), input_output_aliases={0: 0, 1: 1},
        compiler_params=pltpu.CompilerParams(has_side_effects=EFFECT),
    )(x_thru, land_thru, send, recv, after)[1]


def _sib_forward(name, land):
    def body(land_ref, out_ref, send, recv):
        mx, my, mc = _mesh_pos()
        sib = (mx, my, 1 - mc)

        def cp(k, block):
            return pltpu.make_async_remote_copy(src_ref=out_ref.at[_lin(block)], dst_ref=out_ref.at[_lin(block)], send_sem=send.at[k],
                                                recv_sem=recv.at[k], device_id=sib, device_id_type=MESH)

        chips = [_peer(rel)[:2] for rel in SAME_CORE]
        sends = [cp(k, (*chip, mc)) for k, chip in enumerate(chips)]
        for s in sends:
            s.start()
        for k, chip in enumerate(chips):
            cp(k, (*chip, 1 - mc)).wait_recv()
        for s in sends:
            s.wait_send()

    return pl.pallas_call(
        body, name=name, in_specs=[ANY_SPEC], out_specs=ANY_SPEC, out_shape=jax.ShapeDtypeStruct(land.shape, land.dtype),
        input_output_aliases={0: 0}, scratch_shapes=[pltpu.SemaphoreType.DMA((3,)), pltpu.SemaphoreType.DMA((3,))],
    )(land)


def _sib_swap(name, x):
    def body(x_ref, out_ref, send, recv):
        mx, my, mc = _mesh_pos()
        sib = (mx, my, 1 - mc)
        cps = [pltpu.make_async_remote_copy(src_ref=x_ref.at[2 * q + 1 - mc], dst_ref=out_ref.at[q], send_sem=send.at[q], recv_sem=recv.at[q],
                                            device_id=sib, device_id_type=MESH) for q in range(4)]
        for cp in cps:
            cp.start()
        for cp in cps:
            cp.wait_recv()
        for cp in cps:
            cp.wait_send()

    return pl.pallas_call(
        body, name=name, in_specs=[ANY_SPEC], out_specs=ANY_SPEC, out_shape=jax.ShapeDtypeStruct((4,) + x.shape[1:], x.dtype),
        scratch_shapes=[pltpu.SemaphoreType.DMA((4,)), pltpu.SemaphoreType.DMA((4,))],
    )(x)


def _behind(token, a):
    return a + token[0:1, 0:1].astype(a.dtype)


def _sum8(name, g):
    def body(g_ref, o_ref):
        acc = g_ref[0:1, :]
        for p in range(1, NDEV):
            acc = acc + g_ref[p:p + 1, :]
        o_ref[...] = acc

    return pl.pallas_call(body, name=name, out_shape=jax.ShapeDtypeStruct((1, g.shape[1]), f32),
                          compiler_params=pltpu.CompilerParams(vmem_limit_bytes=VMEM_LIMIT))(g)


def _pack(arrs):
    flat = jnp.concatenate([a.reshape(-1).astype(f32) for a in arrs])
    pad = (-flat.shape[0]) % (8 * LANES)
    return jnp.pad(flat, (0, pad)).reshape(1, -1)


def _unpack(vec, shapes):
    out, off = [], 0
    flat = vec.reshape(-1)
    for s in shapes:
        n = 1
        for d in s:
            n *= d
        out.append(flat[off:off + n].reshape(s))
        off += n
    return out


def _adam(name, w, m, v, g, parts, after=None):
    extra = [] if after is None else [after]
    nrows, width = w.shape
    tr = _row_tile(nrows, width) if nrows >= 16 else nrows
    blk = pl.BlockSpec((tr, width), lambda i: (i, 0))
    gblk = pl.BlockSpec((parts, tr, width), lambda i: (0, i, 0)) if parts else blk
    c1 = 1.0 / (1.0 - ADAM_B1 ** ADAM_STEP)
    c2 = 1.0 / (1.0 - ADAM_B2 ** ADAM_STEP)

    def body(w_ref, m_ref, v_ref, g_ref, *rest):
        go_ref, d_ref, mo_ref, vo_ref = rest[len(extra):]
        if parts:
            gg = g_ref[0].astype(f32)
            for p in range(1, parts):
                gg = gg + g_ref[p].astype(f32)
        else:
            gg = g_ref[...]
        mn = ADAM_B1 * m_ref[...] + (1.0 - ADAM_B1) * gg
        vn = ADAM_B2 * v_ref[...] + (1.0 - ADAM_B2) * (gg * gg)
        go_ref[...] = gg
        mo_ref[...] = mn
        vo_ref[...] = vn
        d_ref[...] = -ADAM_LR * ((mn * c1) / (jnp.sqrt(vn * c2) + ADAM_EPS) + ADAM_WD * w_ref[...])

    sh = jax.ShapeDtypeStruct((nrows, width), f32)
    return pl.pallas_call(body, name=name, grid=(nrows // tr,), in_specs=[blk, blk, blk, gblk] + [pl.BlockSpec(memory_space=pl.ANY)] * len(extra),
                          out_specs=[blk] * 4, out_shape=[sh] * 4, compiler_params=_cparams(("parallel",)))(w, m, v, g, *extra)


def kernel(x, c, ctx, c_ctx, ln_in_g, ln_in_b, w_mod, b_mod, w_in, w_qkv_conv, a_log_f, dt_bias_f, a_log_b, dt_bias_b, dn_norm_g, conf_dw_w, conf_dw_b, conf_ln_g, conf_ln_b, w_out, ln1_g, ln1_b, w_mlp1, b_mlp1, w_mlp2, b_mlp2, ln2_g, ln2_b, loss_target, m_c_ctx, m_ln_in_g, m_ln_in_b, m_w_mod, m_b_mod, m_w_in, m_w_qkv_conv, m_a_log_f, m_dt_bias_f, m_a_log_b, m_dt_bias_b, m_dn_norm_g, m_conf_dw_w, m_conf_dw_b, m_conf_ln_g, m_conf_ln_b, m_w_out, m_ln1_g, m_ln1_b, m_w_mlp1, m_b_mlp1, m_w_mlp2, m_b_mlp2, m_ln2_g, m_ln2_b, v_c_ctx, v_ln_in_g, v_ln_in_b, v_w_mod, v_b_mod, v_w_in, v_w_qkv_conv, v_a_log_f, v_dt_bias_f, v_a_log_b, v_dt_bias_b, v_dn_norm_g, v_conf_dw_w, v_conf_dw_b, v_conf_ln_g, v_conf_ln_b, v_w_out, v_ln1_g, v_ln1_b, v_w_mlp1, v_b_mlp1, v_w_mlp2, v_b_mlp2, v_ln2_g, v_ln2_b):
    weights = dict(c_ctx=c_ctx, ln_in_g=ln_in_g, ln_in_b=ln_in_b, w_mod=w_mod, b_mod=b_mod, w_in=w_in, w_qkv_conv=w_qkv_conv, a_log_f=a_log_f, dt_bias_f=dt_bias_f, a_log_b=a_log_b, dt_bias_b=dt_bias_b, dn_norm_g=dn_norm_g, conf_dw_w=conf_dw_w, conf_dw_b=conf_dw_b, conf_ln_g=conf_ln_g, conf_ln_b=conf_ln_b, w_out=w_out, ln1_g=ln1_g, ln1_b=ln1_b, w_mlp1=w_mlp1, b_mlp1=b_mlp1, w_mlp2=w_mlp2, b_mlp2=b_mlp2, ln2_g=ln2_g, ln2_b=ln2_b)
    mom1 = dict(c_ctx=m_c_ctx, ln_in_g=m_ln_in_g, ln_in_b=m_ln_in_b, w_mod=m_w_mod, b_mod=m_b_mod, w_in=m_w_in, w_qkv_conv=m_w_qkv_conv, a_log_f=m_a_log_f, dt_bias_f=m_dt_bias_f, a_log_b=m_a_log_b, dt_bias_b=m_dt_bias_b, dn_norm_g=m_dn_norm_g, conf_dw_w=m_conf_dw_w, conf_dw_b=m_conf_dw_b, conf_ln_g=m_conf_ln_g, conf_ln_b=m_conf_ln_b, w_out=m_w_out, ln1_g=m_ln1_g, ln1_b=m_ln1_b, w_mlp1=m_w_mlp1, b_mlp1=m_b_mlp1, w_mlp2=m_w_mlp2, b_mlp2=m_b_mlp2, ln2_g=m_ln2_g, ln2_b=m_ln2_b)
    mom2 = dict(c_ctx=v_c_ctx, ln_in_g=v_ln_in_g, ln_in_b=v_ln_in_b, w_mod=v_w_mod, b_mod=v_b_mod, w_in=v_w_in, w_qkv_conv=v_w_qkv_conv, a_log_f=v_a_log_f, dt_bias_f=v_dt_bias_f, a_log_b=v_a_log_b, dt_bias_b=v_dt_bias_b, dn_norm_g=v_dn_norm_g, conf_dw_w=v_conf_dw_w, conf_dw_b=v_conf_dw_b, conf_ln_g=v_conf_ln_g, conf_ln_b=v_conf_ln_b, w_out=v_w_out, ln1_g=v_ln1_g, ln1_b=v_ln1_b, w_mlp1=v_w_mlp1, b_mlp1=v_b_mlp1, w_mlp2=v_w_mlp2, b_mlp2=v_b_mlp2, ln2_g=v_ln2_g, ln2_b=v_ln2_b)
    names = list(weights)

    me = _lin(_mesh_pos())
    S, D = x.shape[1], x.shape[2]
    T = ctx.shape[1]
    R = S + T
    DN = D // 2
    NH = DN // HD
    CONF = D - DN
    K7, K31 = w_qkv_conv.shape[1], conf_dw_w.shape[1]
    DFF = w_mlp1.shape[2] * NDEV
    INC = w_in.shape[2] * NDEV
    CONF_OFF = 4 * DN + 4 * NH
    NC = 4 * DN + 2 * CONF + GPAD
    QB, ZB, VB, GB = 0, 3 * DN // LANES, 4 * DN // LANES, (4 * DN + 2 * CONF) // LANES
    MODC = w_mod.shape[2]
    x2d, ctx2d, tgt = x[0], ctx[0], loss_target[0]
    row = lambda a: a.reshape(1, -1).astype(f32)

    c_all, w7_all, w31_all = _all_gather("ag_small", [c.astype(f32), w_qkv_conv[0], conf_dw_w[0]])
    w7 = jnp.pad(jnp.transpose(w7_all, (1, 0, 2)).reshape(K7, 3 * DN), ((0, (-K7) % 8), (0, 0)))
    w31 = jnp.pad(jnp.transpose(w31_all, (1, 0, 2)).reshape(K31, CONF), ((0, (-K31) % 8), (0, 0)))
    c16 = jnp.concatenate([c_all.reshape(NDEV, D), c_ctx.reshape(1, D).astype(f32), jnp.zeros((7, D), f32)], axis=0)
    (sc16,) = _rowcall("silu_c", lambda j, a: (_silu(a),), 16, 16, [(c16, D, 0, 0, False)], [], [(D, f32, False)], [])
    bmod_mine = lax.dynamic_slice_in_dim(b_mod.astype(f32), me * MODC, MODC, axis=1)
    mod_part = _mm("mod_fwd", sc16, w_mod[0], "nn", f32, extras=[("row", bmod_mine)], epilogue=lambda r, b: (r + b,))
    (mod_g,) = _all_gather("ag_mod", [mod_part])
    mod_all = jnp.transpose(mod_g, (1, 0, 2)).reshape(16, 6 * D)
    mod_me = lax.dynamic_slice_in_dim(mod_all, me, 1, axis=0)
    sh_a, sc_a, g_a, sh_m, sc_m, g_m = [mod_me[:, i * D:(i + 1) * D] for i in range(6)]
    csh_a, csc_a = mod_all[8:9, 0:D], mod_all[8:9, D:2 * D]

    (g_win,) = _all_gather("ag_w_in", [w_in[0].astype(bf16)], after=mod_g)
    ag_wout = _xchg_start("ag_wout_start", w_out[0].astype(bf16), True, g_win)
    ag_w1 = _xchg_start("ag_w1_start", w_mlp1[0].astype(bf16), True, ag_wout[4])
    win_full = jnp.transpose(g_win, (1, 0, 2)).reshape(D, INC)
    w_cat = jnp.concatenate([win_full[:, :4 * DN], win_full[:, CONF_OFF:], win_full[:, 4 * DN:CONF_OFF],
                             jnp.zeros((D, GPAD - 4 * NH), bf16)], axis=1)
    g0, b0 = row(ln_in_g), row(ln_in_b)
    g0_fwd = _behind(ag_w1[4], g0)

    def ln_mod(j, xt, g, b, sh, sc):
        xh, _ = _ln_stats(xt)
        xn = xh * g + b
        return xn, xn * (1.0 + sc) + sh

    tr = _row_tile(T, D)
    vec = lambda a: (a, a.shape[1], 0, False)
    xn, xm = _rowcall("ln_in_lat", ln_mod, S, tr, [(x2d, D, 0, 0, False)], [vec(g0_fwd), vec(b0), vec(sh_a), vec(sc_a)], [(D, f32, False), (D, bf16, False)], [])
    xcn, xcm = _rowcall("ln_in_ctx", ln_mod, T, tr, [(ctx2d, D, 0, 0, False)], [vec(g0), vec(b0), vec(csh_a), vec(csc_a)], [(D, f32, False), (D, bf16, False)], [])
    xm_all = jnp.concatenate([xm, xcm], axis=0)

    h_all = _mm("in_proj", xm_all, w_cat, "nn", f32, tm=1088, tn=1280)

    qkv = _qkv_conv_fwd(h_all, w7, R, S, NH, K7).reshape(3, NH, R, HD)
    lane = jnp.arange(LANES)
    is_a = ((lane < 4 * NH) & ((lane // NH) % 2 == 0)).astype(f32).reshape(1, LANES)
    pad_g = lambda a, b: jnp.concatenate([a.reshape(-1), jnp.zeros((NH,), f32), b.reshape(-1), jnp.zeros((LANES - 3 * NH,), f32)]).reshape(1, LANES)
    neg_a = pad_g(-jnp.exp(a_log_f.astype(f32)), -jnp.exp(a_log_b.astype(f32)))
    dt_v = pad_g(dt_bias_f.astype(f32), dt_bias_b.astype(f32))

    def gates_f(j, hg, isa, na, dt):
        return (jnp.where(isa > 0.5, na * _softplus(hg + dt), _sigmoid(hg)),)

    trg = _row_tile(T, LANES)
    ag_w2 = _xchg_start("ag_w2_start", w_mlp2[0].astype(bf16), True, qkv, CORE_AND_SIB)
    (gates,) = _rowcall("gates_fwd", gates_f, R, trg, [(h_all, LANES, 0, GB, False)], [vec(_behind(ag_w2[4], is_a)), vec(neg_a), vec(dt_v)], [(LANES, f32, False)], [])
    gpad = jnp.zeros((R, LANES - 2 * NH), f32)
    gd = jnp.stack([jnp.concatenate([gates[:, :2 * NH], gpad], axis=1), jnp.concatenate([gates[:, 2 * NH:4 * NH], gpad], axis=1)])

    o5, states, *solved = _gdn_fwd(qkv, gd, NH, S, T)
    dng = row(dn_norm_g)
    trh = _row_tile(S, HD, cap=1024)
    dn_out = _gate_norm_fwd(o5, h_all, dng, NH, S, ZB, trh)

    yconv = _conf_conv_fwd(h_all, w31, S, CONF, VB, K31)
    bdw, clg, clb = row(conf_dw_b), row(conf_ln_g), row(conf_ln_b)

    def conf_ln(j, yc, b, g, bb):
        xh, _ = _ln_stats(yc + b)
        return (_silu(xh * g + bb),)

    trc = _row_tile(S, CONF)
    (conf_out,) = _rowcall("conf_ln_fwd", conf_ln, S, trc, [(yconv, CONF, 0, 0, False)], [vec(bdw), vec(clg), vec(clb)], [(CONF, bf16, False)], [])
    mix = jnp.concatenate([dn_out, conf_out], axis=1)
    wout_full = _xchg_wait("ag_wout_wait", ag_wout, True, mix).reshape(DN + CONF, D)
    y = _mm("out_proj", mix, wout_full, "nn", f32)

    l1g, l1b, l2g, l2b = row(ln1_g), row(ln1_b), row(ln2_g), row(ln2_b)

    def ln1_mod(j, xnt, yt, ga, g, b, sh, sc):
        xh, _ = _ln_stats(ALPHA * xnt + ga * yt)
        x1 = xh * g + b
        return x1, x1 * (1.0 + sc) + sh

    trd = _row_tile(S, D)
    x1, u = _rowcall("ln1_fwd", ln1_mod, S, trd, [(xn, D, 0, 0, False), (y, D, 0, 0, False)],
                     [vec(g_a), vec(l1g), vec(l1b), vec(sh_m), vec(sc_m)], [(D, f32, False), (D, bf16, False)], [])
    b1, b2 = row(b_mlp1), row(b_mlp2)
    g_w1 = _xchg_wait("ag_w1_wait", ag_w1, True, u)
    hh, act = _mm("mlp1", u, g_w1, "nn", (f32, bf16), extras=[("row", b1)], epilogue=lambda r, b: (r, jnp.square(jnp.maximum(r + b, 0.0))))
    w2_full = _sib_forward("ag_w2_fwd", _xchg_wait("ag_w2_wait", ag_w2, True, act, CORE_AND_SIB)).reshape(DFF, D)
    y2 = _mm("mlp2", act, w2_full, "nn", f32)

    def ln2_loss(j, x1t, y2t, tg, bb2, gm, g, b):
        y2b = y2t + bb2
        xh, rstd = _ln_stats(ALPHA * x1t + gm * y2b)
        err = xh * g + b - tg
        dx2 = err * (1.0 / D)
        dr2 = _ln_bwd(dx2, xh, rstd, g)
        dy2 = dr2 * gm
        lsum = jnp.broadcast_to(jnp.sum(err * err).reshape(1, 1), (1, LANES))
        return dr2, dy2, _csum(dx2 * xh), _csum(dx2), _csum(dr2 * y2b), _csum(dy2), lsum

    dr2, dy2, d_l2g, d_l2b, d_gm, d_b2, lsum = _rowcall(
        "ln2_loss", ln2_loss, S, trd, [(x1, D, 0, 0, False), (y2, D, 0, 0, False), (tgt, D, 0, 0, False)],
        [vec(b2), vec(g_m), vec(l2g), vec(l2b)], [(D, f32, False), (D, bf16, False)], [(1, D, False)] * 4 + [(1, LANES, False)])
    loss = lax.psum(0.5 * lsum[0, 0] / D, ("x", "y", "c"))

    dw2_p = _mm("mlp2_dw", act, dy2, "tn", bf16)
    a2a_w2 = _xchg_start("a2a_w2_start", dw2_p.reshape(NDEV, DFF // NDEV, D), False, b2)
    dhh = _mm("mlp2_dx", dy2, w2_full, "nt", bf16, extras=[("tile", hh), ("row", _behind(a2a_w2[4], b1))],
              epilogue=lambda r, h, b: (r * (2.0 * jnp.maximum(h + b, 0.0)),))
    (d_b1,) = _rowcall("b1_grad", lambda j, a: (_csum(a.astype(f32)),), S, _row_tile(S, DFF // 8), [(dhh, DFF, 0, 0, False)], [], [], [(1, DFF, False)])
    dw1_p = _mm("mlp1_dw", u, dhh, "tn", bf16, out_split=NDEV)
    a2a_w1 = _xchg_start("a2a_w1_start", dw1_p, False, d_b1)
    du = _mm("mlp1_dx", dhh, g_w1, "nt", f32)

    def ln1_bwd(j, dr2t, dut, xnt, yt, ga, g, b, sc):
        xh, rstd = _ln_stats(ALPHA * xnt + ga * yt)
        x1t = xh * g + b
        dx1 = ALPHA * dr2t + dut * (1.0 + sc)
        dr1 = _ln_bwd(dx1, xh, rstd, g)
        return dr1, dr1 * ga, _csum(dut * x1t), _csum(dut), _csum(dx1 * xh), _csum(dx1), _csum(dr1 * yt)

    dr1, dy, d_scm, d_shm, d_l1g, d_l1b, d_ga = _rowcall(
        "ln1_bwd", ln1_bwd, S, trd, [(dr2, D, 0, 0, False), (du, D, 0, 0, False), (xn, D, 0, 0, False), (y, D, 0, 0, False)],
        [vec(g_a), vec(l1g), vec(l1b), vec(_behind(a2a_w1[4], sc_m))], [(D, f32, False), (D, bf16, False)], [(1, D, False)] * 5)
    dwout_p = _mm("out_proj_dw", mix, dy, "tn", bf16)
    a2a_wout = _xchg_start("a2a_wout_start", dwout_p.reshape(NDEV, (DN + CONF) // NDEV, D), False, d_l1g)
    dmix = _mm("out_proj_dx", dy, wout_full, "nt", f32)

    def conf_ln_b(j, dm, yc, b, g, bb):
        xh, rstd = _ln_stats(yc + b)
        dln = dm * _dsilu(xh * g + bb)
        dyc = _ln_bwd(dln, xh, rstd, g)
        return dyc, _csum(dln * xh), _csum(dln), _csum(dyc)

    dyc, d_clg, d_clb, d_bdw = _rowcall("conf_ln_bwd", conf_ln_b, S, trc, [(dmix, CONF, 0, DN // CONF, False), (yconv, CONF, 0, 0, False)],
                                        [vec(_behind(a2a_wout[4], bdw)), vec(clg), vec(clb)], [(CONF, f32, False)], [(1, CONF, False)] * 3)
    dval, dgate, dw31 = _conf_conv_bwd(h_all, w31, dyc, S, CONF, VB, K31)

    do, dz, d_dng = _gate_norm_bwd(dmix, o5, h_all, dng, NH, S, ZB, trh)
    dqkv, dgd = _gdn_bwd(qkv, gd, states, solved, do, NH, S, T)
    dh_qkv, dw7 = _qkv_conv_bwd(h_all, w7, dqkv.reshape(2, 3 * NH, R, HD), R, S, NH, K7)
    dgate_cols = jnp.concatenate([dgd[0][:, :2 * NH], dgd[1][:, :2 * NH], jnp.zeros((R, LANES - 4 * NH), f32)], axis=1)

    def gates_b(j, hg, dgt, isa, na, dt):
        sg = _sigmoid(hg)
        sp = _softplus(hg + dt)
        dpre = jnp.where(isa > 0.5, dgt * na * _sigmoid(hg + dt), dgt * sg * (1.0 - sg))
        return dpre, _csum(jnp.where(isa > 0.5, dgt * na * sp, 0.0)), _csum(jnp.where(isa > 0.5, dpre, 0.0))

    dh_g, d_alog, d_dt = _rowcall("gates_bwd", gates_b, R, trg, [(h_all, LANES, 0, GB, False), (dgate_cols, LANES, 0, 0, False)],
                                  [vec(is_a), vec(neg_a), vec(dt_v)], [(LANES, bf16, False)], [(1, LANES, False)] * 2)

    zpad = lambda a: jnp.concatenate([a, jnp.zeros((T, a.shape[1]), bf16)], axis=0)
    dh_all = jnp.concatenate([dh_qkv, zpad(dz), zpad(dval), zpad(dgate), dh_g, jnp.zeros((R, GPAD - LANES), bf16)], axis=1)
    dxm_all = _mm("in_proj_dx", dh_all, w_cat, "nt", f32, tm=1088, tk=2560)

    def ln_in_bwd_lat(j, xt, dr1t, dxm, g, b, sc):
        xh, rstd = _ln_stats(xt)
        xnt = xh * g + b
        dxn = ALPHA * dr1t + dxm * (1.0 + sc)
        return _ln_bwd(dxn, xh, rstd, g), _csum(dxm * xnt), _csum(dxm), _csum(dxn * xh), _csum(dxn)

    def ln_in_bwd_ctx(j, xt, dxm, g, b, sc):
        xh, rstd = _ln_stats(xt)
        xnt = xh * g + b
        dxn = dxm * (1.0 + sc)
        return _csum(dxm * xnt), _csum(dxm), _csum(dxn * xh), _csum(dxn)

    grad_x, d_sca, d_sha, d_g0a, d_b0a = _rowcall(
        "ln_in_bwd_lat", ln_in_bwd_lat, S, trd, [(x2d, D, 0, 0, False), (dr1, D, 0, 0, False), (dxm_all, D, 0, 0, False)],
        [vec(g0), vec(b0), vec(sc_a)], [(D, f32, False)], [(1, D, False)] * 4)
    d_csca, d_csha, d_g0b, d_b0b = _rowcall(
        "ln_in_bwd_ctx", ln_in_bwd_ctx, T, tr, [(ctx2d, D, 0, 0, False), (dxm_all, D, S // tr, 0, False)],
        [vec(g0), vec(b0), vec(csc_a)], [], [(1, D, False)] * 4)

    zD = jnp.zeros((1, D), f32)
    dmod_me = jnp.concatenate([d_sha, d_sca, d_ga, d_shm, d_scm, d_gm], axis=1)
    dmodc_me = jnp.concatenate([d_csha, d_csca, zD, zD, zD, zD], axis=1)
    small_names = ["ln_in_g", "ln_in_b", "a_log_f", "dt_bias_f", "a_log_b", "dt_bias_b", "dn_norm_g", "conf_dw_b", "conf_ln_g", "conf_ln_b",
                   "ln1_g", "ln1_b", "b_mlp1", "b_mlp2", "ln2_g", "ln2_b"]
    small_parts = [d_g0a + d_g0b, d_b0a + d_b0b, d_alog[:, 0:NH], d_dt[:, 0:NH], d_alog[:, 2 * NH:3 * NH], d_dt[:, 2 * NH:3 * NH], d_dng, d_bdw, d_clg, d_clb,
                   d_l1g, d_l1b, d_b1, d_b2, d_l2g, d_l2b]
    conv_parts = [dw7[:K7], dw31[:K31]]
    packed = _pack(small_parts + conv_parts + [dmodc_me])
    (pk_all, dmod_all) = _all_gather("ag_small_grads", [packed, dmod_me])
    summed = _sum8("sum_small_grads", pk_all.reshape(NDEV, -1))
    parts_sh = [a.shape for a in small_parts + conv_parts + [dmodc_me]]
    un = _unpack(summed, parts_sh)
    gsmall = dict(zip(small_names, un[:len(small_names)]))
    g_w7_full, g_w31_full, dmodc = un[len(small_names):]
    gsmall["w_qkv_conv"] = lax.dynamic_slice_in_dim(g_w7_full, me * w_qkv_conv.shape[2], w_qkv_conv.shape[2], axis=1)
    gsmall["conf_dw_w"] = lax.dynamic_slice_in_dim(g_w31_full, me * conf_dw_w.shape[2], conf_dw_w.shape[2], axis=1)

    dm16 = jnp.concatenate([dmod_all.reshape(NDEV, 6 * D), dmodc, jnp.zeros((7, 6 * D), f32)], axis=0)
    (gbmod,) = _rowcall("bmod_grad", lambda j, a: (_csum(a),), 16, 16, [(dm16, 6 * D, 0, 0, False)], [], [], [(1, 6 * D, False)])
    gsmall["b_mod"] = gbmod
    dm16_mine = lax.dynamic_slice_in_dim(dm16, me * MODC, MODC, axis=1)
    g_wmod = _mm("mod_dw", sc16, dm16_mine, "tn", f32)
    dsc16_part = _mm("mod_dx", dm16_mine, w_mod[0], "nt", f32)
    (dsc_all,) = _all_gather("ag_cctx", [dsc16_part[8:9]])
    dsilu_cctx = _sum8("sum_cctx", dsc_all.reshape(NDEV, D))
    (g_cctx,) = _rowcall("cctx_grad", lambda j, ds, cc: (ds * _dsilu(cc),), 1, 1, [(dsilu_cctx, D, 0, 0, False), (c_ctx.reshape(1, D).astype(f32), D, 0, 0, False)],
                         [], [(D, f32, False)], [])
    gsmall["c_ctx"] = g_cctx

    dwcat_p = _mm("in_proj_dw", xm_all, dh_all, "tn", bf16, tn=1280, tk=4352, after=g_cctx)
    dwin = jnp.concatenate([dwcat_p[:, :4 * DN], dwcat_p[:, 4 * DN + 2 * CONF:4 * DN + 2 * CONF + 4 * NH], dwcat_p[:, 4 * DN:4 * DN + 2 * CONF]], axis=1)
    dwin_p = jnp.transpose(dwin.reshape(D, NDEV, INC // NDEV), (1, 0, 2))
    from_sib = _sib_swap("sib_win", dwin_p)
    mine4 = lax.dynamic_index_in_dim(dwin_p.reshape(4, 2, D, INC // NDEV), lax.axis_index("c"), 1, keepdims=False)
    (chip_sum,) = _rowcall("add_win", lambda j, a, b: (a.astype(f32) + b.astype(f32),), 4 * D, _row_tile(4 * D, INC // NDEV),
                           [(mine4.reshape(4 * D, INC // NDEV), INC // NDEV, 0, 0, False), (from_sib.reshape(4 * D, INC // NDEV), INC // NDEV, 0, 0, False)],
                           [], [(INC // NDEV, bf16, False)], [])
    a2a_win = _xchg_start("a2a_win_start", chip_sum.reshape(4, D, INC // NDEV), False, g_cctx, SAME_CORE)

    grads, deltas, new_m, new_v = {}, {}, {}, {}
    res = _adam("adam_w_mod", w_mod[0], m_w_mod[0], v_w_mod[0], g_wmod, 0, after=a2a_win[4])
    grads["w_mod"], deltas["w_mod"], new_m["w_mod"], new_v["w_mod"] = [a.reshape(w_mod.shape) for a in res]
    after = res[1]
    for nm, started in (("w_mlp2", a2a_w2), ("w_mlp1", a2a_w1), ("w_out", a2a_wout), ("w_in", a2a_win)):
        w3 = weights[nm]
        rels = SAME_CORE if nm == "w_in" else ALL_PEERS
        g = _xchg_wait("a2a_" + nm + "_wait", started, False, after, rels)
        res = _adam("adam_" + nm, w3[0], mom1[nm][0], mom2[nm][0], g, len(rels) + 1)
        grads[nm], deltas[nm], new_m[nm], new_v[nm] = [a.reshape(w3.shape) for a in res]
        after = res[1]
    snames = [n for n in names if n not in grads]
    res = _adam("adam_small", _pack([weights[n] for n in snames]), _pack([mom1[n] for n in snames]), _pack([mom2[n] for n in snames]),
                _pack([gsmall[n] for n in snames]), 0)
    shapes = [weights[n].shape for n in snames]
    for dst, packed_out in zip((grads, deltas, new_m, new_v), res):
        for n, a in zip(snames, _unpack(packed_out, shapes)):
            dst[n] = a

    return (loss, grad_x.reshape(x.shape), *[grads[n] for n in names], *[deltas[n] for n in names],
            *[new_m[n] for n in names], *[new_v[n] for n in names])
```

```python
import functools

import jax
import jax.numpy as jnp
from jax import lax
from jax.experimental import pallas as pl
from jax.experimental.pallas import tpu as pltpu

f32 = jnp.float32
bf16 = jnp.bfloat16
MESH = pl.DeviceIdType.MESH

NDEV = 8
HD = 128
CH = 64
GW = 64
LANES = 128
GPAD = 512
LN_EPS = 1e-5
ALPHA = 2.0 ** 0.25
ADAM_LR, ADAM_B1, ADAM_B2, ADAM_EPS, ADAM_WD, ADAM_STEP = 0.001, 0.9, 0.999, 1e-08, 0.01, 10
VMEM_LIMIT = 60 * 1024 * 1024
ROW_BLOCK_BYTES = 1 << 21


def _cparams(sem):
    return pltpu.CompilerParams(dimension_semantics=sem, vmem_limit_bytes=VMEM_LIMIT)


def _tile(dim, pref, align):
    t = min(pref, dim)
    t -= t % align
    while t >= align:
        if dim % t == 0:
            return t
        t -= align
    return dim


def _row_tile(nrows, width, cap=256):
    t = max(16, min(cap, ROW_BLOCK_BYTES // (4 * width)))
    t = 1 << (t.bit_length() - 1)
    while nrows % t:
        t //= 2
    return max(t, 1)


def _sigmoid(x):
    return 1.0 / (1.0 + jnp.exp(-x))


def _silu(x):
    return x * _sigmoid(x)


def _dsilu(x):
    s = _sigmoid(x)
    return s * (1.0 + x * (1.0 - s))


def _softplus(x):
    return jnp.maximum(x, 0.0) + jnp.log(1.0 + jnp.exp(-jnp.abs(x)))


def _ln_stats(r):
    mu = jnp.mean(r, axis=-1, keepdims=True)
    xc = r - mu
    rstd = lax.rsqrt(jnp.mean(xc * xc, axis=-1, keepdims=True) + LN_EPS)
    return xc * rstd, rstd


def _ln_bwd(dy, xhat, rstd, g):
    dxh = dy * g
    return rstd * (dxh - jnp.mean(dxh, axis=-1, keepdims=True) - xhat * jnp.mean(dxh * xhat, axis=-1, keepdims=True))


def _csum(a):
    return jnp.sum(a, axis=0, keepdims=True)


def _mm(name, a, b, mode, out_dtype, out_split=1, tm=1024, tn=1024, tk=4096, extras=(), epilogue=None, after=None):
    if mode == "tn":
        K, M = a.shape
    else:
        M, K = a.shape
    bp = b.shape[0] if b.ndim == 3 else 1
    brows, bcols = b.shape[-2], b.shape[-1] * bp
    N = brows if mode == "nt" else bcols
    assert K == (bcols if mode == "nt" else brows), (name, a.shape, b.shape)
    tm = _tile(M, tm, LANES if mode == "tn" else 16)
    nsplit = max(out_split, bp if mode != "nt" else 1)
    tn = _tile(N // nsplit, tn, LANES)
    kgroup = 1
    if mode == "nt" and bp > 1 and tk >= 2 * (K // bp):
        kgroup = min(tk // (K // bp), bp)
        tk = kgroup * (K // bp)
    else:
        tk = _tile(K // (bp if mode == "nt" else 1), tk, LANES)
    nk = K // tk
    dims = {"nn": (((1,), (0,)), ((), ())), "nt": (((1,), (1,)), ((), ())), "tn": (((0,), (0,)), ((), ()))}[mode]
    out_dtypes = out_dtype if isinstance(out_dtype, tuple) else (out_dtype,)

    a_spec = pl.BlockSpec((tk, tm), lambda i, j, k: (k, i)) if mode == "tn" else pl.BlockSpec((tm, tk), lambda i, j, k: (i, k))
    if b.ndim == 3:
        if mode == "nt" and kgroup > 1:
            b_spec = pl.BlockSpec((kgroup, tn, K // bp), lambda i, j, k: (k, j, 0))
        elif mode == "nt":
            per = (K // bp) // tk
            b_spec = pl.BlockSpec((None, tn, tk), lambda i, j, k: (k // per, j, k % per))
        else:
            per = (N // bp) // tn
            b_spec = pl.BlockSpec((None, tk, tn), lambda i, j, k: (j // per, k, j % per))
    elif mode == "nt":
        b_spec = pl.BlockSpec((tn, tk), lambda i, j, k: (j, k))
    else:
        b_spec = pl.BlockSpec((tk, tn), lambda i, j, k: (k, j))
    if out_split > 1:
        pero = (N // out_split) // tn
        o_spec = pl.BlockSpec((None, tm, tn), lambda i, j, k: (j // pero, i, j % pero))
        o_shapes = [jax.ShapeDtypeStruct((out_split, M, N // out_split), dt) for dt in out_dtypes]
    else:
        o_spec = pl.BlockSpec((tm, tn), lambda i, j, k: (i, j))
        o_shapes = [jax.ShapeDtypeStruct((M, N), dt) for dt in out_dtypes]
    in_specs = [a_spec, b_spec]
    args = [a, b]
    for kind, arr in extras:
        in_specs.append(pl.BlockSpec((1, tn), lambda i, j, k: (0, j)) if kind == "row" else pl.BlockSpec((tm, tn), lambda i, j, k: (i, j)))
        args.append(arr)
    n_in, n_out = len(args), len(out_dtypes)
    if after is not None:
        in_specs.append(pl.BlockSpec(memory_space=pl.ANY))
        args.append(after)
    n_all = len(args)

    def finish(refs, r):
        outs = (r,) if epilogue is None else epilogue(r, *[e[...] for e in refs[2:n_in]])
        for o_ref, val in zip(refs[n_all:n_all + n_out], outs):
            o_ref[...] = val.astype(o_ref.dtype)

    def body(*refs):
        if kgroup > 1:
            kp = K // bp
            part = lax.dot_general(refs[0][:, 0:kp].astype(bf16), refs[1][0].astype(bf16), dims, preferred_element_type=f32)
            for gi in range(1, kgroup):
                part = part + lax.dot_general(refs[0][:, gi * kp:(gi + 1) * kp].astype(bf16), refs[1][gi].astype(bf16), dims, preferred_element_type=f32)
        else:
            part = lax.dot_general(refs[0][...].astype(bf16), refs[1][...].astype(bf16), dims, preferred_element_type=f32)
        if nk == 1:
            finish(refs, part)
            return
        acc = refs[-1]
        k = pl.program_id(2)

        @pl.when(k == 0)
        def _():
            acc[...] = part

        @pl.when(jnp.logical_and(k > 0, k < nk - 1))
        def _():
            acc[...] += part

        @pl.when(k == nk - 1)
        def _():
            finish(refs, acc[...] + part)

    res = pl.pallas_call(
        body, name=name, grid=(M // tm, N // tn, nk), in_specs=in_specs, out_specs=[o_spec] * n_out, out_shape=o_shapes,
        scratch_shapes=[pltpu.VMEM((tm, tn), f32)] if nk > 1 else [], compiler_params=_cparams(("parallel", "parallel", "arbitrary")),
    )(*args)
    return res[0] if n_out == 1 else res


def _rowcall(name, fn, nrows, tr, rows_in, vecs_in, rows_out, accs_out, ncol=1):
    nrt = nrows // tr
    in_specs, args = [], []
    for arr, w, ro, co, pc in rows_in:
        in_specs.append(pl.BlockSpec((tr, w), functools.partial(lambda j, i, ro, co, pc: (i + ro, co + (j if pc else 0)), ro=ro, co=co, pc=pc)))
        args.append(arr)
    for arr, w, co, pc in vecs_in:
        in_specs.append(pl.BlockSpec((arr.shape[0], w), functools.partial(lambda j, i, co, pc: (0, co + (j if pc else 0)), co=co, pc=pc)))
        args.append(arr)
    out_specs, out_shape = [], []
    for w, dt, pc in rows_out:
        out_specs.append(pl.BlockSpec((tr, w), functools.partial(lambda j, i, pc: (i, j if pc else 0), pc=pc)))
        out_shape.append(jax.ShapeDtypeStruct((nrows, w * (ncol if pc else 1)), dt))
    for k, w, pc in accs_out:
        out_specs.append(pl.BlockSpec((k, w), functools.partial(lambda j, i, pc: (0, j if pc else 0), pc=pc)))
        out_shape.append(jax.ShapeDtypeStruct((k, w * (ncol if pc else 1)), f32))
    n_in, n_ro = len(args), len(rows_out)

    def body(*refs):
        j, i = pl.program_id(0), pl.program_id(1)
        outs = fn(j, *[r[...] for r in refs[:n_in]])
        for r, val in zip(refs[n_in:n_in + n_ro], outs[:n_ro]):
            r[...] = val.astype(r.dtype)
        for (k, w, pc), r, val in zip(accs_out, refs[n_in + n_ro:], outs[n_ro:]):
            first = (i == 0) if pc else jnp.logical_and(i == 0, j == 0)

            @pl.when(first)
            def _(r=r):
                r[...] = jnp.zeros_like(r)

            r[...] += val

    res = pl.pallas_call(
        body, name=name, grid=(ncol, nrt), in_specs=in_specs, out_specs=out_specs, out_shape=out_shape,
        compiler_params=_cparams(("arbitrary", "arbitrary")),
    )(*args)
    return res


def _tap_room(mode, nrows, nlat):
    t = lax.broadcasted_iota(jnp.int32, (nrows, 1), 0)
    if mode == "seg":
        lat = t < nlat
        return t - jnp.where(lat, 0, nlat), jnp.where(lat, nlat, nrows) - 1 - t
    if mode == "row":
        p = t & (GW - 1)
        return p, GW - 1 - p
    return t // GW, (nrows - 1 - t) // GW


def _tap_valid(room, d):
    return room[1] >= d if d > 0 else room[0] >= -d


def _conv(x, w_ref, ktaps, mode, nlat, flip=False):
    nrows = x.shape[0]
    stride = GW if mode == "col" else 1
    room = _tap_room(mode, nrows, nlat)
    acc = jnp.zeros_like(x)
    for j in range(ktaps):
        d = j - ktaps // 2
        jj = ktaps - 1 - j if flip else j
        wj = w_ref[jj:jj + 1, :]
        if d == 0:
            acc = acc + x * wj
        else:
            xs = pltpu.roll(x, (-d * stride) % nrows, 0)
            acc = acc + jnp.where(_tap_valid(room, d), xs * wj, 0.0)
    return acc


def _conv_wgrad(dy, x, dw_ref, ktaps, mode, nlat):
    nrows = x.shape[0]
    stride = GW if mode == "col" else 1
    room = _tap_room(mode, nrows, nlat)
    dw_ref[...] = jnp.zeros_like(dw_ref)
    for j in range(ktaps):
        d = j - ktaps // 2
        if d == 0:
            prod = x * dy
        else:
            xs = pltpu.roll(x, (-d * stride) % nrows, 0)
            prod = jnp.where(_tap_valid(room, d), xs * dy, 0.0)
        dw_ref[j:j + 1, :] = _csum(prod)


def _qkv_post(j, pre, nh):
    a = _silu(pre)
    inv = lax.rsqrt(jnp.sum(a * a, axis=-1, keepdims=True) + 1e-6)
    scale = jnp.where(j < nh, HD ** -0.5, 1.0).astype(f32)
    nrm = jnp.where(j < 2 * nh, inv, 1.0) * scale
    return a, inv, nrm


def _qkv_conv_fwd(h_all, w7, nrows, nlat, nh, ktaps):
    ntile = 3 * nh

    def body(h_ref, w_ref, o_ref):
        j = pl.program_id(0)
        pre = _conv(h_ref[...], w_ref, ktaps, "seg", nlat)
        a, _, nrm = _qkv_post(j, pre, nh)
        o_ref[...] = a * nrm

    return pl.pallas_call(
        body, name="qkv_conv_fwd", grid=(ntile,),
        in_specs=[pl.BlockSpec((nrows, HD), lambda j: (0, j)), pl.BlockSpec((w7.shape[0], HD), lambda j: (0, j))],
        out_specs=pl.BlockSpec((None, nrows, HD), lambda j: (j, 0, 0)),
        out_shape=jax.ShapeDtypeStruct((ntile, nrows, HD), f32), compiler_params=_cparams(("parallel",)),
    )(h_all, w7)


def _qkv_conv_bwd(h_all, w7, dqkv, nrows, nlat, nh, ktaps):
    ntile = 3 * nh

    def body(h_ref, w_ref, d0_ref, d1_ref, dh_ref, dw_ref):
        j = pl.program_id(0)
        hx = h_ref[...]
        pre = _conv(hx, w_ref, ktaps, "seg", nlat)
        a, inv, nrm = _qkv_post(j, pre, nh)
        dn = (d0_ref[...] + d1_ref[...]) * jnp.where(j < nh, HD ** -0.5, 1.0).astype(f32)
        n = a * inv
        da_norm = inv * (dn - n * jnp.sum(dn * n, axis=-1, keepdims=True))
        da = jnp.where(j < 2 * nh, da_norm, dn)
        dpre = da * _dsilu(pre)
        dh_ref[...] = _conv(dpre, w_ref, ktaps, "seg", nlat, flip=True).astype(dh_ref.dtype)
        _conv_wgrad(dpre, hx, dw_ref, ktaps, "seg", nlat)

    blk = pl.BlockSpec((nrows, HD), lambda j: (0, j))
    wblk = pl.BlockSpec((w7.shape[0], HD), lambda j: (0, j))
    dblk = lambda d: pl.BlockSpec((None, None, nrows, HD), lambda j: (d, j, 0, 0))
    return pl.pallas_call(
        body, name="qkv_conv_bwd", grid=(ntile,), in_specs=[blk, wblk, dblk(0), dblk(1)], out_specs=[blk, wblk],
        out_shape=[jax.ShapeDtypeStruct((nrows, ntile * HD), bf16), jax.ShapeDtypeStruct((w7.shape[0], ntile * HD), f32)],
        compiler_params=_cparams(("parallel",)),
    )(h_all, w7, dqkv, dqkv)


def _conf_conv_fwd(h_all, w31, nlat, conf, val_blk, ktaps):
    nt = conf // LANES
    nhalf = nt // 2

    def body(v_ref, g_ref, w_ref, o_ref):
        j = pl.program_id(0)
        glu = v_ref[...] * _sigmoid(g_ref[...])

        @pl.when(j < nhalf)
        def _():
            o_ref[...] = _conv(glu, w_ref, ktaps, "row", nlat)

        @pl.when(j >= nhalf)
        def _():
            o_ref[...] = _conv(glu, w_ref, ktaps, "col", nlat)

    return pl.pallas_call(
        body, name="conf_conv_fwd", grid=(nt,),
        in_specs=[pl.BlockSpec((nlat, LANES), lambda j: (0, val_blk + j)), pl.BlockSpec((nlat, LANES), lambda j: (0, val_blk + nt + j)),
                  pl.BlockSpec((w31.shape[0], LANES), lambda j: (0, j))],
        out_specs=pl.BlockSpec((nlat, LANES), lambda j: (0, j)),
        out_shape=jax.ShapeDtypeStruct((nlat, conf), f32), compiler_params=_cparams(("parallel",)),
    )(h_all, h_all, w31)


def _conf_conv_bwd(h_all, w31, dyc, nlat, conf, val_blk, ktaps):
    nt = conf // LANES
    nhalf = nt // 2

    def body(v_ref, g_ref, w_ref, dy_ref, dv_ref, dg_ref, dw_ref):
        j = pl.program_id(0)
        val, sg = v_ref[...], _sigmoid(g_ref[...])
        glu = val * sg
        dy = dy_ref[...]

        def run(mode):
            dglu = _conv(dy, w_ref, ktaps, mode, nlat, flip=True)
            dv_ref[...] = (dglu * sg).astype(dv_ref.dtype)
            dg_ref[...] = (dglu * val * sg * (1.0 - sg)).astype(dg_ref.dtype)
            _conv_wgrad(dy, glu, dw_ref, ktaps, mode, nlat)

        @pl.when(j < nhalf)
        def _():
            run("row")

        @pl.when(j >= nhalf)
        def _():
            run("col")

    blk = pl.BlockSpec((nlat, LANES), lambda j: (0, j))
    wblk = pl.BlockSpec((w31.shape[0], LANES), lambda j: (0, j))
    return pl.pallas_call(
        body, name="conf_conv_bwd", grid=(nt,),
        in_specs=[pl.BlockSpec((nlat, LANES), lambda j: (0, val_blk + j)), pl.BlockSpec((nlat, LANES), lambda j: (0, val_blk + nt + j)), wblk, blk],
        out_specs=[blk, blk, wblk],
        out_shape=[jax.ShapeDtypeStruct((nlat, conf), bf16), jax.ShapeDtypeStruct((nlat, conf), bf16),
                   jax.ShapeDtypeStruct((w31.shape[0], conf), f32)],
        compiler_params=_cparams(("parallel",)),
    )(h_all, h_all, w31, dyc)


def _bd(eq, a, b):
    return jnp.einsum(eq, a.astype(bf16), b.astype(bf16), preferred_element_type=f32)


def _split(x, pieces):
    out = []
    for _ in range(pieces - 1):
        p = x.astype(bf16)
        out.append(p)
        x = x - p.astype(f32)
    return out + [x.astype(bf16)]


def _h3(eq, a, b):
    (ah, al), (bh, bl) = _split(a, 2), _split(b, 2)
    d = lambda p, q: jnp.einsum(eq, p, q, preferred_element_type=f32)
    return d(ah, bh) + (d(ah, bl) + d(al, bh))


def _cumsum_rows(x, rev):
    row = lax.broadcasted_iota(jnp.int32, x.shape, 0)
    down, up, s = x, x, 1
    while s < CH:
        down = down + jnp.where(row >= s, pltpu.roll(down, s, 0), 0.0)
        up = up + jnp.where(row < CH - s, pltpu.roll(up, CH - s, 0), 0.0)
        s *= 2
    return jnp.where(rev, up, down)


def _gdn_chunk(q, k, v, gt, nh, rev, solved=None):
    beta = jnp.stack([jnp.broadcast_to(gt[:, nh + h:nh + h + 1], (CH, LANES)) for h in range(nh)])
    ii = lax.broadcasted_iota(jnp.int32, (CH, CH), 0)
    jj = lax.broadcasted_iota(jnp.int32, (CH, CH), 1)
    si, sj = jnp.where(rev, jj, ii), jnp.where(rev, ii, jj)
    tril, stril = (si >= sj)[None], (si > sj)[None]
    gct = _cumsum_rows(gt, rev)
    gct_t = gct.T
    gc = jnp.stack([jnp.broadcast_to(gct[:, h:h + 1], (CH, LANES)) for h in range(nh)])
    gc_row = jnp.stack([jnp.broadcast_to(gct_t[h:h + 1, :], (CH, CH)) for h in range(nh)])
    diff = gc[:, :, :CH] - gc_row
    gam = jnp.where(tril, jnp.exp(jnp.where(tril, diff, 0.0)), 0.0)
    e = jnp.exp(gc)
    gl = jnp.where(rev, gc[:, 0:1, :], gc[:, CH - 1:CH, :])
    el = jnp.exp(gl)
    r = jnp.exp(gl - gc)
    kb, vb = k * beta, v * beta
    kbe = kb * e
    amat = jnp.where(stril, _bd("hik,hjk->hij", kb, k) * gam, 0.0)
    if solved is None:
        xp = -amat
        tinv = (ii == jj).astype(f32)[None] + xp
        for _ in range(5):
            xp = _h3("hij,hjk->hik", xp, xp)
            tinv = tinv + _h3("hij,hjk->hik", tinv, xp)
        u = _h3("hij,hjv->hiv", tinv, vb)
        w = _h3("hij,hjk->hik", tinv, kbe)
    else:
        tinv, u, w = solved
    pmat = jnp.where(tril, _bd("hik,hjk->hij", q, k) * gam, 0.0)
    return dict(beta=beta, gam=gam, e=e, el=el, r=r, kb=kb, vb=vb, kbe=kbe, amat=amat, tinv=tinv, u=u, w=w, pmat=pmat,
                qd=q * e, kd=k * r, tril=tril, stril=stril)


def _scan_row_chunk(d, n, ns, nt):
    fwd = jnp.where(n < nt, ns + n, n - nt)
    return jnp.where(d == 0, fwd, ns + nt - 1 - n)


def _gdn_fwd(qkv, gd, nh, nlat, nctx):
    nrows = nlat + nctx
    ns, nt = nlat // CH, nctx // CH
    nch = ns + nt

    def body(qkv_ref, g_ref, o_ref, st_ref, ti_ref, u_ref, w_ref, state):
        d, n = pl.program_id(0), pl.program_id(1)

        @pl.when(n == 0)
        def _():
            state[...] = jnp.zeros_like(state)

        c = _gdn_chunk(qkv_ref[0], qkv_ref[1], qkv_ref[2], g_ref[...], nh, d == 1)
        s = state[...]
        vn = c["u"] - _bd("hck,hkv->hcv", c["w"], s)
        o_ref[...] = _bd("hck,hkv->hcv", c["qd"], s) + _bd("hcd,hdv->hcv", c["pmat"], vn)
        st_ref[...] = s
        ti_ref[...], u_ref[...], w_ref[...] = c["tinv"], c["u"], c["w"]
        state[...] = s * c["el"] + _bd("hck,hcv->hkv", c["kd"], vn)

    hblk = pl.BlockSpec((None, nh, CH, HD), lambda d, n: (d, 0, _scan_row_chunk(d, n, ns, nt), 0))
    hshape = jax.ShapeDtypeStruct((2, nh, nrows, HD), f32)
    return pl.pallas_call(
        body, name="gdn_fwd", grid=(2, nch),
        in_specs=[pl.BlockSpec((3, nh, CH, HD), lambda d, n: (0, 0, _scan_row_chunk(d, n, ns, nt), 0)),
                  pl.BlockSpec((None, CH, LANES), lambda d, n: (d, _scan_row_chunk(d, n, ns, nt), 0))],
        out_specs=[hblk, pl.BlockSpec((None, None, nh, HD, HD), lambda d, n: (d, n, 0, 0, 0)),
                   pl.BlockSpec((None, None, nh, CH, CH), lambda d, n: (d, n, 0, 0, 0)), hblk, hblk],
        out_shape=[hshape, jax.ShapeDtypeStruct((2, nch, nh, HD, HD), f32), jax.ShapeDtypeStruct((2, nch, nh, CH, CH), f32), hshape, hshape],
        scratch_shapes=[pltpu.VMEM((nh, HD, HD), f32)], compiler_params=_cparams(("arbitrary", "arbitrary")),
    )(qkv, gd)


def _gdn_bwd(qkv, gd, states, solved, do, nh, nlat, nctx):
    nrows = nlat + nctx
    ns, nt = nlat // CH, nctx // CH
    nch = ns + nt

    def body(qkv_ref, g_ref, st_ref, ti_ref, u_ref, w_ref, do_ref, dqkv_ref, dgd_ref, dstate):
        d, step = pl.program_id(0), pl.program_id(1)

        @pl.when(step == 0)
        def _():
            dstate[...] = jnp.zeros_like(dstate)

        q, k, v = qkv_ref[0], qkv_ref[1], qkv_ref[2]
        c = _gdn_chunk(q, k, v, g_ref[...], nh, d == 1, solved=(ti_ref[...], u_ref[...], w_ref[...]))
        in_ctx = (nch - 1 - step) < nt
        s, dsp = st_ref[...], dstate[...]
        dout = jnp.where(in_ctx, 0.0, do_ref[...])
        beta, gam, e, el, r = c["beta"], c["gam"], c["e"], c["el"], c["r"]
        tinv, u, w, pmat, amat = c["tinv"], c["u"], c["w"], c["pmat"], c["amat"]
        vn = u - _bd("hck,hkv->hcv", w, s)
        dvn = _bd("hdc,hdv->hcv", pmat, dout) + _bd("hck,hkv->hcv", c["kd"], dsp)
        dp = jnp.where(c["tril"], _bd("hcv,hdv->hcd", dout, vn), 0.0)
        dqd = _bd("hcv,hkv->hck", dout, s)
        dkd = _bd("hcv,hkv->hck", vn, dsp)
        dstate[...] = _bd("hck,hcv->hkv", c["qd"], dout) + dsp * el - _bd("hck,hcv->hkv", w, dvn)
        del_ = jnp.sum(jnp.sum(s * dsp, axis=2, keepdims=True), axis=1, keepdims=True)
        dw = -_bd("hcv,hkv->hck", dvn, s)
        dvb = _h3("hji,hjv->hiv", tinv, dvn)
        dkbe = _h3("hji,hjk->hik", tinv, dw)
        z = _bd("hiv,hjv->hij", dvn, u) + _bd("hik,hjk->hij", dw, w)
        da = jnp.where(c["stril"], -_h3("hji,hjl->hil", tinv, z), 0.0)
        dm = da * gam
        dkb = _bd("hij,hjk->hik", dm, k) + dkbe * e
        dn = dp * gam
        dqkv_ref[0] = _bd("hij,hjk->hik", dn, k) + dqd * e
        dqkv_ref[1] = _bd("hji,hjk->hik", dm, c["kb"]) + _bd("hji,hjk->hik", dn, q) + dkd * r + dkb * beta
        dqkv_ref[2] = dvb * beta
        gmat = da * amat + dp * pmat
        rows_minus_cols = jnp.sum(gmat, axis=-1, keepdims=True) - jnp.sum(jnp.swapaxes(gmat, 1, 2), axis=-1, keepdims=True)
        de = jnp.sum(dqd * q + dkbe * c["kb"], axis=-1, keepdims=True)
        drr = jnp.sum(dkd * k, axis=-1, keepdims=True) * r
        dgc = rows_minus_cols + de * e - drr
        dgl = jnp.sum(drr, axis=1, keepdims=True) + del_ * el
        row = lax.broadcasted_iota(jnp.int32, (1, CH, 1), 1)
        total_row = row == jnp.where(d == 1, 0, CH - 1)
        dgc = dgc + jnp.where(total_row, dgl, 0.0)
        db = jnp.sum(dvb * v + dkb * k, axis=-1, keepdims=True)
        lane = lax.broadcasted_iota(jnp.int32, (CH, LANES), 1)
        dgc_cols, db_cols = jnp.zeros((CH, LANES), f32), jnp.zeros((CH, LANES), f32)
        for h in range(nh):
            dgc_cols = dgc_cols + jnp.where(lane == h, dgc[h], 0.0)
            db_cols = db_cols + jnp.where(lane == nh + h, db[h], 0.0)
        dgd_ref[...] = _cumsum_rows(dgc_cols, d == 0) + db_cols

    def rc(d, n):
        return _scan_row_chunk(d, nch - 1 - n, ns, nt)

    gblk = pl.BlockSpec((None, CH, LANES), lambda d, n: (d, rc(d, n), 0))
    return pl.pallas_call(
        body, name="gdn_bwd", grid=(2, nch),
        in_specs=[pl.BlockSpec((3, nh, CH, HD), lambda d, n: (0, 0, rc(d, n), 0)),
                  gblk,
                  pl.BlockSpec((None, None, nh, HD, HD), lambda d, n: (d, nch - 1 - n, 0, 0, 0)),
                  pl.BlockSpec((None, None, nh, CH, CH), lambda d, n: (d, nch - 1 - n, 0, 0, 0)),
                  pl.BlockSpec((None, nh, CH, HD), lambda d, n: (d, 0, rc(d, n), 0)),
                  pl.BlockSpec((None, nh, CH, HD), lambda d, n: (d, 0, rc(d, n), 0)),
                  pl.BlockSpec((nh, CH, HD), lambda d, n: (0, jnp.minimum(rc(d, n), ns - 1), 0))],
        out_specs=[pl.BlockSpec((None, 3, nh, CH, HD), lambda d, n: (d, 0, 0, rc(d, n), 0)), gblk],
        out_shape=[jax.ShapeDtypeStruct((2, 3, nh, nrows, HD), f32), jax.ShapeDtypeStruct((2, nrows, LANES), f32)],
        scratch_shapes=[pltpu.VMEM((nh, HD, HD), f32)], compiler_params=_cparams(("arbitrary", "arbitrary")),
    )(qkv, gd, states, *solved, do)


def _gate_norm_fwd(o5, h_all, dng, nh, nlat, zblk, tr):
    def body(of_ref, ob_ref, z_ref, g_ref, out_ref):
        o = of_ref[...] + ob_ref[...]
        rs = lax.rsqrt(jnp.mean(o * o, axis=-1, keepdims=True) + 1e-6)
        out_ref[...] = ((o * rs * g_ref[...]) * _silu(z_ref[...])).astype(out_ref.dtype)

    return pl.pallas_call(
        body, name="gate_norm_fwd", grid=(nh, nlat // tr),
        in_specs=[pl.BlockSpec((None, None, tr, HD), lambda j, i: (0, j, i, 0)), pl.BlockSpec((None, None, tr, HD), lambda j, i: (1, j, i, 0)),
                  pl.BlockSpec((tr, HD), lambda j, i: (i, zblk + j)), pl.BlockSpec((1, HD), lambda j, i: (0, 0))],
        out_specs=pl.BlockSpec((tr, HD), lambda j, i: (i, j)), out_shape=jax.ShapeDtypeStruct((nlat, nh * HD), bf16),
        compiler_params=_cparams(("parallel", "parallel")),
    )(o5, o5, h_all, dng)


def _gate_norm_bwd(dmix, o5, h_all, dng, nh, nlat, zblk, tr):
    def body(dm_ref, of_ref, ob_ref, z_ref, g_ref, do_ref, dz_ref, dg_ref):
        o = of_ref[...] + ob_ref[...]
        z, g, dm = z_ref[...], g_ref[...], dm_ref[...]
        rs = lax.rsqrt(jnp.mean(o * o, axis=-1, keepdims=True) + 1e-6)
        don = dm * _silu(z)
        dz_ref[...] = (dm * (o * rs * g) * _dsilu(z)).astype(dz_ref.dtype)
        dog = don * g
        do_ref[...] = rs * (dog - o * rs * rs * jnp.mean(dog * o, axis=-1, keepdims=True))

        @pl.when(jnp.logical_and(pl.program_id(0) == 0, pl.program_id(1) == 0))
        def _():
            dg_ref[...] = jnp.zeros_like(dg_ref)

        dg_ref[...] += _csum(don * o * rs)

    return pl.pallas_call(
        body, name="gate_norm_bwd", grid=(nh, nlat // tr),
        in_specs=[pl.BlockSpec((tr, HD), lambda j, i: (i, j)),
                  pl.BlockSpec((None, None, tr, HD), lambda j, i: (0, j, i, 0)), pl.BlockSpec((None, None, tr, HD), lambda j, i: (1, j, i, 0)),
                  pl.BlockSpec((tr, HD), lambda j, i: (i, zblk + j)), pl.BlockSpec((1, HD), lambda j, i: (0, 0))],
        out_specs=[pl.BlockSpec((None, tr, HD), lambda j, i: (j, i, 0)), pl.BlockSpec((tr, HD), lambda j, i: (i, j)),
                   pl.BlockSpec((1, HD), lambda j, i: (0, 0))],
        out_shape=[jax.ShapeDtypeStruct((nh, nlat, HD), f32), jax.ShapeDtypeStruct((nlat, nh * HD), bf16), jax.ShapeDtypeStruct((1, HD), f32)],
        compiler_params=_cparams(("arbitrary", "arbitrary")),
    )(dmix, o5, o5, h_all, dng)


def _mesh_pos():
    return lax.axis_index("x"), lax.axis_index("y"), lax.axis_index("c")


def _lin(p):
    return 4 * p[0] + 2 * p[1] + p[2]


def _all_gather(name, xs, after=None):
    nx = len(xs)
    extra = [] if after is None else [after]

    def body(*refs):
        xr, outr = refs[:nx], refs[nx + len(extra):2 * nx + len(extra)]
        send, recv, loc = refs[2 * nx + len(extra):]
        x, y, c = _mesh_pos()
        me, sib = (x, y, c), (x, y, 1 - c)
        chips = [(1 - x, y), (x, 1 - y), (1 - x, 1 - y)]

        def cp(l, k, block, to, src=None):
            rows = outr[l].at[_lin(block)]
            return pltpu.make_async_remote_copy(src_ref=rows if src is None else src, dst_ref=rows, send_sem=send.at[l, k],
                                                recv_sem=recv.at[l, k], device_id=to, device_id_type=MESH)

        mine = [pltpu.make_async_copy(xr[l], outr[l].at[_lin(me)], loc.at[l]) for l in range(nx)]
        for m in mine:
            m.start()
        first = []
        for l in range(nx):
            first.append(cp(l, 0, me, sib, src=xr[l]))
            first += [cp(l, 1 + j, me, (*chip, c), src=xr[l]) for j, chip in enumerate(chips)]
        for f in first:
            f.start()
        passed = []
        for l in range(nx):
            for j, chip in enumerate(chips):
                cp(l, 1 + j, (*chip, c), me).wait_recv()
                fwd = cp(l, 4 + j, (*chip, c), sib)
                fwd.start()
                passed.append(fwd)
        for l in range(nx):
            cp(l, 0, sib, me).wait_recv()
            for j, chip in enumerate(chips):
                cp(l, 4 + j, (*chip, 1 - c), me).wait_recv()
        for f in first + passed:
            f.wait_send()
        for m in mine:
            m.wait()

    anyspec = pl.BlockSpec(memory_space=pl.ANY)
    return pl.pallas_call(
        body, name=name, in_specs=[anyspec] * (nx + len(extra)), out_specs=[anyspec] * nx,
        out_shape=[jax.ShapeDtypeStruct((NDEV,) + a.shape, a.dtype) for a in xs],
        scratch_shapes=[pltpu.SemaphoreType.DMA((nx, 7)), pltpu.SemaphoreType.DMA((nx, 7)), pltpu.SemaphoreType.DMA((nx,))],
    )(*xs, *extra)


HBM_SPEC = pl.BlockSpec(memory_space=pltpu.HBM)
SEM_SPEC = pl.BlockSpec(memory_space=pltpu.SEMAPHORE)
ANY_SPEC = pl.BlockSpec(memory_space=pl.ANY)
EFFECT = pltpu.SideEffectType.DATAFLOW_SIDE_EFFECTING


def _peer(rel):
    x, y, c = _mesh_pos()
    return (1 - x if rel & 4 else x, 1 - y if rel & 2 else y, 1 - c if rel & 1 else c)


ALL_PEERS = (1, 2, 3, 4, 5, 6, 7)
SAME_CORE = (2, 4, 6)
CORE_AND_SIB = (1, 2, 4, 6)


def _slot(p, rels):
    return 2 * p[0] + p[1] if rels == SAME_CORE else _lin(p)


def _xchg_start(name, x, gather, after, rels=ALL_PEERS, land_init=None):
    me = _slot(_mesh_pos(), rels)
    shape = ((4 if rels == SAME_CORE else NDEV),) + x.shape if gather else x.shape
    own = x if gather else lax.dynamic_index_in_dim(x, me, 0, keepdims=False)
    land = lax.dynamic_update_index_in_dim(lax.empty(shape, x.dtype) if land_init is None else land_init, own, me, 0)

    def body(x_ref, land_ref, after_ref, send, recv, x_thru, land_thru, token):
        mine = _slot(_mesh_pos(), rels)
        for k, rel in enumerate(rels):
            p = _peer(rel)
            pltpu.make_async_remote_copy(src_ref=x_ref if gather else x_ref.at[_slot(p, rels)], dst_ref=land_ref.at[mine],
                                         send_sem=send.at[k], recv_sem=recv.at[k], device_id=p, device_id_type=MESH).start()
        token[...] = jnp.zeros_like(token)

    return pl.pallas_call(
        body, name=name,
        out_shape=(pltpu.SemaphoreType.DMA((len(rels),)), pltpu.SemaphoreType.DMA((len(rels),)), pltpu.HBM(x.shape, x.dtype),
                   pltpu.HBM(shape, x.dtype), jax.ShapeDtypeStruct((8, LANES), f32)),
        in_specs=(HBM_SPEC, HBM_SPEC, ANY_SPEC), out_specs=(SEM_SPEC, SEM_SPEC, HBM_SPEC, HBM_SPEC, pl.BlockSpec(memory_space=pltpu.VMEM)),
        input_output_aliases={0: 2, 1: 3}, compiler_params=pltpu.CompilerParams(has_side_effects=EFFECT),
    )(pltpu.with_memory_space_constraint(x, pltpu.HBM), pltpu.with_memory_space_constraint(land, pltpu.HBM), after)


def _xchg_wait(name, started, gather, after, rels=ALL_PEERS):
    send, recv, x_thru, land_thru, _ = started

    def body(x_ref, land_ref, send, recv, after_ref, x_dead, got_ref):
        for k, rel in enumerate(rels):
            p = _peer(rel)
            cp = pltpu.make_async_remote_copy(src_ref=x_ref if gather else x_ref.at[_slot(p, rels)], dst_ref=land_ref.at[_slot(p, rels)],
                                              send_sem=send.at[k], recv_sem=recv.at[k], device_id=p, device_id_type=MESH)
            cp.wait_send()
            cp.wait_recv()

    return pl.pallas_call(
        body, name=name, out_shape=(pltpu.HBM(x_thru.shape, x_thru.dtype), pltpu.HBM(land_thru.shape, land_thru.dtype)),
        in_specs=(HBM_SPEC, HBM_SPEC, SEM_SPEC, SEM_SPEC, ANY_SPEC), out_specs=(HBM_SPEC, HBM_SPEC), input_output_aliases={0: 0, 1: 1},
        compiler_params=pltpu.CompilerParams(has_side_effects=EFFECT),
    )(x_thru, land_thru, send, recv, after)[1]


def _sib_forward(name, land):
    def body(land_ref, out_ref, send, recv):
        mx, my, mc = _mesh_pos()
        sib = (mx, my, 1 - mc)

        def cp(k, block):
            return pltpu.make_async_remote_copy(src_ref=out_ref.at[_lin(block)], dst_ref=out_ref.at[_lin(block)], send_sem=send.at[k],
                                                recv_sem=recv.at[k], device_id=sib, device_id_type=MESH)

        chips = [_peer(rel)[:2] for rel in SAME_CORE]
        sends = [cp(k, (*chip, mc)) for k, chip in enumerate(chips)]
        for s in sends:
            s.start()
        for k, chip in enumerate(chips):
            cp(k, (*chip, 1 - mc)).wait_recv()
        for s in sends:
            s.wait_send()

    return pl.pallas_call(
        body, name=name, in_specs=[ANY_SPEC], out_specs=ANY_SPEC, out_shape=jax.ShapeDtypeStruct(land.shape, land.dtype),
        input_output_aliases={0: 0}, scratch_shapes=[pltpu.SemaphoreType.DMA((3,)), pltpu.SemaphoreType.DMA((3,))],
    )(land)


def _sib_swap(name, x):
    def body(x_ref, out_ref, send, recv):
        mx, my, mc = _mesh_pos()
        sib = (mx, my, 1 - mc)
        cps = [pltpu.make_async_remote_copy(src_ref=x_ref.at[2 * q + 1 - mc], dst_ref=out_ref.at[q], send_sem=send.at[q], recv_sem=recv.at[q],
                                            device_id=sib, device_id_type=MESH) for q in range(4)]
        for cp in cps:
            cp.start()
        for cp in cps:
            cp.wait_recv()
        for cp in cps:
            cp.wait_send()

    return pl.pallas_call(
        body, name=name, in_specs=[ANY_SPEC], out_specs=ANY_SPEC, out_shape=jax.ShapeDtypeStruct((4,) + x.shape[1:], x.dtype),
        scratch_shapes=[pltpu.SemaphoreType.DMA((4,)), pltpu.SemaphoreType.DMA((4,))],
    )(x)


def _behind(token, a):
    return a + token[0:1, 0:1].astype(a.dtype)


def _sum8(name, g):
    def body(g_ref, o_ref):
        acc = g_ref[0:1, :]
        for p in range(1, NDEV):
            acc = acc + g_ref[p:p + 1, :]
        o_ref[...] = acc

    return pl.pallas_call(body, name=name, out_shape=jax.ShapeDtypeStruct((1, g.shape[1]), f32),
                          compiler_params=pltpu.CompilerParams(vmem_limit_bytes=VMEM_LIMIT))(g)


def _pack(arrs):
    flat = jnp.concatenate([a.reshape(-1).astype(f32) for a in arrs])
    pad = (-flat.shape[0]) % (8 * LANES)
    return jnp.pad(flat, (0, pad)).reshape(1, -1)


def _unpack(vec, shapes):
    out, off = [], 0
    flat = vec.reshape(-1)
    for s in shapes:
        n = 1
        for d in s:
            n *= d
        out.append(flat[off:off + n].reshape(s))
        off += n
    return out


def _adam(name, w, m, v, g, parts, after=None):
    extra = [] if after is None else [after]
    nrows, width = w.shape
    tr = _row_tile(nrows, width) if nrows >= 16 else nrows
    blk = pl.BlockSpec((tr, width), lambda i: (i, 0))
    gblk = pl.BlockSpec((parts, tr, width), lambda i: (0, i, 0)) if parts else blk
    c1 = 1.0 / (1.0 - ADAM_B1 ** ADAM_STEP)
    c2 = 1.0 / (1.0 - ADAM_B2 ** ADAM_STEP)

    def body(w_ref, m_ref, v_ref, g_ref, *rest):
        go_ref, d_ref, mo_ref, vo_ref = rest[len(extra):]
        if parts:
            gg = g_ref[0].astype(f32)
            for p in range(1, parts):
                gg = gg + g_ref[p].astype(f32)
        else:
            gg = g_ref[...]
        mn = ADAM_B1 * m_ref[...] + (1.0 - ADAM_B1) * gg
        vn = ADAM_B2 * v_ref[...] + (1.0 - ADAM_B2) * (gg * gg)
        go_ref[...] = gg
        mo_ref[...] = mn
        vo_ref[...] = vn
        d_ref[...] = -ADAM_LR * ((mn * c1) / (jnp.sqrt(vn * c2) + ADAM_EPS) + ADAM_WD * w_ref[...])

    sh = jax.ShapeDtypeStruct((nrows, width), f32)
    return pl.pallas_call(body, name=name, grid=(nrows // tr,), in_specs=[blk, blk, blk, gblk] + [pl.BlockSpec(memory_space=pl.ANY)] * len(extra),
                          out_specs=[blk] * 4, out_shape=[sh] * 4, compiler_params=_cparams(("parallel",)))(w, m, v, g, *extra)


def kernel(x, c, ctx, c_ctx, ln_in_g, ln_in_b, w_mod, b_mod, w_in, w_qkv_conv, a_log_f, dt_bias_f, a_log_b, dt_bias_b, dn_norm_g, conf_dw_w, conf_dw_b, conf_ln_g, conf_ln_b, w_out, ln1_g, ln1_b, w_mlp1, b_mlp1, w_mlp2, b_mlp2, ln2_g, ln2_b, loss_target, m_c_ctx, m_ln_in_g, m_ln_in_b, m_w_mod, m_b_mod, m_w_in, m_w_qkv_conv, m_a_log_f, m_dt_bias_f, m_a_log_b, m_dt_bias_b, m_dn_norm_g, m_conf_dw_w, m_conf_dw_b, m_conf_ln_g, m_conf_ln_b, m_w_out, m_ln1_g, m_ln1_b, m_w_mlp1, m_b_mlp1, m_w_mlp2, m_b_mlp2, m_ln2_g, m_ln2_b, v_c_ctx, v_ln_in_g, v_ln_in_b, v_w_mod, v_b_mod, v_w_in, v_w_qkv_conv, v_a_log_f, v_dt_bias_f, v_a_log_b, v_dt_bias_b, v_dn_norm_g, v_conf_dw_w, v_conf_dw_b, v_conf_ln_g, v_conf_ln_b, v_w_out, v_ln1_g, v_ln1_b, v_w_mlp1, v_b_mlp1, v_w_mlp2, v_b_mlp2, v_ln2_g, v_ln2_b):
    weights = dict(c_ctx=c_ctx, ln_in_g=ln_in_g, ln_in_b=ln_in_b, w_mod=w_mod, b_mod=b_mod, w_in=w_in, w_qkv_conv=w_qkv_conv, a_log_f=a_log_f, dt_bias_f=dt_bias_f, a_log_b=a_log_b, dt_bias_b=dt_bias_b, dn_norm_g=dn_norm_g, conf_dw_w=conf_dw_w, conf_dw_b=conf_dw_b, conf_ln_g=conf_ln_g, conf_ln_b=conf_ln_b, w_out=w_out, ln1_g=ln1_g, ln1_b=ln1_b, w_mlp1=w_mlp1, b_mlp1=b_mlp1, w_mlp2=w_mlp2, b_mlp2=b_mlp2, ln2_g=ln2_g, ln2_b=ln2_b)
    mom1 = dict(c_ctx=m_c_ctx, ln_in_g=m_ln_in_g, ln_in_b=m_ln_in_b, w_mod=m_w_mod, b_mod=m_b_mod, w_in=m_w_in, w_qkv_conv=m_w_qkv_conv, a_log_f=m_a_log_f, dt_bias_f=m_dt_bias_f, a_log_b=m_a_log_b, dt_bias_b=m_dt_bias_b, dn_norm_g=m_dn_norm_g, conf_dw_w=m_conf_dw_w, conf_dw_b=m_conf_dw_b, conf_ln_g=m_conf_ln_g, conf_ln_b=m_conf_ln_b, w_out=m_w_out, ln1_g=m_ln1_g, ln1_b=m_ln1_b, w_mlp1=m_w_mlp1, b_mlp1=m_b_mlp1, w_mlp2=m_w_mlp2, b_mlp2=m_b_mlp2, ln2_g=m_ln2_g, ln2_b=m_ln2_b)
    mom2 = dict(c_ctx=v_c_ctx, ln_in_g=v_ln_in_g, ln_in_b=v_ln_in_b, w_mod=v_w_mod, b_mod=v_b_mod, w_in=v_w_in, w_qkv_conv=v_w_qkv_conv, a_log_f=v_a_log_f, dt_bias_f=v_dt_bias_f, a_log_b=v_a_log_b, dt_bias_b=v_dt_bias_b, dn_norm_g=v_dn_norm_g, conf_dw_w=v_conf_dw_w, conf_dw_b=v_conf_dw_b, conf_ln_g=v_conf_ln_g, conf_ln_b=v_conf_ln_b, w_out=v_w_out, ln1_g=v_ln1_g, ln1_b=v_ln1_b, w_mlp1=v_w_mlp1, b_mlp1=v_b_mlp1, w_mlp2=v_w_mlp2, b_mlp2=v_b_mlp2, ln2_g=v_ln2_g, ln2_b=v_ln2_b)
    names = list(weights)

    me = _lin(_mesh_pos())
    S, D = x.shape[1], x.shape[2]
    T = ctx.shape[1]
    R = S + T
    DN = D // 2
    NH = DN // HD
    CONF = D - DN
    K7, K31 = w_qkv_conv.shape[1], conf_dw_w.shape[1]
    DFF = w_mlp1.shape[2] * NDEV
    INC = w_in.shape[2] * NDEV
    CONF_OFF = 4 * DN + 4 * NH
    NC = 4 * DN + 2 * CONF + GPAD
    QB, ZB, VB, GB = 0, 3 * DN // LANES, 4 * DN // LANES, (4 * DN + 2 * CONF) // LANES
    MODC = w_mod.shape[2]
    x2d, ctx2d, tgt = x[0], ctx[0], loss_target[0]
    row = lambda a: a.reshape(1, -1).astype(f32)

    c_all, w7_all, w31_all = _all_gather("ag_small", [c.astype(f32), w_qkv_conv[0], conf_dw_w[0]])
    w7 = jnp.pad(jnp.transpose(w7_all, (1, 0, 2)).reshape(K7, 3 * DN), ((0, (-K7) % 8), (0, 0)))
    w31 = jnp.pad(jnp.transpose(w31_all, (1, 0, 2)).reshape(K31, CONF), ((0, (-K31) % 8), (0, 0)))
    c16 = jnp.concatenate([c_all.reshape(NDEV, D), c_ctx.reshape(1, D).astype(f32), jnp.zeros((7, D), f32)], axis=0)
    (sc16,) = _rowcall("silu_c", lambda j, a: (_silu(a),), 16, 16, [(c16, D, 0, 0, False)], [], [(D, f32, False)], [])
    bmod_mine = lax.dynamic_slice_in_dim(b_mod.astype(f32), me * MODC, MODC, axis=1)
    mod_part = _mm("mod_fwd", sc16, w_mod[0], "nn", f32, extras=[("row", bmod_mine)], epilogue=lambda r, b: (r + b,))
    (mod_g,) = _all_gather("ag_mod", [mod_part])
    mod_all = jnp.transpose(mod_g, (1, 0, 2)).reshape(16, 6 * D)
    mod_me = lax.dynamic_slice_in_dim(mod_all, me, 1, axis=0)
    sh_a, sc_a, g_a, sh_m, sc_m, g_m = [mod_me[:, i * D:(i + 1) * D] for i in range(6)]
    csh_a, csc_a = mod_all[8:9, 0:D], mod_all[8:9, D:2 * D]

    (g_win,) = _all_gather("ag_w_in", [w_in[0].astype(bf16)], after=mod_g)
    ag_wout = _xchg_start("ag_wout_start", w_out[0].astype(bf16), True, g_win)
    ag_w1 = _xchg_start("ag_w1_start", w_mlp1[0].astype(bf16), True, ag_wout[4])
    win_full = jnp.transpose(g_win, (1, 0, 2)).reshape(D, INC)
    w_cat = jnp.concatenate([win_full[:, :4 * DN], win_full[:, CONF_OFF:], win_full[:, 4 * DN:CONF_OFF],
                             jnp.zeros((D, GPAD - 4 * NH), bf16)], axis=1)
    g0, b0 = row(ln_in_g), row(ln_in_b)
    g0_fwd = _behind(ag_w1[4], g0)

    def ln_mod(j, xt, g, b, sh, sc):
        xh, _ = _ln_stats(xt)
        xn = xh * g + b
        return xn, xn * (1.0 + sc) + sh

    tr = _row_tile(T, D)
    vec = lambda a: (a, a.shape[1], 0, False)
    xn, xm = _rowcall("ln_in_lat", ln_mod, S, tr, [(x2d, D, 0, 0, False)], [vec(g0_fwd), vec(b0), vec(sh_a), vec(sc_a)], [(D, f32, False), (D, bf16, False)], [])
    xcn, xcm = _rowcall("ln_in_ctx", ln_mod, T, tr, [(ctx2d, D, 0, 0, False)], [vec(g0), vec(b0), vec(csh_a), vec(csc_a)], [(D, f32, False), (D, bf16, False)], [])
    xm_all = jnp.concatenate([xm, xcm], axis=0)

    h_all = _mm("in_proj", xm_all, w_cat, "nn", f32, tm=1088, tn=1280)

    qkv = _qkv_conv_fwd(h_all, w7, R, S, NH, K7).reshape(3, NH, R, HD)
    lane = jnp.arange(LANES)
    is_a = ((lane < 4 * NH) & ((lane // NH) % 2 == 0)).astype(f32).reshape(1, LANES)
    pad_g = lambda a, b: jnp.concatenate([a.reshape(-1), jnp.zeros((NH,), f32), b.reshape(-1), jnp.zeros((LANES - 3 * NH,), f32)]).reshape(1, LANES)
    neg_a = pad_g(-jnp.exp(a_log_f.astype(f32)), -jnp.exp(a_log_b.astype(f32)))
    dt_v = pad_g(dt_bias_f.astype(f32), dt_bias_b.astype(f32))

    def gates_f(j, hg, isa, na, dt):
        return (jnp.where(isa > 0.5, na * _softplus(hg + dt), _sigmoid(hg)),)

    trg = _row_tile(T, LANES)
    ag_w2 = _xchg_start("ag_w2_start", w_mlp2[0].astype(bf16), True, qkv, CORE_AND_SIB)
    (gates,) = _rowcall("gates_fwd", gates_f, R, trg, [(h_all, LANES, 0, GB, False)], [vec(_behind(ag_w2[4], is_a)), vec(neg_a), vec(dt_v)], [(LANES, f32, False)], [])
    gpad = jnp.zeros((R, LANES - 2 * NH), f32)
    gd = jnp.stack([jnp.concatenate([gates[:, :2 * NH], gpad], axis=1), jnp.concatenate([gates[:, 2 * NH:4 * NH], gpad], axis=1)])

    o5, states, *solved = _gdn_fwd(qkv, gd, NH, S, T)
    dng = row(dn_norm_g)
    trh = _row_tile(S, HD, cap=1024)
    dn_out = _gate_norm_fwd(o5, h_all, dng, NH, S, ZB, trh)

    yconv = _conf_conv_fwd(h_all, w31, S, CONF, VB, K31)
    bdw, clg, clb = row(conf_dw_b), row(conf_ln_g), row(conf_ln_b)

    def conf_ln(j, yc, b, g, bb):
        xh, _ = _ln_stats(yc + b)
        return (_silu(xh * g + bb),)

    trc = _row_tile(S, CONF)
    (conf_out,) = _rowcall("conf_ln_fwd", conf_ln, S, trc, [(yconv, CONF, 0, 0, False)], [vec(bdw), vec(clg), vec(clb)], [(CONF, bf16, False)], [])
    mix = jnp.concatenate([dn_out, conf_out], axis=1)
    wout_full = _xchg_wait("ag_wout_wait", ag_wout, True, mix).reshape(DN + CONF, D)
    y = _mm("out_proj", mix, wout_full, "nn", f32)

    l1g, l1b, l2g, l2b = row(ln1_g), row(ln1_b), row(ln2_g), row(ln2_b)

    def ln1_mod(j, xnt, yt, ga, g, b, sh, sc):
        xh, _ = _ln_stats(ALPHA * xnt + ga * yt)
        x1 = xh * g + b
        return x1, x1 * (1.0 + sc) + sh

    trd = _row_tile(S, D)
    x1, u = _rowcall("ln1_fwd", ln1_mod, S, trd, [(xn, D, 0, 0, False), (y, D, 0, 0, False)],
                     [vec(g_a), vec(l1g), vec(l1b), vec(sh_m), vec(sc_m)], [(D, f32, False), (D, bf16, False)], [])
    b1, b2 = row(b_mlp1), row(b_mlp2)
    g_w1 = _xchg_wait("ag_w1_wait", ag_w1, True, u)
    hh, act = _mm("mlp1", u, g_w1, "nn", (f32, bf16), extras=[("row", b1)], epilogue=lambda r, b: (r, jnp.square(jnp.maximum(r + b, 0.0))))
    w2_full = _sib_forward("ag_w2_fwd", _xchg_wait("ag_w2_wait", ag_w2, True, act, CORE_AND_SIB)).reshape(DFF, D)
    y2 = _mm("mlp2", act, w2_full, "nn", f32)

    def ln2_loss(j, x1t, y2t, tg, bb2, gm, g, b):
        y2b = y2t + bb2
        xh, rstd = _ln_stats(ALPHA * x1t + gm * y2b)
        err = xh * g + b - tg
        dx2 = err * (1.0 / D)
        dr2 = _ln_bwd(dx2, xh, rstd, g)
        dy2 = dr2 * gm
        lsum = jnp.broadcast_to(jnp.sum(err * err).reshape(1, 1), (1, LANES))
        return dr2, dy2, _csum(dx2 * xh), _csum(dx2), _csum(dr2 * y2b), _csum(dy2), lsum

    dr2, dy2, d_l2g, d_l2b, d_gm, d_b2, lsum = _rowcall(
        "ln2_loss", ln2_loss, S, trd, [(x1, D, 0, 0, False), (y2, D, 0, 0, False), (tgt, D, 0, 0, False)],
        [vec(b2), vec(g_m), vec(l2g), vec(l2b)], [(D, f32, False), (D, bf16, False)], [(1, D, False)] * 4 + [(1, LANES, False)])
    loss = lax.psum(0.5 * lsum[0, 0] / D, ("x", "y", "c"))

    dhh = _mm("mlp2_dx", dy2, w2_full, "nt", bf16, extras=[("tile", hh), ("row", b1)],
              epilogue=lambda r, h, b: (r * (2.0 * jnp.maximum(h + b, 0.0)),))
    dw2_p = _mm("mlp2_dw", act, dy2, "tn", bf16)
    a2a_w2 = _xchg_start("a2a_w2_start", dw2_p.reshape(NDEV, DFF // NDEV, D), False, dhh, land_init=w2_full.reshape(NDEV, DFF // NDEV, D))
    (d_b1,) = _rowcall("b1_grad", lambda j, a: (_csum(a.astype(f32)),), S, _row_tile(S, DFF // 8), [(dhh, DFF, 0, 0, False)], [], [], [(1, DFF, False)])
    du = _mm("mlp1_dx", dhh, g_w1, "nt", f32, after=a2a_w2[4])
    dw1_p = _mm("mlp1_dw", u, dhh, "tn", bf16, out_split=NDEV)
    a2a_w1 = _xchg_start("a2a_w1_start", dw1_p, False, du, land_init=g_w1)

    def ln1_bwd(j, dr2t, dut, xnt, yt, ga, g, b, sc):
        xh, rstd = _ln_stats(ALPHA * xnt + ga * yt)
        x1t = xh * g + b
        dx1 = ALPHA * dr2t + dut * (1.0 + sc)
        dr1 = _ln_bwd(dx1, xh, rstd, g)
        return dr1, dr1 * ga, _csum(dut * x1t), _csum(dut), _csum(dx1 * xh), _csum(dx1), _csum(dr1 * yt)

    dr1, dy, d_scm, d_shm, d_l1g, d_l1b, d_ga = _rowcall(
        "ln1_bwd", ln1_bwd, S, trd, [(dr2, D, 0, 0, False), (du, D, 0, 0, False), (xn, D, 0, 0, False), (y, D, 0, 0, False)],
        [vec(g_a), vec(l1g), vec(l1b), vec(_behind(a2a_w1[4], sc_m))], [(D, f32, False), (D, bf16, False)], [(1, D, False)] * 5)
    dmix = _mm("out_proj_dx", dy, wout_full, "nt", f32)
    dwout_p = _mm("out_proj_dw", mix, dy, "tn", bf16)
    a2a_wout = _xchg_start("a2a_wout_start", dwout_p.reshape(NDEV, (DN + CONF) // NDEV, D), False, dmix,
                           land_init=wout_full.reshape(NDEV, (DN + CONF) // NDEV, D))

    def conf_ln_b(j, dm, yc, b, g, bb):
        xh, rstd = _ln_stats(yc + b)
        dln = dm * _dsilu(xh * g + bb)
        dyc = _ln_bwd(dln, xh, rstd, g)
        return dyc, _csum(dln * xh), _csum(dln), _csum(dyc)

    dyc, d_clg, d_clb, d_bdw = _rowcall("conf_ln_bwd", conf_ln_b, S, trc, [(dmix, CONF, 0, DN // CONF, False), (yconv, CONF, 0, 0, False)],
                                        [vec(_behind(a2a_wout[4], bdw)), vec(clg), vec(clb)], [(CONF, f32, False)], [(1, CONF, False)] * 3)
    dval, dgate, dw31 = _conf_conv_bwd(h_all, w31, dyc, S, CONF, VB, K31)

    do, dz, d_dng = _gate_norm_bwd(dmix, o5, h_all, dng, NH, S, ZB, trh)
    dqkv, dgd = _gdn_bwd(qkv, gd, states, solved, do, NH, S, T)
    dh_qkv, dw7 = _qkv_conv_bwd(h_all, w7, dqkv.reshape(2, 3 * NH, R, HD), R, S, NH, K7)
    dgate_cols = jnp.concatenate([dgd[0][:, :2 * NH], dgd[1][:, :2 * NH], jnp.zeros((R, LANES - 4 * NH), f32)], axis=1)

    def gates_b(j, hg, dgt, isa, na, dt):
        sg = _sigmoid(hg)
        sp = _softplus(hg + dt)
        dpre = jnp.where(isa > 0.5, dgt * na * _sigmoid(hg + dt), dgt * sg * (1.0 - sg))
        return dpre, _csum(jnp.where(isa > 0.5, dgt * na * sp, 0.0)), _csum(jnp.where(isa > 0.5, dpre, 0.0))

    dh_g, d_alog, d_dt = _rowcall("gates_bwd", gates_b, R, trg, [(h_all, LANES, 0, GB, False), (dgate_cols, LANES, 0, 0, False)],
                                  [vec(is_a), vec(neg_a), vec(dt_v)], [(LANES, bf16, False)], [(1, LANES, False)] * 2)

    zpad = lambda a: jnp.concatenate([a, jnp.zeros((T, a.shape[1]), bf16)], axis=0)
    dh_all = jnp.concatenate([dh_qkv, zpad(dz), zpad(dval), zpad(dgate), dh_g, jnp.zeros((R, GPAD - LANES), bf16)], axis=1)
    dxm_all = _mm("in_proj_dx", dh_all, w_cat, "nt", f32, tm=1088, tk=2560)

    def ln_in_bwd_lat(j, xt, dr1t, dxm, g, b, sc):
        xh, rstd = _ln_stats(xt)
        xnt = xh * g + b
        dxn = ALPHA * dr1t + dxm * (1.0 + sc)
        return _ln_bwd(dxn, xh, rstd, g), _csum(dxm * xnt), _csum(dxm), _csum(dxn * xh), _csum(dxn)

    def ln_in_bwd_ctx(j, xt, dxm, g, b, sc):
        xh, rstd = _ln_stats(xt)
        xnt = xh * g + b
        dxn = dxm * (1.0 + sc)
        return _csum(dxm * xnt), _csum(dxm), _csum(dxn * xh), _csum(dxn)

    grad_x, d_sca, d_sha, d_g0a, d_b0a = _rowcall(
        "ln_in_bwd_lat", ln_in_bwd_lat, S, trd, [(x2d, D, 0, 0, False), (dr1, D, 0, 0, False), (dxm_all, D, 0, 0, False)],
        [vec(g0), vec(b0), vec(sc_a)], [(D, f32, False)], [(1, D, False)] * 4)
    d_csca, d_csha, d_g0b, d_b0b = _rowcall(
        "ln_in_bwd_ctx", ln_in_bwd_ctx, T, tr, [(ctx2d, D, 0, 0, False), (dxm_all, D, S // tr, 0, False)],
        [vec(g0), vec(b0), vec(csc_a)], [], [(1, D, False)] * 4)

    zD = jnp.zeros((1, D), f32)
    dmod_me = jnp.concatenate([d_sha, d_sca, d_ga, d_shm, d_scm, d_gm], axis=1)
    dmodc_me = jnp.concatenate([d_csha, d_csca, zD, zD, zD, zD], axis=1)
    small_names = ["ln_in_g", "ln_in_b", "a_log_f", "dt_bias_f", "a_log_b", "dt_bias_b", "dn_norm_g", "conf_dw_b", "conf_ln_g", "conf_ln_b",
                   "ln1_g", "ln1_b", "b_mlp1", "b_mlp2", "ln2_g", "ln2_b"]
    small_parts = [d_g0a + d_g0b, d_b0a + d_b0b, d_alog[:, 0:NH], d_dt[:, 0:NH], d_alog[:, 2 * NH:3 * NH], d_dt[:, 2 * NH:3 * NH], d_dng, d_bdw, d_clg, d_clb,
                   d_l1g, d_l1b, d_b1, d_b2, d_l2g, d_l2b]
    conv_parts = [dw7[:K7], dw31[:K31]]
    packed = _pack(small_parts + conv_parts + [dmodc_me])
    (pk_all, dmod_all) = _all_gather("ag_small_grads", [packed, dmod_me])
    summed = _sum8("sum_small_grads", pk_all.reshape(NDEV, -1))
    parts_sh = [a.shape for a in small_parts + conv_parts + [dmodc_me]]
    un = _unpack(summed, parts_sh)
    gsmall = dict(zip(small_names, un[:len(small_names)]))
    g_w7_full, g_w31_full, dmodc = un[len(small_names):]
    gsmall["w_qkv_conv"] = lax.dynamic_slice_in_dim(g_w7_full, me * w_qkv_conv.shape[2], w_qkv_conv.shape[2], axis=1)
    gsmall["conf_dw_w"] = lax.dynamic_slice_in_dim(g_w31_full, me * conf_dw_w.shape[2], conf_dw_w.shape[2], axis=1)

    dm16 = jnp.concatenate([dmod_all.reshape(NDEV, 6 * D), dmodc, jnp.zeros((7, 6 * D), f32)], axis=0)
    (gbmod,) = _rowcall("bmod_grad", lambda j, a: (_csum(a),), 16, 16, [(dm16, 6 * D, 0, 0, False)], [], [], [(1, 6 * D, False)])
    gsmall["b_mod"] = gbmod
    dm16_mine = lax.dynamic_slice_in_dim(dm16, me * MODC, MODC, axis=1)
    dsc16_part = _mm("mod_dx", dm16_mine, w_mod[0], "nt", f32)
    (dsc_all,) = _all_gather("ag_cctx", [dsc16_part[8:9]])
    dsilu_cctx = _sum8("sum_cctx", dsc_all.reshape(NDEV, D))
    (g_cctx,) = _rowcall("cctx_grad", lambda j, ds, cc: (ds * _dsilu(cc),), 1, 1, [(dsilu_cctx, D, 0, 0, False), (c_ctx.reshape(1, D).astype(f32), D, 0, 0, False)],
                         [], [(D, f32, False)], [])
    gsmall["c_ctx"] = g_cctx

    dwcat_p = _mm("in_proj_dw", xm_all, dh_all, "tn", bf16, tn=1280, tk=4352, after=g_cctx)
    dwin = jnp.concatenate([dwcat_p[:, :4 * DN], dwcat_p[:, 4 * DN + 2 * CONF:4 * DN + 2 * CONF + 4 * NH], dwcat_p[:, 4 * DN:4 * DN + 2 * CONF]], axis=1)
    dwin_p = jnp.transpose(dwin.reshape(D, NDEV, INC // NDEV), (1, 0, 2))
    from_sib = _sib_swap("sib_win", dwin_p)
    mine4 = lax.dynamic_index_in_dim(dwin_p.reshape(4, 2, D, INC // NDEV), lax.axis_index("c"), 1, keepdims=False)
    (chip_sum,) = _rowcall("add_win", lambda j, a, b: (a.astype(f32) + b.astype(f32),), 4 * D, _row_tile(4 * D, INC // NDEV),
                           [(mine4.reshape(4 * D, INC // NDEV), INC // NDEV, 0, 0, False), (from_sib.reshape(4 * D, INC // NDEV), INC // NDEV, 0, 0, False)],
                           [], [(INC // NDEV, bf16, False)], [])
    a2a_win = _xchg_start("a2a_win_start", chip_sum.reshape(4, D, INC // NDEV), False, g_cctx, SAME_CORE)

    g_wmod = _mm("mod_dw", sc16, dm16_mine, "tn", f32, after=a2a_win[4])

    grads, deltas, new_m, new_v = {}, {}, {}, {}
    res = _adam("adam_w_mod", w_mod[0], m_w_mod[0], v_w_mod[0], g_wmod, 0)
    grads["w_mod"], deltas["w_mod"], new_m["w_mod"], new_v["w_mod"] = [a.reshape(w_mod.shape) for a in res]
    after = res[1]
    for nm, started in (("w_mlp2", a2a_w2), ("w_mlp1", a2a_w1), ("w_out", a2a_wout), ("w_in", a2a_win)):
        w3 = weights[nm]
        rels = SAME_CORE if nm == "w_in" else ALL_PEERS
        g = _xchg_wait("a2a_" + nm + "_wait", started, False, after, rels)
        res = _adam("adam_" + nm, w3[0], mom1[nm][0], mom2[nm][0], g, len(rels) + 1)
        grads[nm], deltas[nm], new_m[nm], new_v[nm] = [a.reshape(w3.shape) for a in res]
        after = res[1]
    snames = [n for n in names if n not in grads]
    res = _adam("adam_small", _pack([weights[n] for n in snames]), _pack([mom1[n] for n in snames]), _pack([mom2[n] for n in snames]),
                _pack([gsmall[n] for n in snames]), 0)
    shapes = [weights[n].shape for n in snames]
    for dst, packed_out in zip((grads, deltas, new_m, new_v), res):
        for n, a in zip(snames, _unpack(packed_out, shapes)):
            dst[n] = a

    return (loss, grad_x.reshape(x.shape), *[grads[n] for n in names], *[deltas[n] for n in names],
            *[new_m[n] for n in names], *[new_v[n] for n in names])
```

```python
import functools

import jax
import jax.numpy as jnp
from jax import lax
from jax.experimental import pallas as pl
from jax.experimental.pallas import tpu as pltpu

f32 = jnp.float32
bf16 = jnp.bfloat16
MESH = pl.DeviceIdType.MESH

NDEV = 8
HD = 128
CH = 64
GW = 64
LANES = 128
GPAD = 512
LN_EPS = 1e-5
ALPHA = 2.0 ** 0.25
ADAM_LR, ADAM_B1, ADAM_B2, ADAM_EPS, ADAM_WD, ADAM_STEP = 0.001, 0.9, 0.999, 1e-08, 0.01, 10
VMEM_LIMIT = 60 * 1024 * 1024
ROW_BLOCK_BYTES = 1 << 21


def _cparams(sem):
    return pltpu.CompilerParams(dimension_semantics=sem, vmem_limit_bytes=VMEM_LIMIT)


def _tile(dim, pref, align):
    t = min(pref, dim)
    t -= t % align
    while t >= align:
        if dim % t == 0:
            return t
        t -= align
    return dim


def _row_tile(nrows, width, cap=256):
    t = max(16, min(cap, ROW_BLOCK_BYTES // (4 * width)))
    t = 1 << (t.bit_length() - 1)
    while nrows % t:
        t //= 2
    return max(t, 1)


def _sigmoid(x):
    return 1.0 / (1.0 + jnp.exp(-x))


def _silu(x):
    return x * _sigmoid(x)


def _dsilu(x):
    s = _sigmoid(x)
    return s * (1.0 + x * (1.0 - s))


def _softplus(x):
    return jnp.maximum(x, 0.0) + jnp.log(1.0 + jnp.exp(-jnp.abs(x)))


def _ln_stats(r):
    mu = jnp.mean(r, axis=-1, keepdims=True)
    xc = r - mu
    rstd = lax.rsqrt(jnp.mean(xc * xc, axis=-1, keepdims=True) + LN_EPS)
    return xc * rstd, rstd


def _ln_bwd(dy, xhat, rstd, g):
    dxh = dy * g
    return rstd * (dxh - jnp.mean(dxh, axis=-1, keepdims=True) - xhat * jnp.mean(dxh * xhat, axis=-1, keepdims=True))


def _csum(a):
    return jnp.sum(a, axis=0, keepdims=True)


def _mm(name, a, b, mode, out_dtype, out_split=1, tm=1024, tn=1024, tk=4096, extras=(), epilogue=None, after=None):
    if mode == "tn":
        K, M = a.shape
    else:
        M, K = a.shape
    bp = b.shape[0] if b.ndim == 3 else 1
    brows, bcols = b.shape[-2], b.shape[-1] * bp
    N = brows if mode == "nt" else bcols
    assert K == (bcols if mode == "nt" else brows), (name, a.shape, b.shape)
    tm = _tile(M, tm, LANES if mode == "tn" else 16)
    nsplit = max(out_split, bp if mode != "nt" else 1)
    tn = _tile(N // nsplit, tn, LANES)
    kgroup = 1
    if mode == "nt" and bp > 1 and tk >= 2 * (K // bp):
        kgroup = min(tk // (K // bp), bp)
        tk = kgroup * (K // bp)
    else:
        tk = _tile(K // (bp if mode == "nt" else 1), tk, LANES)
    nk = K // tk
    dims = {"nn": (((1,), (0,)), ((), ())), "nt": (((1,), (1,)), ((), ())), "tn": (((0,), (0,)), ((), ()))}[mode]
    out_dtypes = out_dtype if isinstance(out_dtype, tuple) else (out_dtype,)

    a_spec = pl.BlockSpec((tk, tm), lambda i, j, k: (k, i)) if mode == "tn" else pl.BlockSpec((tm, tk), lambda i, j, k: (i, k))
    if b.ndim == 3:
        if mode == "nt" and kgroup > 1:
            b_spec = pl.BlockSpec((kgroup, tn, K // bp), lambda i, j, k: (k, j, 0))
        elif mode == "nt":
            per = (K // bp) // tk
            b_spec = pl.BlockSpec((None, tn, tk), lambda i, j, k: (k // per, j, k % per))
        else:
            per = (N // bp) // tn
            b_spec = pl.BlockSpec((None, tk, tn), lambda i, j, k: (j // per, k, j % per))
    elif mode == "nt":
        b_spec = pl.BlockSpec((tn, tk), lambda i, j, k: (j, k))
    else:
        b_spec = pl.BlockSpec((tk, tn), lambda i, j, k: (k, j))
    if out_split > 1:
        pero = (N // out_split) // tn
        o_spec = pl.BlockSpec((None, tm, tn), lambda i, j, k: (j // pero, i, j % pero))
        o_shapes = [jax.ShapeDtypeStruct((out_split, M, N // out_split), dt) for dt in out_dtypes]
    else:
        o_spec = pl.BlockSpec((tm, tn), lambda i, j, k: (i, j))
        o_shapes = [jax.ShapeDtypeStruct((M, N), dt) for dt in out_dtypes]
    in_specs = [a_spec, b_spec]
    args = [a, b]
    for kind, arr in extras:
        in_specs.append(pl.BlockSpec((1, tn), lambda i, j, k: (0, j)) if kind == "row" else pl.BlockSpec((tm, tn), lambda i, j, k: (i, j)))
        args.append(arr)
    n_in, n_out = len(args), len(out_dtypes)
    if after is not None:
        in_specs.append(pl.BlockSpec(memory_space=pl.ANY))
        args.append(after)
    n_all = len(args)

    def finish(refs, r):
        outs = (r,) if epilogue is None else epilogue(r, *[e[...] for e in refs[2:n_in]])
        for o_ref, val in zip(refs[n_all:n_all + n_out], outs):
            o_ref[...] = val.astype(o_ref.dtype)

    def body(*refs):
        if kgroup > 1:
            kp = K // bp
            part = lax.dot_general(refs[0][:, 0:kp].astype(bf16), refs[1][0].astype(bf16), dims, preferred_element_type=f32)
            for gi in range(1, kgroup):
                part = part + lax.dot_general(refs[0][:, gi * kp:(gi + 1) * kp].astype(bf16), refs[1][gi].astype(bf16), dims, preferred_element_type=f32)
        else:
            part = lax.dot_general(refs[0][...].astype(bf16), refs[1][...].astype(bf16), dims, preferred_element_type=f32)
        if nk == 1:
            finish(refs, part)
            return
        acc = refs[-1]
        k = pl.program_id(2)

        @pl.when(k == 0)
        def _():
            acc[...] = part

        @pl.when(jnp.logical_and(k > 0, k < nk - 1))
        def _():
            acc[...] += part

        @pl.when(k == nk - 1)
        def _():
            finish(refs, acc[...] + part)

    res = pl.pallas_call(
        body, name=name, grid=(M // tm, N // tn, nk), in_specs=in_specs, out_specs=[o_spec] * n_out, out_shape=o_shapes,
        scratch_shapes=[pltpu.VMEM((tm, tn), f32)] if nk > 1 else [], compiler_params=_cparams(("parallel", "parallel", "arbitrary")),
    )(*args)
    return res[0] if n_out == 1 else res


def _rowcall(name, fn, nrows, tr, rows_in, vecs_in, rows_out, accs_out, ncol=1):
    nrt = nrows // tr
    in_specs, args = [], []
    for arr, w, ro, co, pc in rows_in:
        in_specs.append(pl.BlockSpec((tr, w), functools.partial(lambda j, i, ro, co, pc: (i + ro, co + (j if pc else 0)), ro=ro, co=co, pc=pc)))
        args.append(arr)
    for arr, w, co, pc in vecs_in:
        in_specs.append(pl.BlockSpec((arr.shape[0], w), functools.partial(lambda j, i, co, pc: (0, co + (j if pc else 0)), co=co, pc=pc)))
        args.append(arr)
    out_specs, out_shape = [], []
    for w, dt, pc in rows_out:
        out_specs.append(pl.BlockSpec((tr, w), functools.partial(lambda j, i, pc: (i, j if pc else 0), pc=pc)))
        out_shape.append(jax.ShapeDtypeStruct((nrows, w * (ncol if pc else 1)), dt))
    for k, w, pc in accs_out:
        out_specs.append(pl.BlockSpec((k, w), functools.partial(lambda j, i, pc: (0, j if pc else 0), pc=pc)))
        out_shape.append(jax.ShapeDtypeStruct((k, w * (ncol if pc else 1)), f32))
    n_in, n_ro = len(args), len(rows_out)

    def body(*refs):
        j, i = pl.program_id(0), pl.program_id(1)
        outs = fn(j, *[r[...] for r in refs[:n_in]])
        for r, val in zip(refs[n_in:n_in + n_ro], outs[:n_ro]):
            r[...] = val.astype(r.dtype)
        for (k, w, pc), r, val in zip(accs_out, refs[n_in + n_ro:], outs[n_ro:]):
            first = (i == 0) if pc else jnp.logical_and(i == 0, j == 0)

            @pl.when(first)
            def _(r=r):
                r[...] = jnp.zeros_like(r)

            r[...] += val

    res = pl.pallas_call(
        body, name=name, grid=(ncol, nrt), in_specs=in_specs, out_specs=out_specs, out_shape=out_shape,
        compiler_params=_cparams(("arbitrary", "arbitrary")),
    )(*args)
    return res


def _tap_room(mode, nrows, nlat):
    t = lax.broadcasted_iota(jnp.int32, (nrows, 1), 0)
    if mode == "seg":
        lat = t < nlat
        return t - jnp.where(lat, 0, nlat), jnp.where(lat, nlat, nrows) - 1 - t
    if mode == "row":
        p = t & (GW - 1)
        return p, GW - 1 - p
    return t // GW, (nrows - 1 - t) // GW


def _tap_valid(room, d):
    return room[1] >= d if d > 0 else room[0] >= -d


def _conv(x, w_ref, ktaps, mode, nlat, flip=False):
    nrows = x.shape[0]
    stride = GW if mode == "col" else 1
    room = _tap_room(mode, nrows, nlat)
    acc = jnp.zeros_like(x)
    for j in range(ktaps):
        d = j - ktaps // 2
        jj = ktaps - 1 - j if flip else j
        wj = w_ref[jj:jj + 1, :]
        if d == 0:
            acc = acc + x * wj
        else:
            xs = pltpu.roll(x, (-d * stride) % nrows, 0)
            acc = acc + jnp.where(_tap_valid(room, d), xs * wj, 0.0)
    return acc


def _conv_wgrad(dy, x, dw_ref, ktaps, mode, nlat):
    nrows = x.shape[0]
    stride = GW if mode == "col" else 1
    room = _tap_room(mode, nrows, nlat)
    dw_ref[...] = jnp.zeros_like(dw_ref)
    for j in range(ktaps):
        d = j - ktaps // 2
        if d == 0:
            prod = x * dy
        else:
            xs = pltpu.roll(x, (-d * stride) % nrows, 0)
            prod = jnp.where(_tap_valid(room, d), xs * dy, 0.0)
        dw_ref[j:j + 1, :] = _csum(prod)


def _qkv_post(j, pre, nh):
    a = _silu(pre)
    inv = lax.rsqrt(jnp.sum(a * a, axis=-1, keepdims=True) + 1e-6)
    scale = jnp.where(j < nh, HD ** -0.5, 1.0).astype(f32)
    nrm = jnp.where(j < 2 * nh, inv, 1.0) * scale
    return a, inv, nrm


def _qkv_conv_fwd(h_all, w7, nrows, nlat, nh, ktaps):
    ntile = 3 * nh

    def body(h_ref, w_ref, o_ref):
        j = pl.program_id(0)
        pre = _conv(h_ref[...], w_ref, ktaps, "seg", nlat)
        a, _, nrm = _qkv_post(j, pre, nh)
        o_ref[...] = a * nrm

    return pl.pallas_call(
        body, name="qkv_conv_fwd", grid=(ntile,),
        in_specs=[pl.BlockSpec((nrows, HD), lambda j: (0, j)), pl.BlockSpec((w7.shape[0], HD), lambda j: (0, j))],
        out_specs=pl.BlockSpec((None, nrows, HD), lambda j: (j, 0, 0)),
        out_shape=jax.ShapeDtypeStruct((ntile, nrows, HD), f32), compiler_params=_cparams(("parallel",)),
    )(h_all, w7)


def _qkv_conv_bwd(h_all, w7, dqkv, nrows, nlat, nh, ktaps):
    ntile = 3 * nh

    def body(h_ref, w_ref, d0_ref, d1_ref, dh_ref, dw_ref):
        j = pl.program_id(0)
        hx = h_ref[...]
        pre = _conv(hx, w_ref, ktaps, "seg", nlat)
        a, inv, nrm = _qkv_post(j, pre, nh)
        dn = (d0_ref[...] + d1_ref[...]) * jnp.where(j < nh, HD ** -0.5, 1.0).astype(f32)
        n = a * inv
        da_norm = inv * (dn - n * jnp.sum(dn * n, axis=-1, keepdims=True))
        da = jnp.where(j < 2 * nh, da_norm, dn)
        dpre = da * _dsilu(pre)
        dh_ref[...] = _conv(dpre, w_ref, ktaps, "seg", nlat, flip=True).astype(dh_ref.dtype)
        _conv_wgrad(dpre, hx, dw_ref, ktaps, "seg", nlat)

    blk = pl.BlockSpec((nrows, HD), lambda j: (0, j))
    wblk = pl.BlockSpec((w7.shape[0], HD), lambda j: (0, j))
    dblk = lambda d: pl.BlockSpec((None, None, nrows, HD), lambda j: (d, j, 0, 0))
    return pl.pallas_call(
        body, name="qkv_conv_bwd", grid=(ntile,), in_specs=[blk, wblk, dblk(0), dblk(1)], out_specs=[blk, wblk],
        out_shape=[jax.ShapeDtypeStruct((nrows, ntile * HD), bf16), jax.ShapeDtypeStruct((w7.shape[0], ntile * HD), f32)],
        compiler_params=_cparams(("parallel",)),
    )(h_all, w7, dqkv, dqkv)


def _conf_conv_fwd(h_all, w31, nlat, conf, val_blk, ktaps):
    nt = conf // LANES
    nhalf = nt // 2

    def body(v_ref, g_ref, w_ref, o_ref):
        j = pl.program_id(0)
        glu = v_ref[...] * _sigmoid(g_ref[...])

        @pl.when(j < nhalf)
        def _():
            o_ref[...] = _conv(glu, w_ref, ktaps, "row", nlat)

        @pl.when(j >= nhalf)
        def _():
            o_ref[...] = _conv(glu, w_ref, ktaps, "col", nlat)

    return pl.pallas_call(
        body, name="conf_conv_fwd", grid=(nt,),
        in_specs=[pl.BlockSpec((nlat, LANES), lambda j: (0, val_blk + j)), pl.BlockSpec((nlat, LANES), lambda j: (0, val_blk + nt + j)),
                  pl.BlockSpec((w31.shape[0], LANES), lambda j: (0, j))],
        out_specs=pl.BlockSpec((nlat, LANES), lambda j: (0, j)),
        out_shape=jax.ShapeDtypeStruct((nlat, conf), f32), compiler_params=_cparams(("parallel",)),
    )(h_all, h_all, w31)


def _conf_conv_bwd(h_all, w31, dyc, nlat, conf, val_blk, ktaps):
    nt = conf // LANES
    nhalf = nt // 2

    def body(v_ref, g_ref, w_ref, dy_ref, dv_ref, dg_ref, dw_ref):
        j = pl.program_id(0)
        val, sg = v_ref[...], _sigmoid(g_ref[...])
        glu = val * sg
        dy = dy_ref[...]

        def run(mode):
            dglu = _conv(dy, w_ref, ktaps, mode, nlat, flip=True)
            dv_ref[...] = (dglu * sg).astype(dv_ref.dtype)
            dg_ref[...] = (dglu * val * sg * (1.0 - sg)).astype(dg_ref.dtype)
            _conv_wgrad(dy, glu, dw_ref, ktaps, mode, nlat)

        @pl.when(j < nhalf)
        def _():
            run("row")

        @pl.when(j >= nhalf)
        def _():
            run("col")

    blk = pl.BlockSpec((nlat, LANES), lambda j: (0, j))
    wblk = pl.BlockSpec((w31.shape[0], LANES), lambda j: (0, j))
    return pl.pallas_call(
        body, name="conf_conv_bwd", grid=(nt,),
        in_specs=[pl.BlockSpec((nlat, LANES), lambda j: (0, val_blk + j)), pl.BlockSpec((nlat, LANES), lambda j: (0, val_blk + nt + j)), wblk, blk],
        out_specs=[blk, blk, wblk],
        out_shape=[jax.ShapeDtypeStruct((nlat, conf), bf16), jax.ShapeDtypeStruct((nlat, conf), bf16),
                   jax.ShapeDtypeStruct((w31.shape[0], conf), f32)],
        compiler_params=_cparams(("parallel",)),
    )(h_all, h_all, w31, dyc)


def _bd(eq, a, b):
    return jnp.einsum(eq, a.astype(bf16), b.astype(bf16), preferred_element_type=f32)


def _split(x, pieces):
    out = []
    for _ in range(pieces - 1):
        p = x.astype(bf16)
        out.append(p)
        x = x - p.astype(f32)
    return out + [x.astype(bf16)]


def _h3(eq, a, b):
    (ah, al), (bh, bl) = _split(a, 2), _split(b, 2)
    d = lambda p, q: jnp.einsum(eq, p, q, preferred_element_type=f32)
    return d(ah, bh) + (d(ah, bl) + d(al, bh))


def _cumsum_rows(x, rev):
    row = lax.broadcasted_iota(jnp.int32, x.shape, 0)
    down, up, s = x, x, 1
    while s < CH:
        down = down + jnp.where(row >= s, pltpu.roll(down, s, 0), 0.0)
        up = up + jnp.where(row < CH - s, pltpu.roll(up, CH - s, 0), 0.0)
        s *= 2
    return jnp.where(rev, up, down)


def _gdn_chunk(q, k, v, gt, nh, rev, solved=None):
    beta = jnp.stack([jnp.broadcast_to(gt[:, nh + h:nh + h + 1], (CH, LANES)) for h in range(nh)])
    ii = lax.broadcasted_iota(jnp.int32, (CH, CH), 0)
    jj = lax.broadcasted_iota(jnp.int32, (CH, CH), 1)
    si, sj = jnp.where(rev, jj, ii), jnp.where(rev, ii, jj)
    tril, stril = (si >= sj)[None], (si > sj)[None]
    gct = _cumsum_rows(gt, rev)
    gct_t = gct.T
    gc = jnp.stack([jnp.broadcast_to(gct[:, h:h + 1], (CH, LANES)) for h in range(nh)])
    gc_row = jnp.stack([jnp.broadcast_to(gct_t[h:h + 1, :], (CH, CH)) for h in range(nh)])
    diff = gc[:, :, :CH] - gc_row
    gam = jnp.where(tril, jnp.exp(jnp.where(tril, diff, 0.0)), 0.0)
    e = jnp.exp(gc)
    gl = jnp.where(rev, gc[:, 0:1, :], gc[:, CH - 1:CH, :])
    el = jnp.exp(gl)
    r = jnp.exp(gl - gc)
    kb, vb = k * beta, v * beta
    kbe = kb * e
    amat = jnp.where(stril, _bd("hik,hjk->hij", kb, k) * gam, 0.0)
    if solved is None:
        xp = -amat
        tinv = (ii == jj).astype(f32)[None] + xp
        for _ in range(5):
            xp = _h3("hij,hjk->hik", xp, xp)
            tinv = tinv + _h3("hij,hjk->hik", tinv, xp)
        u = _h3("hij,hjv->hiv", tinv, vb)
        w = _h3("hij,hjk->hik", tinv, kbe)
    else:
        tinv, u, w = solved
    pmat = jnp.where(tril, _bd("hik,hjk->hij", q, k) * gam, 0.0)
    return dict(beta=beta, gam=gam, e=e, el=el, r=r, kb=kb, vb=vb, kbe=kbe, amat=amat, tinv=tinv, u=u, w=w, pmat=pmat,
                qd=q * e, kd=k * r, tril=tril, stril=stril)


def _scan_row_chunk(d, n, ns, nt):
    fwd = jnp.where(n < nt, ns + n, n - nt)
    return jnp.where(d == 0, fwd, ns + nt - 1 - n)


def _gdn_fwd(qkv, gd, nh, nlat, nctx):
    nrows = nlat + nctx
    ns, nt = nlat // CH, nctx // CH
    nch = ns + nt

    def body(qkv_ref, g_ref, o_ref, st_ref, ti_ref, u_ref, w_ref, state):
        d, n = pl.program_id(0), pl.program_id(1)

        @pl.when(n == 0)
        def _():
            state[...] = jnp.zeros_like(state)

        c = _gdn_chunk(qkv_ref[0], qkv_ref[1], qkv_ref[2], g_ref[...], nh, d == 1)
        s = state[...]
        vn = c["u"] - _bd("hck,hkv->hcv", c["w"], s)
        o_ref[...] = _bd("hck,hkv->hcv", c["qd"], s) + _bd("hcd,hdv->hcv", c["pmat"], vn)
        st_ref[...] = s
        ti_ref[...], u_ref[...], w_ref[...] = c["tinv"], c["u"], c["w"]
        state[...] = s * c["el"] + _bd("hck,hcv->hkv", c["kd"], vn)

    hblk = pl.BlockSpec((None, nh, CH, HD), lambda d, n: (d, 0, _scan_row_chunk(d, n, ns, nt), 0))
    hshape = jax.ShapeDtypeStruct((2, nh, nrows, HD), f32)
    return pl.pallas_call(
        body, name="gdn_fwd", grid=(2, nch),
        in_specs=[pl.BlockSpec((3, nh, CH, HD), lambda d, n: (0, 0, _scan_row_chunk(d, n, ns, nt), 0)),
                  pl.BlockSpec((None, CH, LANES), lambda d, n: (d, _scan_row_chunk(d, n, ns, nt), 0))],
        out_specs=[hblk, pl.BlockSpec((None, None, nh, HD, HD), lambda d, n: (d, n, 0, 0, 0)),
                   pl.BlockSpec((None, None, nh, CH, CH), lambda d, n: (d, n, 0, 0, 0)), hblk, hblk],
        out_shape=[hshape, jax.ShapeDtypeStruct((2, nch, nh, HD, HD), f32), jax.ShapeDtypeStruct((2, nch, nh, CH, CH), f32), hshape, hshape],
        scratch_shapes=[pltpu.VMEM((nh, HD, HD), f32)], compiler_params=_cparams(("arbitrary", "arbitrary")),
    )(qkv, gd)


def _gdn_bwd(qkv, gd, states, solved, do, nh, nlat, nctx):
    nrows = nlat + nctx
    ns, nt = nlat // CH, nctx // CH
    nch = ns + nt

    def body(qkv_ref, g_ref, st_ref, ti_ref, u_ref, w_ref, do_ref, dqkv_ref, dgd_ref, dstate):
        d, step = pl.program_id(0), pl.program_id(1)

        @pl.when(step == 0)
        def _():
            dstate[...] = jnp.zeros_like(dstate)

        q, k, v = qkv_ref[0], qkv_ref[1], qkv_ref[2]
        c = _gdn_chunk(q, k, v, g_ref[...], nh, d == 1, solved=(ti_ref[...], u_ref[...], w_ref[...]))
        in_ctx = (nch - 1 - step) < nt
        s, dsp = st_ref[...], dstate[...]
        dout = jnp.where(in_ctx, 0.0, do_ref[...])
        beta, gam, e, el, r = c["beta"], c["gam"], c["e"], c["el"], c["r"]
        tinv, u, w, pmat, amat = c["tinv"], c["u"], c["w"], c["pmat"], c["amat"]
        vn = u - _bd("hck,hkv->hcv", w, s)
        dvn = _bd("hdc,hdv->hcv", pmat, dout) + _bd("hck,hkv->hcv", c["kd"], dsp)
        dp = jnp.where(c["tril"], _bd("hcv,hdv->hcd", dout, vn), 0.0)
        dqd = _bd("hcv,hkv->hck", dout, s)
        dkd = _bd("hcv,hkv->hck", vn, dsp)
        dstate[...] = _bd("hck,hcv->hkv", c["qd"], dout) + dsp * el - _bd("hck,hcv->hkv", w, dvn)
        del_ = jnp.sum(jnp.sum(s * dsp, axis=2, keepdims=True), axis=1, keepdims=True)
        dw = -_bd("hcv,hkv->hck", dvn, s)
        dvb = _h3("hji,hjv->hiv", tinv, dvn)
        dkbe = _h3("hji,hjk->hik", tinv, dw)
        z = _bd("hiv,hjv->hij", dvn, u) + _bd("hik,hjk->hij", dw, w)
        da = jnp.where(c["stril"], -_h3("hji,hjl->hil", tinv, z), 0.0)
        dm = da * gam
        dkb = _bd("hij,hjk->hik", dm, k) + dkbe * e
        dn = dp * gam
        dqkv_ref[0] = _bd("hij,hjk->hik", dn, k) + dqd * e
        dqkv_ref[1] = _bd("hji,hjk->hik", dm, c["kb"]) + _bd("hji,hjk->hik", dn, q) + dkd * r + dkb * beta
        dqkv_ref[2] = dvb * beta
        gmat = da * amat + dp * pmat
        rows_minus_cols = jnp.sum(gmat, axis=-1, keepdims=True) - jnp.sum(jnp.swapaxes(gmat, 1, 2), axis=-1, keepdims=True)
        de = jnp.sum(dqd * q + dkbe * c["kb"], axis=-1, keepdims=True)
        drr = jnp.sum(dkd * k, axis=-1, keepdims=True) * r
        dgc = rows_minus_cols + de * e - drr
        dgl = jnp.sum(drr, axis=1, keepdims=True) + del_ * el
        row = lax.broadcasted_iota(jnp.int32, (1, CH, 1), 1)
        total_row = row == jnp.where(d == 1, 0, CH - 1)
        dgc = dgc + jnp.where(total_row, dgl, 0.0)
        db = jnp.sum(dvb * v + dkb * k, axis=-1, keepdims=True)
        lane = lax.broadcasted_iota(jnp.int32, (CH, LANES), 1)
        dgc_cols, db_cols = jnp.zeros((CH, LANES), f32), jnp.zeros((CH, LANES), f32)
        for h in range(nh):
            dgc_cols = dgc_cols + jnp.where(lane == h, dgc[h], 0.0)
            db_cols = db_cols + jnp.where(lane == nh + h, db[h], 0.0)
        dgd_ref[...] = _cumsum_rows(dgc_cols, d == 0) + db_cols

    def rc(d, n):
        return _scan_row_chunk(d, nch - 1 - n, ns, nt)

    gblk = pl.BlockSpec((None, CH, LANES), lambda d, n: (d, rc(d, n), 0))
    return pl.pallas_call(
        body, name="gdn_bwd", grid=(2, nch),
        in_specs=[pl.BlockSpec((3, nh, CH, HD), lambda d, n: (0, 0, rc(d, n), 0)),
                  gblk,
                  pl.BlockSpec((None, None, nh, HD, HD), lambda d, n: (d, nch - 1 - n, 0, 0, 0)),
                  pl.BlockSpec((None, None, nh, CH, CH), lambda d, n: (d, nch - 1 - n, 0, 0, 0)),
                  pl.BlockSpec((None, nh, CH, HD), lambda d, n: (d, 0, rc(d, n), 0)),
                  pl.BlockSpec((None, nh, CH, HD), lambda d, n: (d, 0, rc(d, n), 0)),
                  pl.BlockSpec((nh, CH, HD), lambda d, n: (0, jnp.minimum(rc(d, n), ns - 1), 0))],
        out_specs=[pl.BlockSpec((None, 3, nh, CH, HD), lambda d, n: (d, 0, 0, rc(d, n), 0)), gblk],
        out_shape=[jax.ShapeDtypeStruct((2, 3, nh, nrows, HD), f32), jax.ShapeDtypeStruct((2, nrows, LANES), f32)],
        scratch_shapes=[pltpu.VMEM((nh, HD, HD), f32)], compiler_params=_cparams(("arbitrary", "arbitrary")),
    )(qkv, gd, states, *solved, do)


def _gate_norm_fwd(o5, h_all, dng, nh, nlat, zblk, tr):
    def body(of_ref, ob_ref, z_ref, g_ref, out_ref):
        o = of_ref[...] + ob_ref[...]
        rs = lax.rsqrt(jnp.mean(o * o, axis=-1, keepdims=True) + 1e-6)
        out_ref[...] = ((o * rs * g_ref[...]) * _silu(z_ref[...])).astype(out_ref.dtype)

    return pl.pallas_call(
        body, name="gate_norm_fwd", grid=(nh, nlat // tr),
        in_specs=[pl.BlockSpec((None, None, tr, HD), lambda j, i: (0, j, i, 0)), pl.BlockSpec((None, None, tr, HD), lambda j, i: (1, j, i, 0)),
                  pl.BlockSpec((tr, HD), lambda j, i: (i, zblk + j)), pl.BlockSpec((1, HD), lambda j, i: (0, 0))],
        out_specs=pl.BlockSpec((tr, HD), lambda j, i: (i, j)), out_shape=jax.ShapeDtypeStruct((nlat, nh * HD), bf16),
        compiler_params=_cparams(("parallel", "parallel")),
    )(o5, o5, h_all, dng)


def _gate_norm_bwd(dmix, o5, h_all, dng, nh, nlat, zblk, tr):
    def body(dm_ref, of_ref, ob_ref, z_ref, g_ref, do_ref, dz_ref, dg_ref):
        o = of_ref[...] + ob_ref[...]
        z, g, dm = z_ref[...], g_ref[...], dm_ref[...]
        rs = lax.rsqrt(jnp.mean(o * o, axis=-1, keepdims=True) + 1e-6)
        don = dm * _silu(z)
        dz_ref[...] = (dm * (o * rs * g) * _dsilu(z)).astype(dz_ref.dtype)
        dog = don * g
        do_ref[...] = rs * (dog - o * rs * rs * jnp.mean(dog * o, axis=-1, keepdims=True))

        @pl.when(jnp.logical_and(pl.program_id(0) == 0, pl.program_id(1) == 0))
        def _():
            dg_ref[...] = jnp.zeros_like(dg_ref)

        dg_ref[...] += _csum(don * o * rs)

    return pl.pallas_call(
        body, name="gate_norm_bwd", grid=(nh, nlat // tr),
        in_specs=[pl.BlockSpec((tr, HD), lambda j, i: (i, j)),
                  pl.BlockSpec((None, None, tr, HD), lambda j, i: (0, j, i, 0)), pl.BlockSpec((None, None, tr, HD), lambda j, i: (1, j, i, 0)),
                  pl.BlockSpec((tr, HD), lambda j, i: (i, zblk + j)), pl.BlockSpec((1, HD), lambda j, i: (0, 0))],
        out_specs=[pl.BlockSpec((None, tr, HD), lambda j, i: (j, i, 0)), pl.BlockSpec((tr, HD), lambda j, i: (i, j)),
                   pl.BlockSpec((1, HD), lambda j, i: (0, 0))],
        out_shape=[jax.ShapeDtypeStruct((nh, nlat, HD), f32), jax.ShapeDtypeStruct((nlat, nh * HD), bf16), jax.ShapeDtypeStruct((1, HD), f32)],
        compiler_params=_cparams(("arbitrary", "arbitrary")),
    )(dmix, o5, o5, h_all, dng)


def _mesh_pos():
    return lax.axis_index("x"), lax.axis_index("y"), lax.axis_index("c")


def _lin(p):
    return 4 * p[0] + 2 * p[1] + p[2]


def _all_gather(name, xs, after=None):
    nx = len(xs)
    extra = [] if after is None else [after]

    def body(*refs):
        xr, outr = refs[:nx], refs[nx + len(extra):2 * nx + len(extra)]
        send, recv, loc = refs[2 * nx + len(extra):]
        x, y, c = _mesh_pos()
        me, sib = (x, y, c), (x, y, 1 - c)
        chips = [(1 - x, y), (x, 1 - y), (1 - x, 1 - y)]

        def cp(l, k, block, to, src=None):
            rows = outr[l].at[_lin(block)]
            return pltpu.make_async_remote_copy(src_ref=rows if src is None else src, dst_ref=rows, send_sem=send.at[l, k],
                                                recv_sem=recv.at[l, k], device_id=to, device_id_type=MESH)

        mine = [pltpu.make_async_copy(xr[l], outr[l].at[_lin(me)], loc.at[l]) for l in range(nx)]
        for m in mine:
            m.start()
        first = []
        for l in range(nx):
            first.append(cp(l, 0, me, sib, src=xr[l]))
            first += [cp(l, 1 + j, me, (*chip, c), src=xr[l]) for j, chip in enumerate(chips)]
        for f in first:
            f.start()
        passed = []
        for l in range(nx):
            for j, chip in enumerate(chips):
                cp(l, 1 + j, (*chip, c), me).wait_recv()
                fwd = cp(l, 4 + j, (*chip, c), sib)
                fwd.start()
                passed.append(fwd)
        for l in range(nx):
            cp(l, 0, sib, me).wait_recv()
            for j, chip in enumerate(chips):
                cp(l, 4 + j, (*chip, 1 - c), me).wait_recv()
        for f in first + passed:
            f.wait_send()
        for m in mine:
            m.wait()

    anyspec = pl.BlockSpec(memory_space=pl.ANY)
    return pl.pallas_call(
        body, name=name, in_specs=[anyspec] * (nx + len(extra)), out_specs=[anyspec] * nx,
        out_shape=[jax.ShapeDtypeStruct((NDEV,) + a.shape, a.dtype) for a in xs],
        scratch_shapes=[pltpu.SemaphoreType.DMA((nx, 7)), pltpu.SemaphoreType.DMA((nx, 7)), pltpu.SemaphoreType.DMA((nx,))],
    )(*xs, *extra)


HBM_SPEC = pl.BlockSpec(memory_space=pltpu.HBM)
SEM_SPEC = pl.BlockSpec(memory_space=pltpu.SEMAPHORE)
ANY_SPEC = pl.BlockSpec(memory_space=pl.ANY)
EFFECT = pltpu.SideEffectType.DATAFLOW_SIDE_EFFECTING


def _peer(rel):
    x, y, c = _mesh_pos()
    return (1 - x if rel & 4 else x, 1 - y if rel & 2 else y, 1 - c if rel & 1 else c)


ALL_PEERS = (1, 2, 3, 4, 5, 6, 7)
SAME_CORE = (2, 4, 6)
CORE_AND_SIB = (1, 2, 4, 6)


def _slot(p, rels):
    return 2 * p[0] + p[1] if rels == SAME_CORE else _lin(p)


def _xchg_start(name, x, gather, after, rels=ALL_PEERS, land_init=None):
    me = _slot(_mesh_pos(), rels)
    shape = ((4 if rels == SAME_CORE else NDEV),) + x.shape if gather else x.shape
    own = x if gather else lax.dynamic_index_in_dim(x, me, 0, keepdims=False)
    land = lax.dynamic_update_index_in_dim(lax.empty(shape, x.dtype) if land_init is None else land_init, own, me, 0)

    def body(x_ref, land_ref, after_ref, send, recv, x_thru, land_thru, token):
        mine = _slot(_mesh_pos(), rels)
        for k, rel in enumerate(rels):
            p = _peer(rel)
            pltpu.make_async_remote_copy(src_ref=x_ref if gather else x_ref.at[_slot(p, rels)], dst_ref=land_ref.at[mine],
                                         send_sem=send.at[k], recv_sem=recv.at[k], device_id=p, device_id_type=MESH).start()
        token[...] = jnp.zeros_like(token)

    return pl.pallas_call(
        body, name=name,
        out_shape=(pltpu.SemaphoreType.DMA((len(rels),)), pltpu.SemaphoreType.DMA((len(rels),)), pltpu.HBM(x.shape, x.dtype),
                   pltpu.HBM(shape, x.dtype), jax.ShapeDtypeStruct((8, LANES), f32)),
        in_specs=(HBM_SPEC, HBM_SPEC, ANY_SPEC), out_specs=(SEM_SPEC, SEM_SPEC, HBM_SPEC, HBM_SPEC, pl.BlockSpec(memory_space=pltpu.VMEM)),
        input_output_aliases={0: 2, 1: 3}, compiler_params=pltpu.CompilerParams(has_side_effects=EFFECT),
    )(pltpu.with_memory_space_constraint(x, pltpu.HBM), pltpu.with_memory_space_constraint(land, pltpu.HBM), after)


def _xchg_wait(name, started, gather, after, rels=ALL_PEERS):
    send, recv, x_thru, land_thru, _ = started

    def body(x_ref, land_ref, send, recv, after_ref, x_dead, got_ref):
        for k, rel in enumerate(rels):
            p = _peer(rel)
            cp = pltpu.make_async_remote_copy(src_ref=x_ref if gather else x_ref.at[_slot(p, rels)], dst_ref=land_ref.at[_slot(p, rels)],
                                              send_sem=send.at[k], recv_sem=recv.at[k], device_id=p, device_id_type=MESH)
            cp.wait_send()
            cp.wait_recv()

    return pl.pallas_call(
        body, name=name, out_shape=(pltpu.HBM(x_thru.shape, x_thru.dtype), pltpu.HBM(land_thru.shape, land_thru.dtype)),
        in_specs=(HBM_SPEC, HBM_SPEC, SEM_SPEC, SEM_SPEC, ANY_SPEC), out_specs=(HBM_SPEC, HBM_SPEC), input_output_aliases={0: 0, 1: 1},
        compiler_params=pltpu.CompilerParams(has_side_effects=EFFECT),
    )(x_thru, land_thru, send, recv, after)[1]


def _sib_forward(name, land):
    def body(land_ref, out_ref, send, recv):
        mx, my, mc = _mesh_pos()
        sib = (mx, my, 1 - mc)

        def cp(k, block):
            return pltpu.make_async_remote_copy(src_ref=out_ref.at[_lin(block)], dst_ref=out_ref.at[_lin(block)], send_sem=send.at[k],
                                                recv_sem=recv.at[k], device_id=sib, device_id_type=MESH)

        chips = [_peer(rel)[:2] for rel in SAME_CORE]
        sends = [cp(k, (*chip, mc)) for k, chip in enumerate(chips)]
        for s in sends:
            s.start()
        for k, chip in enumerate(chips):
            cp(k, (*chip, 1 - mc)).wait_recv()
        for s in sends:
            s.wait_send()

    return pl.pallas_call(
        body, name=name, in_specs=[ANY_SPEC], out_specs=ANY_SPEC, out_shape=jax.ShapeDtypeStruct(land.shape, land.dtype),
        input_output_aliases={0: 0}, scratch_shapes=[pltpu.SemaphoreType.DMA((3,)), pltpu.SemaphoreType.DMA((3,))],
    )(land)


def _sib_swap(name, x):
    def body(x_ref, out_ref, send, recv):
        mx, my, mc = _mesh_pos()
        sib = (mx, my, 1 - mc)
        cps = [pltpu.make_async_remote_copy(src_ref=x_ref.at[2 * q + 1 - mc], dst_ref=out_ref.at[q], send_sem=send.at[q], recv_sem=recv.at[q],
                                            device_id=sib, device_id_type=MESH) for q in range(4)]
        for cp in cps:
            cp.start()
        for cp in cps:
            cp.wait_recv()
        for cp in cps:
            cp.wait_send()

    return pl.pallas_call(
        body, name=name, in_specs=[ANY_SPEC], out_specs=ANY_SPEC, out_shape=jax.ShapeDtypeStruct((4,) + x.shape[1:], x.dtype),
        scratch_shapes=[pltpu.SemaphoreType.DMA((4,)), pltpu.SemaphoreType.DMA((4,))],
    )(x)


def _behind(token, a):
    return a + token[0:1, 0:1].astype(a.dtype)


def _sum8(name, g):
    def body(g_ref, o_ref):
        acc = g_ref[0:1, :]
        for p in range(1, NDEV):
            acc = acc + g_ref[p:p + 1, :]
        o_ref[...] = acc

    return pl.pallas_call(body, name=name, out_shape=jax.ShapeDtypeStruct((1, g.shape[1]), f32),
                          compiler_params=pltpu.CompilerParams(vmem_limit_bytes=VMEM_LIMIT))(g)


def _pack(arrs):
    flat = jnp.concatenate([a.reshape(-1).astype(f32) for a in arrs])
    pad = (-flat.shape[0]) % (8 * LANES)
    return jnp.pad(flat, (0, pad)).reshape(1, -1)


def _unpack(vec, shapes):
    out, off = [], 0
    flat = vec.reshape(-1)
    for s in shapes:
        n = 1
        for d in s:
            n *= d
        out.append(flat[off:off + n].reshape(s))
        off += n
    return out


def _adam(name, w, m, v, g, parts, after=None):
    extra = [] if after is None else [after]
    nrows, width = w.shape
    tr = _row_tile(nrows, width) if nrows >= 16 else nrows
    blk = pl.BlockSpec((tr, width), lambda i: (i, 0))
    gblk = pl.BlockSpec((parts, tr, width), lambda i: (0, i, 0)) if parts else blk
    c1 = 1.0 / (1.0 - ADAM_B1 ** ADAM_STEP)
    c2 = 1.0 / (1.0 - ADAM_B2 ** ADAM_STEP)

    def body(w_ref, m_ref, v_ref, g_ref, *rest):
        go_ref, d_ref, mo_ref, vo_ref = rest[len(extra):]
        if parts:
            gg = g_ref[0].astype(f32)
            for p in range(1, parts):
                gg = gg + g_ref[p].astype(f32)
        else:
            gg = g_ref[...]
        mn = ADAM_B1 * m_ref[...] + (1.0 - ADAM_B1) * gg
        vn = ADAM_B2 * v_ref[...] + (1.0 - ADAM_B2) * (gg * gg)
        go_ref[...] = gg
        mo_ref[...] = mn
        vo_ref[...] = vn
        d_ref[...] = -ADAM_LR * ((mn * c1) / (jnp.sqrt(vn * c2) + ADAM_EPS) + ADAM_WD * w_ref[...])

    sh = jax.ShapeDtypeStruct((nrows, width), f32)
    return pl.pallas_call(body, name=name, grid=(nrows // tr,), in_specs=[blk, blk, blk, gblk] + [pl.BlockSpec(memory_space=pl.ANY)] * len(extra),
                          out_specs=[blk] * 4, out_shape=[sh] * 4, compiler_params=_cparams(("parallel",)))(w, m, v, g, *extra)


def kernel(x, c, ctx, c_ctx, ln_in_g, ln_in_b, w_mod, b_mod, w_in, w_qkv_conv, a_log_f, dt_bias_f, a_log_b, dt_bias_b, dn_norm_g, conf_dw_w, conf_dw_b, conf_ln_g, conf_ln_b, w_out, ln1_g, ln1_b, w_mlp1, b_mlp1, w_mlp2, b_mlp2, ln2_g, ln2_b, loss_target, m_c_ctx, m_ln_in_g, m_ln_in_b, m_w_mod, m_b_mod, m_w_in, m_w_qkv_conv, m_a_log_f, m_dt_bias_f, m_a_log_b, m_dt_bias_b, m_dn_norm_g, m_conf_dw_w, m_conf_dw_b, m_conf_ln_g, m_conf_ln_b, m_w_out, m_ln1_g, m_ln1_b, m_w_mlp1, m_b_mlp1, m_w_mlp2, m_b_mlp2, m_ln2_g, m_ln2_b, v_c_ctx, v_ln_in_g, v_ln_in_b, v_w_mod, v_b_mod, v_w_in, v_w_qkv_conv, v_a_log_f, v_dt_bias_f, v_a_log_b, v_dt_bias_b, v_dn_norm_g, v_conf_dw_w, v_conf_dw_b, v_conf_ln_g, v_conf_ln_b, v_w_out, v_ln1_g, v_ln1_b, v_w_mlp1, v_b_mlp1, v_w_mlp2, v_b_mlp2, v_ln2_g, v_ln2_b):
    weights = dict(c_ctx=c_ctx, ln_in_g=ln_in_g, ln_in_b=ln_in_b, w_mod=w_mod, b_mod=b_mod, w_in=w_in, w_qkv_conv=w_qkv_conv, a_log_f=a_log_f, dt_bias_f=dt_bias_f, a_log_b=a_log_b, dt_bias_b=dt_bias_b, dn_norm_g=dn_norm_g, conf_dw_w=conf_dw_w, conf_dw_b=conf_dw_b, conf_ln_g=conf_ln_g, conf_ln_b=conf_ln_b, w_out=w_out, ln1_g=ln1_g, ln1_b=ln1_b, w_mlp1=w_mlp1, b_mlp1=b_mlp1, w_mlp2=w_mlp2, b_mlp2=b_mlp2, ln2_g=ln2_g, ln2_b=ln2_b)
    mom1 = dict(c_ctx=m_c_ctx, ln_in_g=m_ln_in_g, ln_in_b=m_ln_in_b, w_mod=m_w_mod, b_mod=m_b_mod, w_in=m_w_in, w_qkv_conv=m_w_qkv_conv, a_log_f=m_a_log_f, dt_bias_f=m_dt_bias_f, a_log_b=m_a_log_b, dt_bias_b=m_dt_bias_b, dn_norm_g=m_dn_norm_g, conf_dw_w=m_conf_dw_w, conf_dw_b=m_conf_dw_b, conf_ln_g=m_conf_ln_g, conf_ln_b=m_conf_ln_b, w_out=m_w_out, ln1_g=m_ln1_g, ln1_b=m_ln1_b, w_mlp1=m_w_mlp1, b_mlp1=m_b_mlp1, w_mlp2=m_w_mlp2, b_mlp2=m_b_mlp2, ln2_g=m_ln2_g, ln2_b=m_ln2_b)
    mom2 = dict(c_ctx=v_c_ctx, ln_in_g=v_ln_in_g, ln_in_b=v_ln_in_b, w_mod=v_w_mod, b_mod=v_b_mod, w_in=v_w_in, w_qkv_conv=v_w_qkv_conv, a_log_f=v_a_log_f, dt_bias_f=v_dt_bias_f, a_log_b=v_a_log_b, dt_bias_b=v_dt_bias_b, dn_norm_g=v_dn_norm_g, conf_dw_w=v_conf_dw_w, conf_dw_b=v_conf_dw_b, conf_ln_g=v_conf_ln_g, conf_ln_b=v_conf_ln_b, w_out=v_w_out, ln1_g=v_ln1_g, ln1_b=v_ln1_b, w_mlp1=v_w_mlp1, b_mlp1=v_b_mlp1, w_mlp2=v_w_mlp2, b_mlp2=v_b_mlp2, ln2_g=v_ln2_g, ln2_b=v_ln2_b)
    names = list(weights)

    me = _lin(_mesh_pos())
    S, D = x.shape[1], x.shape[2]
    T = ctx.shape[1]
    R = S + T
    DN = D // 2
    NH = DN // HD
    CONF = D - DN
    K7, K31 = w_qkv_conv.shape[1], conf_dw_w.shape[1]
    DFF = w_mlp1.shape[2] * NDEV
    INC = w_in.shape[2] * NDEV
    CONF_OFF = 4 * DN + 4 * NH
    NC = 4 * DN + 2 * CONF + GPAD
    QB, ZB, VB, GB = 0, 3 * DN // LANES, 4 * DN // LANES, (4 * DN + 2 * CONF) // LANES
    MODC = w_mod.shape[2]
    x2d, ctx2d, tgt = x[0], ctx[0], loss_target[0]
    row = lambda a: a.reshape(1, -1).astype(f32)

    c_all, w7_all, w31_all = _all_gather("ag_small", [c.astype(f32), w_qkv_conv[0], conf_dw_w[0]])
    w7 = jnp.pad(jnp.transpose(w7_all, (1, 0, 2)).reshape(K7, 3 * DN), ((0, (-K7) % 8), (0, 0)))
    w31 = jnp.pad(jnp.transpose(w31_all, (1, 0, 2)).reshape(K31, CONF), ((0, (-K31) % 8), (0, 0)))
    c16 = jnp.concatenate([c_all.reshape(NDEV, D), c_ctx.reshape(1, D).astype(f32), jnp.zeros((7, D), f32)], axis=0)
    (sc16,) = _rowcall("silu_c", lambda j, a: (_silu(a),), 16, 16, [(c16, D, 0, 0, False)], [], [(D, f32, False)], [])
    bmod_mine = lax.dynamic_slice_in_dim(b_mod.astype(f32), me * MODC, MODC, axis=1)
    mod_part = _mm("mod_fwd", sc16, w_mod[0], "nn", f32, extras=[("row", bmod_mine)], epilogue=lambda r, b: (r + b,))
    (mod_g,) = _all_gather("ag_mod", [mod_part])
    mod_all = jnp.transpose(mod_g, (1, 0, 2)).reshape(16, 6 * D)
    mod_me = lax.dynamic_slice_in_dim(mod_all, me, 1, axis=0)
    sh_a, sc_a, g_a, sh_m, sc_m, g_m = [mod_me[:, i * D:(i + 1) * D] for i in range(6)]
    csh_a, csc_a = mod_all[8:9, 0:D], mod_all[8:9, D:2 * D]

    (g_win,) = _all_gather("ag_w_in", [w_in[0].astype(bf16)], after=mod_g)
    ag_wout = _xchg_start("ag_wout_start", w_out[0].astype(bf16), True, g_win)
    ag_w1 = _xchg_start("ag_w1_start", w_mlp1[0].astype(bf16), True, ag_wout[4])
    win_full = jnp.transpose(g_win, (1, 0, 2)).reshape(D, INC)
    w_cat = jnp.concatenate([win_full[:, :4 * DN], win_full[:, CONF_OFF:], win_full[:, 4 * DN:CONF_OFF],
                             jnp.zeros((D, GPAD - 4 * NH), bf16)], axis=1)
    g0, b0 = row(ln_in_g), row(ln_in_b)
    g0_fwd = _behind(ag_w1[4], g0)

    def ln_mod(j, xt, g, b, sh, sc):
        xh, _ = _ln_stats(xt)
        xn = xh * g + b
        return xn, xn * (1.0 + sc) + sh

    tr = _row_tile(T, D)
    vec = lambda a: (a, a.shape[1], 0, False)
    xn, xm = _rowcall("ln_in_lat", ln_mod, S, tr, [(x2d, D, 0, 0, False)], [vec(g0_fwd), vec(b0), vec(sh_a), vec(sc_a)], [(D, f32, False), (D, bf16, False)], [])
    xcn, xcm = _rowcall("ln_in_ctx", ln_mod, T, tr, [(ctx2d, D, 0, 0, False)], [vec(g0), vec(b0), vec(csh_a), vec(csc_a)], [(D, f32, False), (D, bf16, False)], [])
    xm_all = jnp.concatenate([xm, xcm], axis=0)

    h_all = _mm("in_proj", xm_all, w_cat, "nn", f32, tm=1088, tn=1280)

    qkv = _qkv_conv_fwd(h_all, w7, R, S, NH, K7).reshape(3, NH, R, HD)
    lane = jnp.arange(LANES)
    is_a = ((lane < 4 * NH) & ((lane // NH) % 2 == 0)).astype(f32).reshape(1, LANES)
    pad_g = lambda a, b: jnp.concatenate([a.reshape(-1), jnp.zeros((NH,), f32), b.reshape(-1), jnp.zeros((LANES - 3 * NH,), f32)]).reshape(1, LANES)
    neg_a = pad_g(-jnp.exp(a_log_f.astype(f32)), -jnp.exp(a_log_b.astype(f32)))
    dt_v = pad_g(dt_bias_f.astype(f32), dt_bias_b.astype(f32))

    def gates_f(j, hg, isa, na, dt):
        return (jnp.where(isa > 0.5, na * _softplus(hg + dt), _sigmoid(hg)),)

    trg = _row_tile(T, LANES)
    ag_w2 = _xchg_start("ag_w2_start", w_mlp2[0].astype(bf16), True, qkv, CORE_AND_SIB)
    (gates,) = _rowcall("gates_fwd", gates_f, R, trg, [(h_all, LANES, 0, GB, False)], [vec(_behind(ag_w2[4], is_a)), vec(neg_a), vec(dt_v)], [(LANES, f32, False)], [])
    gpad = jnp.zeros((R, LANES - 2 * NH), f32)
    gd = jnp.stack([jnp.concatenate([gates[:, :2 * NH], gpad], axis=1), jnp.concatenate([gates[:, 2 * NH:4 * NH], gpad], axis=1)])

    o5, states, *solved = _gdn_fwd(qkv, gd, NH, S, T)
    dng = row(dn_norm_g)
    trh = _row_tile(S, HD, cap=1024)
    dn_out = _gate_norm_fwd(o5, h_all, dng, NH, S, ZB, trh)

    yconv = _conf_conv_fwd(h_all, w31, S, CONF, VB, K31)
    bdw, clg, clb = row(conf_dw_b), row(conf_ln_g), row(conf_ln_b)

    def conf_ln(j, yc, b, g, bb):
        xh, _ = _ln_stats(yc + b)
        return (_silu(xh * g + bb),)

    trc = _row_tile(S, CONF)
    (conf_out,) = _rowcall("conf_ln_fwd", conf_ln, S, trc, [(yconv, CONF, 0, 0, False)], [vec(bdw), vec(clg), vec(clb)], [(CONF, bf16, False)], [])
    mix = jnp.concatenate([dn_out, conf_out], axis=1)
    wout_full = _xchg_wait("ag_wout_wait", ag_wout, True, mix).reshape(DN + CONF, D)
    y = _mm("out_proj", mix, wout_full, "nn", f32)

    l1g, l1b, l2g, l2b = row(ln1_g), row(ln1_b), row(ln2_g), row(ln2_b)

    def ln1_mod(j, xnt, yt, ga, g, b, sh, sc):
        xh, _ = _ln_stats(ALPHA * xnt + ga * yt)
        x1 = xh * g + b
        return x1, x1 * (1.0 + sc) + sh

    trd = _row_tile(S, D)
    x1, u = _rowcall("ln1_fwd", ln1_mod, S, trd, [(xn, D, 0, 0, False), (y, D, 0, 0, False)],
                     [vec(g_a), vec(l1g), vec(l1b), vec(sh_m), vec(sc_m)], [(D, f32, False), (D, bf16, False)], [])
    b1, b2 = row(b_mlp1), row(b_mlp2)
    g_w1 = _xchg_wait("ag_w1_wait", ag_w1, True, u)
    hh, act = _mm("mlp1", u, g_w1, "nn", (f32, bf16), extras=[("row", b1)], epilogue=lambda r, b: (r, jnp.square(jnp.maximum(r + b, 0.0))))
    w2_full = _sib_forward("ag_w2_fwd", _xchg_wait("ag_w2_wait", ag_w2, True, act, CORE_AND_SIB)).reshape(DFF, D)
    y2 = _mm("mlp2", act, w2_full, "nn", f32)

    def ln2_loss(j, x1t, y2t, tg, bb2, gm, g, b):
        y2b = y2t + bb2
        xh, rstd = _ln_stats(ALPHA * x1t + gm * y2b)
        err = xh * g + b - tg
        dx2 = err * (1.0 / D)
        dr2 = _ln_bwd(dx2, xh, rstd, g)
        dy2 = dr2 * gm
        lsum = jnp.broadcast_to(jnp.sum(err * err).reshape(1, 1), (1, LANES))
        return dr2, dy2, _csum(dx2 * xh), _csum(dx2), _csum(dr2 * y2b), _csum(dy2), lsum

    dr2, dy2, d_l2g, d_l2b, d_gm, d_b2, lsum = _rowcall(
        "ln2_loss", ln2_loss, S, trd, [(x1, D, 0, 0, False), (y2, D, 0, 0, False), (tgt, D, 0, 0, False)],
        [vec(b2), vec(g_m), vec(l2g), vec(l2b)], [(D, f32, False), (D, bf16, False)], [(1, D, False)] * 4 + [(1, LANES, False)])
    loss = lax.psum(0.5 * lsum[0, 0] / D, ("x", "y", "c"))

    dhh = _mm("mlp2_dx", dy2, w2_full, "nt", bf16, extras=[("tile", hh), ("row", b1)],
              epilogue=lambda r, h, b: (r * (2.0 * jnp.maximum(h + b, 0.0)),))
    dw2_p = _mm("mlp2_dw", act, dy2, "tn", bf16)
    a2a_w2 = _xchg_start("a2a_w2_start", dw2_p.reshape(NDEV, DFF // NDEV, D), False, dhh, land_init=w2_full.reshape(NDEV, DFF // NDEV, D))
    (d_b1,) = _rowcall("b1_grad", lambda j, a: (_csum(a.astype(f32)),), S, _row_tile(S, DFF // 8), [(dhh, DFF, 0, 0, False)], [], [], [(1, DFF, False)])
    du = _mm("mlp1_dx", dhh, g_w1, "nt", f32, after=a2a_w2[4])
    dw1_p = _mm("mlp1_dw", u, dhh, "tn", bf16, out_split=NDEV)
    a2a_w1 = _xchg_start("a2a_w1_start", dw1_p, False, du, land_init=g_w1)

    def ln1_bwd(j, dr2t, dut, xnt, yt, ga, g, b, sc):
        xh, rstd = _ln_stats(ALPHA * xnt + ga * yt)
        x1t = xh * g + b
        dx1 = ALPHA * dr2t + dut * (1.0 + sc)
        dr1 = _ln_bwd(dx1, xh, rstd, g)
        return dr1, dr1 * ga, _csum(dut * x1t), _csum(dut), _csum(dx1 * xh), _csum(dx1), _csum(dr1 * yt)

    dr1, dy, d_scm, d_shm, d_l1g, d_l1b, d_ga = _rowcall(
        "ln1_bwd", ln1_bwd, S, trd, [(dr2, D, 0, 0, False), (du, D, 0, 0, False), (xn, D, 0, 0, False), (y, D, 0, 0, False)],
        [vec(g_a), vec(l1g), vec(l1b), vec(_behind(a2a_w1[4], sc_m))], [(D, f32, False), (D, bf16, False)], [(1, D, False)] * 5)
    dmix = _mm("out_proj_dx", dy, wout_full, "nt", f32)
    dwout_p = _mm("out_proj_dw", mix, dy, "tn", bf16)
    a2a_wout = _xchg_start("a2a_wout_start", dwout_p.reshape(NDEV, (DN + CONF) // NDEV, D), False, dmix,
                           land_init=wout_full.reshape(NDEV, (DN + CONF) // NDEV, D))

    def conf_ln_b(j, dm, yc, b, g, bb):
        xh, rstd = _ln_stats(yc + b)
        dln = dm * _dsilu(xh * g + bb)
        dyc = _ln_bwd(dln, xh, rstd, g)
        return dyc, _csum(dln * xh), _csum(dln), _csum(dyc)

    dyc, d_clg, d_clb, d_bdw = _rowcall("conf_ln_bwd", conf_ln_b, S, trc, [(dmix, CONF, 0, DN // CONF, False), (yconv, CONF, 0, 0, False)],
                                        [vec(_behind(a2a_wout[4], bdw)), vec(clg), vec(clb)], [(CONF, f32, False)], [(1, CONF, False)] * 3)
    dval, dgate, dw31 = _conf_conv_bwd(h_all, w31, dyc, S, CONF, VB, K31)

    do, dz, d_dng = _gate_norm_bwd(dmix, o5, h_all, dng, NH, S, ZB, trh)
    dqkv, dgd = _gdn_bwd(qkv, gd, states, solved, do, NH, S, T)
    dh_qkv, dw7 = _qkv_conv_bwd(h_all, w7, dqkv.reshape(2, 3 * NH, R, HD), R, S, NH, K7)
    dgate_cols = jnp.concatenate([dgd[0][:, :2 * NH], dgd[1][:, :2 * NH], jnp.zeros((R, LANES - 4 * NH), f32)], axis=1)

    def gates_b(j, hg, dgt, isa, na, dt):
        sg = _sigmoid(hg)
        sp = _softplus(hg + dt)
        dpre = jnp.where(isa > 0.5, dgt * na * _sigmoid(hg + dt), dgt * sg * (1.0 - sg))
        return dpre, _csum(jnp.where(isa > 0.5, dgt * na * sp, 0.0)), _csum(jnp.where(isa > 0.5, dpre, 0.0))

    dh_g, d_alog, d_dt = _rowcall("gates_bwd", gates_b, R, trg, [(h_all, LANES, 0, GB, False), (dgate_cols, LANES, 0, 0, False)],
                                  [vec(is_a), vec(neg_a), vec(dt_v)], [(LANES, bf16, False)], [(1, LANES, False)] * 2)

    zpad = lambda a: jnp.concatenate([a, jnp.zeros((T, a.shape[1]), bf16)], axis=0)
    dh_all = jnp.concatenate([dh_qkv, zpad(dz), zpad(dval), zpad(dgate), dh_g, jnp.zeros((R, GPAD - LANES), bf16)], axis=1)
    dwcat_p = _mm("in_proj_dw", xm_all, dh_all, "tn", bf16, tn=1280, tk=4352)
    dwin = jnp.concatenate([dwcat_p[:, :4 * DN], dwcat_p[:, 4 * DN + 2 * CONF:4 * DN + 2 * CONF + 4 * NH], dwcat_p[:, 4 * DN:4 * DN + 2 * CONF]], axis=1)
    dwin_p = jnp.transpose(dwin.reshape(D, NDEV, INC // NDEV), (1, 0, 2))
    from_sib = _sib_swap("sib_win", dwin_p)
    mine4 = lax.dynamic_index_in_dim(dwin_p.reshape(4, 2, D, INC // NDEV), lax.axis_index("c"), 1, keepdims=False)
    (chip_sum,) = _rowcall("add_win", lambda j, a, b: (a.astype(f32) + b.astype(f32),), 4 * D, _row_tile(4 * D, INC // NDEV),
                           [(mine4.reshape(4 * D, INC // NDEV), INC // NDEV, 0, 0, False), (from_sib.reshape(4 * D, INC // NDEV), INC // NDEV, 0, 0, False)],
                           [], [(INC // NDEV, bf16, False)], [])
    a2a_win = _xchg_start("a2a_win_start", chip_sum.reshape(4, D, INC // NDEV), False, dh_g, SAME_CORE)
    dxm_all = _mm("in_proj_dx", dh_all, w_cat, "nt", f32, tm=1088, tk=2560, after=a2a_win[4])

    def ln_in_bwd_lat(j, xt, dr1t, dxm, g, b, sc):
        xh, rstd = _ln_stats(xt)
        xnt = xh * g + b
        dxn = ALPHA * dr1t + dxm * (1.0 + sc)
        return _ln_bwd(dxn, xh, rstd, g), _csum(dxm * xnt), _csum(dxm), _csum(dxn * xh), _csum(dxn)

    def ln_in_bwd_ctx(j, xt, dxm, g, b, sc):
        xh, rstd = _ln_stats(xt)
        xnt = xh * g + b
        dxn = dxm * (1.0 + sc)
        return _csum(dxm * xnt), _csum(dxm), _csum(dxn * xh), _csum(dxn)

    grad_x, d_sca, d_sha, d_g0a, d_b0a = _rowcall(
        "ln_in_bwd_lat", ln_in_bwd_lat, S, trd, [(x2d, D, 0, 0, False), (dr1, D, 0, 0, False), (dxm_all, D, 0, 0, False)],
        [vec(g0), vec(b0), vec(sc_a)], [(D, f32, False)], [(1, D, False)] * 4)
    d_csca, d_csha, d_g0b, d_b0b = _rowcall(
        "ln_in_bwd_ctx", ln_in_bwd_ctx, T, tr, [(ctx2d, D, 0, 0, False), (dxm_all, D, S // tr, 0, False)],
        [vec(g0), vec(b0), vec(csc_a)], [], [(1, D, False)] * 4)

    zD = jnp.zeros((1, D), f32)
    dmod_me = jnp.concatenate([d_sha, d_sca, d_ga, d_shm, d_scm, d_gm], axis=1)
    dmodc_me = jnp.concatenate([d_csha, d_csca, zD, zD, zD, zD], axis=1)
    small_names = ["ln_in_g", "ln_in_b", "a_log_f", "dt_bias_f", "a_log_b", "dt_bias_b", "dn_norm_g", "conf_dw_b", "conf_ln_g", "conf_ln_b",
                   "ln1_g", "ln1_b", "b_mlp1", "b_mlp2", "ln2_g", "ln2_b"]
    small_parts = [d_g0a + d_g0b, d_b0a + d_b0b, d_alog[:, 0:NH], d_dt[:, 0:NH], d_alog[:, 2 * NH:3 * NH], d_dt[:, 2 * NH:3 * NH], d_dng, d_bdw, d_clg, d_clb,
                   d_l1g, d_l1b, d_b1, d_b2, d_l2g, d_l2b]
    conv_parts = [dw7[:K7], dw31[:K31]]
    packed = _pack(small_parts + conv_parts + [dmodc_me])
    (pk_all, dmod_all) = _all_gather("ag_small_grads", [packed, dmod_me])
    summed = _sum8("sum_small_grads", pk_all.reshape(NDEV, -1))
    parts_sh = [a.shape for a in small_parts + conv_parts + [dmodc_me]]
    un = _unpack(summed, parts_sh)
    gsmall = dict(zip(small_names, un[:len(small_names)]))
    g_w7_full, g_w31_full, dmodc = un[len(small_names):]
    gsmall["w_qkv_conv"] = lax.dynamic_slice_in_dim(g_w7_full, me * w_qkv_conv.shape[2], w_qkv_conv.shape[2], axis=1)
    gsmall["conf_dw_w"] = lax.dynamic_slice_in_dim(g_w31_full, me * conf_dw_w.shape[2], conf_dw_w.shape[2], axis=1)

    dm16 = jnp.concatenate([dmod_all.reshape(NDEV, 6 * D), dmodc, jnp.zeros((7, 6 * D), f32)], axis=0)
    (gbmod,) = _rowcall("bmod_grad", lambda j, a: (_csum(a),), 16, 16, [(dm16, 6 * D, 0, 0, False)], [], [], [(1, 6 * D, False)])
    gsmall["b_mod"] = gbmod
    dm16_mine = lax.dynamic_slice_in_dim(dm16, me * MODC, MODC, axis=1)
    dsc16_part = _mm("mod_dx", dm16_mine, w_mod[0], "nt", f32)
    (dsc_all,) = _all_gather("ag_cctx", [dsc16_part[8:9]])
    dsilu_cctx = _sum8("sum_cctx", dsc_all.reshape(NDEV, D))
    (g_cctx,) = _rowcall("cctx_grad", lambda j, ds, cc: (ds * _dsilu(cc),), 1, 1, [(dsilu_cctx, D, 0, 0, False), (c_ctx.reshape(1, D).astype(f32), D, 0, 0, False)],
                         [], [(D, f32, False)], [])
    gsmall["c_ctx"] = g_cctx

    g_wmod = _mm("mod_dw", sc16, dm16_mine, "tn", f32, after=a2a_win[4])

    grads, deltas, new_m, new_v = {}, {}, {}, {}
    res = _adam("adam_w_mod", w_mod[0], m_w_mod[0], v_w_mod[0], g_wmod, 0)
    grads["w_mod"], deltas["w_mod"], new_m["w_mod"], new_v["w_mod"] = [a.reshape(w_mod.shape) for a in res]
    after = res[1]
    for nm, started in (("w_mlp2", a2a_w2), ("w_mlp1", a2a_w1), ("w_out", a2a_wout), ("w_in", a2a_win)):
        w3 = weights[nm]
        rels = SAME_CORE if nm == "w_in" else ALL_PEERS
        g = _xchg_wait("a2a_" + nm + "_wait", started, False, after, rels)
        res = _adam("adam_" + nm, w3[0], mom1[nm][0], mom2[nm][0], g, len(rels) + 1)
        grads[nm], deltas[nm], new_m[nm], new_v[nm] = [a.reshape(w3.shape) for a in res]
        after = res[1]
    snames = [n for n in names if n not in grads]
    res = _adam("adam_small", _pack([weights[n] for n in snames]), _pack([mom1[n] for n in snames]), _pack([mom2[n] for n in snames]),
                _pack([gsmall[n] for n in snames]), 0)
    shapes = [weights[n].shape for n in snames]
    for dst, packed_out in zip((grads, deltas, new_m, new_v), res):
        for n, a in zip(snames, _unpack(packed_out, shapes)):
            dst[n] = a

    return (loss, grad_x.reshape(x.shape), *[grads[n] for n in names], *[deltas[n] for n in names],
            *[new_m[n] for n in names], *[new_v[n] for n in names])
```
